```python
import jax
import jax.numpy as jnp
from jax import lax
import numpy as np

D_MODEL = 2048
BATCH = 2
SEQ = 4096
DEPTH = 1
DEC_BATCH = 128
DEC_SEQ = 8
PAST_LEN = 2048
PAGE_SIZE = 128

NSA_HEADS = 16
NSA_KV_HEADS = 4
NSA_GROUP = NSA_HEADS // NSA_KV_HEADS
HEAD_DIM = 64
NSA_WIDTH = NSA_HEADS * HEAD_DIM
L_CMP = 32
D_CMP = 16
CMP_HID = HEAD_DIM
L_SEL = 64
N_SEL_TOP = 16
WINDOW = 512
Q_BLOCK = 128
ROPE_THETA = 10000.0
HGRN_HEADS = 8
HGRN_DK = 128
HGRN_DV = 128
HGRN_KW = HGRN_HEADS * HGRN_DK
HGRN_VW = HGRN_HEADS * HGRN_DV
HGRN_CHUNK = 64
D_FF = 4 * D_MODEL
EPS = 1e-6
NEG = -1e30
BIG = 1e9
_SIZES = (NSA_WIDTH, 6 * NSA_KV_HEADS * HEAD_DIM, 3 * NSA_HEADS, HGRN_KW, HGRN_KW, HGRN_VW, HGRN_VW, D_MODEL, D_MODEL)
_SPLITS = tuple(int(v) for v in np.cumsum(_SIZES)[:-1])
N_IN = sum(_SIZES)
F32 = jnp.float32

kernel_name = 'nsa_hgrn2_gated_hybrid_step'


def _rmsnorm(x, g):
    xf = x.astype(F32)
    y = xf * lax.rsqrt(jnp.mean(xf * xf, axis=-1, keepdims=True) + EPS)
    return (y * g.astype(F32)).astype(x.dtype)


def _rope(x, pos):
    hd = x.shape[-1]
    half = hd // 2
    inv = ROPE_THETA ** (-jnp.arange(half, dtype=F32) * 2.0 / hd)
    ang = pos.astype(F32)[:, None] * inv[None, :]
    shape = (ang.shape[0],) + (1,) * (x.ndim - 3) + (half,)
    cos = jnp.cos(ang).reshape(shape)
    sin = jnp.sin(ang).reshape(shape)
    xf = x.astype(F32)
    x1, x2 = xf[..., :half], xf[..., half:]
    return jnp.concatenate([x1 * cos - x2 * sin, x2 * cos + x1 * sin], axis=-1).astype(x.dtype)


def _masked_softmax(s, valid):
    s = jnp.where(valid, s, NEG)
    m = jnp.max(s, axis=-1, keepdims=True)
    e = jnp.exp(s - m) * valid
    return e / jnp.maximum(jnp.sum(e, axis=-1, keepdims=True), 1.0)


def _overlap(nc, ns):
    cs = np.arange(nc) * D_CMP
    ce = cs + L_CMP
    ss = np.arange(ns) * L_SEL
    se = ss + L_SEL
    ov = np.clip(np.minimum(ce[:, None], se[None, :]) - np.maximum(cs[:, None], ss[None, :]), 0, None)
    return jnp.asarray((ov / D_CMP).astype(np.float32))


def _compress(rows, pe, w1, w2):
    B, T, G, hd = rows.shape
    n_ch = T // D_CMP
    m = L_CMP // D_CMP
    nc = n_ch - m + 1
    ch = rows[:, :n_ch * D_CMP].reshape(B, n_ch, D_CMP, G, hd).transpose(0, 1, 3, 2, 4).reshape(B, n_ch, G, D_CMP * hd)
    w1p = w1.reshape(m, D_CMP * hd, CMP_HID)
    h = pe.reshape(-1) @ w1
    for i in range(m):
        h = h + ch[:, i:i + nc] @ w1p[i]
    return jax.nn.silu(h) @ w2


def _selected_attn(q, qpos, idx, ks_t, vs_t):
    B, Q, G, R, hd = q.shape
    scale = hd ** -0.5
    bi = jnp.arange(B)[:, None, None]
    gi = jnp.arange(G)[None, None, :]

    def step(carry, j):
        m, l, acc = carry
        kb = ks_t[bi, gi, j]
        vb = vs_t[bi, gi, j]
        s = jnp.einsum('bqgrd,bqgld->bqgrl', q, kb) * scale
        kpos = j[..., None] * L_SEL + jnp.arange(L_SEL)
        valid = (kpos <= qpos[None, :, None, None])[:, :, :, None, :]
        s = jnp.where(valid, s, NEG)
        m_new = jnp.maximum(m, jnp.max(s, axis=-1))
        corr = jnp.exp(m - m_new)
        p = jnp.exp(s - m_new[..., None]) * valid
        l = l * corr + jnp.sum(p, axis=-1)
        acc = acc * corr[..., None] + jnp.einsum('bqgrl,bqgld->bqgrd', p, vb)
        return (m_new, l, acc), None

    init = (jnp.full((B, Q, G, R), NEG, F32), jnp.zeros((B, Q, G, R), F32), jnp.zeros((B, Q, G, R, hd), F32))
    (m, l, acc), _ = lax.scan(step, init, jnp.moveaxis(idx, -1, 0))
    return acc / l[..., None]


def _cmp_sel(q, qpos, kc, vc, ks_t, vs_t):
    hd = q.shape[-1]
    nc = kc.shape[1]
    ns = ks_t.shape[2]
    s = jnp.einsum('bqgrd,bcgd->bqgrc', q, kc) * (hd ** -0.5)
    c_end = jnp.arange(nc) * D_CMP + (L_CMP - 1)
    cvalid = (c_end[None, :] <= qpos[:, None])[None, :, None, None, :]
    p_cmp = _masked_softmax(s, cvalid)
    o_cmp = jnp.einsum('bqgrc,bcgd->bqgrd', p_cmp, vc)
    imp = jnp.einsum('bqgrc,cn->bqgn', p_cmp, _overlap(nc, ns))
    blk = jnp.arange(ns)[None, :]
    cur = (qpos // L_SEL)[:, None]
    visible = blk * L_SEL <= qpos[:, None]
    forced = (blk == 0) | (blk == cur) | (blk == cur - 1)
    score = jnp.where(visible[None, :, None, :], imp + jnp.where(forced, BIG, 0.0)[None, :, None, :], -BIG)
    _, idx = lax.top_k(score, min(N_SEL_TOP, ns))
    o_sel = _selected_attn(q, qpos, idx, ks_t, vs_t)
    return o_cmp, o_sel


def _window_attn(q, qpos, k, v, kpos):
    s = jnp.einsum('bqgrd,bkgd->bqgrk', q, k) * (q.shape[-1] ** -0.5)
    valid = (kpos[None, :] <= qpos[:, None]) & (kpos[None, :] > qpos[:, None] - WINDOW) & (kpos[None, :] >= 0)
    p = _masked_softmax(s, valid[None, :, None, None, :])
    return jnp.einsum('bqgrk,bkgd->bqgrd', p, v)


def _gate_branches(br_gate, o_c, o_s, o_w):
    B, T = o_c.shape[:2]
    o = br_gate[:, :, 0, :, :, None] * o_c + br_gate[:, :, 1, :, :, None] * o_s + br_gate[:, :, 2, :, :, None] * o_w
    return o.reshape(B, T, NSA_WIDTH)


def _hgrn2(q, k, v, log_f, s0, chunk):
    B, T, H, DK = q.shape
    DV = v.shape[-1]
    nck = T // chunk

    def blocks(a):
        return a.reshape(B, nck, chunk, H, a.shape[-1]).transpose(1, 0, 3, 2, 4)

    causal = jnp.tril(jnp.ones((chunk, chunk), bool))[:, :, None]

    def step(S, inp):
        qc, kc, vc, ac = inp
        A = jnp.cumsum(ac, axis=2)
        o = jnp.einsum('bhtk,bhkv->bhtv', qc * jnp.exp(A), S)
        decay = jnp.exp(jnp.where(causal, A[:, :, :, None, :] - A[:, :, None, :, :], -jnp.inf))
        att = jnp.einsum('bhtk,bhsk,bhtsk->bhts', qc, kc, decay)
        o = o + jnp.einsum('bhts,bhsv->bhtv', att, vc)
        a_last = A[:, :, -1:, :]
        S = jnp.exp(a_last[:, :, 0, :])[..., None] * S + jnp.einsum('bhsk,bhsv->bhkv', kc * jnp.exp(a_last - A), vc)
        return S, o

    S, o = lax.scan(step, s0, (blocks(q), blocks(k), blocks(v), blocks(log_f)))
    return o.transpose(1, 0, 3, 2, 4).reshape(B, T, H, DV), S


def _project(x, pos, lb, norm1_g, w_in, q_norm_g, k_norm_g):
    B, T, _ = x.shape
    z = _rmsnorm(x, norm1_g) @ w_in
    zq, zkv, zga, zhq, zhf, zhi, zhg, gate_a, gate_b = jnp.split(z, _SPLITS, axis=-1)
    q = _rope(_rmsnorm(zq.reshape(B, T, NSA_HEADS, HEAD_DIM), q_norm_g), pos)
    q = q.astype(F32).reshape(B, T, NSA_KV_HEADS, NSA_GROUP, HEAD_DIM)
    kv = zkv.reshape(B, T, 3, 2, NSA_KV_HEADS, HEAD_DIM)
    k = _rope(_rmsnorm(kv[:, :, :, 0], k_norm_g[:, None, :]), pos)
    rows = jnp.stack([k, kv[:, :, :, 1]], axis=3)
    br_gate = jax.nn.sigmoid(zga.astype(F32)).reshape(B, T, 3, NSA_KV_HEADS, NSA_GROUP)
    hq = jax.nn.silu(zhq.astype(F32)).reshape(B, T, HGRN_HEADS, HGRN_DK)
    zf = zhf.astype(F32).reshape(B, T, HGRN_HEADS, HGRN_DK)
    log_f = jnp.logaddexp(jnp.log(lb), jnp.log1p(-lb) + jax.nn.log_sigmoid(zf))
    hk = (1.0 - lb) * jax.nn.sigmoid(-zf)
    hv = zhi.astype(F32).reshape(B, T, HGRN_HEADS, HGRN_DV)
    return q, rows, br_gate, hq, hk, hv, log_f, zhg, gate_a, gate_b


def _finish(x, oa, ob, zhg, gate_a, gate_b, hgrn_norm_g, w_proj_a, w_proj_b, w_out, norm2_g, w_ff1, w_ff2):
    dt = x.dtype
    B, T, _ = x.shape
    og = jax.nn.silu(zhg.astype(F32)).reshape(B, T, HGRN_HEADS, HGRN_DV)
    ob = _rmsnorm(ob, hgrn_norm_g) * og
    ya = oa.astype(dt) @ w_proj_a
    yb = ob.reshape(B, T, HGRN_VW).astype(dt) @ w_proj_b
    mix = jax.nn.sigmoid(gate_a) * ya + jax.nn.sigmoid(gate_b) * yb
    x = x + mix @ w_out
    hf = _rmsnorm(x, norm2_g) @ w_ff1
    return x + jnp.square(jax.nn.relu(hf)) @ w_ff2


def _layer_prompt(x, lb, lw):
    (norm1_g, w_in, q_norm_g, k_norm_g, cmp_pe, cmp_w1, cmp_w2, w_proj_a,
     hgrn_norm_g, w_proj_b, w_out, norm2_g, w_ff1, w_ff2) = lw
    B, T, _ = x.shape
    pos = jnp.arange(T)
    q, rows, br_gate, hq, hk, hv, log_f, zhg, gate_a, gate_b = _project(x, pos, lb, norm1_g, w_in, q_norm_g, k_norm_g)
    rows_f = rows.astype(F32)
    kc = _compress(rows_f[:, :, 0, 0], cmp_pe[0], cmp_w1[0], cmp_w2[0])
    vc = _compress(rows_f[:, :, 0, 1], cmp_pe[1], cmp_w1[1], cmp_w2[1])
    ns = T // L_SEL
    sel_blk = rows_f[:, :, 1].reshape(B, ns, L_SEL, 2, NSA_KV_HEADS, HEAD_DIM).transpose(3, 0, 4, 1, 2, 5)
    win_pad = jnp.pad(rows_f[:, :, 2], ((0, 0), (WINDOW, 0), (0, 0), (0, 0), (0, 0)))

    def q_block(b):
        s0 = b * Q_BLOCK
        qb = lax.dynamic_slice_in_dim(q, s0, Q_BLOCK, axis=1)
        qpos = s0 + jnp.arange(Q_BLOCK)
        o_c, o_s = _cmp_sel(qb, qpos, kc, vc, sel_blk[0], sel_blk[1])
        kw = lax.dynamic_slice_in_dim(win_pad, s0, WINDOW + Q_BLOCK, axis=1)
        o_w = _window_attn(qb, qpos, kw[:, :, 0], kw[:, :, 1], s0 - WINDOW + jnp.arange(WINDOW + Q_BLOCK))
        return o_c, o_s, o_w

    outs = lax.map(q_block, jnp.arange(T // Q_BLOCK))
    o_c, o_s, o_w = [jnp.moveaxis(o, 0, 1).reshape(B, T, NSA_KV_HEADS, NSA_GROUP, HEAD_DIM) for o in outs]
    oa = _gate_branches(br_gate, o_c, o_s, o_w)
    ob, s_fin = _hgrn2(hq, hk, hv, log_f, jnp.zeros((B, HGRN_HEADS, HGRN_DK, HGRN_DV), F32), HGRN_CHUNK)
    y = _finish(x, oa, ob, zhg, gate_a, gate_b, hgrn_norm_g, w_proj_a, w_proj_b, w_out, norm2_g, w_ff1, w_ff2)
    wp = min(WINDOW, T)
    return y, (rows[:, :, 0], rows[:, :, 1], rows[:, T - wp:, 2], s_fin)


def _layer_sample(x, cache_cmp, cache_sel, win_buf, s_past, page_table, lb, lw):
    (norm1_g, w_in, q_norm_g, k_norm_g, cmp_pe, cmp_w1, cmp_w2, w_proj_a,
     hgrn_norm_g, w_proj_b, w_out, norm2_g, w_ff1, w_ff2) = lw
    DB, S, _ = x.shape
    P = page_table.shape[1] * cache_cmp.shape[1]
    WB = win_buf.shape[1]
    pos = P + jnp.arange(S)
    q, rows, br_gate, hq, hk, hv, log_f, zhg, gate_a, gate_b = _project(x, pos, lb, norm1_g, w_in, q_norm_g, k_norm_g)
    rows_f = rows.astype(F32)
    kv_shape = (DB, P, 2, NSA_KV_HEADS, HEAD_DIM)
    cmp_all = jnp.concatenate([cache_cmp[page_table].reshape(kv_shape).astype(F32), rows_f[:, :, 0]], axis=1)
    sel_all = jnp.concatenate([cache_sel[page_table].reshape(kv_shape).astype(F32), rows_f[:, :, 1]], axis=1)
    kc = _compress(cmp_all[:, :, 0], cmp_pe[0], cmp_w1[0], cmp_w2[0])
    vc = _compress(cmp_all[:, :, 1], cmp_pe[1], cmp_w1[1], cmp_w2[1])
    t_all = P + S
    ns = -(-t_all // L_SEL)
    sel_all = jnp.pad(sel_all, ((0, 0), (0, ns * L_SEL - t_all), (0, 0), (0, 0), (0, 0)))
    sel_blk = sel_all.reshape(DB, ns, L_SEL, 2, NSA_KV_HEADS, HEAD_DIM).transpose(3, 0, 4, 1, 2, 5)
    o_c, o_s = _cmp_sel(q, pos, kc, vc, sel_blk[0], sel_blk[1])
    win_all = jnp.concatenate([win_buf.astype(F32), rows_f[:, :, 2]], axis=1)
    o_w = _window_attn(q, pos, win_all[:, :, 0], win_all[:, :, 1], P - WB + jnp.arange(WB + S))
    oa = _gate_branches(br_gate, o_c, o_s, o_w)
    ob, s_new = _hgrn2(hq, hk, hv, log_f, s_past.astype(F32), S)
    y = _finish(x, oa, ob, zhg, gate_a, gate_b, hgrn_norm_g, w_proj_a, w_proj_b, w_out, norm2_g, w_ff1, w_ff2)
    return y, (rows[:, :, 0], rows[:, :, 1], win_all[:, S:].astype(rows.dtype), s_new)


def setup_inputs(seed: int = 0) -> dict:
    key = jax.random.key(seed)
    ks = jax.random.split(key, 24)

    def nrm(k, shape, scale):
        return scale * jax.random.normal(k, shape, F32)

    n_pages = PAST_LEN // PAGE_SIZE
    n_used = DEC_BATCH * n_pages
    n_pool = n_used + max(1, n_used // 4)
    win_buf = min(WINDOW, PAST_LEN)
    page_table = jax.random.permutation(ks[6], n_pool)[:n_used].reshape(DEC_BATCH, n_pages).astype(jnp.int32)
    kv_shape = (DEPTH, n_pool, PAGE_SIZE, 2, NSA_KV_HEADS, HEAD_DIM)
    return {
        'x_prompt': nrm(ks[0], (BATCH, SEQ, D_MODEL), 1.0),
        'x_sample': nrm(ks[1], (DEC_BATCH, DEC_SEQ, D_MODEL), 1.0),
        'cache_cmp_kv': nrm(ks[2], kv_shape, 1.0),
        'cache_sel_kv': nrm(ks[3], kv_shape, 1.0),
        'cache_win_kv': nrm(ks[4], (DEPTH, DEC_BATCH, win_buf, 2, NSA_KV_HEADS, HEAD_DIM), 1.0),
        'state_hgrn': nrm(ks[5], (DEPTH, DEC_BATCH, HGRN_HEADS, HGRN_DK, HGRN_DV), 0.3),
        'page_table': page_table,
        'norm1_g': 1.0 + nrm(ks[7], (DEPTH, D_MODEL), 0.02),
        'w_in': nrm(ks[8], (DEPTH, D_MODEL, N_IN), D_MODEL ** -0.5),
        'q_norm_g': 1.0 + nrm(ks[9], (DEPTH, HEAD_DIM), 0.02),
        'k_norm_g': 1.0 + nrm(ks[10], (DEPTH, 3, HEAD_DIM), 0.02),
        'cmp_pe': nrm(ks[11], (DEPTH, 2, L_CMP, HEAD_DIM), 0.1),
        'cmp_w1': nrm(ks[12], (DEPTH, 2, L_CMP * HEAD_DIM, CMP_HID), (L_CMP * HEAD_DIM) ** -0.5),
        'cmp_w2': nrm(ks[13], (DEPTH, 2, CMP_HID, HEAD_DIM), CMP_HID ** -0.5),
        'w_proj_a': nrm(ks[14], (DEPTH, NSA_WIDTH, D_MODEL), NSA_WIDTH ** -0.5),
        'hgrn_lb_logits': nrm(ks[15], (DEPTH + 1, HGRN_KW), 1.0),
        'hgrn_norm_g': 1.0 + nrm(ks[16], (DEPTH, HGRN_DV), 0.02),
        'w_proj_b': nrm(ks[17], (DEPTH, HGRN_VW, D_MODEL), HGRN_VW ** -0.5),
        'w_out': nrm(ks[18], (DEPTH, D_MODEL, D_MODEL), D_MODEL ** -0.5),
        'norm2_g': 1.0 + nrm(ks[19], (DEPTH, D_MODEL), 0.02),
        'w_ff1': nrm(ks[20], (DEPTH, D_MODEL, D_FF), D_MODEL ** -0.5),
        'w_ff2': nrm(ks[21], (DEPTH, D_FF, D_MODEL), D_FF ** -0.5),
    }


def reference(x_prompt, x_sample, cache_cmp_kv, cache_sel_kv, cache_win_kv, state_hgrn, page_table,
              norm1_g, w_in, q_norm_g, k_norm_g, cmp_pe, cmp_w1, cmp_w2, w_proj_a, hgrn_lb_logits,
              hgrn_norm_g, w_proj_b, w_out, norm2_g, w_ff1, w_ff2):
    lb_all = jnp.cumsum(jax.nn.softmax(hgrn_lb_logits.astype(F32), axis=0), axis=0).reshape(DEPTH + 1, HGRN_HEADS, HGRN_DK)
    xp, xs = x_prompt, x_sample
    new_p, new_s = [], []
    for l in range(DEPTH):
        lw = (norm1_g[l], w_in[l], q_norm_g[l], k_norm_g[l], cmp_pe[l], cmp_w1[l], cmp_w2[l], w_proj_a[l],
              hgrn_norm_g[l], w_proj_b[l], w_out[l], norm2_g[l], w_ff1[l], w_ff2[l])
        xp, st_p = _layer_prompt(xp, lb_all[l], lw)
        xs, st_s = _layer_sample(xs, cache_cmp_kv[l], cache_sel_kv[l], cache_win_kv[l], state_hgrn[l], page_table, lb_all[l], lw)
        new_p.append(st_p)
        new_s.append(st_s)
    p_cmp, p_sel, p_win, p_hgrn = [jnp.stack([st[i] for st in new_p]) for i in range(4)]
    s_cmp, s_sel, s_win, s_hgrn = [jnp.stack([st[i] for st in new_s]) for i in range(4)]
    return (xp, xs, p_cmp, p_sel, p_win, p_hgrn, s_cmp, s_sel, s_win, s_hgrn)
```

```python
import functools

import numpy as np
import jax
import jax.numpy as jnp
from jax import lax
from jax.experimental import pallas as pl
from jax.experimental.pallas import tpu as pltpu

F32 = jnp.float32
BF16 = jnp.bfloat16

D_MODEL = 2048
HEADS = 16
KVH = 4
GRP = HEADS // KVH
HD = 64
NSA_W = HEADS * HD
L_CMP = 32
D_CMP = 16
L_SEL = 64
N_TOP = 16
WINDOW = 512
Q_BLK = 128
THETA = 10000.0
HG_H = 8
HG_DK = 128
HG_DV = 128
HG_W = HG_H * HG_DK
HG_SUB = 16
EPS = 1e-6
NEG = -1e30
BIG = 1e9

LANES = 128
KVW = 2 * KVH * HD
KW = KVH * HD

C_Q, C_HQ, C_HF, C_HI, C_GA, C_GB, C_HG, C_KV, C_BR = 0, 1024, 2048, 3072, 4096, 6144, 8192, 9216, 10752
N_Z = 10880
_SIZES = (NSA_W, 6 * KVH * HD, 3 * HEADS, HG_W, HG_W, HG_W, HG_W, D_MODEL, D_MODEL)

VMEM_LIMIT = 56 * 1024 * 1024


def _params(*sem):
    return pltpu.CompilerParams(dimension_semantics=sem, vmem_limit_bytes=VMEM_LIMIT)


def _nt(a, b):
    return lax.dot_general(a, b, (((1,), (1,)), ((), ())), preferred_element_type=F32)


def _dot(a, b):
    return jnp.dot(a, b, preferred_element_type=F32)


def _split3(x):
    hi = x.astype(BF16)
    r1 = x - hi.astype(F32)
    mid = r1.astype(BF16)
    lo = (r1 - mid.astype(F32)).astype(BF16)
    return hi, mid, lo


def _dot_exact_rhs(m, x):
    hi, mid, lo = _split3(x)
    return _dot(m, hi) + _dot(m, mid) + _dot(m, lo)


def _dot_exact_lhs(x, m):
    hi, mid, lo = _split3(x)
    return _dot(hi, m) + _dot(mid, m) + _dot(lo, m)


def _pick_tile(n, cands):
    for c in cands:
        if n % c == 0:
            return c
    raise ValueError(f"no tile in {cands} divides {n}")


def _rms_matmul_kernel(x_ref, g_ref, w_ref, o_ref, xn_ref):
    @pl.when(pl.program_id(1) == 0)
    def _():
        x = x_ref[...]
        ms = jnp.mean(x * x, axis=-1, keepdims=True)
        xn_ref[...] = ((x * lax.rsqrt(ms + EPS)) * g_ref[...]).astype(BF16)

    o_ref[...] = _dot(xn_ref[...], w_ref[...])


def _in_proj(x, g, w):
    n, d = x.shape
    nout = w.shape[1]
    tm = _pick_tile(n, (1024, 512, 256))
    tn = 640
    return pl.pallas_call(
        _rms_matmul_kernel,
        grid=(n // tm, nout // tn),
        in_specs=[
            pl.BlockSpec((tm, d), lambda i, j: (i, 0)),
            pl.BlockSpec((1, d), lambda i, j: (0, 0)),
            pl.BlockSpec((d, tn), lambda i, j: (0, j)),
        ],
        out_specs=pl.BlockSpec((tm, tn), lambda i, j: (i, j)),
        out_shape=jax.ShapeDtypeStruct((n, nout), F32),
        scratch_shapes=[pltpu.VMEM((tm, d), BF16)],
        compiler_params=_params("parallel", "arbitrary"),
        name="in_proj",
    )(x, g, w)


def _qkv_post_kernel(zq_ref, zkv_ref, cos_ref, sin_ref, gq_ref, gk_ref,
                     qtok_ref, qh_ref, rc_ref, rs_ref, rw_ref, ks_ref, vs_ref, kw_ref, vw_ref):
    cos = cos_ref[...]
    sin = sin_ref[...]
    lane = lax.broadcasted_iota(jnp.int32, cos.shape, 1)
    lo = lane < HD
    first = (lane % HD) < (HD // 2)

    def norm_rope(xc, g_row):
        sq = xc * xc
        s_lo = jnp.sum(jnp.where(lo, sq, 0.0), axis=-1, keepdims=True)
        s_hi = jnp.sum(jnp.where(lo, 0.0, sq), axis=-1, keepdims=True)
        ms = jnp.where(lo, s_lo, s_hi) * (1.0 / HD)
        y = (xc * lax.rsqrt(ms + EPS)) * g_row
        partner = jnp.where(first, pltpu.roll(y, LANES - HD // 2, 1), pltpu.roll(y, HD // 2, 1))
        return y * cos + partner * sin

    gq = gq_ref[...]
    for c in range(NSA_W // LANES):
        sl = slice(c * LANES, (c + 1) * LANES)
        qc = norm_rope(zq_ref[:, sl], gq)
        qtok_ref[:, sl] = qc
        qb = qc.astype(BF16)
        qh_ref[2 * c] = qb[:, :HD]
        qh_ref[2 * c + 1] = qb[:, HD:]

    rows = (rc_ref, rs_ref, rw_ref)
    kg = (None, ks_ref, kw_ref)
    vg = (None, vs_ref, vw_ref)
    for br in range(3):
        gk = gk_ref[br:br + 1, :]
        for c in range(KW // LANES):
            sl = slice(c * LANES, (c + 1) * LANES)
            kc = norm_rope(zkv_ref[:, br * KVW + c * LANES:br * KVW + (c + 1) * LANES], gk)
            rows[br][:, sl] = kc
            vc = zkv_ref[:, br * KVW + KW + c * LANES:br * KVW + KW + (c + 1) * LANES]
            rows[br][:, KW + c * LANES:KW + (c + 1) * LANES] = vc
            if kg[br] is not None:
                kb = kc.astype(BF16)
                vb = vc.astype(BF16)
                kg[br][2 * c] = kb[:, :HD]
                kg[br][2 * c + 1] = kb[:, HD:]
                vg[br][2 * c] = vb[:, :HD]
                vg[br][2 * c + 1] = vb[:, HD:]


def _qkv_post(z, cos_tab, sin_tab, gq, gk, n_prompt, seq):
    n = z.shape[0]
    tm = 256
    n_ptiles = n_prompt // tm
    tiles_per_seq = seq // tm

    def tab_map(i):
        return (jnp.where(i < n_ptiles, i % tiles_per_seq, tiles_per_seq), 0)

    row = lambda i: (i, 0)
    head = lambda i: (0, i, 0)
    out_shape = (
        jax.ShapeDtypeStruct((n, NSA_W), F32),
        jax.ShapeDtypeStruct((HEADS, n, HD), BF16),
        jax.ShapeDtypeStruct((n, KVW), F32),
        jax.ShapeDtypeStruct((n, KVW), F32),
        jax.ShapeDtypeStruct((n, KVW), F32),
        jax.ShapeDtypeStruct((KVH, n, HD), BF16),
        jax.ShapeDtypeStruct((KVH, n, HD), BF16),
        jax.ShapeDtypeStruct((KVH, n, HD), BF16),
        jax.ShapeDtypeStruct((KVH, n, HD), BF16),
    )
    out_specs = (
        pl.BlockSpec((tm, NSA_W), row),
        pl.BlockSpec((HEADS, tm, HD), head),
        pl.BlockSpec((tm, KVW), row),
        pl.BlockSpec((tm, KVW), row),
        pl.BlockSpec((tm, KVW), row),
        pl.BlockSpec((KVH, tm, HD), head),
        pl.BlockSpec((KVH, tm, HD), head),
        pl.BlockSpec((KVH, tm, HD), head),
        pl.BlockSpec((KVH, tm, HD), head),
    )
    return pl.pallas_call(
        _qkv_post_kernel,
        grid=(n // tm,),
        in_specs=[
            pl.BlockSpec((tm, NSA_W), lambda i: (i, C_Q // NSA_W)),
            pl.BlockSpec((tm, 3 * KVW), lambda i: (i, C_KV // (3 * KVW))),
            pl.BlockSpec((tm, LANES), tab_map),
            pl.BlockSpec((tm, LANES), tab_map),
            pl.BlockSpec((1, LANES), lambda i: (0, 0)),
            pl.BlockSpec((3, LANES), lambda i: (0, 0)),
        ],
        out_specs=out_specs,
        out_shape=out_shape,
        compiler_params=_params("parallel"),
        name="qkv_post",
    )(z, z, cos_tab, sin_tab, gq, gk)


def _compress_pair(flat, w2, pe8, w2bd, n_ch):
    ab = _dot(flat, w2)
    hp = _dot(pe8.astype(BF16), w2)
    hpe = hp[0:1, :LANES] + hp[1:2, LANES:]
    nxt = pltpu.roll(ab[:, LANES:], n_ch - 1, 0)
    h = ab[:, :LANES] + nxt + hpe
    act = h * jax.nn.sigmoid(h)
    return _dot(act.astype(BF16), w2bd)


def _scatter_chunk_rows(xs_ref, perm, tile, c0):
    rpp = tile.shape[0] // D_CMP
    xp = _dot(perm, tile)
    for t in range(D_CMP):
        xs_ref[t, pl.ds(c0, rpp), :] = xp[t * rpp:(t + 1) * rpp, :]


def _chunk_flat(xs_ref, p):
    return jnp.concatenate([xs_ref[t, :, p * LANES:(p + 1) * LANES] for t in range(D_CMP)], axis=1).astype(BF16)


def _compress_prompt_kernel(rows_ref, w2k_ref, w2v_ref, pek_ref, pev_ref, bdk_ref, bdv_ref, perm_ref, kc_ref, vc_ref,
                            xs_ref, *, n_ch):
    w2 = (w2k_ref, w2v_ref)
    pe = (pek_ref, pev_ref)
    bd = (bdk_ref, bdv_ref)
    outs = (kc_ref, vc_ref)
    tile_rows = perm_ref.shape[0]
    rpp = tile_rows // D_CMP

    def scatter(i, carry):
        r0 = pl.multiple_of(i * tile_rows, tile_rows)
        _scatter_chunk_rows(xs_ref, perm_ref[...], rows_ref[pl.ds(r0, tile_rows), :].astype(BF16),
                            pl.multiple_of(i * rpp, rpp))
        return carry

    lax.fori_loop(0, n_ch // rpp, scatter, 0)
    for p in range(KVW // LANES):
        kv = p // 2
        flat = _chunk_flat(xs_ref, p)
        res = _compress_pair(flat, w2[kv][...], pe[kv][...], bd[kv][...], n_ch).astype(BF16)
        g0 = 2 * (p % 2)
        outs[kv][0, g0] = res[:, :HD]
        outs[kv][0, g0 + 1] = res[:, HD:]


def _compress_prompt(rows_cmp, cw, batch, seq):
    n_ch = seq // D_CMP
    const = lambda b: (0, 0)
    out_sd = jax.ShapeDtypeStruct((batch, KVH, n_ch, HD), BF16)
    out_spec = pl.BlockSpec((1, KVH, n_ch, HD), lambda b: (b, 0, 0, 0))
    return pl.pallas_call(
        functools.partial(_compress_prompt_kernel, n_ch=n_ch),
        grid=(batch,),
        in_specs=[pl.BlockSpec((seq, KVW), lambda b: (b, 0))] + [pl.BlockSpec(w.shape, const) for w in cw],
        out_specs=(out_spec, out_spec),
        out_shape=(out_sd, out_sd),
        scratch_shapes=[pltpu.VMEM((D_CMP, n_ch, KVW), F32)],
        compiler_params=_params("parallel"),
        name="compress_prompt",
    )(rows_cmp, *cw)


def _rank_select(score_t, n_blocks, n_rows, n_top):
    st = score_t[:n_rows]
    jio = lax.broadcasted_iota(jnp.int32, st.shape, 0)
    cnt = jnp.zeros(st.shape, F32)
    for i in range(n_blocks):
        row = score_t[i:i + 1, :]
        beats = (row > st) | ((row == st) & (jio > i))
        cnt = cnt + jnp.where(beats, 1.0, 0.0)
    return jnp.where(cnt < n_top, 1.0, 0.0)


def _pad_rows(x, n):
    if x.shape[0] == n:
        return x
    return jnp.concatenate([x, jnp.zeros((n - x.shape[0], x.shape[1]), x.dtype)], axis=0)


def _cmp_select_kernel(qh_ref, kc_ref, vc_ref, ov_ref, ocmp_ref, sel_ref, *, nc, ns, n_top):
    s0 = pl.program_id(1) * Q_BLK
    ncp = kc_ref.shape[2]
    rows = GRP * Q_BLK
    qpos_r = s0 + lax.broadcasted_iota(jnp.int32, (rows, ncp), 0) % Q_BLK
    cidx = lax.broadcasted_iota(jnp.int32, (rows, ncp), 1)
    cvalid = (cidx * D_CMP + (L_CMP - 1) <= qpos_r) & (cidx < nc)
    qpos = s0 + lax.broadcasted_iota(jnp.int32, (Q_BLK, LANES), 0)
    blk = lax.broadcasted_iota(jnp.int32, (Q_BLK, LANES), 1)
    visible = (blk * L_SEL <= qpos) & (blk < ns)
    cur = qpos // L_SEL
    forced = (blk == 0) | (blk == cur) | (blk == cur - 1)
    nsp = -(-ns // 8) * 8
    ov = ov_ref[...]
    for g in range(KVH):
        qg = jnp.concatenate([qh_ref[GRP * g + r] for r in range(GRP)], axis=0)
        s = _nt(qg, kc_ref[0, g]) * (HD ** -0.5)
        s = jnp.where(cvalid, s, NEG)
        m = jnp.max(s, axis=-1, keepdims=True)
        e = jnp.where(cvalid, jnp.exp(s - m), 0.0)
        p = e / jnp.maximum(jnp.sum(e, axis=-1, keepdims=True), 1.0)
        o = _dot(p.astype(BF16), vc_ref[0, g])
        for r in range(GRP):
            ocmp_ref[GRP * g + r] = o[r * Q_BLK:(r + 1) * Q_BLK]
        psum = p[0:Q_BLK]
        for r in range(1, GRP):
            psum = psum + p[r * Q_BLK:(r + 1) * Q_BLK]
        imp = _dot_exact_lhs(psum, ov)
        score = jnp.where(visible, imp + jnp.where(forced, BIG, 0.0), -BIG)
        sel_t = _rank_select(score.T, ns, nsp, n_top)
        sel = _pad_rows(sel_t, LANES).T
        sel_ref[:, g * LANES:(g + 1) * LANES] = jnp.where(visible, sel, 0.0).astype(BF16)


def _cmp_select(q_heads, kc, vc, ov, batch, seq):
    nqt = seq // Q_BLK
    n_ch = kc.shape[2]
    ns = seq // L_SEL
    kern = functools.partial(_cmp_select_kernel, nc=n_ch - 1, ns=ns, n_top=min(N_TOP, ns))
    kv_spec = pl.BlockSpec((1, KVH, n_ch, HD), lambda b, t: (b, 0, 0, 0))
    return pl.pallas_call(
        kern,
        grid=(batch, nqt),
        in_specs=[
            pl.BlockSpec((HEADS, Q_BLK, HD), lambda b, t: (0, b * nqt + t, 0)),
            kv_spec, kv_spec,
            pl.BlockSpec(ov.shape, lambda b, t: (0, 0)),
        ],
        out_specs=(
            pl.BlockSpec((HEADS, Q_BLK, HD), lambda b, t: (0, b * nqt + t, 0)),
            pl.BlockSpec((Q_BLK, KVH * LANES), lambda b, t: (b * nqt + t, 0)),
        ),
        out_shape=(
            jax.ShapeDtypeStruct((HEADS, batch * seq, HD), F32),
            jax.ShapeDtypeStruct((batch * seq, KVH * LANES), BF16),
        ),
        compiler_params=_params("parallel", "parallel"),
        name="cmp_select",
    )(q_heads, kc, vc, ov)


def _sel_win_kernel(qh_ref, ks_ref, vs_ref, kw_ref, vw_ref, sel_ref, e_ref, zbr_ref, ocmp_ref, oa_ref, *, kt):
    s0 = pl.program_id(1) * Q_BLK
    rows = GRP * Q_BLK
    n_kt = (s0 + Q_BLK + kt - 1) // kt
    qpos = s0 + lax.broadcasted_iota(jnp.int32, (Q_BLK, kt), 0)
    lane_k = lax.broadcasted_iota(jnp.int32, (Q_BLK, kt), 1)
    gates = jax.nn.sigmoid(zbr_ref[...])
    wl = WINDOW + Q_BLK
    wstart = pl.multiple_of(jnp.maximum(s0 - WINDOW, 0), Q_BLK)
    kposw = wstart + lax.broadcasted_iota(jnp.int32, (Q_BLK, wl), 1)
    qposw = s0 + lax.broadcasted_iota(jnp.int32, (Q_BLK, wl), 0)
    wmask = (kposw <= qposw) & (kposw > qposw - WINDOW)

    def masked(mask1, x, fill):
        return jnp.concatenate([jnp.where(mask1, x[r * Q_BLK:(r + 1) * Q_BLK], fill) for r in range(GRP)], axis=0)

    for g in range(KVH):
        qg = jnp.concatenate([qh_ref[GRP * g + r] for r in range(GRP)], axis=0) * (HD ** -0.5)
        selg = sel_ref[:, g * LANES:(g + 1) * LANES]

        def body(j, carry, g=g, qg=qg, selg=selg):
            m, l, acc = carry
            k0 = pl.multiple_of(j * kt, kt)
            k = ks_ref[g, pl.ds(k0, kt), :]
            v = vs_ref[g, pl.ds(k0, kt), :]
            mf = _dot(selg, e_ref[j])
            mask1 = (mf > 0.5) & (k0 + lane_k <= qpos)
            s = masked(mask1, _nt(qg, k), NEG)
            m_new = jnp.maximum(m, jnp.max(s, axis=-1, keepdims=True))
            corr = jnp.exp(m - m_new)
            p = masked(mask1, jnp.exp(s - m_new), 0.0)
            l = l * corr + jnp.sum(p, axis=-1, keepdims=True)
            acc = acc * corr + _dot(p.astype(BF16), v)
            return m_new, l, acc

        init = (jnp.full((rows, 1), NEG, F32), jnp.zeros((rows, 1), F32), jnp.zeros((rows, HD), F32))
        _, l, acc = lax.fori_loop(0, n_kt, body, init)
        o_sel = acc / l

        kwin = kw_ref[g, pl.ds(wstart, wl), :]
        vwin = vw_ref[g, pl.ds(wstart, wl), :]
        s = masked(wmask, _nt(qg, kwin), NEG)
        m = jnp.max(s, axis=-1, keepdims=True)
        e = masked(wmask, jnp.exp(s - m), 0.0)
        den = jnp.maximum(jnp.sum(e, axis=-1, keepdims=True), 1.0)
        o_win = _dot(e.astype(BF16), vwin) / den

        for r in range(GRP):
            h = GRP * g + r
            rs = slice(r * Q_BLK, (r + 1) * Q_BLK)
            o = (gates[:, h:h + 1] * ocmp_ref[h]
                 + gates[:, HEADS + h:HEADS + h + 1] * o_sel[rs]
                 + gates[:, 2 * HEADS + h:2 * HEADS + h + 1] * o_win[rs])
            oa_ref[:, h * HD:(h + 1) * HD] = o


def _sel_win(q_heads, ks, vs, kw, vw, sel, e_mat, z, ocmp, batch, seq):
    nqt = seq // Q_BLK
    kt = 512
    e_mat = e_mat.reshape(LANES, seq // kt, kt).transpose(1, 0, 2)
    kv_spec = pl.BlockSpec((KVH, seq, HD), lambda b, t: (0, b, 0), pipeline_mode=pl.Buffered(1))
    qspec = pl.BlockSpec((HEADS, Q_BLK, HD), lambda b, t: (0, b * nqt + t, 0))
    return pl.pallas_call(
        functools.partial(_sel_win_kernel, kt=kt),
        grid=(batch, nqt),
        in_specs=[
            qspec, kv_spec, kv_spec, kv_spec, kv_spec,
            pl.BlockSpec((Q_BLK, KVH * LANES), lambda b, t: (b * nqt + t, 0)),
            pl.BlockSpec(e_mat.shape, lambda b, t: (0, 0, 0)),
            pl.BlockSpec((Q_BLK, LANES), lambda b, t: (b * nqt + t, C_BR // LANES)),
            qspec,
        ],
        out_specs=pl.BlockSpec((Q_BLK, NSA_W), lambda b, t: (b * nqt + t, 0)),
        out_shape=jax.ShapeDtypeStruct((batch * seq, NSA_W), F32),
        compiler_params=_params("parallel", "arbitrary"),
        name="sel_win",
    )(q_heads, ks, vs, kw, vw, sel, e_mat, z, ocmp)


def _hgrn_gates(zq, zf, logits):
    mx = jnp.max(logits, axis=0, keepdims=True)
    ex = jnp.exp(logits - mx)
    lb = ex[0:1] / jnp.sum(ex, axis=0, keepdims=True)
    hq = zq * jax.nn.sigmoid(zq)
    log_sig = -(jnp.maximum(-zf, 0.0) + jnp.log1p(jnp.exp(-jnp.abs(zf))))
    a = jnp.log(lb)
    b = jnp.log1p(-lb) + log_sig
    log_f = jnp.maximum(a, b) + jnp.log1p(jnp.exp(-jnp.abs(a - b)))
    hk = (1.0 - lb) * jax.nn.sigmoid(-zf)
    return hq, hk, log_f


def _intra_chunk(ac, qc, kc, vc):
    n = ac.shape[0]
    tio = lax.broadcasted_iota(jnp.int32, ac.shape, 0)
    acc = jnp.zeros(vc.shape, F32)
    for s in range(n):
        d = jnp.exp(ac - ac[s:s + 1, :])
        w = jnp.where(tio >= s, qc * kc[s:s + 1, :] * d, 0.0)
        col = jnp.sum(w, axis=-1, keepdims=True)
        acc = acc + col * vc[s:s + 1, :]
    return acc


def _hgrn_prompt_kernel(zq_ref, zf_ref, zi_ref, lg_ref, tri_ref, full_ref, o_ref, sout_ref,
                        st_ref, a_s, q_s, k_s, qt_s, ktb_s, eal_s, vt_s):
    t = pl.program_id(1)
    tt = zq_ref.shape[0]

    @pl.when(t == 0)
    def _():
        st_ref[...] = jnp.zeros(st_ref.shape, F32)

    hq, hk, log_f = _hgrn_gates(zq_ref[...], zf_ref[...], lg_ref[...])
    a = _dot_exact_rhs(tri_ref[...], log_f)
    al = _dot_exact_rhs(full_ref[...], log_f)
    a_s[...] = a
    q_s[...] = hq
    k_s[...] = hk
    qt_s[...] = hq * jnp.exp(a)
    ktb_s[...] = (hk * jnp.exp(al - a)).astype(BF16)
    eal_s[...] = jnp.exp(al)
    for h in range(HG_H):
        vt_s[h] = zi_ref[:, h * HG_DV:(h + 1) * HG_DV].T.astype(BF16)
    lane_tok = lax.broadcasted_iota(jnp.int32, (HG_DV, tt), 1)

    def chunk(c, carry):
        r0 = pl.multiple_of(c * HG_SUB, HG_SUB)
        in_chunk = (lane_tok >= r0) & (lane_tok < r0 + HG_SUB)
        for h in range(HG_H):
            sl = slice(h * HG_DK, (h + 1) * HG_DK)
            st = st_ref[h]
            o = _nt(qt_s[pl.ds(r0, HG_SUB), sl].astype(BF16), st.astype(BF16))
            o = o + _intra_chunk(a_s[pl.ds(r0, HG_SUB), sl], q_s[pl.ds(r0, HG_SUB), sl],
                                 k_s[pl.ds(r0, HG_SUB), sl], zi_ref[pl.ds(r0, HG_SUB), sl])
            o_ref[pl.ds(r0, HG_SUB), sl] = o
            vtm = jnp.where(in_chunk, vt_s[h], jnp.zeros_like(vt_s[h]))
            st_ref[h] = st * eal_s[pl.ds(r0, 1), sl] + _dot(vtm, ktb_s[:, sl])
        return carry

    lax.fori_loop(0, tt // HG_SUB, chunk, 0)

    @pl.when(t == pl.num_programs(1) - 1)
    def _():
        for h in range(HG_H):
            sout_ref[0, h] = st_ref[h].T


def _hgrn_prompt(z, logits, batch, seq):
    tt = 128
    nt = seq // tt
    sub = np.arange(tt) // HG_SUB
    same = sub[:, None] == sub[None, :]
    tri = jnp.asarray((same & (np.arange(tt)[None, :] <= np.arange(tt)[:, None])).astype(np.float32), BF16)
    full = jnp.asarray(same.astype(np.float32), BF16)
    col = lambda c: (lambda b, t: (b * nt + t, c // HG_W))
    const = lambda b, t: (0, 0)
    return pl.pallas_call(
        _hgrn_prompt_kernel,
        grid=(batch, nt),
        in_specs=[
            pl.BlockSpec((tt, HG_W), col(C_HQ)),
            pl.BlockSpec((tt, HG_W), col(C_HF)),
            pl.BlockSpec((tt, HG_W), col(C_HI)),
            pl.BlockSpec((2, HG_W), const),
            pl.BlockSpec((tt, tt), const),
            pl.BlockSpec((tt, tt), const),
        ],
        out_specs=(
            pl.BlockSpec((tt, HG_W), lambda b, t: (b * nt + t, 0)),
            pl.BlockSpec((1, HG_H, HG_DK, HG_DV), lambda b, t: (b, 0, 0, 0)),
        ),
        out_shape=(
            jax.ShapeDtypeStruct((batch * seq, HG_W), F32),
            jax.ShapeDtypeStruct((batch, HG_H, HG_DK, HG_DV), F32),
        ),
        scratch_shapes=[
            pltpu.VMEM((HG_H, HG_DV, HG_DK), F32),
            pltpu.VMEM((tt, HG_W), F32),
            pltpu.VMEM((tt, HG_W), F32),
            pltpu.VMEM((tt, HG_W), F32),
            pltpu.VMEM((tt, HG_W), F32),
            pltpu.VMEM((tt, HG_W), BF16),
            pltpu.VMEM((tt, HG_W), F32),
            pltpu.VMEM((HG_H, HG_DV, tt), BF16),
        ],
        compiler_params=_params("parallel", "arbitrary"),
        name="hgrn_prompt",
    )(z, z, z, logits, tri, full)


def _hgrn_sample_kernel(zq_ref, zf_ref, zi_ref, lg_ref, s0_ref, o_ref, sout_ref):
    s_len = zq_ref.shape[0]
    hq, hk, log_f = _hgrn_gates(zq_ref[...], zf_ref[...], lg_ref[...])
    rio = lax.broadcasted_iota(jnp.int32, log_f.shape, 0)
    a = log_f
    d = 1
    while d < s_len:
        a = a + jnp.where(rio >= d, pltpu.roll(a, d, 0), 0.0)
        d *= 2
    a_last = jnp.sum(log_f, axis=0, keepdims=True)
    qt = hq * jnp.exp(a)
    kt = hk * jnp.exp(a_last - a)
    ea = jnp.exp(a_last)
    hv = zi_ref[...]
    row0 = lax.broadcasted_iota(jnp.int32, (s_len, HG_DK), 0) == 0
    pad = HG_DK - 2 * s_len
    for h in range(HG_H):
        sl = slice(h * HG_DK, (h + 1) * HG_DK)
        s_old = s0_ref[0, h]
        qt16 = _pad_rows(qt[:, sl], 16).astype(BF16)
        o = _dot(qt16, s_old.astype(BF16))[:s_len]
        o = o + _intra_chunk(a[:, sl], hq[:, sl], hk[:, sl], hv[:, sl])
        o_ref[:, sl] = o
        m = jnp.concatenate([kt[:, sl], jnp.where(row0, jnp.broadcast_to(ea[:, sl], (s_len, HG_DK)), 0.0),
                             jnp.zeros((pad, HG_DK), F32)], axis=0)
        mt = m.T
        e_col = mt[:, s_len:s_len + 1]
        vpad = _pad_rows(hv[:, sl], HG_DK).astype(BF16)
        sout_ref[0, h] = e_col * s_old + _dot(mt.astype(BF16), vpad)


def _hgrn_sample(z, logits, state, n_prompt, dec_batch, s_len):
    base = n_prompt // s_len
    col = lambda c: (lambda b: (base + b, c // HG_W))
    st_spec = pl.BlockSpec((1, HG_H, HG_DK, HG_DV), lambda b: (b, 0, 0, 0))
    return pl.pallas_call(
        _hgrn_sample_kernel,
        grid=(dec_batch,),
        in_specs=[
            pl.BlockSpec((s_len, HG_W), col(C_HQ)),
            pl.BlockSpec((s_len, HG_W), col(C_HF)),
            pl.BlockSpec((s_len, HG_W), col(C_HI)),
            pl.BlockSpec((2, HG_W), lambda b: (0, 0)),
            st_spec,
        ],
        out_specs=(pl.BlockSpec((s_len, HG_W), lambda b: (b, 0)), st_spec),
        out_shape=(
            jax.ShapeDtypeStruct((dec_batch * s_len, HG_W), F32),
            jax.ShapeDtypeStruct(state.shape, F32),
        ),
        compiler_params=_params("parallel"),
        name="hgrn_sample",
    )(z, z, z, logits, state)


def _sample_nsa_kernel(pt_ref, *refs, npg, s_len, past):
    del pt_ref
    cmp_pages = refs[:npg]
    sel_pages = refs[npg:2 * npg]
    (win_ref, q_ref, rsel_ref, rwin_ref, zbr_ref, w2k_ref, w2v_ref, pek_ref, pev_ref, bdk_ref, bdv_ref, perm_ref,
     ov_ref, e_ref, oa_ref, swin_ref, xs_ref) = refs[2 * npg:]
    page = cmp_pages[0].shape[0]
    rpp = page // D_CMP
    n_ch = npg * rpp
    nc = n_ch - 1
    wb = win_ref.shape[0]
    ns = -(-(past + s_len) // L_SEL)
    nsp = -(-ns // 8) * 8
    n_top = min(N_TOP, ns)
    rows = HEADS * s_len

    w2 = (w2k_ref, w2v_ref)
    pe = (pek_ref, pev_ref)
    bd = (bdk_ref, bdv_ref)
    comp = []
    for i, pg in enumerate(cmp_pages):
        _scatter_chunk_rows(xs_ref, perm_ref[...], pg[...].astype(BF16), i * rpp)
    for p in range(KVW // LANES):
        kv = p // 2
        comp.append(_compress_pair(_chunk_flat(xs_ref, p), w2[kv][...], pe[kv][...], bd[kv][...], n_ch))
    kc_all = jnp.concatenate(comp[0:2], axis=1).astype(BF16)
    vc_all = jnp.concatenate(comp[2:4], axis=1).astype(BF16)

    q = q_ref[...]
    lane8 = lax.broadcasted_iota(jnp.int32, (s_len, LANES), 1)
    zero8 = jnp.zeros((s_len, LANES), F32)
    blocks = []
    for g in range(KVH):
        for r in range(GRP):
            h = GRP * g + r
            src = q[:, (h // 2) * LANES:(h // 2 + 1) * LANES]
            if (r % 2) != (g % 2):
                src = pltpu.roll(src, HD, 1)
            keep = (lane8 < HD) if g % 2 == 0 else (lane8 >= HD)
            piece = jnp.where(keep, src, 0.0)
            blocks.append(jnp.concatenate([piece, zero8] if g // 2 == 0 else [zero8, piece], axis=1))
    qbd = (jnp.concatenate(blocks, axis=0) * (HD ** -0.5)).astype(BF16)

    def rowpos(shape):
        return past + lax.broadcasted_iota(jnp.int32, shape, 0) % s_len

    s = _nt(qbd, kc_all)
    cidx = lax.broadcasted_iota(jnp.int32, s.shape, 1)
    cvalid = (cidx * D_CMP + (L_CMP - 1) <= rowpos(s.shape)) & (cidx < nc)
    s = jnp.where(cvalid, s, NEG)
    m = jnp.max(s, axis=-1, keepdims=True)
    e = jnp.where(cvalid, jnp.exp(s - m), 0.0)
    p = e / jnp.maximum(jnp.sum(e, axis=-1, keepdims=True), 1.0)
    ocmp = _dot(p.astype(BF16), vc_all)

    gs = GRP * s_len
    psum = []
    for g in range(KVH):
        acc = p[g * gs:g * gs + s_len]
        for r in range(1, GRP):
            acc = acc + p[g * gs + r * s_len:g * gs + (r + 1) * s_len]
        psum.append(acc)
    imp = _dot_exact_lhs(jnp.concatenate(psum, axis=0), ov_ref[...])
    qpos = rowpos(imp.shape)
    blk = lax.broadcasted_iota(jnp.int32, imp.shape, 1)
    visible = (blk * L_SEL <= qpos) & (blk < ns)
    cur = qpos // L_SEL
    forced = (blk == 0) | (blk == cur) | (blk == cur - 1)
    score = jnp.where(visible, imp + jnp.where(forced, BIG, 0.0), -BIG)
    sel_t = _rank_select(_pad_rows(score, LANES).T, ns, nsp, n_top)
    sel = _pad_rows(sel_t, LANES).T[:KVH * s_len]
    sel = jnp.where(visible, sel, 0.0)
    sel_rows = jnp.concatenate([sel[g * s_len:(g + 1) * s_len] for g in range(KVH) for _ in range(GRP)], axis=0)

    zpad = jnp.zeros((page - s_len, KW), F32)
    knew = jnp.concatenate([rsel_ref[:, :KW], zpad], axis=0).astype(BF16)
    vnew = jnp.concatenate([rsel_ref[:, KW:], zpad], axis=0).astype(BF16)
    s = jnp.concatenate([_nt(qbd, pg[:, :KW].astype(BF16)) for pg in sel_pages] + [_nt(qbd, knew)], axis=1)
    mf = _dot(sel_rows.astype(BF16), e_ref[...])
    kpos = lax.broadcasted_iota(jnp.int32, s.shape, 1)
    mask = (mf > 0.5) & (kpos <= rowpos(s.shape))
    s = jnp.where(mask, s, NEG)
    m = jnp.max(s, axis=-1, keepdims=True)
    e = jnp.where(mask, jnp.exp(s - m), 0.0)
    l = jnp.sum(e, axis=-1, keepdims=True)
    eb = e.astype(BF16)
    acc = _dot(eb[:, npg * page:], vnew)
    for i in range(npg):
        acc = acc + _dot(eb[:, i * page:(i + 1) * page], sel_pages[i][:, KW:].astype(BF16))
    osel = acc / l

    kwn = jnp.concatenate([rwin_ref[:, :KW], zpad], axis=0).astype(BF16)
    vwn = jnp.concatenate([rwin_ref[:, KW:], zpad], axis=0).astype(BF16)
    s = jnp.concatenate([_nt(qbd, win_ref[:, :KW].astype(BF16)), _nt(qbd, kwn)], axis=1)
    kposw = (past - wb) + lax.broadcasted_iota(jnp.int32, s.shape, 1)
    qposw = rowpos(s.shape)
    wmask = (kposw <= qposw) & (kposw > qposw - WINDOW) & (kposw >= 0)
    s = jnp.where(wmask, s, NEG)
    m = jnp.max(s, axis=-1, keepdims=True)
    e = jnp.where(wmask, jnp.exp(s - m), 0.0)
    den = jnp.maximum(jnp.sum(e, axis=-1, keepdims=True), 1.0)
    eb = e.astype(BF16)
    owin = (_dot(eb[:, :wb], win_ref[:, KW:].astype(BF16)) + _dot(eb[:, wb:], vwn)) / den

    gates = jax.nn.sigmoid(zbr_ref[...])
    for g in range(KVH):
        gl = slice((g // 2) * LANES, (g // 2 + 1) * LANES)
        for k2 in range(2):
            halves = []
            for r in (2 * k2, 2 * k2 + 1):
                h = GRP * g + r
                rs = slice(h * s_len, (h + 1) * s_len)
                val = (gates[:, h:h + 1] * ocmp[rs, gl]
                       + gates[:, HEADS + h:HEADS + h + 1] * osel[rs, gl]
                       + gates[:, 2 * HEADS + h:2 * HEADS + h + 1] * owin[rs, gl])
                if (r % 2) != (g % 2):
                    val = pltpu.roll(val, HD, 1)
                halves.append(val)
            c = 2 * g + k2
            oa_ref[:, c * LANES:(c + 1) * LANES] = jnp.where(lane8 < HD, halves[0], halves[1])

    swin_ref[0:wb - s_len] = win_ref[s_len:wb]
    swin_ref[wb - s_len:wb] = rwin_ref[...]


def _sample_nsa(page_table, cache_cmp, cache_sel, cache_win, q_tok, rows_sel, rows_win, z, cw, ov, e_mat,
                n_prompt, dec_batch, s_len):
    npg = page_table.shape[1]
    page = cache_cmp.shape[1]
    past = npg * page
    wb = cache_win.shape[1]
    base = n_prompt // s_len
    page_specs = [pl.BlockSpec((None, page, KVW), (lambda b, pt, i=i: (pt[b, i], 0, 0))) for i in range(npg)]
    row = lambda b, pt: (base + b, 0)
    const = lambda b, pt: (0, 0)
    in_specs = (
        page_specs + page_specs
        + [pl.BlockSpec((None, wb, KVW), lambda b, pt: (b, 0, 0)),
           pl.BlockSpec((s_len, NSA_W), row),
           pl.BlockSpec((s_len, KVW), row),
           pl.BlockSpec((s_len, KVW), row),
           pl.BlockSpec((s_len, LANES), lambda b, pt: (base + b, C_BR // LANES))]
        + [pl.BlockSpec(w.shape, const) for w in cw]
        + [pl.BlockSpec(ov.shape, const), pl.BlockSpec(e_mat.shape, const)]
    )
    grid_spec = pltpu.PrefetchScalarGridSpec(
        num_scalar_prefetch=1,
        grid=(dec_batch,),
        in_specs=in_specs,
        out_specs=(
            pl.BlockSpec((s_len, NSA_W), lambda b, pt: (b, 0)),
            pl.BlockSpec((None, wb, KVW), lambda b, pt: (b, 0, 0)),
        ),
        scratch_shapes=[pltpu.VMEM((D_CMP, past // D_CMP, KVW), F32)],
    )
    return pl.pallas_call(
        functools.partial(_sample_nsa_kernel, npg=npg, s_len=s_len, past=past),
        grid_spec=grid_spec,
        out_shape=(
            jax.ShapeDtypeStruct((dec_batch * s_len, NSA_W), F32),
            jax.ShapeDtypeStruct((dec_batch, wb, KVW), F32),
        ),
        compiler_params=_params("parallel"),
        name="sample_nsa",
    )(page_table, *([cache_cmp] * npg), *([cache_sel] * npg), cache_win, q_tok, rows_sel, rows_win, z, *cw, ov, e_mat)


def _mix_kernel(x_ref, oa_ref, ob_ref, zhg_ref, ga_ref, gb_ref, gn_ref, wa_ref, wb_ref, wo_ref, o_ref):
    ya = _dot(oa_ref[...].astype(BF16), wa_ref[...])
    gn = gn_ref[...]
    parts = []
    for h in range(HG_H):
        sl = slice(h * HG_DV, (h + 1) * HG_DV)
        ob = ob_ref[:, sl]
        ms = jnp.mean(ob * ob, axis=-1, keepdims=True)
        zg = zhg_ref[:, sl]
        y = ((ob * lax.rsqrt(ms + EPS)) * gn) * (zg * jax.nn.sigmoid(zg))
        parts.append(y.astype(BF16))
    yb = _dot(jnp.concatenate(parts, axis=1), wb_ref[...])
    mix = jax.nn.sigmoid(ga_ref[...]) * ya + jax.nn.sigmoid(gb_ref[...]) * yb
    o_ref[...] = x_ref[...] + _dot(mix.astype(BF16), wo_ref[...])


def _mix(x, oa, ob, z, gn, wa, wb, wo):
    n, d = x.shape
    tm = 256
    row = lambda i: (i, 0)
    const = lambda i: (0, 0)
    resident = lambda w: pl.BlockSpec(w.shape, const, pipeline_mode=pl.Buffered(1))
    return pl.pallas_call(
        _mix_kernel,
        grid=(n // tm,),
        in_specs=[
            pl.BlockSpec((tm, d), row),
            pl.BlockSpec((tm, NSA_W), row),
            pl.BlockSpec((tm, HG_W), row),
            pl.BlockSpec((tm, HG_W), lambda i: (i, C_HG // HG_W)),
            pl.BlockSpec((tm, d), lambda i: (i, C_GA // D_MODEL)),
            pl.BlockSpec((tm, d), lambda i: (i, C_GB // D_MODEL)),
            pl.BlockSpec((1, HG_DV), const),
            resident(wa), resident(wb), resident(wo),
        ],
        out_specs=pl.BlockSpec((tm, d), row),
        out_shape=jax.ShapeDtypeStruct((n, d), F32),
        compiler_params=_params("parallel"),
        name="mix_out",
    )(x, oa, ob, z, z, z, gn, wa, wb, wo)


def _ffn_kernel(x_ref, g_ref, w1_ref, w2_ref, o_ref, xn_ref, acc_ref):
    j = pl.program_id(1)

    @pl.when(j == 0)
    def _():
        x = x_ref[...]
        ms = jnp.mean(x * x, axis=-1, keepdims=True)
        xn_ref[...] = ((x * lax.rsqrt(ms + EPS)) * g_ref[...]).astype(BF16)
        acc_ref[...] = jnp.zeros(acc_ref.shape, F32)

    h = _dot(xn_ref[...], w1_ref[...])
    h = jnp.square(jnp.maximum(h, 0.0)).astype(BF16)
    acc_ref[...] += _dot(h, w2_ref[...])

    @pl.when(j == pl.num_programs(1) - 1)
    def _():
        o_ref[...] = x_ref[...] + acc_ref[...]


def _ffn(x, g, w1, w2):
    n, d = x.shape
    dff = w1.shape[1]
    tm = _pick_tile(n, (512, 256))
    tf = 512
    return pl.pallas_call(
        _ffn_kernel,
        grid=(n // tm, dff // tf),
        in_specs=[
            pl.BlockSpec((tm, d), lambda i, j: (i, 0)),
            pl.BlockSpec((1, d), lambda i, j: (0, 0)),
            pl.BlockSpec((d, tf), lambda i, j: (0, j)),
            pl.BlockSpec((tf, d), lambda i, j: (j, 0)),
        ],
        out_specs=pl.BlockSpec((tm, d), lambda i, j: (i, 0)),
        out_shape=jax.ShapeDtypeStruct((n, d), F32),
        scratch_shapes=[pltpu.VMEM((tm, d), BF16), pltpu.VMEM((tm, d), F32)],
        compiler_params=_params("parallel", "arbitrary"),
        name="ffn",
    )(x, g, w1, w2)


def _reorder_w_in(w_in):
    offs = np.concatenate([[0], np.cumsum(_SIZES)])
    seg = [w_in[:, int(offs[i]):int(offs[i + 1])] for i in range(len(_SIZES))]
    zq, zkv, zbr, zhq, zhf, zhi, zhg, ga, gb = seg
    pad = jnp.zeros((w_in.shape[0], N_Z - C_BR - zbr.shape[1]), w_in.dtype)
    return jnp.concatenate([zq, zhq, zhf, zhi, ga, gb, zhg, zkv, zbr, pad], axis=1).astype(BF16)


def _compress_weights(cmp_pe, cmp_w1, cmp_w2):
    m = L_CMP // D_CMP
    eye2 = jnp.eye(2, dtype=F32)
    out = []
    w2s, pes, bds = [], [], []
    for kv in range(2):
        w1p = cmp_w1[kv].reshape(m, D_CMP, HD, HD).transpose(1, 2, 0, 3)
        w = w1p[:, None, :, :, None, :] * eye2[None, :, None, None, :, None]
        w2s.append(w.reshape(D_CMP * LANES, m * LANES).astype(BF16))
        pe = cmp_pe[kv].reshape(m, D_CMP, 1, HD)
        pe = jnp.broadcast_to(pe, (m, D_CMP, 2, HD)).reshape(m, D_CMP * LANES)
        pes.append(jnp.concatenate([pe, jnp.zeros((8 - m, D_CMP * LANES), F32)], axis=0))
        bds.append(jnp.kron(eye2, cmp_w2[kv]).astype(BF16))
    tile_rows = LANES
    rpp = tile_rows // D_CMP
    src = np.arange(tile_rows)
    perm = np.zeros((tile_rows, tile_rows), np.float32)
    perm[(src % D_CMP) * rpp + src // D_CMP, src] = 1.0
    return (w2s[0], w2s[1], pes[0], pes[1], bds[0], bds[1], jnp.asarray(perm, BF16))


def _overlap_matrix(nc_pad, ns):
    cs = np.arange(nc_pad) * D_CMP
    ss = np.arange(LANES) * L_SEL
    ov = np.clip(np.minimum(cs[:, None] + L_CMP, ss[None, :] + L_SEL) - np.maximum(cs[:, None], ss[None, :]), 0, None)
    ov = (ov / D_CMP) * (np.arange(LANES)[None, :] < ns)
    return jnp.asarray(ov.astype(np.float32), BF16)


def _expand_matrix(n_keys):
    e = (np.arange(n_keys)[None, :] // L_SEL) == np.arange(LANES)[:, None]
    return jnp.asarray(e.astype(np.float32), BF16)


def _rope_tables(seq, past, s_len, tm):
    half = HD // 2
    inv = THETA ** (-jnp.arange(half, dtype=F32) * 2.0 / HD)
    pos = jnp.concatenate([jnp.arange(seq), past + jnp.arange(tm) % s_len]).astype(F32)
    ang = pos[:, None] * inv[None, :]
    cos = jnp.cos(ang)
    sin = jnp.sin(ang)
    cos_t = jnp.concatenate([cos, cos, cos, cos], axis=1)
    sin_t = jnp.concatenate([-sin, sin, -sin, sin], axis=1)
    return cos_t, sin_t


def kernel(x_prompt, x_sample, cache_cmp_kv, cache_sel_kv, cache_win_kv, state_hgrn, page_table, norm1_g, w_in,
           q_norm_g, k_norm_g, cmp_pe, cmp_w1, cmp_w2, w_proj_a, hgrn_lb_logits, hgrn_norm_g, w_proj_b, w_out,
           norm2_g, w_ff1, w_ff2):
    assert w_in.shape[0] == 1, "single-layer step"
    batch, seq, d = x_prompt.shape
    dec_batch, s_len, _ = x_sample.shape
    n_prompt = batch * seq
    n_sample = dec_batch * s_len
    npg = page_table.shape[1]
    page = cache_cmp_kv.shape[2]
    past = npg * page
    wb = cache_win_kv.shape[2]
    assert seq % Q_BLK == 0 and seq >= WINDOW + Q_BLK and n_prompt % 256 == 0 and n_sample % 256 == 0
    assert s_len % 8 == 0 and page % D_CMP == 0 and 2 * s_len <= HG_DK

    x = jnp.concatenate([x_prompt.reshape(n_prompt, d), x_sample.reshape(n_sample, d)], axis=0)

    w_z = _reorder_w_in(w_in[0])
    cw = _compress_weights(cmp_pe[0], cmp_w1[0], cmp_w2[0])
    cos_t, sin_t = _rope_tables(seq, past, s_len, 256)
    gq = jnp.tile(q_norm_g[0][None, :], (1, 2))
    gk = jnp.tile(k_norm_g[0], (1, 2))
    logits = hgrn_lb_logits.astype(F32)

    z = _in_proj(x, norm1_g, w_z)
    q_tok, q_heads, rows_cmp, rows_sel, rows_win, ks, vs, kw, vw = _qkv_post(z, cos_t, sin_t, gq, gk, n_prompt, seq)

    kc, vc = _compress_prompt(rows_cmp, cw, batch, seq)
    ov_p = _overlap_matrix(seq // D_CMP, seq // L_SEL)
    ocmp, sel = _cmp_select(q_heads, kc, vc, ov_p, batch, seq)
    oa_p = _sel_win(q_heads, ks, vs, kw, vw, sel, _expand_matrix(seq), z, ocmp, batch, seq)
    ob_p, p_state = _hgrn_prompt(z, logits, batch, seq)

    ns_s = -(-(past + s_len) // L_SEL)
    ov_s = _overlap_matrix(past // D_CMP, ns_s)
    oa_s, s_win = _sample_nsa(
        page_table, cache_cmp_kv[0].reshape(-1, page, KVW), cache_sel_kv[0].reshape(-1, page, KVW),
        cache_win_kv[0].reshape(dec_batch, wb, KVW), q_tok, rows_sel, rows_win, z, cw, ov_s,
        _expand_matrix(past + page), n_prompt, dec_batch, s_len)
    ob_s, s_state = _hgrn_sample(z, logits, state_hgrn[0], n_prompt, dec_batch, s_len)

    oa = jnp.concatenate([oa_p, oa_s], axis=0)
    ob = jnp.concatenate([ob_p, ob_s], axis=0)
    x2 = _mix(x, oa, ob, z, hgrn_norm_g, w_proj_a[0].astype(BF16), w_proj_b[0].astype(BF16), w_out[0].astype(BF16))
    y = _ffn(x2, norm2_g, w_ff1[0].astype(BF16), w_ff2[0].astype(BF16))

    kv_shape = (2, KVH, HD)
    wp = min(WINDOW, seq)
    return (
        y[:n_prompt].reshape(batch, seq, d),
        y[n_prompt:].reshape(dec_batch, s_len, d),
        rows_cmp[:n_prompt].reshape((1, batch, seq) + kv_shape),
        rows_sel[:n_prompt].reshape((1, batch, seq) + kv_shape),
        rows_win[:n_prompt].reshape((batch, seq) + kv_shape)[None, :, seq - wp:],
        p_state[None],
        rows_cmp[n_prompt:].reshape((1, dec_batch, s_len) + kv_shape),
        rows_sel[n_prompt:].reshape((1, dec_batch, s_len) + kv_shape),
        s_win.reshape((1, dec_batch, wb) + kv_shape),
        s_state[None],
    )
```

```python
import functools

import numpy as np
import jax
import jax.numpy as jnp
from jax import lax
from jax.experimental import pallas as pl
from jax.experimental.pallas import tpu as pltpu

F32 = jnp.float32
BF16 = jnp.bfloat16

D_MODEL = 2048
HEADS = 16
KVH = 4
GRP = HEADS // KVH
HD = 64
NSA_W = HEADS * HD
L_CMP = 32
D_CMP = 16
L_SEL = 64
N_TOP = 16
WINDOW = 512
Q_BLK = 128
THETA = 10000.0
HG_H = 8
HG_DK = 128
HG_DV = 128
HG_W = HG_H * HG_DK
HG_SUB = 16
EPS = 1e-6
NEG = -1e30
BIG = 1e9

LANES = 128
KVW = 2 * KVH * HD
KW = KVH * HD

_SIZES = (NSA_W, 6 * KVH * HD, 3 * HEADS, HG_W, HG_W, HG_W, HG_W, D_MODEL, D_MODEL)
C1_Q, C1_KV, C1_BR, N_Z1 = 0, 1024, 2560, 2688
C2_HQ, C2_HF, C2_HI, C2_HG, C2_GA, C2_GB, N_Z2 = 0, 1024, 2048, 3072, 4096, 6144, 8192

VMEM_LIMIT = 56 * 1024 * 1024


def _params(*sem):
    return pltpu.CompilerParams(dimension_semantics=sem, vmem_limit_bytes=VMEM_LIMIT)


def _nt(a, b):
    return lax.dot_general(a, b, (((1,), (1,)), ((), ())), preferred_element_type=F32)


def _dot(a, b):
    return jnp.dot(a, b, preferred_element_type=F32)


def _split3(x):
    hi = x.astype(BF16)
    r1 = x - hi.astype(F32)
    mid = r1.astype(BF16)
    lo = (r1 - mid.astype(F32)).astype(BF16)
    return hi, mid, lo


def _dot_exact_rhs(m, x):
    hi, mid, lo = _split3(x)
    return _dot(m, hi) + _dot(m, mid) + _dot(m, lo)


def _dot_exact_lhs(x, m):
    hi, mid, lo = _split3(x)
    return _dot(hi, m) + _dot(mid, m) + _dot(lo, m)


def _pick_tile(n, cands):
    for c in cands:
        if n % c == 0:
            return c
    raise ValueError(f"no tile in {cands} divides {n}")


def _rms_matmul_kernel(x_ref, g_ref, w_ref, o_ref, xn_ref):
    @pl.when(pl.program_id(1) == 0)
    def _():
        x = x_ref[...]
        ms = jnp.mean(x * x, axis=-1, keepdims=True)
        xn_ref[...] = ((x * lax.rsqrt(ms + EPS)) * g_ref[...]).astype(BF16)

    o_ref[...] = _dot(xn_ref[...], w_ref[...])


def _in_proj(x, g, w, tn):
    n, d = x.shape
    nout = w.shape[1]
    tm = _pick_tile(n, (1024, 512, 256))
    return pl.pallas_call(
        _rms_matmul_kernel,
        grid=(n // tm, nout // tn),
        in_specs=[
            pl.BlockSpec((tm, d), lambda i, j: (i, 0)),
            pl.BlockSpec((1, d), lambda i, j: (0, 0)),
            pl.BlockSpec((d, tn), lambda i, j: (0, j)),
        ],
        out_specs=pl.BlockSpec((tm, tn), lambda i, j: (i, j)),
        out_shape=jax.ShapeDtypeStruct((n, nout), F32),
        scratch_shapes=[pltpu.VMEM((tm, d), BF16)],
        compiler_params=_params("parallel", "arbitrary"),
        name="in_proj",
    )(x, g, w)


def _qkv_post_kernel(zq_ref, zc_ref, zs_ref, zw_ref, cos_ref, sin_ref, gq_ref, gk_ref,
                     qtok_ref, qh_ref, rc_ref, rs_ref, rw_ref, tc_ref, ts_ref, tw_ref,
                     ks_ref, vs_ref, kw_ref, vw_ref):
    cos = cos_ref[...]
    sin = sin_ref[...]
    lane = lax.broadcasted_iota(jnp.int32, cos.shape, 1)
    lo = lane < HD
    first = (lane % HD) < (HD // 2)

    def norm_rope(xc, g_row):
        sq = xc * xc
        s_lo = jnp.sum(jnp.where(lo, sq, 0.0), axis=-1, keepdims=True)
        s_hi = jnp.sum(jnp.where(lo, 0.0, sq), axis=-1, keepdims=True)
        ms = jnp.where(lo, s_lo, s_hi) * (1.0 / HD)
        y = (xc * lax.rsqrt(ms + EPS)) * g_row
        partner = jnp.where(first, pltpu.roll(y, LANES - HD // 2, 1), pltpu.roll(y, HD // 2, 1))
        return y * cos + partner * sin

    gq = gq_ref[...]
    for c in range(NSA_W // LANES):
        sl = slice(c * LANES, (c + 1) * LANES)
        qc = norm_rope(zq_ref[:, sl], gq)
        qtok_ref[:, sl] = qc
        qb = qc.astype(BF16)
        qh_ref[2 * c] = qb[:, :HD]
        qh_ref[2 * c + 1] = qb[:, HD:]

    zin = (zc_ref, zs_ref, zw_ref)
    rows = (rc_ref, rs_ref, rw_ref)
    rows_t = (tc_ref, ts_ref, tw_ref)
    kg = (None, ks_ref, kw_ref)
    vg = (None, vs_ref, vw_ref)
    for br in range(3):
        gk = gk_ref[br:br + 1, :]
        cols = []
        for c in range(KW // LANES):
            cols.append(norm_rope(zin[br][:, c * LANES:(c + 1) * LANES], gk))
        for c in range(KW // LANES):
            cols.append(zin[br][:, KW + c * LANES:KW + (c + 1) * LANES])
        for c, col in enumerate(cols):
            rows[br][:, c * LANES:(c + 1) * LANES] = col
            rows_t[br][c * LANES:(c + 1) * LANES, :] = col.T
        if kg[br] is not None:
            for c in range(KW // LANES):
                kb = cols[c].astype(BF16)
                vb = cols[KW // LANES + c].astype(BF16)
                kg[br][2 * c] = kb[:, :HD]
                kg[br][2 * c + 1] = kb[:, HD:]
                vg[br][2 * c] = vb[:, :HD]
                vg[br][2 * c + 1] = vb[:, HD:]


def _qkv_post(z1, cos_tab, sin_tab, gq, gk, n_prompt, seq):
    n = z1.shape[0]
    tm = 256
    n_ptiles = n_prompt // tm
    tiles_per_seq = seq // tm
    n_slabs = -(-n // seq)

    def tab_map(i):
        return (jnp.where(i < n_ptiles, i % tiles_per_seq, tiles_per_seq), 0)

    row = lambda i: (i, 0)
    head = lambda i: (0, i, 0)
    slab = lambda i: (i // tiles_per_seq, 0, i % tiles_per_seq)
    rows_sd = jax.ShapeDtypeStruct((n, KVW), F32)
    rows_t_sd = jax.ShapeDtypeStruct((n_slabs, KVW, seq), F32)
    grp_sd = jax.ShapeDtypeStruct((KVH, n, HD), BF16)
    out_shape = (
        jax.ShapeDtypeStruct((n, NSA_W), F32),
        jax.ShapeDtypeStruct((HEADS, n, HD), BF16),
        rows_sd, rows_sd, rows_sd,
        rows_t_sd, rows_t_sd, rows_t_sd,
        grp_sd, grp_sd, grp_sd, grp_sd,
    )
    rows_spec = pl.BlockSpec((tm, KVW), row)
    rows_t_spec = pl.BlockSpec((None, KVW, tm), slab)
    grp_spec = pl.BlockSpec((KVH, tm, HD), head)
    out_specs = (
        pl.BlockSpec((tm, NSA_W), row),
        pl.BlockSpec((HEADS, tm, HD), head),
        rows_spec, rows_spec, rows_spec,
        rows_t_spec, rows_t_spec, rows_t_spec,
        grp_spec, grp_spec, grp_spec, grp_spec,
    )
    kv_in = lambda br: pl.BlockSpec((tm, KVW), lambda i: (i, C1_KV // KVW + br))
    return pl.pallas_call(
        _qkv_post_kernel,
        grid=(n // tm,),
        in_specs=[
            pl.BlockSpec((tm, NSA_W), lambda i: (i, C1_Q // NSA_W)),
            kv_in(0), kv_in(1), kv_in(2),
            pl.BlockSpec((tm, LANES), tab_map),
            pl.BlockSpec((tm, LANES), tab_map),
            pl.BlockSpec((1, LANES), lambda i: (0, 0)),
            pl.BlockSpec((3, LANES), lambda i: (0, 0)),
        ],
        out_specs=out_specs,
        out_shape=out_shape,
        compiler_params=_params("parallel"),
        name="qkv_post",
    )(z1, z1, z1, z1, cos_tab, sin_tab, gq, gk)


def _compress_pair(flat, w2, pe8, w2bd, n_ch):
    ab = _dot(flat, w2)
    hp = _dot(pe8.astype(BF16), w2)
    hpe = hp[0:1, :LANES] + hp[1:2, LANES:]
    nxt = pltpu.roll(ab[:, LANES:], n_ch - 1, 0)
    h = ab[:, :LANES] + nxt + hpe
    act = h * jax.nn.sigmoid(h)
    return _dot(act.astype(BF16), w2bd)


def _scatter_chunk_rows(xs_ref, perm, tile_t, c0):
    rpp = tile_t.shape[1] // D_CMP
    xp = _nt(perm, tile_t)
    for t in range(D_CMP):
        xs_ref[t, c0:c0 + rpp, :] = xp[t * rpp:(t + 1) * rpp, :]


def _chunk_flat(xs_ref, p):
    return jnp.concatenate([xs_ref[t, :, p * LANES:(p + 1) * LANES] for t in range(D_CMP)], axis=1).astype(BF16)


def _compress_prompt_kernel(rows_ref, w2k_ref, w2v_ref, pek_ref, pev_ref, bdk_ref, bdv_ref, perm_ref, kc_ref, vc_ref,
                            xs_ref, *, n_ch):
    w2 = (w2k_ref, w2v_ref)
    pe = (pek_ref, pev_ref)
    bd = (bdk_ref, bdv_ref)
    outs = (kc_ref, vc_ref)
    tile_rows = perm_ref.shape[0]
    rpp = tile_rows // D_CMP
    for i in range(n_ch // rpp):
        _scatter_chunk_rows(xs_ref, perm_ref[...], rows_ref[:, i * tile_rows:(i + 1) * tile_rows].astype(BF16), i * rpp)
    for p in range(KVW // LANES):
        kv = p // 2
        flat = _chunk_flat(xs_ref, p)
        res = _compress_pair(flat, w2[kv][...], pe[kv][...], bd[kv][...], n_ch).astype(BF16)
        g0 = 2 * (p % 2)
        outs[kv][0, g0] = res[:, :HD]
        outs[kv][0, g0 + 1] = res[:, HD:]


def _compress_prompt(rows_cmp, cw, batch, seq):
    n_ch = seq // D_CMP
    const = lambda b: (0, 0)
    out_sd = jax.ShapeDtypeStruct((batch, KVH, n_ch, HD), BF16)
    out_spec = pl.BlockSpec((1, KVH, n_ch, HD), lambda b: (b, 0, 0, 0))
    return pl.pallas_call(
        functools.partial(_compress_prompt_kernel, n_ch=n_ch),
        grid=(batch,),
        in_specs=[pl.BlockSpec((None, KVW, seq), lambda b: (b, 0, 0))] + [pl.BlockSpec(w.shape, const) for w in cw],
        out_specs=(out_spec, out_spec),
        out_shape=(out_sd, out_sd),
        scratch_shapes=[pltpu.VMEM((D_CMP, n_ch, KVW), F32)],
        compiler_params=_params("parallel"),
        name="compress_prompt",
    )(rows_cmp, *cw)


def _rank_select(score_t, n_blocks, n_rows, n_top):
    st = score_t[:n_rows]
    jio = lax.broadcasted_iota(jnp.int32, st.shape, 0)
    cnt = jnp.zeros(st.shape, F32)
    for i in range(n_blocks):
        row = score_t[i:i + 1, :]
        beats = (row > st) | ((row == st) & (jio > i))
        cnt = cnt + jnp.where(beats, 1.0, 0.0)
    return jnp.where(cnt < n_top, 1.0, 0.0)


def _pad_rows(x, n):
    if x.shape[0] == n:
        return x
    return jnp.concatenate([x, jnp.zeros((n - x.shape[0], x.shape[1]), x.dtype)], axis=0)


def _cmp_select_kernel(qh_ref, kc_ref, vc_ref, ov_ref, ocmp_ref, sel_ref, *, nc, ns, n_top):
    s0 = pl.program_id(1) * Q_BLK
    ncp = kc_ref.shape[2]
    rows = GRP * Q_BLK
    qpos_r = s0 + lax.broadcasted_iota(jnp.int32, (rows, ncp), 0) % Q_BLK
    cidx = lax.broadcasted_iota(jnp.int32, (rows, ncp), 1)
    cvalid = (cidx * D_CMP + (L_CMP - 1) <= qpos_r) & (cidx < nc)
    qpos = s0 + lax.broadcasted_iota(jnp.int32, (Q_BLK, LANES), 0)
    blk = lax.broadcasted_iota(jnp.int32, (Q_BLK, LANES), 1)
    visible = (blk * L_SEL <= qpos) & (blk < ns)
    cur = qpos // L_SEL
    forced = (blk == 0) | (blk == cur) | (blk == cur - 1)
    nsp = -(-ns // 8) * 8
    ov = ov_ref[...]
    for g in range(KVH):
        qg = jnp.concatenate([qh_ref[GRP * g + r] for r in range(GRP)], axis=0)
        s = _nt(qg, kc_ref[0, g]) * (HD ** -0.5)
        s = jnp.where(cvalid, s, NEG)
        m = jnp.max(s, axis=-1, keepdims=True)
        e = jnp.where(cvalid, jnp.exp(s - m), 0.0)
        p = e / jnp.maximum(jnp.sum(e, axis=-1, keepdims=True), 1.0)
        o = _dot(p.astype(BF16), vc_ref[0, g])
        for r in range(GRP):
            ocmp_ref[GRP * g + r] = o[r * Q_BLK:(r + 1) * Q_BLK]
        psum = p[0:Q_BLK]
        for r in range(1, GRP):
            psum = psum + p[r * Q_BLK:(r + 1) * Q_BLK]
        imp = _dot_exact_lhs(psum, ov)
        score = jnp.where(visible, imp + jnp.where(forced, BIG, 0.0), -BIG)
        sel_t = _rank_select(score.T, ns, nsp, n_top)
        sel = _pad_rows(sel_t, LANES).T
        sel_ref[:, g * LANES:(g + 1) * LANES] = jnp.where(visible, sel, 0.0).astype(BF16)


def _cmp_select(q_heads, kc, vc, ov, batch, seq):
    nqt = seq // Q_BLK
    n_ch = kc.shape[2]
    ns = seq // L_SEL
    kern = functools.partial(_cmp_select_kernel, nc=n_ch - 1, ns=ns, n_top=min(N_TOP, ns))
    kv_spec = pl.BlockSpec((1, KVH, n_ch, HD), lambda b, t: (b, 0, 0, 0))
    return pl.pallas_call(
        kern,
        grid=(batch, nqt),
        in_specs=[
            pl.BlockSpec((HEADS, Q_BLK, HD), lambda b, t: (0, b * nqt + t, 0)),
            kv_spec, kv_spec,
            pl.BlockSpec(ov.shape, lambda b, t: (0, 0)),
        ],
        out_specs=(
            pl.BlockSpec((HEADS, Q_BLK, HD), lambda b, t: (0, b * nqt + t, 0)),
            pl.BlockSpec((Q_BLK, KVH * LANES), lambda b, t: (b * nqt + t, 0)),
        ),
        out_shape=(
            jax.ShapeDtypeStruct((HEADS, batch * seq, HD), F32),
            jax.ShapeDtypeStruct((batch * seq, KVH * LANES), BF16),
        ),
        compiler_params=_params("parallel", "parallel"),
        name="cmp_select",
    )(q_heads, kc, vc, ov)


def _sel_win_kernel(qh_ref, ks_ref, vs_ref, kw_ref, vw_ref, sel_ref, e_ref, zbr_ref, ocmp_ref, oa_ref,
                    q_s, *, kt):
    s0 = pl.program_id(1) * Q_BLK
    rows = GRP * Q_BLK
    n_kt = (s0 + Q_BLK + kt - 1) // kt
    qpos = s0 + lax.broadcasted_iota(jnp.int32, (Q_BLK, kt), 0)
    lane_k = lax.broadcasted_iota(jnp.int32, (Q_BLK, kt), 1)
    gates = jax.nn.sigmoid(zbr_ref[...])
    wl = WINDOW + Q_BLK
    wstart = pl.multiple_of(jnp.maximum(s0 - WINDOW, 0), Q_BLK)
    kposw = wstart + lax.broadcasted_iota(jnp.int32, (Q_BLK, wl), 1)
    qposw = s0 + lax.broadcasted_iota(jnp.int32, (Q_BLK, wl), 0)
    wbias = jnp.where((kposw <= qposw) & (kposw > qposw - WINDOW), 0.0, NEG)

    def biased(x, bias):
        return jnp.concatenate([x[r * Q_BLK:(r + 1) * Q_BLK] + bias for r in range(GRP)], axis=0)

    for g in range(KVH):
        q_s[g] = jnp.concatenate([qh_ref[GRP * g + r] for r in range(GRP)], axis=0) * (HD ** -0.5)

    def body(j, carry):
        k0 = pl.multiple_of(j * kt, kt)
        causal = k0 + lane_k <= qpos
        out = []
        for g in range(KVH):
            m, l, acc = carry[g]
            k = ks_ref[g, pl.ds(k0, kt), :]
            v = vs_ref[g, pl.ds(k0, kt), :]
            mf = _dot(sel_ref[:, g * LANES:(g + 1) * LANES], e_ref[j])
            bias = jnp.where((mf > 0.5) & causal, 0.0, NEG)
            s = biased(_nt(q_s[g], k), bias)
            m_new = jnp.maximum(m, jnp.max(s, axis=-1, keepdims=True))
            corr = jnp.exp(m - m_new)
            p = jnp.exp(s - m_new)
            l = l * corr + jnp.sum(p, axis=-1, keepdims=True)
            acc = acc * corr + _dot(p.astype(BF16), v)
            out.append((m_new, l, acc))
        return tuple(out)

    init = tuple((jnp.full((rows, 1), NEG, F32), jnp.zeros((rows, 1), F32), jnp.zeros((rows, HD), F32))
                 for _ in range(KVH))
    res = lax.fori_loop(0, n_kt, body, init)

    for g in range(KVH):
        qg = q_s[g]
        o_sel = res[g][2] / res[g][1]
        kwin = kw_ref[g, pl.ds(wstart, wl), :]
        vwin = vw_ref[g, pl.ds(wstart, wl), :]
        s = biased(_nt(qg, kwin), wbias)
        m = jnp.max(s, axis=-1, keepdims=True)
        e = jnp.exp(s - m)
        den = jnp.maximum(jnp.sum(e, axis=-1, keepdims=True), 1.0)
        o_win = _dot(e.astype(BF16), vwin) / den

        for r in range(GRP):
            h = GRP * g + r
            rs = slice(r * Q_BLK, (r + 1) * Q_BLK)
            o = (gates[:, h:h + 1] * ocmp_ref[h]
                 + gates[:, HEADS + h:HEADS + h + 1] * o_sel[rs]
                 + gates[:, 2 * HEADS + h:2 * HEADS + h + 1] * o_win[rs])
            oa_ref[:, h * HD:(h + 1) * HD] = o


def _sel_win(q_heads, ks, vs, kw, vw, sel, e_mat, z, ocmp, batch, seq):
    nqt = seq // Q_BLK
    kt = 512
    e_mat = e_mat.reshape(LANES, seq // kt, kt).transpose(1, 0, 2)
    kv_spec = pl.BlockSpec((KVH, seq, HD), lambda b, t: (0, b, 0), pipeline_mode=pl.Buffered(1))
    qspec = pl.BlockSpec((HEADS, Q_BLK, HD), lambda b, t: (0, b * nqt + t, 0))
    return pl.pallas_call(
        functools.partial(_sel_win_kernel, kt=kt),
        grid=(batch, nqt),
        in_specs=[
            qspec, kv_spec, kv_spec, kv_spec, kv_spec,
            pl.BlockSpec((Q_BLK, KVH * LANES), lambda b, t: (b * nqt + t, 0)),
            pl.BlockSpec(e_mat.shape, lambda b, t: (0, 0, 0)),
            pl.BlockSpec((Q_BLK, LANES), lambda b, t: (b * nqt + t, C1_BR // LANES)),
            qspec,
        ],
        out_specs=pl.BlockSpec((Q_BLK, NSA_W), lambda b, t: (b * nqt + t, 0)),
        out_shape=jax.ShapeDtypeStruct((batch * seq, NSA_W), F32),
        scratch_shapes=[pltpu.VMEM((KVH, GRP * Q_BLK, HD), BF16)],
        compiler_params=_params("parallel", "arbitrary"),
        name="sel_win",
    )(q_heads, ks, vs, kw, vw, sel, e_mat, z, ocmp)


def _hgrn_gates(zq, zf, logits):
    mx = jnp.max(logits, axis=0, keepdims=True)
    ex = jnp.exp(logits - mx)
    lb = ex[0:1] / jnp.sum(ex, axis=0, keepdims=True)
    hq = zq * jax.nn.sigmoid(zq)
    log_sig = -(jnp.maximum(-zf, 0.0) + jnp.log1p(jnp.exp(-jnp.abs(zf))))
    a = jnp.log(lb)
    b = jnp.log1p(-lb) + log_sig
    log_f = jnp.maximum(a, b) + jnp.log1p(jnp.exp(-jnp.abs(a - b)))
    hk = (1.0 - lb) * jax.nn.sigmoid(-zf)
    return hq, hk, log_f


def _intra_chunk(ac, qc, kc, vc):
    n = ac.shape[0]
    tio = lax.broadcasted_iota(jnp.int32, ac.shape, 0)
    acc = jnp.zeros(vc.shape, F32)
    for s in range(n):
        d = jnp.exp(ac - ac[s:s + 1, :])
        w = jnp.where(tio >= s, qc * kc[s:s + 1, :] * d, 0.0)
        col = jnp.sum(w, axis=-1, keepdims=True)
        acc = acc + col * vc[s:s + 1, :]
    return acc


def _hgrn_prompt_kernel(zq_ref, zf_ref, zi_ref, lg_ref, tri_ref, full_ref, o_ref, sout_ref,
                        st_ref, a_s, q_s, k_s, qt_s, ktb_s, eal_s, vt_s):
    t = pl.program_id(1)
    tt = zq_ref.shape[0]

    @pl.when(t == 0)
    def _():
        st_ref[...] = jnp.zeros(st_ref.shape, F32)

    hq, hk, log_f = _hgrn_gates(zq_ref[...], zf_ref[...], lg_ref[...])
    a = _dot_exact_rhs(tri_ref[...], log_f)
    al = _dot_exact_rhs(full_ref[...], log_f)
    a_s[...] = a
    q_s[...] = hq
    k_s[...] = hk
    qt_s[...] = hq * jnp.exp(a)
    ktb_s[...] = (hk * jnp.exp(al - a)).astype(BF16)
    eal_s[...] = jnp.exp(al)
    for h in range(HG_H):
        vt_s[h] = zi_ref[:, h * HG_DV:(h + 1) * HG_DV].T.astype(BF16)
    lane_tok = lax.broadcasted_iota(jnp.int32, (HG_DV, tt), 1)

    def chunk(c, carry):
        r0 = pl.multiple_of(c * HG_SUB, HG_SUB)
        in_chunk = (lane_tok >= r0) & (lane_tok < r0 + HG_SUB)
        for h in range(HG_H):
            sl = slice(h * HG_DK, (h + 1) * HG_DK)
            st = st_ref[h]
            o = _nt(qt_s[pl.ds(r0, HG_SUB), sl].astype(BF16), st.astype(BF16))
            o = o + _intra_chunk(a_s[pl.ds(r0, HG_SUB), sl], q_s[pl.ds(r0, HG_SUB), sl],
                                 k_s[pl.ds(r0, HG_SUB), sl], zi_ref[pl.ds(r0, HG_SUB), sl])
            o_ref[pl.ds(r0, HG_SUB), sl] = o
            vtm = jnp.where(in_chunk, vt_s[h], jnp.zeros_like(vt_s[h]))
            st_ref[h] = st * eal_s[pl.ds(r0, 1), sl] + _dot(vtm, ktb_s[:, sl])
        return carry

    lax.fori_loop(0, tt // HG_SUB, chunk, 0)

    @pl.when(t == pl.num_programs(1) - 1)
    def _():
        for h in range(HG_H):
            sout_ref[0, h] = st_ref[h].T


def _hgrn_prompt(z, logits, batch, seq):
    tt = 128
    nt = seq // tt
    sub = np.arange(tt) // HG_SUB
    same = sub[:, None] == sub[None, :]
    tri = jnp.asarray((same & (np.arange(tt)[None, :] <= np.arange(tt)[:, None])).astype(np.float32), BF16)
    full = jnp.asarray(same.astype(np.float32), BF16)
    col = lambda c: (lambda b, t: (b * nt + t, c // HG_W))
    const = lambda b, t: (0, 0)
    return pl.pallas_call(
        _hgrn_prompt_kernel,
        grid=(batch, nt),
        in_specs=[
            pl.BlockSpec((tt, HG_W), col(C2_HQ)),
            pl.BlockSpec((tt, HG_W), col(C2_HF)),
            pl.BlockSpec((tt, HG_W), col(C2_HI)),
            pl.BlockSpec((2, HG_W), const),
            pl.BlockSpec((tt, tt), const),
            pl.BlockSpec((tt, tt), const),
        ],
        out_specs=(
            pl.BlockSpec((tt, HG_W), lambda b, t: (b * nt + t, 0)),
            pl.BlockSpec((1, HG_H, HG_DK, HG_DV), lambda b, t: (b, 0, 0, 0)),
        ),
        out_shape=(
            jax.ShapeDtypeStruct((batch * seq, HG_W), F32),
            jax.ShapeDtypeStruct((batch, HG_H, HG_DK, HG_DV), F32),
        ),
        scratch_shapes=[
            pltpu.VMEM((HG_H, HG_DV, HG_DK), F32),
            pltpu.VMEM((tt, HG_W), F32),
            pltpu.VMEM((tt, HG_W), F32),
            pltpu.VMEM((tt, HG_W), F32),
            pltpu.VMEM((tt, HG_W), F32),
            pltpu.VMEM((tt, HG_W), BF16),
            pltpu.VMEM((tt, HG_W), F32),
            pltpu.VMEM((HG_H, HG_DV, tt), BF16),
        ],
        compiler_params=_params("parallel", "arbitrary"),
        name="hgrn_prompt",
    )(z, z, z, logits, tri, full)


def _hgrn_sample_kernel(zq_ref, zf_ref, zi_ref, lg_ref, s0_ref, o_ref, sout_ref):
    s_len = zq_ref.shape[0]
    hq, hk, log_f = _hgrn_gates(zq_ref[...], zf_ref[...], lg_ref[...])
    rio = lax.broadcasted_iota(jnp.int32, log_f.shape, 0)
    a = log_f
    d = 1
    while d < s_len:
        a = a + jnp.where(rio >= d, pltpu.roll(a, d, 0), 0.0)
        d *= 2
    a_last = jnp.sum(log_f, axis=0, keepdims=True)
    qt = hq * jnp.exp(a)
    kt = hk * jnp.exp(a_last - a)
    ea = jnp.exp(a_last)
    hv = zi_ref[...]
    row0 = lax.broadcasted_iota(jnp.int32, (s_len, HG_DK), 0) == 0
    pad = HG_DK - 2 * s_len
    for h in range(HG_H):
        sl = slice(h * HG_DK, (h + 1) * HG_DK)
        s_old = s0_ref[0, h]
        qt16 = _pad_rows(qt[:, sl], 16).astype(BF16)
        o = _dot(qt16, s_old.astype(BF16))[:s_len]
        o = o + _intra_chunk(a[:, sl], hq[:, sl], hk[:, sl], hv[:, sl])
        o_ref[:, sl] = o
        m = jnp.concatenate([kt[:, sl], jnp.where(row0, jnp.broadcast_to(ea[:, sl], (s_len, HG_DK)), 0.0),
                             jnp.zeros((pad, HG_DK), F32)], axis=0)
        mt = m.T
        e_col = mt[:, s_len:s_len + 1]
        vpad = _pad_rows(hv[:, sl], HG_DK).astype(BF16)
        sout_ref[0, h] = e_col * s_old + _dot(mt.astype(BF16), vpad)


def _hgrn_sample(z, logits, state, n_prompt, dec_batch, s_len):
    base = n_prompt // s_len
    col = lambda c: (lambda b: (base + b, c // HG_W))
    st_spec = pl.BlockSpec((1, HG_H, HG_DK, HG_DV), lambda b: (b, 0, 0, 0))
    return pl.pallas_call(
        _hgrn_sample_kernel,
        grid=(dec_batch,),
        in_specs=[
            pl.BlockSpec((s_len, HG_W), col(C2_HQ)),
            pl.BlockSpec((s_len, HG_W), col(C2_HF)),
            pl.BlockSpec((s_len, HG_W), col(C2_HI)),
            pl.BlockSpec((2, HG_W), lambda b: (0, 0)),
            st_spec,
        ],
        out_specs=(pl.BlockSpec((s_len, HG_W), lambda b: (b, 0)), st_spec),
        out_shape=(
            jax.ShapeDtypeStruct((dec_batch * s_len, HG_W), F32),
            jax.ShapeDtypeStruct(state.shape, F32),
        ),
        compiler_params=_params("parallel"),
        name="hgrn_sample",
    )(z, z, z, logits, state)


def _sample_nsa_kernel(pt_ref, *refs, npg, s_len, past):
    del pt_ref
    cmp_pages = refs[:npg]
    sel_pages = refs[npg:2 * npg]
    (win_ref, q_ref, rsel_ref, rwin_ref, zbr_ref, w2k_ref, w2v_ref, pek_ref, pev_ref, bdk_ref, bdv_ref, perm_ref,
     ov_ref, e_ref, oa_ref, swin_ref, xs_ref) = refs[2 * npg:]
    page = cmp_pages[0].shape[1]
    rpp = page // D_CMP
    n_ch = npg * rpp
    nc = n_ch - 1
    wb = win_ref.shape[1]
    ns = -(-(past + s_len) // L_SEL)
    nsp = -(-ns // 8) * 8
    n_top = min(N_TOP, ns)
    rows = HEADS * s_len

    w2 = (w2k_ref, w2v_ref)
    pe = (pek_ref, pev_ref)
    bd = (bdk_ref, bdv_ref)
    comp = []
    for i, pg in enumerate(cmp_pages):
        _scatter_chunk_rows(xs_ref, perm_ref[...], pg[...].astype(BF16), i * rpp)
    for p in range(KVW // LANES):
        kv = p // 2
        comp.append(_compress_pair(_chunk_flat(xs_ref, p), w2[kv][...], pe[kv][...], bd[kv][...], n_ch))
    kc_all = jnp.concatenate(comp[0:2], axis=1).astype(BF16)
    vc_all = jnp.concatenate(comp[2:4], axis=1).astype(BF16)

    q = q_ref[...]
    lane8 = lax.broadcasted_iota(jnp.int32, (s_len, LANES), 1)
    zero8 = jnp.zeros((s_len, LANES), F32)
    blocks = []
    for g in range(KVH):
        for r in range(GRP):
            h = GRP * g + r
            src = q[:, (h // 2) * LANES:(h // 2 + 1) * LANES]
            if (r % 2) != (g % 2):
                src = pltpu.roll(src, HD, 1)
            keep = (lane8 < HD) if g % 2 == 0 else (lane8 >= HD)
            piece = jnp.where(keep, src, 0.0)
            blocks.append(jnp.concatenate([piece, zero8] if g // 2 == 0 else [zero8, piece], axis=1))
    qbd = (jnp.concatenate(blocks, axis=0) * (HD ** -0.5)).astype(BF16)

    def rowpos(shape):
        return past + lax.broadcasted_iota(jnp.int32, shape, 0) % s_len

    s = _nt(qbd, kc_all)
    cidx = lax.broadcasted_iota(jnp.int32, s.shape, 1)
    cvalid = (cidx * D_CMP + (L_CMP - 1) <= rowpos(s.shape)) & (cidx < nc)
    s = jnp.where(cvalid, s, NEG)
    m = jnp.max(s, axis=-1, keepdims=True)
    e = jnp.where(cvalid, jnp.exp(s - m), 0.0)
    p = e / jnp.maximum(jnp.sum(e, axis=-1, keepdims=True), 1.0)
    ocmp = _dot(p.astype(BF16), vc_all)

    gs = GRP * s_len
    psum = []
    for g in range(KVH):
        acc = p[g * gs:g * gs + s_len]
        for r in range(1, GRP):
            acc = acc + p[g * gs + r * s_len:g * gs + (r + 1) * s_len]
        psum.append(acc)
    imp = _dot_exact_lhs(jnp.concatenate(psum, axis=0), ov_ref[...])
    qpos = rowpos(imp.shape)
    blk = lax.broadcasted_iota(jnp.int32, imp.shape, 1)
    visible = (blk * L_SEL <= qpos) & (blk < ns)
    cur = qpos // L_SEL
    forced = (blk == 0) | (blk == cur) | (blk == cur - 1)
    score = jnp.where(visible, imp + jnp.where(forced, BIG, 0.0), -BIG)
    sel_t = _rank_select(_pad_rows(score, LANES).T, ns, nsp, n_top)
    sel = _pad_rows(sel_t, LANES).T[:KVH * s_len]
    sel = jnp.where(visible, sel, 0.0)
    sel_rows = jnp.concatenate([sel[g * s_len:(g + 1) * s_len] for g in range(KVH) for _ in range(GRP)], axis=0)

    zpad = jnp.zeros((page - s_len, KW), F32)
    knew = jnp.concatenate([rsel_ref[:, :KW], zpad], axis=0).astype(BF16)
    vnew = jnp.concatenate([rsel_ref[:, KW:], zpad], axis=0).astype(BF16)
    s = jnp.concatenate([_dot(qbd, pg[:KW, :].astype(BF16)) for pg in sel_pages] + [_nt(qbd, knew)], axis=1)
    mf = _dot(sel_rows.astype(BF16), e_ref[...])
    kpos = lax.broadcasted_iota(jnp.int32, s.shape, 1)
    mask = (mf > 0.5) & (kpos <= rowpos(s.shape))
    s = jnp.where(mask, s, NEG)
    m = jnp.max(s, axis=-1, keepdims=True)
    e = jnp.where(mask, jnp.exp(s - m), 0.0)
    l = jnp.sum(e, axis=-1, keepdims=True)
    eb = e.astype(BF16)
    acc = _dot(eb[:, npg * page:], vnew)
    for i in range(npg):
        acc = acc + _nt(eb[:, i * page:(i + 1) * page], sel_pages[i][KW:, :].astype(BF16))
    osel = acc / l

    kwn = jnp.concatenate([rwin_ref[:, :KW], zpad], axis=0).astype(BF16)
    vwn = jnp.concatenate([rwin_ref[:, KW:], zpad], axis=0).astype(BF16)
    s = jnp.concatenate([_dot(qbd, win_ref[:KW, :].astype(BF16)), _nt(qbd, kwn)], axis=1)
    kposw = (past - wb) + lax.broadcasted_iota(jnp.int32, s.shape, 1)
    qposw = rowpos(s.shape)
    wmask = (kposw <= qposw) & (kposw > qposw - WINDOW) & (kposw >= 0)
    s = jnp.where(wmask, s, NEG)
    m = jnp.max(s, axis=-1, keepdims=True)
    e = jnp.where(wmask, jnp.exp(s - m), 0.0)
    den = jnp.maximum(jnp.sum(e, axis=-1, keepdims=True), 1.0)
    eb = e.astype(BF16)
    owin = (_nt(eb[:, :wb], win_ref[KW:, :].astype(BF16)) + _dot(eb[:, wb:], vwn)) / den

    gates = jax.nn.sigmoid(zbr_ref[...])
    for g in range(KVH):
        gl = slice((g // 2) * LANES, (g // 2 + 1) * LANES)
        for k2 in range(2):
            halves = []
            for r in (2 * k2, 2 * k2 + 1):
                h = GRP * g + r
                rs = slice(h * s_len, (h + 1) * s_len)
                val = (gates[:, h:h + 1] * ocmp[rs, gl]
                       + gates[:, HEADS + h:HEADS + h + 1] * osel[rs, gl]
                       + gates[:, 2 * HEADS + h:2 * HEADS + h + 1] * owin[rs, gl])
                if (r % 2) != (g % 2):
                    val = pltpu.roll(val, HD, 1)
                halves.append(val)
            c = 2 * g + k2
            oa_ref[:, c * LANES:(c + 1) * LANES] = jnp.where(lane8 < HD, halves[0], halves[1])

    new_t = [_pad_rows(rwin_ref[:, c * LANES:(c + 1) * LANES], LANES).T for c in range(KVW // LANES)]
    new_t = jnp.concatenate(new_t, axis=0)
    lane_w = lax.broadcasted_iota(jnp.int32, (KVW, LANES), 1)
    n_col = wb // LANES
    rolled = [pltpu.roll(win_ref[:, c * LANES:(c + 1) * LANES], LANES - s_len, 1) for c in range(n_col)]
    rolled.append(pltpu.roll(new_t, LANES - s_len, 1))
    for c in range(n_col):
        swin_ref[:, c * LANES:(c + 1) * LANES] = jnp.where(lane_w < LANES - s_len, rolled[c], rolled[c + 1])


def _sample_nsa(page_table, cache_cmp, cache_sel, cache_win, q_tok, rows_sel, rows_win, z, cw, ov, e_mat,
                n_prompt, dec_batch, s_len):
    npg = page_table.shape[1]
    page = cache_cmp.shape[2]
    past = npg * page
    wb = cache_win.shape[2]
    base = n_prompt // s_len
    page_specs = [pl.BlockSpec((None, KVW, page), (lambda b, pt, i=i: (pt[b, i], 0, 0))) for i in range(npg)]
    row = lambda b, pt: (base + b, 0)
    const = lambda b, pt: (0, 0)
    in_specs = (
        page_specs + page_specs
        + [pl.BlockSpec((None, KVW, wb), lambda b, pt: (b, 0, 0)),
           pl.BlockSpec((s_len, NSA_W), row),
           pl.BlockSpec((s_len, KVW), row),
           pl.BlockSpec((s_len, KVW), row),
           pl.BlockSpec((s_len, LANES), lambda b, pt: (base + b, C1_BR // LANES))]
        + [pl.BlockSpec(w.shape, const) for w in cw]
        + [pl.BlockSpec(ov.shape, const), pl.BlockSpec(e_mat.shape, const)]
    )
    grid_spec = pltpu.PrefetchScalarGridSpec(
        num_scalar_prefetch=1,
        grid=(dec_batch,),
        in_specs=in_specs,
        out_specs=(
            pl.BlockSpec((s_len, NSA_W), lambda b, pt: (b, 0)),
            pl.BlockSpec((None, KVW, wb), lambda b, pt: (b, 0, 0)),
        ),
        scratch_shapes=[pltpu.VMEM((D_CMP, past // D_CMP, KVW), F32)],
    )
    return pl.pallas_call(
        functools.partial(_sample_nsa_kernel, npg=npg, s_len=s_len, past=past),
        grid_spec=grid_spec,
        out_shape=(
            jax.ShapeDtypeStruct((dec_batch * s_len, NSA_W), F32),
            jax.ShapeDtypeStruct((dec_batch, KVW, wb), F32),
        ),
        compiler_params=_params("parallel"),
        name="sample_nsa",
    )(page_table, *([cache_cmp] * npg), *([cache_sel] * npg), cache_win, q_tok, rows_sel, rows_win, z, *cw, ov, e_mat)


def _mix_kernel(x_ref, oa_ref, ob_ref, zhg_ref, ga_ref, gb_ref, gn_ref, wa_ref, wb_ref, wo_ref, o_ref):
    ya = _dot(oa_ref[...].astype(BF16), wa_ref[...])
    gn = gn_ref[...]
    parts = []
    for h in range(HG_H):
        sl = slice(h * HG_DV, (h + 1) * HG_DV)
        ob = ob_ref[:, sl]
        ms = jnp.mean(ob * ob, axis=-1, keepdims=True)
        zg = zhg_ref[:, sl]
        y = ((ob * lax.rsqrt(ms + EPS)) * gn) * (zg * jax.nn.sigmoid(zg))
        parts.append(y.astype(BF16))
    yb = _dot(jnp.concatenate(parts, axis=1), wb_ref[...])
    mix = jax.nn.sigmoid(ga_ref[...]) * ya + jax.nn.sigmoid(gb_ref[...]) * yb
    o_ref[...] = x_ref[...] + _dot(mix.astype(BF16), wo_ref[...])


def _mix(x, oa, ob, z, gn, wa, wb, wo):
    n, d = x.shape
    tm = 256
    row = lambda i: (i, 0)
    const = lambda i: (0, 0)
    resident = lambda w: pl.BlockSpec(w.shape, const, pipeline_mode=pl.Buffered(1))
    return pl.pallas_call(
        _mix_kernel,
        grid=(n // tm,),
        in_specs=[
            pl.BlockSpec((tm, d), row),
            pl.BlockSpec((tm, NSA_W), row),
            pl.BlockSpec((tm, HG_W), row),
            pl.BlockSpec((tm, HG_W), lambda i: (i, C2_HG // HG_W)),
            pl.BlockSpec((tm, d), lambda i: (i, C2_GA // D_MODEL)),
            pl.BlockSpec((tm, d), lambda i: (i, C2_GB // D_MODEL)),
            pl.BlockSpec((1, HG_DV), const),
            resident(wa), resident(wb), resident(wo),
        ],
        out_specs=pl.BlockSpec((tm, d), row),
        out_shape=jax.ShapeDtypeStruct((n, d), F32),
        compiler_params=_params("parallel"),
        name="mix_out",
    )(x, oa, ob, z, z, z, gn, wa, wb, wo)


def _ffn_kernel(x_ref, g_ref, w1_ref, w2_ref, o_ref, xn_ref, acc_ref):
    j = pl.program_id(1)

    @pl.when(j == 0)
    def _():
        x = x_ref[...]
        ms = jnp.mean(x * x, axis=-1, keepdims=True)
        xn_ref[...] = ((x * lax.rsqrt(ms + EPS)) * g_ref[...]).astype(BF16)
        acc_ref[...] = jnp.zeros(acc_ref.shape, F32)

    h = _dot(xn_ref[...], w1_ref[...])
    h = jnp.square(jnp.maximum(h, 0.0)).astype(BF16)
    acc_ref[...] += _dot(h, w2_ref[...])

    @pl.when(j == pl.num_programs(1) - 1)
    def _():
        o_ref[...] = x_ref[...] + acc_ref[...]


def _ffn(x, g, w1, w2):
    n, d = x.shape
    dff = w1.shape[1]
    tm = _pick_tile(n, (512, 256))
    tf = 512
    return pl.pallas_call(
        _ffn_kernel,
        grid=(n // tm, dff // tf),
        in_specs=[
            pl.BlockSpec((tm, d), lambda i, j: (i, 0)),
            pl.BlockSpec((1, d), lambda i, j: (0, 0)),
            pl.BlockSpec((d, tf), lambda i, j: (0, j)),
            pl.BlockSpec((tf, d), lambda i, j: (j, 0)),
        ],
        out_specs=pl.BlockSpec((tm, d), lambda i, j: (i, 0)),
        out_shape=jax.ShapeDtypeStruct((n, d), F32),
        scratch_shapes=[pltpu.VMEM((tm, d), BF16), pltpu.VMEM((tm, d), F32)],
        compiler_params=_params("parallel", "arbitrary"),
        name="ffn",
    )(x, g, w1, w2)


def _split_w_in(w_in):
    n1 = sum(_SIZES[:3])
    w1 = jnp.pad(w_in[:, :n1].astype(BF16), ((0, 0), (0, N_Z1 - n1)))
    w2 = w_in[:, n1:].astype(BF16)
    assert w2.shape[1] == N_Z2
    return w1, w2


def _compress_weights(cmp_pe, cmp_w1, cmp_w2):
    m = L_CMP // D_CMP
    eye2 = jnp.eye(2, dtype=F32)
    out = []
    w2s, pes, bds = [], [], []
    for kv in range(2):
        w1p = cmp_w1[kv].reshape(m, D_CMP, HD, HD).transpose(1, 2, 0, 3)
        w = w1p[:, None, :, :, None, :] * eye2[None, :, None, None, :, None]
        w2s.append(w.reshape(D_CMP * LANES, m * LANES).astype(BF16))
        pe = cmp_pe[kv].reshape(m, D_CMP, 1, HD)
        pe = jnp.broadcast_to(pe, (m, D_CMP, 2, HD)).reshape(m, D_CMP * LANES)
        pes.append(jnp.concatenate([pe, jnp.zeros((8 - m, D_CMP * LANES), F32)], axis=0))
        bds.append(jnp.kron(eye2, cmp_w2[kv]).astype(BF16))
    tile_rows = LANES
    rpp = tile_rows // D_CMP
    src = np.arange(tile_rows)
    perm = np.zeros((tile_rows, tile_rows), np.float32)
    perm[(src % D_CMP) * rpp + src // D_CMP, src] = 1.0
    return (w2s[0], w2s[1], pes[0], pes[1], bds[0], bds[1], jnp.asarray(perm, BF16))


def _overlap_matrix(nc_pad, ns):
    cs = np.arange(nc_pad) * D_CMP
    ss = np.arange(LANES) * L_SEL
    ov = np.clip(np.minimum(cs[:, None] + L_CMP, ss[None, :] + L_SEL) - np.maximum(cs[:, None], ss[None, :]), 0, None)
    ov = (ov / D_CMP) * (np.arange(LANES)[None, :] < ns)
    return jnp.asarray(ov.astype(np.float32), BF16)


def _expand_matrix(n_keys):
    e = (np.arange(n_keys)[None, :] // L_SEL) == np.arange(LANES)[:, None]
    return jnp.asarray(e.astype(np.float32), BF16)


def _rope_tables(seq, past, s_len, tm):
    half = HD // 2
    inv = THETA ** (-jnp.arange(half, dtype=F32) * 2.0 / HD)
    pos = jnp.concatenate([jnp.arange(seq), past + jnp.arange(tm) % s_len]).astype(F32)
    ang = pos[:, None] * inv[None, :]
    cos = jnp.cos(ang)
    sin = jnp.sin(ang)
    cos_t = jnp.concatenate([cos, cos, cos, cos], axis=1)
    sin_t = jnp.concatenate([-sin, sin, -sin, sin], axis=1)
    return cos_t, sin_t


def kernel(x_prompt, x_sample, cache_cmp_kv, cache_sel_kv, cache_win_kv, state_hgrn, page_table, norm1_g, w_in,
           q_norm_g, k_norm_g, cmp_pe, cmp_w1, cmp_w2, w_proj_a, hgrn_lb_logits, hgrn_norm_g, w_proj_b, w_out,
           norm2_g, w_ff1, w_ff2):
    assert w_in.shape[0] == 1, "single-layer step"
    batch, seq, d = x_prompt.shape
    dec_batch, s_len, _ = x_sample.shape
    n_prompt = batch * seq
    n_sample = dec_batch * s_len
    npg = page_table.shape[1]
    page = cache_cmp_kv.shape[2]
    past = npg * page
    wb = cache_win_kv.shape[2]
    assert seq % Q_BLK == 0 and seq >= WINDOW + Q_BLK and n_prompt % 256 == 0 and n_sample % 256 == 0
    assert s_len % 8 == 0 and page % D_CMP == 0 and 2 * s_len <= HG_DK

    x = jnp.concatenate([x_prompt.reshape(n_prompt, d), x_sample.reshape(n_sample, d)], axis=0)

    w_z1, w_z2 = _split_w_in(w_in[0])
    cw = _compress_weights(cmp_pe[0], cmp_w1[0], cmp_w2[0])
    cos_t, sin_t = _rope_tables(seq, past, s_len, 256)
    gq = jnp.tile(q_norm_g[0][None, :], (1, 2))
    gk = jnp.tile(k_norm_g[0], (1, 2))
    logits = hgrn_lb_logits.astype(F32)

    def token_minor(c):
        return jnp.transpose(c, (0, 2, 3, 4, 1)).reshape(c.shape[0], KVW, c.shape[1])

    def token_major(c_t):
        return jnp.transpose(c_t.reshape(c_t.shape[0], 2, KVH, HD, c_t.shape[2]), (0, 4, 1, 2, 3))

    z1 = _in_proj(x, norm1_g, w_z1, 896)
    z2 = _in_proj(x, norm1_g, w_z2, 1024)
    (q_tok, q_heads, rows_cmp, rows_sel, rows_win, rt_cmp, rt_sel, rt_win,
     ks, vs, kw, vw) = _qkv_post(z1, cos_t, sin_t, gq, gk, n_prompt, seq)

    kc, vc = _compress_prompt(rt_cmp, cw, batch, seq)
    ov_p = _overlap_matrix(seq // D_CMP, seq // L_SEL)
    ocmp, sel = _cmp_select(q_heads, kc, vc, ov_p, batch, seq)
    oa_p = _sel_win(q_heads, ks, vs, kw, vw, sel, _expand_matrix(seq), z1, ocmp, batch, seq)
    ob_p, p_state = _hgrn_prompt(z2, logits, batch, seq)

    ns_s = -(-(past + s_len) // L_SEL)
    ov_s = _overlap_matrix(past // D_CMP, ns_s)
    oa_s, s_win_t = _sample_nsa(
        page_table, token_minor(cache_cmp_kv[0]), token_minor(cache_sel_kv[0]), token_minor(cache_win_kv[0]),
        q_tok, rows_sel, rows_win, z1, cw, ov_s, _expand_matrix(past + page), n_prompt, dec_batch, s_len)
    ob_s, s_state = _hgrn_sample(z2, logits, state_hgrn[0], n_prompt, dec_batch, s_len)

    oa = jnp.concatenate([oa_p, oa_s], axis=0)
    ob = jnp.concatenate([ob_p, ob_s], axis=0)
    x2 = _mix(x, oa, ob, z2, hgrn_norm_g, w_proj_a[0].astype(BF16), w_proj_b[0].astype(BF16), w_out[0].astype(BF16))
    y = _ffn(x2, norm2_g, w_ff1[0].astype(BF16), w_ff2[0].astype(BF16))

    kv_shape = (2, KVH, HD)
    wp = min(WINDOW, seq)
    return (
        y[:n_prompt].reshape(batch, seq, d),
        y[n_prompt:].reshape(dec_batch, s_len, d),
        token_major(rt_cmp[:batch])[None],
        token_major(rt_sel[:batch])[None],
        token_major(rt_win[:batch, :, seq - wp:])[None],
        p_state[None],
        rows_cmp[n_prompt:].reshape((1, dec_batch, s_len) + kv_shape),
        rows_sel[n_prompt:].reshape((1, dec_batch, s_len) + kv_shape),
        token_major(s_win_t)[None],
        s_state[None],
    )
```

```python
import functools

import numpy as np
import jax
import jax.numpy as jnp
from jax import lax
from jax.experimental import pallas as pl
from jax.experimental.pallas import tpu as pltpu

F32 = jnp.float32
BF16 = jnp.bfloat16

D_MODEL = 2048
HEADS = 16
KVH = 4
GRP = HEADS // KVH
HD = 64
NSA_W = HEADS * HD
L_CMP = 32
D_CMP = 16
L_SEL = 64
N_TOP = 16
WINDOW = 512
Q_BLK = 128
THETA = 10000.0
HG_H = 8
HG_DK = 128
HG_DV = 128
HG_W = HG_H * HG_DK
HG_SUB = 16
EPS = 1e-6
NEG = -1e30
BIG = 1e9

LANES = 128
KVW = 2 * KVH * HD
KW = KVH * HD

_SIZES = (NSA_W, 6 * KVH * HD, 3 * HEADS, HG_W, HG_W, HG_W, HG_W, D_MODEL, D_MODEL)
C1_Q, C1_KV, C1_BR, N_Z1 = 0, 1024, 2560, 2688
C2_HQ, C2_HF, C2_HI, C2_HG, C2_GA, C2_GB, N_Z2 = 0, 1024, 2048, 3072, 4096, 6144, 8192

VMEM_LIMIT = 56 * 1024 * 1024


def _params(*sem):
    return pltpu.CompilerParams(dimension_semantics=sem, vmem_limit_bytes=VMEM_LIMIT)


def _nt(a, b):
    return lax.dot_general(a, b, (((1,), (1,)), ((), ())), preferred_element_type=F32)


def _dot(a, b):
    return jnp.dot(a, b, preferred_element_type=F32)


def _split3(x):
    hi = x.astype(BF16)
    r1 = x - hi.astype(F32)
    mid = r1.astype(BF16)
    lo = (r1 - mid.astype(F32)).astype(BF16)
    return hi, mid, lo


def _dot_exact_rhs(m, x):
    hi, mid, lo = _split3(x)
    return _dot(m, hi) + _dot(m, mid) + _dot(m, lo)


def _dot_exact_lhs(x, m):
    hi, mid, lo = _split3(x)
    return _dot(hi, m) + _dot(mid, m) + _dot(lo, m)


def _pick_tile(n, cands):
    for c in cands:
        if n % c == 0:
            return c
    raise ValueError(f"no tile in {cands} divides {n}")


def _rmsnorm_bf16(x, g):
    ms = jnp.mean(x * x, axis=-1, keepdims=True)
    return ((x * lax.rsqrt(ms + EPS)) * g).astype(BF16)


def _two_group_specs(n_p, n_s, tm, width):
    npt = n_p // tm
    assert n_p % tm == 0 and n_s % tm == 0
    spec_p = pl.BlockSpec((tm, width), lambda i, *_: (jnp.minimum(i, npt - 1), 0))
    spec_s = pl.BlockSpec((tm, width), lambda i, *_: (jnp.maximum(i - npt, 0), 0))
    return npt, spec_p, spec_s


def _rms_matmul_kernel(xp_ref, xs_ref, g_ref, w_ref, o_ref, xn_ref, *, npt):
    i = pl.program_id(0)

    @pl.when((pl.program_id(1) == 0) & (i < npt))
    def _():
        xn_ref[...] = _rmsnorm_bf16(xp_ref[...], g_ref[...])

    @pl.when((pl.program_id(1) == 0) & (i >= npt))
    def _():
        xn_ref[...] = _rmsnorm_bf16(xs_ref[...], g_ref[...])

    o_ref[...] = _dot(xn_ref[...], w_ref[...])


def _in_proj(xp, xs, g, w, tn):
    (n_p, d), n_s = xp.shape, xs.shape[0]
    n = n_p + n_s
    nout = w.shape[1]
    tm = _pick_tile(np.gcd(n_p, n_s), (1024, 512, 256))
    npt, spec_p, spec_s = _two_group_specs(n_p, n_s, tm, d)
    return pl.pallas_call(
        functools.partial(_rms_matmul_kernel, npt=npt),
        grid=(n // tm, nout // tn),
        in_specs=[
            spec_p, spec_s,
            pl.BlockSpec((1, d), lambda i, j: (0, 0)),
            pl.BlockSpec((d, tn), lambda i, j: (0, j)),
        ],
        out_specs=pl.BlockSpec((tm, tn), lambda i, j: (i, j)),
        out_shape=jax.ShapeDtypeStruct((n, nout), F32),
        scratch_shapes=[pltpu.VMEM((tm, d), BF16)],
        compiler_params=_params("parallel", "arbitrary"),
        name="in_proj",
    )(xp, xs, g, w)


def _qkv_post_kernel(zq_ref, zc_ref, zs_ref, zw_ref, cos_ref, sin_ref, gq_ref, gk_ref,
                     qtok_ref, qh_ref, rc_ref, rs_ref, rw_ref, tc_ref, ts_ref, tw_ref,
                     ks_ref, vs_ref, kw_ref, vw_ref):
    cos = cos_ref[...]
    sin = sin_ref[...]
    lane = lax.broadcasted_iota(jnp.int32, cos.shape, 1)
    lo = lane < HD
    first = (lane % HD) < (HD // 2)

    def norm_rope(xc, g_row):
        sq = xc * xc
        s_lo = jnp.sum(jnp.where(lo, sq, 0.0), axis=-1, keepdims=True)
        s_hi = jnp.sum(jnp.where(lo, 0.0, sq), axis=-1, keepdims=True)
        ms = jnp.where(lo, s_lo, s_hi) * (1.0 / HD)
        y = (xc * lax.rsqrt(ms + EPS)) * g_row
        partner = jnp.where(first, pltpu.roll(y, LANES - HD // 2, 1), pltpu.roll(y, HD // 2, 1))
        return y * cos + partner * sin

    gq = gq_ref[...]
    for c in range(NSA_W // LANES):
        sl = slice(c * LANES, (c + 1) * LANES)
        qc = norm_rope(zq_ref[:, sl], gq)
        qtok_ref[:, sl] = qc
        qb = qc.astype(BF16)
        qh_ref[2 * c] = qb[:, :HD]
        qh_ref[2 * c + 1] = qb[:, HD:]

    zin = (zc_ref, zs_ref, zw_ref)
    rows = (rc_ref, rs_ref, rw_ref)
    rows_t = (tc_ref, ts_ref, tw_ref)
    kg = (None, ks_ref, kw_ref)
    vg = (None, vs_ref, vw_ref)
    for br in range(3):
        gk = gk_ref[br:br + 1, :]
        cols = []
        for c in range(KW // LANES):
            cols.append(norm_rope(zin[br][:, c * LANES:(c + 1) * LANES], gk))
        for c in range(KW // LANES):
            cols.append(zin[br][:, KW + c * LANES:KW + (c + 1) * LANES])
        for c, col in enumerate(cols):
            rows[br][:, c * LANES:(c + 1) * LANES] = col
            rows_t[br][c * LANES:(c + 1) * LANES, :] = col.T
        if kg[br] is not None:
            for c in range(KW // LANES):
                kb = cols[c].astype(BF16)
                vb = cols[KW // LANES + c].astype(BF16)
                kg[br][2 * c] = kb[:, :HD]
                kg[br][2 * c + 1] = kb[:, HD:]
                vg[br][2 * c] = vb[:, :HD]
                vg[br][2 * c + 1] = vb[:, HD:]


def _qkv_post(z1, cos_tab, sin_tab, gq, gk, n_prompt, seq):
    n = z1.shape[0]
    tm = 256
    n_ptiles = n_prompt // tm
    tiles_per_seq = seq // tm
    n_slabs = -(-n // seq)

    def tab_map(i):
        return (jnp.where(i < n_ptiles, i % tiles_per_seq, tiles_per_seq), 0)

    row = lambda i: (i, 0)
    head = lambda i: (0, i, 0)
    slab = lambda i: (i // tiles_per_seq, 0, i % tiles_per_seq)
    rows_sd = jax.ShapeDtypeStruct((n, KVW), F32)
    rows_t_sd = jax.ShapeDtypeStruct((n_slabs, KVW, seq), F32)
    grp_sd = jax.ShapeDtypeStruct((KVH, n, HD), BF16)
    out_shape = (
        jax.ShapeDtypeStruct((n, NSA_W), F32),
        jax.ShapeDtypeStruct((HEADS, n, HD), BF16),
        rows_sd, rows_sd, rows_sd,
        rows_t_sd, rows_t_sd, rows_t_sd,
        grp_sd, grp_sd, grp_sd, grp_sd,
    )
    rows_spec = pl.BlockSpec((tm, KVW), row)
    rows_t_spec = pl.BlockSpec((None, KVW, tm), slab)
    grp_spec = pl.BlockSpec((KVH, tm, HD), head)
    out_specs = (
        pl.BlockSpec((tm, NSA_W), row),
        pl.BlockSpec((HEADS, tm, HD), head),
        rows_spec, rows_spec, rows_spec,
        rows_t_spec, rows_t_spec, rows_t_spec,
        grp_spec, grp_spec, grp_spec, grp_spec,
    )
    kv_in = lambda br: pl.BlockSpec((tm, KVW), lambda i: (i, C1_KV // KVW + br))
    return pl.pallas_call(
        _qkv_post_kernel,
        grid=(n // tm,),
        in_specs=[
            pl.BlockSpec((tm, NSA_W), lambda i: (i, C1_Q // NSA_W)),
            kv_in(0), kv_in(1), kv_in(2),
            pl.BlockSpec((tm, LANES), tab_map),
            pl.BlockSpec((tm, LANES), tab_map),
            pl.BlockSpec((1, LANES), lambda i: (0, 0)),
            pl.BlockSpec((3, LANES), lambda i: (0, 0)),
        ],
        out_specs=out_specs,
        out_shape=out_shape,
        compiler_params=_params("parallel"),
        name="qkv_post",
    )(z1, z1, z1, z1, cos_tab, sin_tab, gq, gk)


def _compress_pair(flat, w2, pe8, w2bd, n_ch):
    ab = _dot(flat, w2)
    hp = _dot(pe8.astype(BF16), w2)
    hpe = hp[0:1, :LANES] + hp[1:2, LANES:]
    nxt = pltpu.roll(ab[:, LANES:], n_ch - 1, 0)
    h = ab[:, :LANES] + nxt + hpe
    act = h * jax.nn.sigmoid(h)
    return _dot(act.astype(BF16), w2bd)


def _scatter_chunk_rows(xs_ref, perm, tile_t, c0):
    rpp = tile_t.shape[1] // D_CMP
    xp = _nt(perm, tile_t)
    for t in range(D_CMP):
        xs_ref[t, c0:c0 + rpp, :] = xp[t * rpp:(t + 1) * rpp, :]


def _chunk_flat(xs_ref, p):
    return jnp.concatenate([xs_ref[t, :, p * LANES:(p + 1) * LANES] for t in range(D_CMP)], axis=1).astype(BF16)


def _compress_prompt_kernel(rows_ref, w2k_ref, w2v_ref, pek_ref, pev_ref, bdk_ref, bdv_ref, perm_ref, kc_ref, vc_ref,
                            xs_ref, *, n_ch):
    w2 = (w2k_ref, w2v_ref)
    pe = (pek_ref, pev_ref)
    bd = (bdk_ref, bdv_ref)
    outs = (kc_ref, vc_ref)
    tile_rows = perm_ref.shape[0]
    rpp = tile_rows // D_CMP
    for i in range(n_ch // rpp):
        _scatter_chunk_rows(xs_ref, perm_ref[...], rows_ref[:, i * tile_rows:(i + 1) * tile_rows].astype(BF16), i * rpp)
    for p in range(KVW // LANES):
        kv = p // 2
        flat = _chunk_flat(xs_ref, p)
        res = _compress_pair(flat, w2[kv][...], pe[kv][...], bd[kv][...], n_ch).astype(BF16)
        g0 = 2 * (p % 2)
        outs[kv][0, g0] = res[:, :HD]
        outs[kv][0, g0 + 1] = res[:, HD:]


def _compress_prompt(rows_cmp, cw, batch, seq):
    n_ch = seq // D_CMP
    const = lambda b: (0, 0)
    out_sd = jax.ShapeDtypeStruct((batch, KVH, n_ch, HD), BF16)
    out_spec = pl.BlockSpec((1, KVH, n_ch, HD), lambda b: (b, 0, 0, 0))
    return pl.pallas_call(
        functools.partial(_compress_prompt_kernel, n_ch=n_ch),
        grid=(batch,),
        in_specs=[pl.BlockSpec((None, KVW, seq), lambda b: (b, 0, 0))] + [pl.BlockSpec(w.shape, const) for w in cw],
        out_specs=(out_spec, out_spec),
        out_shape=(out_sd, out_sd),
        scratch_shapes=[pltpu.VMEM((D_CMP, n_ch, KVW), F32)],
        compiler_params=_params("parallel"),
        name="compress_prompt",
    )(rows_cmp, *cw)


def _rank_select(score_t, n_blocks, n_rows, n_top):
    st = score_t[:n_rows]
    jio = lax.broadcasted_iota(jnp.int32, st.shape, 0)
    cnt = jnp.zeros(st.shape, F32)
    for i in range(n_blocks):
        cnt = cnt + _beats(score_t[i:i + 1, :], st, jio, i)
    return jnp.where(cnt < n_top, 1.0, 0.0)


def _beats(row, st, jio, i):
    return jnp.where(row > st, 1.0, jnp.where(row == st, jnp.where(jio > i, 1.0, 0.0), 0.0))


def _pad_rows(x, n):
    if x.shape[0] == n:
        return x
    return jnp.concatenate([x, jnp.zeros((n - x.shape[0], x.shape[1]), x.dtype)], axis=0)


def _cmp_select_kernel(qh_ref, kc_ref, vc_ref, ov_ref, ocmp_ref, sel_ref, st_ref, *, nc, ns, n_top):
    s0 = pl.program_id(1) * Q_BLK
    ncp = kc_ref.shape[2]
    rows = GRP * Q_BLK
    qpos_r = s0 + lax.broadcasted_iota(jnp.int32, (rows, ncp), 0) % Q_BLK
    cidx = lax.broadcasted_iota(jnp.int32, (rows, ncp), 1)
    cvalid = (cidx * D_CMP + (L_CMP - 1) <= qpos_r) & (cidx < nc)
    qpos = s0 + lax.broadcasted_iota(jnp.int32, (Q_BLK, LANES), 0)
    blk = lax.broadcasted_iota(jnp.int32, (Q_BLK, LANES), 1)
    visible = (blk * L_SEL <= qpos) & (blk < ns)
    cur = qpos // L_SEL
    forced = (blk == 0) | (blk == cur) | (blk == cur - 1)
    nsp = -(-ns // 8) * 8
    ov = ov_ref[...]
    for g in range(KVH):
        qg = jnp.concatenate([qh_ref[GRP * g + r] for r in range(GRP)], axis=0)
        s = _nt(qg, kc_ref[0, g]) * (HD ** -0.5)
        s = jnp.where(cvalid, s, NEG)
        m = jnp.max(s, axis=-1, keepdims=True)
        e = jnp.where(cvalid, jnp.exp(s - m), 0.0)
        p = e / jnp.maximum(jnp.sum(e, axis=-1, keepdims=True), 1.0)
        o = _dot(p.astype(BF16), vc_ref[0, g])
        for r in range(GRP):
            ocmp_ref[GRP * g + r] = o[r * Q_BLK:(r + 1) * Q_BLK]
        psum = p[0:Q_BLK]
        for r in range(1, GRP):
            psum = psum + p[r * Q_BLK:(r + 1) * Q_BLK]
        imp = _dot_exact_lhs(psum, ov)
        score = jnp.where(visible, imp + jnp.where(forced, BIG, 0.0), -BIG)
        st_ref[g] = score.T

    n_vis = jnp.minimum((s0 + Q_BLK - 1) // L_SEL + 1, ns)
    jio = lax.broadcasted_iota(jnp.int32, (nsp, LANES), 0)

    def count(i, cnts):
        return tuple(c + _beats(st_ref[g, pl.ds(i, 1), :], st_ref[g, 0:nsp, :], jio, i) for g, c in enumerate(cnts))

    cnts = lax.fori_loop(0, n_vis, count, tuple(jnp.zeros((nsp, LANES), F32) for _ in range(KVH)))
    for g in range(KVH):
        sel = _pad_rows(jnp.where(cnts[g] < n_top, 1.0, 0.0), LANES).T
        sel_ref[:, g * LANES:(g + 1) * LANES] = jnp.where(visible, sel, 0.0).astype(BF16)


def _cmp_select(q_heads, kc, vc, ov, batch, seq):
    nqt = seq // Q_BLK
    n_ch = kc.shape[2]
    ns = seq // L_SEL
    kern = functools.partial(_cmp_select_kernel, nc=n_ch - 1, ns=ns, n_top=min(N_TOP, ns))
    kv_spec = pl.BlockSpec((1, KVH, n_ch, HD), lambda b, t: (b, 0, 0, 0))
    return pl.pallas_call(
        kern,
        grid=(batch, nqt),
        in_specs=[
            pl.BlockSpec((HEADS, Q_BLK, HD), lambda b, t: (0, b * nqt + t, 0)),
            kv_spec, kv_spec,
            pl.BlockSpec(ov.shape, lambda b, t: (0, 0)),
        ],
        out_specs=(
            pl.BlockSpec((HEADS, Q_BLK, HD), lambda b, t: (0, b * nqt + t, 0)),
            pl.BlockSpec((Q_BLK, KVH * LANES), lambda b, t: (b * nqt + t, 0)),
        ),
        out_shape=(
            jax.ShapeDtypeStruct((HEADS, batch * seq, HD), F32),
            jax.ShapeDtypeStruct((batch * seq, KVH * LANES), BF16),
        ),
        scratch_shapes=[pltpu.VMEM((KVH, LANES, LANES), F32)],
        compiler_params=_params("parallel", "parallel"),
        name="cmp_select",
    )(q_heads, kc, vc, ov)


def _sel_win_kernel(qh_ref, ks_ref, vs_ref, kw_ref, vw_ref, sel_ref, e_ref, zbr_ref, ocmp_ref, oa_ref,
                    q_s, *, kt):
    s0 = pl.program_id(1) * Q_BLK
    rows = GRP * Q_BLK
    n_kt = (s0 + Q_BLK + kt - 1) // kt
    qpos = s0 + lax.broadcasted_iota(jnp.int32, (Q_BLK, kt), 0)
    lane_k = lax.broadcasted_iota(jnp.int32, (Q_BLK, kt), 1)
    gates = jax.nn.sigmoid(zbr_ref[...])
    wl = WINDOW + Q_BLK
    wstart = pl.multiple_of(jnp.maximum(s0 - WINDOW, 0), Q_BLK)
    kposw = wstart + lax.broadcasted_iota(jnp.int32, (Q_BLK, wl), 1)
    qposw = s0 + lax.broadcasted_iota(jnp.int32, (Q_BLK, wl), 0)
    wbias = jnp.where((kposw <= qposw) & (kposw > qposw - WINDOW), 0.0, NEG)

    def biased(x, bias):
        return jnp.concatenate([x[r * Q_BLK:(r + 1) * Q_BLK] + bias for r in range(GRP)], axis=0)

    for g in range(KVH):
        q_s[g] = jnp.concatenate([qh_ref[GRP * g + r] for r in range(GRP)], axis=0) * (HD ** -0.5)

    def body(j, carry):
        k0 = pl.multiple_of(j * kt, kt)
        causal = k0 + lane_k <= qpos
        out = []
        for g in range(KVH):
            m, l, acc = carry[g]
            k = ks_ref[g, pl.ds(k0, kt), :]
            v = vs_ref[g, pl.ds(k0, kt), :]
            mf = _dot(sel_ref[:, g * LANES:(g + 1) * LANES], e_ref[j])
            bias = jnp.where((mf > 0.5) & causal, 0.0, NEG)
            s = biased(_nt(q_s[g], k), bias)
            m_new = jnp.maximum(m, jnp.max(s, axis=-1, keepdims=True))
            corr = jnp.exp(m - m_new)
            p = jnp.exp(s - m_new)
            l = l * corr + jnp.sum(p, axis=-1, keepdims=True)
            acc = acc * corr + _dot(p.astype(BF16), v)
            out.append((m_new, l, acc))
        return tuple(out)

    init = tuple((jnp.full((rows, 1), NEG, F32), jnp.zeros((rows, 1), F32), jnp.zeros((rows, HD), F32))
                 for _ in range(KVH))
    res = lax.fori_loop(0, n_kt, body, init)

    for g in range(KVH):
        qg = q_s[g]
        o_sel = res[g][2] / res[g][1]
        kwin = kw_ref[g, pl.ds(wstart, wl), :]
        vwin = vw_ref[g, pl.ds(wstart, wl), :]
        s = biased(_nt(qg, kwin), wbias)
        m = jnp.max(s, axis=-1, keepdims=True)
        e = jnp.exp(s - m)
        den = jnp.maximum(jnp.sum(e, axis=-1, keepdims=True), 1.0)
        o_win = _dot(e.astype(BF16), vwin) / den

        for r in range(GRP):
            h = GRP * g + r
            rs = slice(r * Q_BLK, (r + 1) * Q_BLK)
            o = (gates[:, h:h + 1] * ocmp_ref[h]
                 + gates[:, HEADS + h:HEADS + h + 1] * o_sel[rs]
                 + gates[:, 2 * HEADS + h:2 * HEADS + h + 1] * o_win[rs])
            oa_ref[:, h * HD:(h + 1) * HD] = o


def _sel_win(q_heads, ks, vs, kw, vw, sel, e_mat, z, ocmp, batch, seq):
    nqt = seq // Q_BLK
    kt = 512
    e_mat = e_mat.reshape(LANES, seq // kt, kt).transpose(1, 0, 2)
    kv_spec = pl.BlockSpec((KVH, seq, HD), lambda b, t: (0, b, 0), pipeline_mode=pl.Buffered(1))
    qspec = pl.BlockSpec((HEADS, Q_BLK, HD), lambda b, t: (0, b * nqt + t, 0))
    return pl.pallas_call(
        functools.partial(_sel_win_kernel, kt=kt),
        grid=(batch, nqt),
        in_specs=[
            qspec, kv_spec, kv_spec, kv_spec, kv_spec,
            pl.BlockSpec((Q_BLK, KVH * LANES), lambda b, t: (b * nqt + t, 0)),
            pl.BlockSpec(e_mat.shape, lambda b, t: (0, 0, 0)),
            pl.BlockSpec((Q_BLK, LANES), lambda b, t: (b * nqt + t, C1_BR // LANES)),
            qspec,
        ],
        out_specs=pl.BlockSpec((Q_BLK, NSA_W), lambda b, t: (b * nqt + t, 0)),
        out_shape=jax.ShapeDtypeStruct((z.shape[0], NSA_W), F32),
        scratch_shapes=[pltpu.VMEM((KVH, GRP * Q_BLK, HD), BF16)],
        compiler_params=_params("parallel", "arbitrary"),
        name="sel_win",
    )(q_heads, ks, vs, kw, vw, sel, e_mat, z, ocmp)


def _hgrn_gates(zq, zf, logits):
    mx = jnp.max(logits, axis=0, keepdims=True)
    ex = jnp.exp(logits - mx)
    lb = ex[0:1] / jnp.sum(ex, axis=0, keepdims=True)
    hq = zq * jax.nn.sigmoid(zq)
    log_sig = -(jnp.maximum(-zf, 0.0) + jnp.log1p(jnp.exp(-jnp.abs(zf))))
    a = jnp.log(lb)
    b = jnp.log1p(-lb) + log_sig
    log_f = jnp.maximum(a, b) + jnp.log1p(jnp.exp(-jnp.abs(a - b)))
    hk = (1.0 - lb) * jax.nn.sigmoid(-zf)
    return hq, hk, log_f


def _intra_chunk(ac, qc, kc, vc):
    n = ac.shape[0]
    tio = lax.broadcasted_iota(jnp.int32, ac.shape, 0)
    acc = jnp.zeros(vc.shape, F32)
    for s in range(n):
        d = jnp.exp(ac - ac[s:s + 1, :])
        w = jnp.where(tio >= s, qc * kc[s:s + 1, :] * d, 0.0)
        col = jnp.sum(w, axis=-1, keepdims=True)
        acc = acc + col * vc[s:s + 1, :]
    return acc


def _hgrn_prompt_kernel(zq_ref, zf_ref, zi_ref, lg_ref, tri_ref, full_ref, o_ref, sout_ref,
                        st_ref, a_s, q_s, k_s, qt_s, ktb_s, eal_s, vt_s):
    t = pl.program_id(1)
    tt = zq_ref.shape[0]

    @pl.when(t == 0)
    def _():
        st_ref[...] = jnp.zeros(st_ref.shape, F32)

    hq, hk, log_f = _hgrn_gates(zq_ref[...], zf_ref[...], lg_ref[...])
    a = _dot_exact_rhs(tri_ref[...], log_f)
    al = _dot_exact_rhs(full_ref[...], log_f)
    a_s[...] = a
    q_s[...] = hq
    k_s[...] = hk
    qt_s[...] = hq * jnp.exp(a)
    ktb_s[...] = (hk * jnp.exp(al - a)).astype(BF16)
    eal_s[...] = jnp.exp(al)
    for h in range(HG_H):
        vt_s[h] = zi_ref[:, h * HG_DV:(h + 1) * HG_DV].T.astype(BF16)
    lane_tok = lax.broadcasted_iota(jnp.int32, (HG_DV, tt), 1)

    def chunk(c, carry):
        r0 = pl.multiple_of(c * HG_SUB, HG_SUB)
        in_chunk = (lane_tok >= r0) & (lane_tok < r0 + HG_SUB)
        for h in range(HG_H):
            sl = slice(h * HG_DK, (h + 1) * HG_DK)
            st = st_ref[h]
            o = _nt(qt_s[pl.ds(r0, HG_SUB), sl].astype(BF16), st.astype(BF16))
            o = o + _intra_chunk(a_s[pl.ds(r0, HG_SUB), sl], q_s[pl.ds(r0, HG_SUB), sl],
                                 k_s[pl.ds(r0, HG_SUB), sl], zi_ref[pl.ds(r0, HG_SUB), sl])
            o_ref[pl.ds(r0, HG_SUB), sl] = o
            vtm = jnp.where(in_chunk, vt_s[h], jnp.zeros_like(vt_s[h]))
            st_ref[h] = st * eal_s[pl.ds(r0, 1), sl] + _dot(vtm, ktb_s[:, sl])
        return carry

    lax.fori_loop(0, tt // HG_SUB, chunk, 0)

    @pl.when(t == pl.num_programs(1) - 1)
    def _():
        for h in range(HG_H):
            sout_ref[0, h] = st_ref[h].T


def _hgrn_prompt(z, logits, batch, seq):
    tt = 128
    nt = seq // tt
    sub = np.arange(tt) // HG_SUB
    same = sub[:, None] == sub[None, :]
    tri = jnp.asarray((same & (np.arange(tt)[None, :] <= np.arange(tt)[:, None])).astype(np.float32), BF16)
    full = jnp.asarray(same.astype(np.float32), BF16)
    col = lambda c: (lambda b, t: (b * nt + t, c // HG_W))
    const = lambda b, t: (0, 0)
    return pl.pallas_call(
        _hgrn_prompt_kernel,
        grid=(batch, nt),
        in_specs=[
            pl.BlockSpec((tt, HG_W), col(C2_HQ)),
            pl.BlockSpec((tt, HG_W), col(C2_HF)),
            pl.BlockSpec((tt, HG_W), col(C2_HI)),
            pl.BlockSpec((2, HG_W), const),
            pl.BlockSpec((tt, tt), const),
            pl.BlockSpec((tt, tt), const),
        ],
        out_specs=(
            pl.BlockSpec((tt, HG_W), lambda b, t: (b * nt + t, 0)),
            pl.BlockSpec((1, HG_H, HG_DK, HG_DV), lambda b, t: (b, 0, 0, 0)),
        ),
        out_shape=(
            jax.ShapeDtypeStruct((z.shape[0], HG_W), F32),
            jax.ShapeDtypeStruct((batch, HG_H, HG_DK, HG_DV), F32),
        ),
        scratch_shapes=[
            pltpu.VMEM((HG_H, HG_DV, HG_DK), F32),
            pltpu.VMEM((tt, HG_W), F32),
            pltpu.VMEM((tt, HG_W), F32),
            pltpu.VMEM((tt, HG_W), F32),
            pltpu.VMEM((tt, HG_W), F32),
            pltpu.VMEM((tt, HG_W), BF16),
            pltpu.VMEM((tt, HG_W), F32),
            pltpu.VMEM((HG_H, HG_DV, tt), BF16),
        ],
        compiler_params=_params("parallel", "arbitrary"),
        name="hgrn_prompt",
    )(z, z, z, logits, tri, full)


def _hgrn_sample_kernel(zq_ref, zf_ref, zi_ref, lg_ref, s0_ref, ob_hbm_ref, o_ref, sout_ref, *, s_len):
    del ob_hbm_ref
    n_seq = s0_ref.shape[0]
    hq_all, hk_all, logf_all = _hgrn_gates(zq_ref[...], zf_ref[...], lg_ref[...])
    rio = lax.broadcasted_iota(jnp.int32, (s_len, HG_W), 0)
    row0 = lax.broadcasted_iota(jnp.int32, (s_len, HG_DK), 0) == 0
    pad = HG_DK - 2 * s_len
    for q in range(n_seq):
        rs = slice(q * s_len, (q + 1) * s_len)
        hq, hk, log_f = hq_all[rs], hk_all[rs], logf_all[rs]
        a = log_f
        d = 1
        while d < s_len:
            a = a + jnp.where(rio >= d, pltpu.roll(a, d, 0), 0.0)
            d *= 2
        a_last = jnp.sum(log_f, axis=0, keepdims=True)
        qt = hq * jnp.exp(a)
        kt = hk * jnp.exp(a_last - a)
        ea = jnp.exp(a_last)
        hv = zi_ref[rs, :]
        for h in range(HG_H):
            sl = slice(h * HG_DK, (h + 1) * HG_DK)
            s_old = s0_ref[q, h]
            qt16 = _pad_rows(qt[:, sl], 16).astype(BF16)
            o = _dot(qt16, s_old.astype(BF16))[:s_len]
            o = o + _intra_chunk(a[:, sl], hq[:, sl], hk[:, sl], hv[:, sl])
            o_ref[rs, sl] = o
            m = jnp.concatenate([kt[:, sl], jnp.where(row0, jnp.broadcast_to(ea[:, sl], (s_len, HG_DK)), 0.0),
                                 jnp.zeros((pad, HG_DK), F32)], axis=0)
            mt = m.T
            e_col = mt[:, s_len:s_len + 1]
            vpad = _pad_rows(hv[:, sl], HG_DK).astype(BF16)
            sout_ref[q, h] = e_col * s_old + _dot(mt.astype(BF16), vpad)


def _hgrn_sample(z, logits, state, ob, n_prompt, dec_batch, s_len):
    n_seq = 4
    rows = n_seq * s_len
    base = n_prompt // rows
    col = lambda c: (lambda b: (base + b, c // HG_W))
    st_spec = pl.BlockSpec((n_seq, HG_H, HG_DK, HG_DV), lambda b: (b, 0, 0, 0))
    return pl.pallas_call(
        functools.partial(_hgrn_sample_kernel, s_len=s_len),
        grid=(dec_batch // n_seq,),
        in_specs=[
            pl.BlockSpec((rows, HG_W), col(C2_HQ)),
            pl.BlockSpec((rows, HG_W), col(C2_HF)),
            pl.BlockSpec((rows, HG_W), col(C2_HI)),
            pl.BlockSpec((2, HG_W), lambda b: (0, 0)),
            st_spec,
            pl.BlockSpec(memory_space=pl.ANY),
        ],
        out_specs=(pl.BlockSpec((rows, HG_W), lambda b: (base + b, 0)), st_spec),
        out_shape=(
            jax.ShapeDtypeStruct(ob.shape, F32),
            jax.ShapeDtypeStruct(state.shape, F32),
        ),
        input_output_aliases={5: 0},
        compiler_params=_params("parallel"),
        name="hgrn_sample",
    )(z, z, z, logits, state, ob)


def _sample_nsa_kernel(pt_ref, *refs, npg, s_len, past):
    del pt_ref
    cmp_pages = refs[:npg]
    sel_pages = refs[npg:2 * npg]
    (win_ref, q_ref, rsel_ref, rwin_ref, zbr_ref, w2k_ref, w2v_ref, pek_ref, pev_ref, bdk_ref, bdv_ref, perm_ref,
     ov_ref, e_ref, oa_hbm_ref, oa_ref, swin_ref, xs_ref, kcv_ref) = refs[2 * npg:]
    del oa_hbm_ref
    step = pl.program_id(0)

    @pl.when(step == 0)
    def _():
        kcv_ref[...] = jnp.zeros(kcv_ref.shape, BF16)

    kc_all = kcv_ref[(step + 1) % 2, 0]
    vc_all = kcv_ref[(step + 1) % 2, 1]
    page = cmp_pages[0].shape[1]
    rpp = page // D_CMP
    n_ch = npg * rpp
    nc = n_ch - 1
    wb = win_ref.shape[1]
    ns = -(-(past + s_len) // L_SEL)
    nsp = -(-ns // 8) * 8
    n_top = min(N_TOP, ns)
    rows = HEADS * s_len

    w2 = (w2k_ref, w2v_ref)
    pe = (pek_ref, pev_ref)
    bd = (bdk_ref, bdv_ref)
    comp = []

    def stage1_scatter(lo, hi):
        for i in range(lo, hi):
            _scatter_chunk_rows(xs_ref, perm_ref[...], cmp_pages[i][...].astype(BF16), i * rpp)

    def stage1_compress(p):
        kv = p // 2
        comp.append(_compress_pair(_chunk_flat(xs_ref, p), w2[kv][...], pe[kv][...], bd[kv][...], n_ch))

    q = q_ref[...]
    lane8 = lax.broadcasted_iota(jnp.int32, (s_len, LANES), 1)
    zero8 = jnp.zeros((s_len, LANES), F32)
    blocks = []
    for g in range(KVH):
        for r in range(GRP):
            h = GRP * g + r
            src = q[:, (h // 2) * LANES:(h // 2 + 1) * LANES]
            if (r % 2) != (g % 2):
                src = pltpu.roll(src, HD, 1)
            keep = (lane8 < HD) if g % 2 == 0 else (lane8 >= HD)
            piece = jnp.where(keep, src, 0.0)
            blocks.append(jnp.concatenate([piece, zero8] if g // 2 == 0 else [zero8, piece], axis=1))
    qbd = (jnp.concatenate(blocks, axis=0) * (HD ** -0.5)).astype(BF16)

    def rowpos(shape):
        return past + lax.broadcasted_iota(jnp.int32, shape, 0) % s_len

    stage1_scatter(0, npg // 2)

    s = _nt(qbd, kc_all)
    cidx = lax.broadcasted_iota(jnp.int32, s.shape, 1)
    cvalid = (cidx * D_CMP + (L_CMP - 1) <= rowpos(s.shape)) & (cidx < nc)
    s = jnp.where(cvalid, s, NEG)
    m = jnp.max(s, axis=-1, keepdims=True)
    e = jnp.where(cvalid, jnp.exp(s - m), 0.0)
    p = e / jnp.maximum(jnp.sum(e, axis=-1, keepdims=True), 1.0)
    ocmp = _dot(p.astype(BF16), vc_all)

    stage1_scatter(npg // 2, npg)

    gs = GRP * s_len
    psum = []
    for g in range(KVH):
        acc = p[g * gs:g * gs + s_len]
        for r in range(1, GRP):
            acc = acc + p[g * gs + r * s_len:g * gs + (r + 1) * s_len]
        psum.append(acc)
    imp = _dot_exact_lhs(jnp.concatenate(psum, axis=0), ov_ref[...])
    qpos = rowpos(imp.shape)
    blk = lax.broadcasted_iota(jnp.int32, imp.shape, 1)
    visible = (blk * L_SEL <= qpos) & (blk < ns)
    cur = qpos // L_SEL
    forced = (blk == 0) | (blk == cur) | (blk == cur - 1)
    score = jnp.where(visible, imp + jnp.where(forced, BIG, 0.0), -BIG)
    sel_t = _rank_select(_pad_rows(score, LANES).T, ns, nsp, n_top)
    sel = _pad_rows(sel_t, LANES).T[:KVH * s_len]
    sel = jnp.where(visible, sel, 0.0)
    sel_rows = jnp.concatenate([sel[g * s_len:(g + 1) * s_len] for g in range(KVH) for _ in range(GRP)], axis=0)

    stage1_compress(0)

    zpad = jnp.zeros((page - s_len, KW), F32)
    knew = jnp.concatenate([rsel_ref[:, :KW], zpad], axis=0).astype(BF16)
    vnew = jnp.concatenate([rsel_ref[:, KW:], zpad], axis=0).astype(BF16)
    def page_pair(i, lo):
        return jnp.concatenate([sel_pages[2 * i][lo:lo + KW, :], sel_pages[2 * i + 1][lo:lo + KW, :]],
                               axis=1).astype(BF16)

    s = jnp.concatenate([_dot(qbd, page_pair(i, 0)) for i in range(npg // 2)] + [_nt(qbd, knew)], axis=1)
    mf = _dot(sel_rows.astype(BF16), e_ref[...])
    kpos = lax.broadcasted_iota(jnp.int32, s.shape, 1)
    mask = (mf > 0.5) & (kpos <= rowpos(s.shape))
    s = jnp.where(mask, s, NEG)
    m = jnp.max(s, axis=-1, keepdims=True)
    e = jnp.where(mask, jnp.exp(s - m), 0.0)
    l = jnp.sum(e, axis=-1, keepdims=True)
    eb = e.astype(BF16)
    acc = _dot(eb[:, npg * page:], vnew)
    for i in range(npg // 2):
        acc = acc + _nt(eb[:, 2 * i * page:(2 * i + 2) * page], page_pair(i, KW))
    osel = acc / l

    stage1_compress(1)

    kwn = jnp.concatenate([rwin_ref[:, :KW], zpad], axis=0).astype(BF16)
    vwn = jnp.concatenate([rwin_ref[:, KW:], zpad], axis=0).astype(BF16)
    s = jnp.concatenate([_dot(qbd, win_ref[:KW, :].astype(BF16)), _nt(qbd, kwn)], axis=1)
    kposw = (past - wb) + lax.broadcasted_iota(jnp.int32, s.shape, 1)
    qposw = rowpos(s.shape)
    wmask = (kposw <= qposw) & (kposw > qposw - WINDOW) & (kposw >= 0)
    s = jnp.where(wmask, s, NEG)
    m = jnp.max(s, axis=-1, keepdims=True)
    e = jnp.where(wmask, jnp.exp(s - m), 0.0)
    den = jnp.maximum(jnp.sum(e, axis=-1, keepdims=True), 1.0)
    eb = e.astype(BF16)
    owin = (_nt(eb[:, :wb], win_ref[KW:, :].astype(BF16)) + _dot(eb[:, wb:], vwn)) / den

    stage1_compress(2)

    gates = jax.nn.sigmoid(zbr_ref[...])
    for g in range(KVH):
        gl = slice((g // 2) * LANES, (g // 2 + 1) * LANES)
        for k2 in range(2):
            halves = []
            for r in (2 * k2, 2 * k2 + 1):
                h = GRP * g + r
                rs = slice(h * s_len, (h + 1) * s_len)
                val = (gates[:, h:h + 1] * ocmp[rs, gl]
                       + gates[:, HEADS + h:HEADS + h + 1] * osel[rs, gl]
                       + gates[:, 2 * HEADS + h:2 * HEADS + h + 1] * owin[rs, gl])
                if (r % 2) != (g % 2):
                    val = pltpu.roll(val, HD, 1)
                halves.append(val)
            c = 2 * g + k2
            oa_ref[:, c * LANES:(c + 1) * LANES] = jnp.where(lane8 < HD, halves[0], halves[1])

    new_t = [_pad_rows(rwin_ref[:, c * LANES:(c + 1) * LANES], LANES).T for c in range(KVW // LANES)]
    new_t = jnp.concatenate(new_t, axis=0)
    lane_w = lax.broadcasted_iota(jnp.int32, (KVW, LANES), 1)
    n_col = wb // LANES
    rolled = [pltpu.roll(win_ref[:, c * LANES:(c + 1) * LANES], LANES - s_len, 1) for c in range(n_col)]
    rolled.append(pltpu.roll(new_t, LANES - s_len, 1))
    for c in range(n_col):
        swin_ref[:, c * LANES:(c + 1) * LANES] = jnp.where(lane_w < LANES - s_len, rolled[c], rolled[c + 1])

    stage1_compress(3)
    kcv_ref[step % 2, 0] = jnp.concatenate(comp[0:2], axis=1).astype(BF16)
    kcv_ref[step % 2, 1] = jnp.concatenate(comp[2:4], axis=1).astype(BF16)


def _sample_nsa(page_table, cache_cmp, cache_sel, cache_win, q_tok, rows_sel, rows_win, z, cw, ov, e_mat, oa,
                n_prompt, dec_batch, s_len):
    npg = page_table.shape[1]
    page = cache_cmp.shape[2]
    past = npg * page
    wb = cache_win.shape[2]
    base = n_prompt // s_len
    last = dec_batch - 1
    seq1 = lambda b: jnp.minimum(b, last)
    seq2 = lambda b: jnp.maximum(b - 1, 0)
    cmp_specs = [pl.BlockSpec((None, KVW, page), (lambda b, pt, i=i: (pt[seq1(b), i], 0, 0))) for i in range(npg)]
    sel_specs = [pl.BlockSpec((None, KVW, page), (lambda b, pt, i=i: (pt[seq2(b), i], 0, 0))) for i in range(npg)]
    row = lambda b, pt: (base + seq2(b), 0)
    const = lambda b, pt: (0, 0)
    in_specs = (
        cmp_specs + sel_specs
        + [pl.BlockSpec((None, KVW, wb), lambda b, pt: (seq2(b), 0, 0)),
           pl.BlockSpec((s_len, NSA_W), row),
           pl.BlockSpec((s_len, KVW), row),
           pl.BlockSpec((s_len, KVW), row),
           pl.BlockSpec((s_len, LANES), lambda b, pt: (base + seq2(b), C1_BR // LANES))]
        + [pl.BlockSpec(w.shape, const) for w in cw]
        + [pl.BlockSpec(ov.shape, const), pl.BlockSpec(e_mat.shape, const), pl.BlockSpec(memory_space=pl.ANY)]
    )
    n_ch = past // D_CMP
    grid_spec = pltpu.PrefetchScalarGridSpec(
        num_scalar_prefetch=1,
        grid=(dec_batch + 1,),
        in_specs=in_specs,
        out_specs=(
            pl.BlockSpec((s_len, NSA_W), row),
            pl.BlockSpec((None, KVW, wb), lambda b, pt: (seq2(b), 0, 0)),
        ),
        scratch_shapes=[
            pltpu.VMEM((D_CMP, n_ch, KVW), F32),
            pltpu.VMEM((2, 2, n_ch, KW), BF16),
        ],
    )
    args = (page_table, *([cache_cmp] * npg), *([cache_sel] * npg), cache_win, q_tok, rows_sel, rows_win, z, *cw,
            ov, e_mat, oa)
    return pl.pallas_call(
        functools.partial(_sample_nsa_kernel, npg=npg, s_len=s_len, past=past),
        grid_spec=grid_spec,
        out_shape=(
            jax.ShapeDtypeStruct(oa.shape, F32),
            jax.ShapeDtypeStruct((dec_batch, KVW, wb), F32),
        ),
        input_output_aliases={len(args) - 1: 0},
        compiler_params=_params("arbitrary"),
        name="sample_nsa",
    )(*args)


def _mix_kernel(xp_ref, xs_ref, oa_ref, ob_ref, zhg_ref, ga_ref, gb_ref, gn_ref, wa_ref, wb_ref, wo_ref, o_ref, *, npt):
    ya = _dot(oa_ref[...].astype(BF16), wa_ref[...])
    gn = gn_ref[...]
    parts = []
    for h in range(HG_H):
        sl = slice(h * HG_DV, (h + 1) * HG_DV)
        ob = ob_ref[:, sl]
        ms = jnp.mean(ob * ob, axis=-1, keepdims=True)
        zg = zhg_ref[:, sl]
        y = ((ob * lax.rsqrt(ms + EPS)) * gn) * (zg * jax.nn.sigmoid(zg))
        parts.append(y.astype(BF16))
    yb = _dot(jnp.concatenate(parts, axis=1), wb_ref[...])
    mix = jax.nn.sigmoid(ga_ref[...]) * ya + jax.nn.sigmoid(gb_ref[...]) * yb
    y = _dot(mix.astype(BF16), wo_ref[...])

    @pl.when(pl.program_id(0) < npt)
    def _():
        o_ref[...] = xp_ref[...] + y

    @pl.when(pl.program_id(0) >= npt)
    def _():
        o_ref[...] = xs_ref[...] + y


def _mix(xp, xs, oa, ob, z, gn, wa, wb, wo):
    (n_p, d), n_s = xp.shape, xs.shape[0]
    n = n_p + n_s
    tm = 256
    npt, spec_p, spec_s = _two_group_specs(n_p, n_s, tm, d)
    row = lambda i: (i, 0)
    const = lambda i: (0, 0)
    resident = lambda w: pl.BlockSpec(w.shape, const, pipeline_mode=pl.Buffered(1))
    return pl.pallas_call(
        functools.partial(_mix_kernel, npt=npt),
        grid=(n // tm,),
        in_specs=[
            spec_p, spec_s,
            pl.BlockSpec((tm, NSA_W), row),
            pl.BlockSpec((tm, HG_W), row),
            pl.BlockSpec((tm, HG_W), lambda i: (i, C2_HG // HG_W)),
            pl.BlockSpec((tm, d), lambda i: (i, C2_GA // D_MODEL)),
            pl.BlockSpec((tm, d), lambda i: (i, C2_GB // D_MODEL)),
            pl.BlockSpec((1, HG_DV), const),
            resident(wa), resident(wb), resident(wo),
        ],
        out_specs=pl.BlockSpec((tm, d), row),
        out_shape=jax.ShapeDtypeStruct((n, d), F32),
        compiler_params=_params("parallel"),
        name="mix_out",
    )(xp, xs, oa, ob, z, z, z, gn, wa, wb, wo)


def _ffn_kernel(x_ref, g_ref, w1_ref, w2_ref, op_ref, os_ref, xn_ref, acc_ref, *, npt):
    i = pl.program_id(0)
    j = pl.program_id(1)
    last = pl.num_programs(1) - 1

    @pl.when(j == 0)
    def _():
        xn_ref[...] = _rmsnorm_bf16(x_ref[...], g_ref[...])
        acc_ref[...] = jnp.zeros(acc_ref.shape, F32)

    h = _dot(xn_ref[...], w1_ref[...])
    h = jnp.square(jnp.maximum(h, 0.0)).astype(BF16)
    acc_ref[...] += _dot(h, w2_ref[...])

    @pl.when((j == last) & (i < npt))
    def _():
        op_ref[...] = x_ref[...] + acc_ref[...]

    @pl.when((j == last) & (i >= npt))
    def _():
        os_ref[...] = x_ref[...] + acc_ref[...]


def _ffn(x, g, w1, w2, n_p):
    n, d = x.shape
    n_s = n - n_p
    dff = w1.shape[1]
    tm = _pick_tile(np.gcd(n_p, n_s), (512, 256))
    tf = 1024
    npt, spec_p, spec_s = _two_group_specs(n_p, n_s, tm, d)
    return pl.pallas_call(
        functools.partial(_ffn_kernel, npt=npt),
        grid=(n // tm, dff // tf),
        in_specs=[
            pl.BlockSpec((tm, d), lambda i, j: (i, 0)),
            pl.BlockSpec((1, d), lambda i, j: (0, 0)),
            pl.BlockSpec((d, tf), lambda i, j: (0, j)),
            pl.BlockSpec((tf, d), lambda i, j: (j, 0)),
        ],
        out_specs=(spec_p, spec_s),
        out_shape=(jax.ShapeDtypeStruct((n_p, d), F32), jax.ShapeDtypeStruct((n_s, d), F32)),
        scratch_shapes=[pltpu.VMEM((tm, d), BF16), pltpu.VMEM((tm, d), F32)],
        compiler_params=_params("arbitrary", "arbitrary"),
        name="ffn",
    )(x, g, w1, w2)


def _split_w_in(w_in):
    n1 = sum(_SIZES[:3])
    w1 = jnp.pad(w_in[:, :n1].astype(BF16), ((0, 0), (0, N_Z1 - n1)))
    w2 = w_in[:, n1:].astype(BF16)
    assert w2.shape[1] == N_Z2
    return w1, w2


def _compress_weights(cmp_pe, cmp_w1, cmp_w2):
    m = L_CMP // D_CMP
    eye2 = jnp.eye(2, dtype=F32)
    out = []
    w2s, pes, bds = [], [], []
    for kv in range(2):
        w1p = cmp_w1[kv].reshape(m, D_CMP, HD, HD).transpose(1, 2, 0, 3)
        w = w1p[:, None, :, :, None, :] * eye2[None, :, None, None, :, None]
        w2s.append(w.reshape(D_CMP * LANES, m * LANES).astype(BF16))
        pe = cmp_pe[kv].reshape(m, D_CMP, 1, HD)
        pe = jnp.broadcast_to(pe, (m, D_CMP, 2, HD)).reshape(m, D_CMP * LANES)
        pes.append(jnp.concatenate([pe, jnp.zeros((8 - m, D_CMP * LANES), F32)], axis=0))
        bds.append(jnp.kron(eye2, cmp_w2[kv]).astype(BF16))
    tile_rows = LANES
    rpp = tile_rows // D_CMP
    src = np.arange(tile_rows)
    perm = np.zeros((tile_rows, tile_rows), np.float32)
    perm[(src % D_CMP) * rpp + src // D_CMP, src] = 1.0
    return (w2s[0], w2s[1], pes[0], pes[1], bds[0], bds[1], jnp.asarray(perm, BF16))


def _overlap_matrix(nc_pad, ns):
    cs = np.arange(nc_pad) * D_CMP
    ss = np.arange(LANES) * L_SEL
    ov = np.clip(np.minimum(cs[:, None] + L_CMP, ss[None, :] + L_SEL) - np.maximum(cs[:, None], ss[None, :]), 0, None)
    ov = (ov / D_CMP) * (np.arange(LANES)[None, :] < ns)
    return jnp.asarray(ov.astype(np.float32), BF16)


def _expand_matrix(n_keys):
    e = (np.arange(n_keys)[None, :] // L_SEL) == np.arange(LANES)[:, None]
    return jnp.asarray(e.astype(np.float32), BF16)


def _rope_tables(seq, past, s_len, tm):
    half = HD // 2
    inv = THETA ** (-jnp.arange(half, dtype=F32) * 2.0 / HD)
    pos = jnp.concatenate([jnp.arange(seq), past + jnp.arange(tm) % s_len]).astype(F32)
    ang = pos[:, None] * inv[None, :]
    cos = jnp.cos(ang)
    sin = jnp.sin(ang)
    cos_t = jnp.concatenate([cos, cos, cos, cos], axis=1)
    sin_t = jnp.concatenate([-sin, sin, -sin, sin], axis=1)
    return cos_t, sin_t


def kernel(x_prompt, x_sample, cache_cmp_kv, cache_sel_kv, cache_win_kv, state_hgrn, page_table, norm1_g, w_in,
           q_norm_g, k_norm_g, cmp_pe, cmp_w1, cmp_w2, w_proj_a, hgrn_lb_logits, hgrn_norm_g, w_proj_b, w_out,
           norm2_g, w_ff1, w_ff2):
    assert w_in.shape[0] == 1, "single-layer step"
    batch, seq, d = x_prompt.shape
    dec_batch, s_len, _ = x_sample.shape
    n_prompt = batch * seq
    n_sample = dec_batch * s_len
    npg = page_table.shape[1]
    page = cache_cmp_kv.shape[2]
    past = npg * page
    wb = cache_win_kv.shape[2]
    assert seq % Q_BLK == 0 and seq >= WINDOW + Q_BLK and n_prompt % 256 == 0 and n_sample % 256 == 0
    assert s_len % 8 == 0 and page % D_CMP == 0 and 2 * s_len <= HG_DK and dec_batch % 4 == 0 and npg % 2 == 0

    xp = x_prompt.reshape(n_prompt, d)
    xs = x_sample.reshape(n_sample, d)

    w_z1, w_z2 = _split_w_in(w_in[0])
    cw = _compress_weights(cmp_pe[0], cmp_w1[0], cmp_w2[0])
    cos_t, sin_t = _rope_tables(seq, past, s_len, 256)
    gq = jnp.tile(q_norm_g[0][None, :], (1, 2))
    gk = jnp.tile(k_norm_g[0], (1, 2))
    logits = hgrn_lb_logits.astype(F32)

    def token_minor(c):
        return jnp.transpose(c, (0, 2, 3, 4, 1)).reshape(c.shape[0], KVW, c.shape[1])

    def token_major(c_t):
        return jnp.transpose(c_t.reshape(c_t.shape[0], 2, KVH, HD, c_t.shape[2]), (0, 4, 1, 2, 3))

    z1 = _in_proj(xp, xs, norm1_g, w_z1, 896)
    z2 = _in_proj(xp, xs, norm1_g, w_z2, 1024)
    (q_tok, q_heads, rows_cmp, rows_sel, rows_win, rt_cmp, rt_sel, rt_win,
     ks, vs, kw, vw) = _qkv_post(z1, cos_t, sin_t, gq, gk, n_prompt, seq)

    kc, vc = _compress_prompt(rt_cmp, cw, batch, seq)
    ov_p = _overlap_matrix(seq // D_CMP, seq // L_SEL)
    ocmp, sel = _cmp_select(q_heads, kc, vc, ov_p, batch, seq)
    oa = _sel_win(q_heads, ks, vs, kw, vw, sel, _expand_matrix(seq), z1, ocmp, batch, seq)
    ob, p_state = _hgrn_prompt(z2, logits, batch, seq)

    ns_s = -(-(past + s_len) // L_SEL)
    ov_s = _overlap_matrix(past // D_CMP, ns_s)
    oa, s_win_t = _sample_nsa(
        page_table, token_minor(cache_cmp_kv[0]), token_minor(cache_sel_kv[0]), token_minor(cache_win_kv[0]),
        q_tok, rows_sel, rows_win, z1, cw, ov_s, _expand_matrix(past + page), oa, n_prompt, dec_batch, s_len)
    ob, s_state = _hgrn_sample(z2, logits, state_hgrn[0], ob, n_prompt, dec_batch, s_len)

    x2 = _mix(xp, xs, oa, ob, z2, hgrn_norm_g, w_proj_a[0].astype(BF16), w_proj_b[0].astype(BF16),
              w_out[0].astype(BF16))
    y_p, y_s = _ffn(x2, norm2_g, w_ff1[0].astype(BF16), w_ff2[0].astype(BF16), n_prompt)

    kv_shape = (2, KVH, HD)
    wp = min(WINDOW, seq)
    return (
        y_p.reshape(batch, seq, d),
        y_s.reshape(dec_batch, s_len, d),
        token_major(rt_cmp[:batch])[None],
        token_major(rt_sel[:batch])[None],
        token_major(rt_win[:batch, :, seq - wp:])[None],
        p_state[None],
        rows_cmp[n_prompt:].reshape((1, dec_batch, s_len) + kv_shape),
        rows_sel[n_prompt:].reshape((1, dec_batch, s_len) + kv_shape),
        token_major(s_win_t)[None],
        s_state[None],
    )
```

```python
import functools

import numpy as np
import jax
import jax.numpy as jnp
from jax import lax
from jax.experimental import pallas as pl
from jax.experimental.pallas import tpu as pltpu

F32 = jnp.float32
BF16 = jnp.bfloat16

D_MODEL = 2048
HEADS = 16
KVH = 4
GRP = HEADS // KVH
HD = 64
NSA_W = HEADS * HD
L_CMP = 32
D_CMP = 16
L_SEL = 64
N_TOP = 16
WINDOW = 512
Q_BLK = 128
THETA = 10000.0
HG_H = 8
HG_DK = 128
HG_DV = 128
HG_W = HG_H * HG_DK
HG_SUB = 16
SEL_KT = 512
EPS = 1e-6
NEG = -1e30
BIG = 1e9

LANES = 128
KVW = 2 * KVH * HD
KW = KVH * HD

_SIZES = (NSA_W, 6 * KVH * HD, 3 * HEADS, HG_W, HG_W, HG_W, HG_W, D_MODEL, D_MODEL)
C1_Q, C1_KV, C1_BR, N_Z1 = 0, 1024, 2560, 2688
C2_HQ, C2_HF, C2_HI, C2_HG, C2_GA, C2_GB, N_Z2 = 0, 1024, 2048, 3072, 4096, 6144, 8192

VMEM_LIMIT = 56 * 1024 * 1024


def _params(*sem):
    return pltpu.CompilerParams(dimension_semantics=sem, vmem_limit_bytes=VMEM_LIMIT)


def _nt(a, b):
    return lax.dot_general(a, b, (((1,), (1,)), ((), ())), preferred_element_type=F32)


def _dot(a, b):
    return jnp.dot(a, b, preferred_element_type=F32)


def _split3(x):
    hi = x.astype(BF16)
    r1 = x - hi.astype(F32)
    mid = r1.astype(BF16)
    lo = (r1 - mid.astype(F32)).astype(BF16)
    return hi, mid, lo


def _dot_exact_rhs(m, x):
    hi, mid, lo = _split3(x)
    return _dot(m, hi) + _dot(m, mid) + _dot(m, lo)


def _dot_exact_lhs(x, m):
    hi, mid, lo = _split3(x)
    return _dot(hi, m) + _dot(mid, m) + _dot(lo, m)


def _pick_tile(n, cands):
    for c in cands:
        if n % c == 0:
            return c
    raise ValueError(f"no tile in {cands} divides {n}")


def _rmsnorm_bf16(x, g):
    ms = jnp.mean(x * x, axis=-1, keepdims=True)
    return ((x * lax.rsqrt(ms + EPS)) * g).astype(BF16)


def _two_group_specs(n_p, n_s, tm, width, sample_kwargs=None):
    npt = n_p // tm
    assert n_p % tm == 0 and n_s % tm == 0
    spec_p = pl.BlockSpec((tm, width), lambda i, *_: (jnp.minimum(i, npt - 1), 0))
    spec_s = pl.BlockSpec((tm, width), lambda i, *_: (jnp.maximum(i - npt, 0), 0), **(sample_kwargs or {}))
    return npt, spec_p, spec_s


def _rms_matmul_kernel(xp_ref, xs_ref, g_ref, w_ref, o_ref, xn_ref, *, npt):
    i = pl.program_id(0)

    @pl.when((pl.program_id(1) == 0) & (i < npt))
    def _():
        xn_ref[...] = _rmsnorm_bf16(xp_ref[...], g_ref[...])

    @pl.when((pl.program_id(1) == 0) & (i >= npt))
    def _():
        xn_ref[...] = _rmsnorm_bf16(xs_ref[...], g_ref[...])

    o_ref[...] = _dot(xn_ref[...], w_ref[...].astype(BF16))


def _in_proj(xp, xs, g, w, nout, tn):
    (n_p, d), n_s = xp.shape, xs.shape[0]
    n = n_p + n_s
    assert nout % tn == 0 and nout <= -(-w.shape[1] // tn) * tn
    tm = _pick_tile(np.gcd(n_p, n_s), (1024, 512, 256))
    npt, spec_p, spec_s = _two_group_specs(n_p, n_s, tm, d, dict(pipeline_mode=pl.Buffered(1)))
    return pl.pallas_call(
        functools.partial(_rms_matmul_kernel, npt=npt),
        grid=(n // tm, nout // tn),
        in_specs=[
            spec_p, spec_s,
            pl.BlockSpec((1, d), lambda i, j: (0, 0)),
            pl.BlockSpec((d, tn), lambda i, j: (0, j)),
        ],
        out_specs=pl.BlockSpec((tm, tn), lambda i, j: (i, j)),
        out_shape=jax.ShapeDtypeStruct((n, nout), F32),
        scratch_shapes=[pltpu.VMEM((tm, d), BF16)],
        compiler_params=_params("parallel", "arbitrary"),
        name="in_proj",
    )(xp, xs, g, w)


def _qkv_post_kernel(zq_ref, zc_ref, zs_ref, zw_ref, cos_ref, sin_ref, gq_ref, gk_ref,
                     qtok_ref, qh_ref, qt_ref, rc_ref, rs_ref, rw_ref, tc_ref, ts_ref, tw_ref,
                     ks_ref, vs_ref, kw_ref, vw_ref):
    cos = cos_ref[...]
    sin = sin_ref[...]
    lane = lax.broadcasted_iota(jnp.int32, cos.shape, 1)
    first = (lane % HD) < (HD // 2)
    hi_ = lax.broadcasted_iota(jnp.int32, (LANES, LANES), 0) // HD
    hj_ = lax.broadcasted_iota(jnp.int32, (LANES, LANES), 1) // HD
    same_head = jnp.where(hi_ == hj_, 1.0, 0.0).astype(BF16)

    def norm_rope(xc, g_row):
        ms = _dot_exact_lhs(xc * xc, same_head) * (1.0 / HD)
        y = (xc * lax.rsqrt(ms + EPS)) * g_row
        partner = jnp.where(first, pltpu.roll(y, LANES - HD // 2, 1), pltpu.roll(y, HD // 2, 1))
        return y * cos + partner * sin

    gq = gq_ref[...]
    for c in range(NSA_W // LANES):
        sl = slice(c * LANES, (c + 1) * LANES)
        qc = norm_rope(zq_ref[:, sl], gq)
        qtok_ref[:, sl] = qc
        qb = qc.astype(BF16)
        qh_ref[2 * c] = qb[:, :HD]
        qh_ref[2 * c + 1] = qb[:, HD:]
        qtb = qc.T.astype(BF16)
        qt_ref[2 * c] = qtb[:HD]
        qt_ref[2 * c + 1] = qtb[HD:]

    tm = zq_ref.shape[0]
    zin = (zc_ref, zs_ref, zw_ref)
    rows = (rc_ref, rs_ref, rw_ref)
    rows_t = (tc_ref, ts_ref, tw_ref)
    kg = (None, ks_ref, kw_ref)
    vg = (None, vs_ref, vw_ref)
    for br in range(3):
        gk = gk_ref[br:br + 1, :]
        cols = []
        for c in range(KW // LANES):
            cols.append(norm_rope(zin[br][:, c * LANES:(c + 1) * LANES], gk))
        for c in range(KW // LANES):
            cols.append(zin[br][:, KW + c * LANES:KW + (c + 1) * LANES])
        cols_t = [col.T for col in cols]
        for c, col in enumerate(cols):
            rows[br][:, c * LANES:(c + 1) * LANES] = col
            rows_t[br][c * LANES:(c + 1) * LANES, :] = cols_t[c]
        if kg[br] is not None:
            wt = vg[br].shape[-1]
            for c in range(KW // LANES):
                kb = cols[c].astype(BF16)
                kg[br][2 * c] = kb[:, :HD]
                kg[br][2 * c + 1] = kb[:, HD:]
                vtb = cols_t[KW // LANES + c].astype(BF16)
                for t in range(tm // wt):
                    vg[br][2 * c, t] = vtb[:HD, t * wt:(t + 1) * wt]
                    vg[br][2 * c + 1, t] = vtb[HD:, t * wt:(t + 1) * wt]


def _qkv_post(z1, cos_tab, sin_tab, gq, gk, n_prompt, seq):
    n = z1.shape[0]
    tm = 256
    n_ptiles = n_prompt // tm
    tiles_per_seq = seq // tm
    n_slabs = -(-n // seq)

    def tab_map(i):
        return (jnp.where(i < n_ptiles, i % tiles_per_seq, tiles_per_seq), 0)

    row = lambda i: (i, 0)
    head = lambda i: (0, i, 0)
    slab = lambda i: (i // tiles_per_seq, 0, i % tiles_per_seq)
    rows_sd = jax.ShapeDtypeStruct((n, KVW), F32)
    rows_t_sd = jax.ShapeDtypeStruct((n_slabs, KVW, seq), F32)
    grp_sd = jax.ShapeDtypeStruct((KVH, n, HD), BF16)
    assert SEL_KT % tm == 0 and tm % Q_BLK == 0
    out_shape = (
        jax.ShapeDtypeStruct((n, NSA_W), F32),
        jax.ShapeDtypeStruct((HEADS, n, HD), BF16),
        jax.ShapeDtypeStruct((HEADS, HD, n), BF16),
        rows_sd, rows_sd, rows_sd,
        rows_t_sd, rows_t_sd, rows_t_sd,
        grp_sd,
        jax.ShapeDtypeStruct((KVH, -(-n // SEL_KT), HD, SEL_KT), BF16),
        grp_sd,
        jax.ShapeDtypeStruct((KVH, n // Q_BLK, HD, Q_BLK), BF16),
    )
    rows_spec = pl.BlockSpec((tm, KVW), row)
    rows_t_spec = pl.BlockSpec((None, KVW, tm), slab)
    grp_spec = pl.BlockSpec((KVH, tm, HD), head)
    per = SEL_KT // tm
    out_specs = (
        pl.BlockSpec((tm, NSA_W), row),
        pl.BlockSpec((HEADS, tm, HD), head),
        pl.BlockSpec((HEADS, HD, tm), lambda i: (0, 0, i)),
        rows_spec, rows_spec, rows_spec,
        rows_t_spec, rows_t_spec, rows_t_spec,
        grp_spec,
        pl.BlockSpec((KVH, 1, HD, tm), lambda i: (0, i // per, 0, i % per)),
        grp_spec,
        pl.BlockSpec((KVH, tm // Q_BLK, HD, Q_BLK), lambda i: (0, i, 0, 0)),
    )
    kv_in = lambda br: pl.BlockSpec((tm, KVW), lambda i: (i, C1_KV // KVW + br))
    return pl.pallas_call(
        _qkv_post_kernel,
        grid=(n // tm,),
        in_specs=[
            pl.BlockSpec((tm, NSA_W), lambda i: (i, C1_Q // NSA_W)),
            kv_in(0), kv_in(1), kv_in(2),
            pl.BlockSpec((tm, LANES), tab_map),
            pl.BlockSpec((tm, LANES), tab_map),
            pl.BlockSpec((1, LANES), lambda i: (0, 0)),
            pl.BlockSpec((3, LANES), lambda i: (0, 0)),
        ],
        out_specs=out_specs,
        out_shape=out_shape,
        compiler_params=_params("parallel"),
        name="qkv_post",
    )(z1, z1, z1, z1, cos_tab, sin_tab, gq, gk)


def _compress_pair(flat, w2, pe8, w2bd, n_ch):
    ab = _dot(flat, w2)
    hp = _dot(pe8.astype(BF16), w2)
    hpe = hp[0:1, :LANES] + hp[1:2, LANES:]
    nxt = pltpu.roll(ab[:, LANES:], n_ch - 1, 0)
    h = ab[:, :LANES] + nxt + hpe
    act = h * jax.nn.sigmoid(h)
    return _dot(act.astype(BF16), w2bd)


def _scatter_chunk_rows(xs_ref, perm, tile_t, c0):
    rpp = tile_t.shape[1] // D_CMP
    xp = _nt(perm, tile_t)
    for t in range(D_CMP):
        xs_ref[t, c0:c0 + rpp, :] = xp[t * rpp:(t + 1) * rpp, :]


def _chunk_flat(xs_ref, p):
    return jnp.concatenate([xs_ref[t, :, p * LANES:(p + 1) * LANES] for t in range(D_CMP)], axis=1).astype(BF16)


def _compress_prompt_kernel(rows_ref, w2k_ref, w2v_ref, pek_ref, pev_ref, bdk_ref, bdv_ref, perm_ref, kc_ref, vc_ref,
                            xs_ref, *, n_ch):
    w2 = (w2k_ref, w2v_ref)
    pe = (pek_ref, pev_ref)
    bd = (bdk_ref, bdv_ref)
    outs = (kc_ref, vc_ref)
    tile_rows = perm_ref.shape[0]
    rpp = tile_rows // D_CMP
    for i in range(n_ch // rpp):
        _scatter_chunk_rows(xs_ref, perm_ref[...], rows_ref[:, i * tile_rows:(i + 1) * tile_rows].astype(BF16), i * rpp)
    for p in range(KVW // LANES):
        kv = p // 2
        flat = _chunk_flat(xs_ref, p)
        res = _compress_pair(flat, w2[kv][...], pe[kv][...], bd[kv][...], n_ch).astype(BF16)
        g0 = 2 * (p % 2)
        outs[kv][0, g0] = res[:, :HD]
        outs[kv][0, g0 + 1] = res[:, HD:]


def _compress_prompt(rows_cmp, cw, batch, seq):
    n_ch = seq // D_CMP
    const = lambda b: (0, 0)
    out_sd = jax.ShapeDtypeStruct((batch, KVH, n_ch, HD), BF16)
    out_spec = pl.BlockSpec((1, KVH, n_ch, HD), lambda b: (b, 0, 0, 0))
    return pl.pallas_call(
        functools.partial(_compress_prompt_kernel, n_ch=n_ch),
        grid=(batch,),
        in_specs=[pl.BlockSpec((None, KVW, seq), lambda b: (b, 0, 0))] + [pl.BlockSpec(w.shape, const) for w in cw],
        out_specs=(out_spec, out_spec),
        out_shape=(out_sd, out_sd),
        scratch_shapes=[pltpu.VMEM((D_CMP, n_ch, KVW), F32)],
        compiler_params=_params("parallel"),
        name="compress_prompt",
    )(rows_cmp, *cw)


def _rank_select(score_t, n_blocks, n_rows, n_top):
    st = score_t[:n_rows]
    jio = lax.broadcasted_iota(jnp.int32, st.shape, 0)
    cnt = jnp.zeros(st.shape, F32)
    for i in range(n_blocks):
        cnt = cnt + _beats(score_t[i:i + 1, :], st, jio, i)
    return jnp.where(cnt < n_top, 1.0, 0.0)


def _beats(row, st, jio, i):
    return jnp.where(row > st, 1.0, jnp.where(row == st, jnp.where(jio > i, 1.0, 0.0), 0.0))


def _pad_rows(x, n):
    if x.shape[0] == n:
        return x
    return jnp.concatenate([x, jnp.zeros((n - x.shape[0], x.shape[1]), x.dtype)], axis=0)


def _cmp_select_kernel(qh_ref, kc_ref, vc_ref, ov_ref, ocmp_ref, sel_ref, st_ref, *, nc, ns, n_top):
    s0 = pl.program_id(1) * Q_BLK
    ncp = kc_ref.shape[2]
    rows = GRP * Q_BLK
    qpos_r = s0 + lax.broadcasted_iota(jnp.int32, (rows, ncp), 0) % Q_BLK
    cidx = lax.broadcasted_iota(jnp.int32, (rows, ncp), 1)
    cvalid = (cidx * D_CMP + (L_CMP - 1) <= qpos_r) & (cidx < nc)
    qpos = s0 + lax.broadcasted_iota(jnp.int32, (Q_BLK, LANES), 0)
    blk = lax.broadcasted_iota(jnp.int32, (Q_BLK, LANES), 1)
    visible = (blk * L_SEL <= qpos) & (blk < ns)
    cur = qpos // L_SEL
    forced = (blk == 0) | (blk == cur) | (blk == cur - 1)
    nsp = -(-ns // 8) * 8
    ov = ov_ref[...]
    for g in range(KVH):
        qg = jnp.concatenate([qh_ref[GRP * g + r] for r in range(GRP)], axis=0)
        s = _nt(qg, kc_ref[0, g]) * (HD ** -0.5)
        s = jnp.where(cvalid, s, NEG)
        m = jnp.max(s, axis=-1, keepdims=True)
        e = jnp.where(cvalid, jnp.exp(s - m), 0.0)
        p = e / jnp.maximum(jnp.sum(e, axis=-1, keepdims=True), 1.0)
        o = _dot(p.astype(BF16), vc_ref[0, g])
        for r in range(GRP):
            ocmp_ref[GRP * g + r] = o[r * Q_BLK:(r + 1) * Q_BLK]
        psum = p[0:Q_BLK]
        for r in range(1, GRP):
            psum = psum + p[r * Q_BLK:(r + 1) * Q_BLK]
        imp = _dot_exact_lhs(psum, ov)
        score = jnp.where(visible, imp + jnp.where(forced, BIG, 0.0), -BIG)
        st_ref[g] = score.T

    n_vis = jnp.minimum((s0 + Q_BLK - 1) // L_SEL + 1, ns)
    jio = lax.broadcasted_iota(jnp.int32, (nsp, LANES), 0)

    def count(i, cnts):
        return tuple(c + _beats(st_ref[g, pl.ds(i, 1), :], st_ref[g, 0:nsp, :], jio, i) for g, c in enumerate(cnts))

    cnts = lax.fori_loop(0, n_vis, count, tuple(jnp.zeros((nsp, LANES), F32) for _ in range(KVH)))
    vis_t = (jio * L_SEL <= s0 + lax.broadcasted_iota(jnp.int32, (nsp, LANES), 1)) & (jio < ns)
    for g in range(KVH):
        sel_ref[g] = jnp.where((cnts[g] < n_top) & vis_t, 0.0, NEG)


def _cmp_select(q_heads, kc, vc, ov, batch, seq):
    nqt = seq // Q_BLK
    n_ch = kc.shape[2]
    ns = seq // L_SEL
    nsp = -(-ns // 8) * 8
    kern = functools.partial(_cmp_select_kernel, nc=n_ch - 1, ns=ns, n_top=min(N_TOP, ns))
    kv_spec = pl.BlockSpec((1, KVH, n_ch, HD), lambda b, t: (b, 0, 0, 0))
    return pl.pallas_call(
        kern,
        grid=(batch, nqt),
        in_specs=[
            pl.BlockSpec((HEADS, Q_BLK, HD), lambda b, t: (0, b * nqt + t, 0)),
            kv_spec, kv_spec,
            pl.BlockSpec(ov.shape, lambda b, t: (0, 0)),
        ],
        out_specs=(
            pl.BlockSpec((HEADS, Q_BLK, HD), lambda b, t: (0, b * nqt + t, 0)),
            pl.BlockSpec((None, KVH, nsp, LANES), lambda b, t: (b * nqt + t, 0, 0, 0)),
        ),
        out_shape=(
            jax.ShapeDtypeStruct((HEADS, batch * seq, HD), F32),
            jax.ShapeDtypeStruct((batch * nqt, KVH, nsp, LANES), F32),
        ),
        scratch_shapes=[pltpu.VMEM((KVH, LANES, LANES), F32)],
        compiler_params=_params("parallel", "parallel"),
        name="cmp_select",
    )(q_heads, kc, vc, ov)


def _sel_win_kernel(qt_ref, ks_ref, vst_ref, kw_ref, vwt_ref, bias_ref, zbr_ref, ocmp_ref, oa_ref, q_s):
    kt = vst_ref.shape[-1]
    s0 = pl.program_id(1) * Q_BLK
    cols = GRP * Q_BLK
    n_kt = (s0 + Q_BLK + kt - 1) // kt
    bpt = kt // L_SEL
    key_io = lax.broadcasted_iota(jnp.int32, (kt, Q_BLK), 0)
    qpos = s0 + lax.broadcasted_iota(jnp.int32, (kt, Q_BLK), 1)
    ones_row = 2 * 8
    va_rows = HD + ones_row

    for g in range(KVH):
        q_s[g] = jnp.concatenate([qt_ref[GRP * g + r] for r in range(GRP)], axis=1) * (HD ** -0.5)

    def with_bias(st, bias):
        return jnp.concatenate([st[:, r * Q_BLK:(r + 1) * Q_BLK] + bias for r in range(GRP)], axis=1)

    def body(j, carry):
        k0 = pl.multiple_of(j * kt, kt)
        grp = range(KVH)
        causal = jnp.where(k0 + key_io <= qpos, 0.0, NEG)
        st = [_dot(ks_ref[g, pl.ds(k0, kt), :], q_s[g]) for g in grp]
        bias = [jnp.concatenate([jnp.broadcast_to(bias_ref[g, pl.ds(j * bpt + b, 1), :], (L_SEL, Q_BLK))
                                 for b in range(bpt)], axis=0) + causal for g in grp]
        st = [with_bias(st[g], bias[g]) for g in grp]
        m_new = [jnp.maximum(carry[g][0], jnp.max(st[g], axis=0, keepdims=True)) for g in grp]
        corr = [jnp.exp(carry[g][0] - m_new[g]) for g in grp]
        p = [jnp.exp(st[g] - m_new[g]).astype(BF16) for g in grp]
        ones = jnp.ones((ones_row, kt), BF16)
        pv = [_dot(jnp.concatenate([vst_ref[g, j], ones], axis=0), p[g]) for g in grp]
        return tuple((m_new[g], carry[g][1] * corr[g] + pv[g]) for g in grp)

    init = tuple((jnp.full((1, cols), NEG, F32), jnp.zeros((va_rows, cols), F32)) for _ in range(KVH))
    res = lax.fori_loop(0, n_kt, body, init)

    wl = WINDOW + Q_BLK
    wstart = pl.multiple_of(jnp.maximum(s0 - WINDOW, 0), Q_BLK)
    wt0 = wstart // Q_BLK
    kposw = wstart + lax.broadcasted_iota(jnp.int32, (wl, Q_BLK), 0)
    qposw = s0 + lax.broadcasted_iota(jnp.int32, (wl, Q_BLK), 1)
    wbias = jnp.where((kposw <= qposw) & (kposw > qposw - WINDOW), 0.0, NEG)
    ones_w = jnp.ones((ones_row, wl), BF16)
    gates = jax.nn.sigmoid(zbr_ref[...])
    for g in range(KVH):
        acc = res[g][1]
        o_sel_t = acc[:HD] / acc[HD:HD + 1]
        st = with_bias(_dot(kw_ref[g, pl.ds(wstart, wl), :], q_s[g]), wbias)
        m = jnp.max(st, axis=0, keepdims=True)
        e = jnp.exp(st - m).astype(BF16)
        vwt = jnp.concatenate([vwt_ref[g, wt0 + i] for i in range(wl // Q_BLK)], axis=1)
        pv = _dot(jnp.concatenate([vwt, ones_w], axis=0), e)
        o_win_t = pv[:HD] / jnp.maximum(pv[HD:HD + 1], 1.0)
        for r in range(GRP):
            h = GRP * g + r
            cs = slice(r * Q_BLK, (r + 1) * Q_BLK)
            both = jnp.concatenate([o_sel_t[:, cs], o_win_t[:, cs]], axis=0).T
            o = (gates[:, h:h + 1] * ocmp_ref[h]
                 + gates[:, HEADS + h:HEADS + h + 1] * both[:, :HD]
                 + gates[:, 2 * HEADS + h:2 * HEADS + h + 1] * both[:, HD:])
            oa_ref[:, h * HD:(h + 1) * HD] = o


def _sel_win(q_t, ks, vs_t, kw, vw_t, sel_bias, z, ocmp, batch, seq):
    nqt = seq // Q_BLK
    assert seq % SEL_KT == 0
    once = dict(pipeline_mode=pl.Buffered(1))
    k_spec = pl.BlockSpec((KVH, seq, HD), lambda b, t: (0, b, 0), **once)
    tile = lambda b, t: (b * nqt + t, 0, 0, 0)
    return pl.pallas_call(
        _sel_win_kernel,
        grid=(batch, nqt),
        in_specs=[
            pl.BlockSpec((HEADS, HD, Q_BLK), lambda b, t: (0, 0, b * nqt + t)),
            k_spec,
            pl.BlockSpec((KVH, seq // SEL_KT, HD, SEL_KT), lambda b, t: (0, b, 0, 0), **once),
            k_spec,
            pl.BlockSpec((KVH, seq // Q_BLK, HD, Q_BLK), lambda b, t: (0, b, 0, 0), **once),
            pl.BlockSpec((None,) + sel_bias.shape[1:], tile),
            pl.BlockSpec((Q_BLK, LANES), lambda b, t: (b * nqt + t, C1_BR // LANES)),
            pl.BlockSpec((HEADS, Q_BLK, HD), lambda b, t: (0, b * nqt + t, 0)),
        ],
        out_specs=pl.BlockSpec((Q_BLK, NSA_W), lambda b, t: (b * nqt + t, 0)),
        out_shape=jax.ShapeDtypeStruct((z.shape[0], NSA_W), F32),
        scratch_shapes=[pltpu.VMEM((KVH, HD, GRP * Q_BLK), BF16)],
        compiler_params=_params("parallel", "arbitrary"),
        name="sel_win",
    )(q_t, ks, vs_t, kw, vw_t, sel_bias, z, ocmp)


def _hgrn_gates(zq, zf, logits):
    mx = jnp.max(logits, axis=0, keepdims=True)
    ex = jnp.exp(logits - mx)
    lb = ex[0:1] / jnp.sum(ex, axis=0, keepdims=True)
    hq = zq * jax.nn.sigmoid(zq)
    log_sig = -(jnp.maximum(-zf, 0.0) + jnp.log1p(jnp.exp(-jnp.abs(zf))))
    a = jnp.log(lb)
    b = jnp.log1p(-lb) + log_sig
    log_f = jnp.maximum(a, b) + jnp.log1p(jnp.exp(-jnp.abs(a - b)))
    hk = (1.0 - lb) * jax.nn.sigmoid(-zf)
    return hq, hk, log_f


def _intra_chunk(ac, qc, kc, vc):
    n = ac.shape[0]
    tio = lax.broadcasted_iota(jnp.int32, ac.shape, 0)
    acc = jnp.zeros(vc.shape, F32)
    for s in range(n):
        d = jnp.exp(ac - ac[s:s + 1, :])
        w = jnp.where(tio >= s, qc * kc[s:s + 1, :] * d, 0.0)
        col = jnp.sum(w, axis=-1, keepdims=True)
        acc = acc + col * vc[s:s + 1, :]
    return acc


def _hgrn_prompt_kernel(zq_ref, zf_ref, zi_ref, lg_ref, tri_ref, full_ref, o_ref, sout_ref,
                        st_ref, a_s, q_s, k_s, qt_s, ktb_s, eal_s, vt_s):
    t = pl.program_id(1)
    tt = zq_ref.shape[0]

    @pl.when(t == 0)
    def _():
        st_ref[...] = jnp.zeros(st_ref.shape, F32)

    hq, hk, log_f = _hgrn_gates(zq_ref[...], zf_ref[...], lg_ref[...])
    a = _dot_exact_rhs(tri_ref[...], log_f)
    al = _dot_exact_rhs(full_ref[...], log_f)
    a_s[...] = a
    q_s[...] = hq
    k_s[...] = hk
    qt_s[...] = hq * jnp.exp(a)
    ktb_s[...] = (hk * jnp.exp(al - a)).astype(BF16)
    eal_s[...] = jnp.exp(al)
    for h in range(HG_H):
        vt_s[h] = zi_ref[:, h * HG_DV:(h + 1) * HG_DV].T.astype(BF16)
    lane_tok = lax.broadcasted_iota(jnp.int32, (HG_DV, tt), 1)

    def chunk(c, carry):
        r0 = pl.multiple_of(c * HG_SUB, HG_SUB)
        in_chunk = (lane_tok >= r0) & (lane_tok < r0 + HG_SUB)
        for h in range(HG_H):
            sl = slice(h * HG_DK, (h + 1) * HG_DK)
            st = st_ref[h]
            o = _nt(qt_s[pl.ds(r0, HG_SUB), sl].astype(BF16), st.astype(BF16))
            o = o + _intra_chunk(a_s[pl.ds(r0, HG_SUB), sl], q_s[pl.ds(r0, HG_SUB), sl],
                                 k_s[pl.ds(r0, HG_SUB), sl], zi_ref[pl.ds(r0, HG_SUB), sl])
            o_ref[pl.ds(r0, HG_SUB), sl] = o
            vtm = jnp.where(in_chunk, vt_s[h], jnp.zeros_like(vt_s[h]))
            st_ref[h] = st * eal_s[pl.ds(r0, 1), sl] + _dot(vtm, ktb_s[:, sl])
        return carry

    lax.fori_loop(0, tt // HG_SUB, chunk, 0)

    @pl.when(t == pl.num_programs(1) - 1)
    def _():
        for h in range(HG_H):
            sout_ref[0, h] = st_ref[h].T


def _hgrn_prompt(z, logits, batch, seq):
    tt = 128
    nt = seq // tt
    sub = np.arange(tt) // HG_SUB
    same = sub[:, None] == sub[None, :]
    tri = jnp.asarray((same & (np.arange(tt)[None, :] <= np.arange(tt)[:, None])).astype(np.float32), BF16)
    full = jnp.asarray(same.astype(np.float32), BF16)
    col = lambda c: (lambda b, t: (b * nt + t, c // HG_W))
    const = lambda b, t: (0, 0)
    return pl.pallas_call(
        _hgrn_prompt_kernel,
        grid=(batch, nt),
        in_specs=[
            pl.BlockSpec((tt, HG_W), col(C2_HQ)),
            pl.BlockSpec((tt, HG_W), col(C2_HF)),
            pl.BlockSpec((tt, HG_W), col(C2_HI)),
            pl.BlockSpec((2, HG_W), const),
            pl.BlockSpec((tt, tt), const),
            pl.BlockSpec((tt, tt), const),
        ],
        out_specs=(
            pl.BlockSpec((tt, HG_W), lambda b, t: (b * nt + t, 0)),
            pl.BlockSpec((1, HG_H, HG_DK, HG_DV), lambda b, t: (b, 0, 0, 0)),
        ),
        out_shape=(
            jax.ShapeDtypeStruct((z.shape[0], HG_W), F32),
            jax.ShapeDtypeStruct((batch, HG_H, HG_DK, HG_DV), F32),
        ),
        scratch_shapes=[
            pltpu.VMEM((HG_H, HG_DV, HG_DK), F32),
            pltpu.VMEM((tt, HG_W), F32),
            pltpu.VMEM((tt, HG_W), F32),
            pltpu.VMEM((tt, HG_W), F32),
            pltpu.VMEM((tt, HG_W), F32),
            pltpu.VMEM((tt, HG_W), BF16),
            pltpu.VMEM((tt, HG_W), F32),
            pltpu.VMEM((HG_H, HG_DV, tt), BF16),
        ],
        compiler_params=_params("parallel", "arbitrary"),
        name="hgrn_prompt",
    )(z, z, z, logits, tri, full)


def _hgrn_sample_kernel(zq_ref, zf_ref, zi_ref, lg_ref, s0_ref, ob_hbm_ref, o_ref, sout_ref, *, s_len):
    del ob_hbm_ref
    n_seq = s0_ref.shape[0]
    hq_all, hk_all, logf_all = _hgrn_gates(zq_ref[...], zf_ref[...], lg_ref[...])
    rio = lax.broadcasted_iota(jnp.int32, (s_len, HG_W), 0)
    row0 = lax.broadcasted_iota(jnp.int32, (s_len, HG_DK), 0) == 0
    pad = HG_DK - 2 * s_len
    for q in range(n_seq):
        rs = slice(q * s_len, (q + 1) * s_len)
        hq, hk, log_f = hq_all[rs], hk_all[rs], logf_all[rs]
        a = log_f
        d = 1
        while d < s_len:
            a = a + jnp.where(rio >= d, pltpu.roll(a, d, 0), 0.0)
            d *= 2
        a_last = jnp.sum(log_f, axis=0, keepdims=True)
        qt = hq * jnp.exp(a)
        kt = hk * jnp.exp(a_last - a)
        ea = jnp.exp(a_last)
        hv = zi_ref[rs, :]
        for h in range(HG_H):
            sl = slice(h * HG_DK, (h + 1) * HG_DK)
            s_old = s0_ref[q, h]
            qt16 = _pad_rows(qt[:, sl], 16).astype(BF16)
            o = _dot(qt16, s_old.astype(BF16))[:s_len]
            o = o + _intra_chunk(a[:, sl], hq[:, sl], hk[:, sl], hv[:, sl])
            o_ref[rs, sl] = o
            m = jnp.concatenate([kt[:, sl], jnp.where(row0, jnp.broadcast_to(ea[:, sl], (s_len, HG_DK)), 0.0),
                                 jnp.zeros((pad, HG_DK), F32)], axis=0)
            mt = m.T
            e_col = mt[:, s_len:s_len + 1]
            vpad = _pad_rows(hv[:, sl], HG_DK).astype(BF16)
            sout_ref[q, h] = e_col * s_old + _dot(mt.astype(BF16), vpad)


def _hgrn_sample(z, logits, state, ob, n_prompt, dec_batch, s_len):
    n_seq = 4
    rows = n_seq * s_len
    base = n_prompt // rows
    col = lambda c: (lambda b: (base + b, c // HG_W))
    st_spec = pl.BlockSpec((n_seq, HG_H, HG_DK, HG_DV), lambda b: (b, 0, 0, 0))
    return pl.pallas_call(
        functools.partial(_hgrn_sample_kernel, s_len=s_len),
        grid=(dec_batch // n_seq,),
        in_specs=[
            pl.BlockSpec((rows, HG_W), col(C2_HQ)),
            pl.BlockSpec((rows, HG_W), col(C2_HF)),
            pl.BlockSpec((rows, HG_W), col(C2_HI)),
            pl.BlockSpec((2, HG_W), lambda b: (0, 0)),
            st_spec,
            pl.BlockSpec(memory_space=pl.ANY),
        ],
        out_specs=(pl.BlockSpec((rows, HG_W), lambda b: (base + b, 0)), st_spec),
        out_shape=(
            jax.ShapeDtypeStruct(ob.shape, F32),
            jax.ShapeDtypeStruct(state.shape, F32),
        ),
        input_output_aliases={5: 0},
        compiler_params=_params("parallel"),
        name="hgrn_sample",
    )(z, z, z, logits, state, ob)


def _sample_nsa_kernel(pt_ref, *refs, npg, s_len, past):
    del pt_ref
    cmp_pages = refs[:npg]
    sel_pages = refs[npg:2 * npg]
    (win_ref, q_ref, rsel_ref, rwin_ref, zbr_ref, w2k_ref, w2v_ref, pek_ref, pev_ref, bdk_ref, bdv_ref, perm_ref,
     ov_ref, e_ref, oa_hbm_ref, oa_ref, swin_ref, xs_ref, kcv_ref) = refs[2 * npg:]
    del oa_hbm_ref
    step = pl.program_id(0)

    @pl.when(step == 0)
    def _():
        kcv_ref[...] = jnp.zeros(kcv_ref.shape, BF16)

    kc_all = kcv_ref[(step + 1) % 2, 0]
    vc_all = kcv_ref[(step + 1) % 2, 1]
    page = cmp_pages[0].shape[1]
    rpp = page // D_CMP
    n_ch = npg * rpp
    nc = n_ch - 1
    wb = win_ref.shape[1]
    ns = -(-(past + s_len) // L_SEL)
    nsp = -(-ns // 8) * 8
    n_top = min(N_TOP, ns)
    rows = HEADS * s_len

    w2 = (w2k_ref, w2v_ref)
    pe = (pek_ref, pev_ref)
    bd = (bdk_ref, bdv_ref)
    comp = []

    def stage1_scatter(lo, hi):
        for i in range(lo, hi):
            _scatter_chunk_rows(xs_ref, perm_ref[...], cmp_pages[i][...].astype(BF16), i * rpp)

    def stage1_compress(p):
        kv = p // 2
        comp.append(_compress_pair(_chunk_flat(xs_ref, p), w2[kv][...], pe[kv][...], bd[kv][...], n_ch))

    q = q_ref[...]
    lane8 = lax.broadcasted_iota(jnp.int32, (s_len, LANES), 1)
    zero8 = jnp.zeros((s_len, LANES), F32)
    blocks = []
    for g in range(KVH):
        for r in range(GRP):
            h = GRP * g + r
            src = q[:, (h // 2) * LANES:(h // 2 + 1) * LANES]
            if (r % 2) != (g % 2):
                src = pltpu.roll(src, HD, 1)
            keep = (lane8 < HD) if g % 2 == 0 else (lane8 >= HD)
            piece = jnp.where(keep, src, 0.0)
            blocks.append(jnp.concatenate([piece, zero8] if g // 2 == 0 else [zero8, piece], axis=1))
    qbd = (jnp.concatenate(blocks, axis=0) * (HD ** -0.5)).astype(BF16)

    def rowpos(shape):
        return past + lax.broadcasted_iota(jnp.int32, shape, 0) % s_len

    stage1_scatter(0, npg // 2)

    s = _nt(qbd, kc_all)
    cidx = lax.broadcasted_iota(jnp.int32, s.shape, 1)
    cvalid = (cidx * D_CMP + (L_CMP - 1) <= rowpos(s.shape)) & (cidx < nc)
    s = jnp.where(cvalid, s, NEG)
    m = jnp.max(s, axis=-1, keepdims=True)
    e = jnp.where(cvalid, jnp.exp(s - m), 0.0)
    p = e / jnp.maximum(jnp.sum(e, axis=-1, keepdims=True), 1.0)
    ocmp = _dot(p.astype(BF16), vc_all)

    stage1_scatter(npg // 2, npg)

    gs = GRP * s_len
    psum = []
    for g in range(KVH):
        acc = p[g * gs:g * gs + s_len]
        for r in range(1, GRP):
            acc = acc + p[g * gs + r * s_len:g * gs + (r + 1) * s_len]
        psum.append(acc)
    imp = _dot_exact_lhs(jnp.concatenate(psum, axis=0), ov_ref[...])
    qpos = rowpos(imp.shape)
    blk = lax.broadcasted_iota(jnp.int32, imp.shape, 1)
    visible = (blk * L_SEL <= qpos) & (blk < ns)
    cur = qpos // L_SEL
    forced = (blk == 0) | (blk == cur) | (blk == cur - 1)
    score = jnp.where(visible, imp + jnp.where(forced, BIG, 0.0), -BIG)
    sel_t = _rank_select(_pad_rows(score, LANES).T, ns, nsp, n_top)
    sel = _pad_rows(sel_t, LANES).T[:KVH * s_len]
    sel = jnp.where(visible, sel, 0.0)
    sel_rows = jnp.concatenate([sel[g * s_len:(g + 1) * s_len] for g in range(KVH) for _ in range(GRP)], axis=0)

    stage1_compress(0)

    zpad = jnp.zeros((page - s_len, KW), F32)
    knew = jnp.concatenate([rsel_ref[:, :KW], zpad], axis=0).astype(BF16)
    vnew = jnp.concatenate([rsel_ref[:, KW:], zpad], axis=0).astype(BF16)
    def page_pair(i, lo):
        return jnp.concatenate([sel_pages[2 * i][lo:lo + KW, :], sel_pages[2 * i + 1][lo:lo + KW, :]],
                               axis=1).astype(BF16)

    s = jnp.concatenate([_dot(qbd, page_pair(i, 0)) for i in range(npg // 2)] + [_nt(qbd, knew)], axis=1)
    mf = _dot(sel_rows.astype(BF16), e_ref[...])
    kpos = lax.broadcasted_iota(jnp.int32, s.shape, 1)
    mask = (mf > 0.5) & (kpos <= rowpos(s.shape))
    s = jnp.where(mask, s, NEG)
    m = jnp.max(s, axis=-1, keepdims=True)
    e = jnp.where(mask, jnp.exp(s - m), 0.0)
    l = jnp.sum(e, axis=-1, keepdims=True)
    eb = e.astype(BF16)
    acc = _dot(eb[:, npg * page:], vnew)
    for i in range(npg // 2):
        acc = acc + _nt(eb[:, 2 * i * page:(2 * i + 2) * page], page_pair(i, KW))
    osel = acc / l

    stage1_compress(1)

    kwn = jnp.concatenate([rwin_ref[:, :KW], zpad], axis=0).astype(BF16)
    vwn = jnp.concatenate([rwin_ref[:, KW:], zpad], axis=0).astype(BF16)
    s = jnp.concatenate([_dot(qbd, win_ref[:KW, :].astype(BF16)), _nt(qbd, kwn)], axis=1)
    kposw = (past - wb) + lax.broadcasted_iota(jnp.int32, s.shape, 1)
    qposw = rowpos(s.shape)
    wmask = (kposw <= qposw) & (kposw > qposw - WINDOW) & (kposw >= 0)
    s = jnp.where(wmask, s, NEG)
    m = jnp.max(s, axis=-1, keepdims=True)
    e = jnp.where(wmask, jnp.exp(s - m), 0.0)
    den = jnp.maximum(jnp.sum(e, axis=-1, keepdims=True), 1.0)
    eb = e.astype(BF16)
    owin = (_nt(eb[:, :wb], win_ref[KW:, :].astype(BF16)) + _dot(eb[:, wb:], vwn)) / den

    stage1_compress(2)

    gates = jax.nn.sigmoid(zbr_ref[...])
    for g in range(KVH):
        gl = slice((g // 2) * LANES, (g // 2 + 1) * LANES)
        for k2 in range(2):
            halves = []
            for r in (2 * k2, 2 * k2 + 1):
                h = GRP * g + r
                rs = slice(h * s_len, (h + 1) * s_len)
                val = (gates[:, h:h + 1] * ocmp[rs, gl]
                       + gates[:, HEADS + h:HEADS + h + 1] * osel[rs, gl]
                       + gates[:, 2 * HEADS + h:2 * HEADS + h + 1] * owin[rs, gl])
                if (r % 2) != (g % 2):
                    val = pltpu.roll(val, HD, 1)
                halves.append(val)
            c = 2 * g + k2
            oa_ref[:, c * LANES:(c + 1) * LANES] = jnp.where(lane8 < HD, halves[0], halves[1])

    new_t = [_pad_rows(rwin_ref[:, c * LANES:(c + 1) * LANES], LANES).T for c in range(KVW // LANES)]
    new_t = jnp.concatenate(new_t, axis=0)
    lane_w = lax.broadcasted_iota(jnp.int32, (KVW, LANES), 1)
    n_col = wb // LANES
    rolled = [pltpu.roll(win_ref[:, c * LANES:(c + 1) * LANES], LANES - s_len, 1) for c in range(n_col)]
    rolled.append(pltpu.roll(new_t, LANES - s_len, 1))
    for c in range(n_col):
        swin_ref[:, c * LANES:(c + 1) * LANES] = jnp.where(lane_w < LANES - s_len, rolled[c], rolled[c + 1])

    stage1_compress(3)
    kcv_ref[step % 2, 0] = jnp.concatenate(comp[0:2], axis=1).astype(BF16)
    kcv_ref[step % 2, 1] = jnp.concatenate(comp[2:4], axis=1).astype(BF16)


def _sample_nsa(page_table, cache_cmp, cache_sel, cache_win, q_tok, rows_sel, rows_win, z, cw, ov, e_mat, oa,
                n_prompt, dec_batch, s_len):
    npg = page_table.shape[1]
    page = cache_cmp.shape[2]
    past = npg * page
    wb = cache_win.shape[2]
    base = n_prompt // s_len
    last = dec_batch - 1
    seq1 = lambda b: jnp.minimum(b, last)
    seq2 = lambda b: jnp.maximum(b - 1, 0)
    cmp_specs = [pl.BlockSpec((None, KVW, page), (lambda b, pt, i=i: (pt[seq1(b), i], 0, 0))) for i in range(npg)]
    sel_specs = [pl.BlockSpec((None, KVW, page), (lambda b, pt, i=i: (pt[seq2(b), i], 0, 0))) for i in range(npg)]
    row = lambda b, pt: (base + seq2(b), 0)
    const = lambda b, pt: (0, 0)
    in_specs = (
        cmp_specs + sel_specs
        + [pl.BlockSpec((None, KVW, wb), lambda b, pt: (seq2(b), 0, 0)),
           pl.BlockSpec((s_len, NSA_W), row),
           pl.BlockSpec((s_len, KVW), row),
           pl.BlockSpec((s_len, KVW), row),
           pl.BlockSpec((s_len, LANES), lambda b, pt: (base + seq2(b), C1_BR // LANES))]
        + [pl.BlockSpec(w.shape, const) for w in cw]
        + [pl.BlockSpec(ov.shape, const), pl.BlockSpec(e_mat.shape, const), pl.BlockSpec(memory_space=pl.ANY)]
    )
    n_ch = past // D_CMP
    grid_spec = pltpu.PrefetchScalarGridSpec(
        num_scalar_prefetch=1,
        grid=(dec_batch + 1,),
        in_specs=in_specs,
        out_specs=(
            pl.BlockSpec((s_len, NSA_W), row),
            pl.BlockSpec((None, KVW, wb), lambda b, pt: (seq2(b), 0, 0)),
        ),
        scratch_shapes=[
            pltpu.VMEM((D_CMP, n_ch, KVW), F32),
            pltpu.VMEM((2, 2, n_ch, KW), BF16),
        ],
    )
    args = (page_table, *([cache_cmp] * npg), *([cache_sel] * npg), cache_win, q_tok, rows_sel, rows_win, z, *cw,
            ov, e_mat, oa)
    return pl.pallas_call(
        functools.partial(_sample_nsa_kernel, npg=npg, s_len=s_len, past=past),
        grid_spec=grid_spec,
        out_shape=(
            jax.ShapeDtypeStruct(oa.shape, F32),
            jax.ShapeDtypeStruct((dec_batch, KVW, wb), F32),
        ),
        input_output_aliases={len(args) - 1: 0},
        compiler_params=_params("arbitrary"),
        name="sample_nsa",
    )(*args)


def _mix_kernel(xp_ref, xs_ref, oa_ref, ob_ref, zhg_ref, ga_ref, gb_ref, gn_ref, wa_ref, wb_ref, wo_ref, o_ref, *, npt):
    ya = _dot(oa_ref[...].astype(BF16), wa_ref[...])
    gn = gn_ref[...]
    parts = []
    for h in range(HG_H):
        sl = slice(h * HG_DV, (h + 1) * HG_DV)
        ob = ob_ref[:, sl]
        ms = jnp.mean(ob * ob, axis=-1, keepdims=True)
        zg = zhg_ref[:, sl]
        y = ((ob * lax.rsqrt(ms + EPS)) * gn) * (zg * jax.nn.sigmoid(zg))
        parts.append(y.astype(BF16))
    yb = _dot(jnp.concatenate(parts, axis=1), wb_ref[...])
    mix = jax.nn.sigmoid(ga_ref[...]) * ya + jax.nn.sigmoid(gb_ref[...]) * yb
    y = _dot(mix.astype(BF16), wo_ref[...])

    @pl.when(pl.program_id(0) < npt)
    def _():
        o_ref[...] = xp_ref[...] + y

    @pl.when(pl.program_id(0) >= npt)
    def _():
        o_ref[...] = xs_ref[...] + y


def _mix(xp, xs, oa, ob, z, gn, wa, wb, wo):
    (n_p, d), n_s = xp.shape, xs.shape[0]
    n = n_p + n_s
    tm = 256
    npt, spec_p, spec_s = _two_group_specs(n_p, n_s, tm, d)
    row = lambda i: (i, 0)
    const = lambda i: (0, 0)
    resident = lambda w: pl.BlockSpec(w.shape, const, pipeline_mode=pl.Buffered(1))
    return pl.pallas_call(
        functools.partial(_mix_kernel, npt=npt),
        grid=(n // tm,),
        in_specs=[
            spec_p, spec_s,
            pl.BlockSpec((tm, NSA_W), row),
            pl.BlockSpec((tm, HG_W), row),
            pl.BlockSpec((tm, HG_W), lambda i: (i, C2_HG // HG_W)),
            pl.BlockSpec((tm, d), lambda i: (i, C2_GA // D_MODEL)),
            pl.BlockSpec((tm, d), lambda i: (i, C2_GB // D_MODEL)),
            pl.BlockSpec((1, HG_DV), const),
            resident(wa), resident(wb), resident(wo),
        ],
        out_specs=pl.BlockSpec((tm, d), row),
        out_shape=jax.ShapeDtypeStruct((n, d), F32),
        compiler_params=_params("parallel"),
        name="mix_out",
    )(xp, xs, oa, ob, z, z, z, gn, wa, wb, wo)


def _ffn_kernel(x_ref, g_ref, w1_ref, w2_ref, op_ref, os_ref, xn_ref, acc_ref, *, npt):
    i = pl.program_id(0)
    j = pl.program_id(1)
    last = pl.num_programs(1) - 1

    @pl.when(j == 0)
    def _():
        xn_ref[...] = _rmsnorm_bf16(x_ref[...], g_ref[...])
        acc_ref[...] = jnp.zeros(acc_ref.shape, F32)

    h = _dot(xn_ref[...], w1_ref[...])
    h = jnp.square(jnp.maximum(h, 0.0)).astype(BF16)
    acc_ref[...] += _dot(h, w2_ref[...])

    @pl.when((j == last) & (i < npt))
    def _():
        op_ref[...] = x_ref[...] + acc_ref[...]

    @pl.when((j == last) & (i >= npt))
    def _():
        os_ref[...] = x_ref[...] + acc_ref[...]


def _ffn(x, g, w1, w2, n_p):
    n, d = x.shape
    n_s = n - n_p
    dff = w1.shape[1]
    tm = _pick_tile(np.gcd(n_p, n_s), (512, 256))
    tf = 1024
    npt, spec_p, spec_s = _two_group_specs(n_p, n_s, tm, d)
    return pl.pallas_call(
        functools.partial(_ffn_kernel, npt=npt),
        grid=(n // tm, dff // tf),
        in_specs=[
            pl.BlockSpec((tm, d), lambda i, j: (i, 0)),
            pl.BlockSpec((1, d), lambda i, j: (0, 0)),
            pl.BlockSpec((d, tf), lambda i, j: (0, j)),
            pl.BlockSpec((tf, d), lambda i, j: (j, 0)),
        ],
        out_specs=(spec_p, spec_s),
        out_shape=(jax.ShapeDtypeStruct((n_p, d), F32), jax.ShapeDtypeStruct((n_s, d), F32)),
        scratch_shapes=[pltpu.VMEM((tm, d), BF16), pltpu.VMEM((tm, d), F32)],
        compiler_params=_params("arbitrary", "arbitrary"),
        name="ffn",
    )(x, g, w1, w2)


def _tail_w_in(w_in):
    w2 = w_in[:, sum(_SIZES[:3]):].astype(BF16)
    assert w2.shape[1] == N_Z2
    return w2


def _compress_weights(cmp_pe, cmp_w1, cmp_w2):
    m = L_CMP // D_CMP
    eye2 = jnp.eye(2, dtype=F32)
    out = []
    w2s, pes, bds = [], [], []
    for kv in range(2):
        w1p = cmp_w1[kv].reshape(m, D_CMP, HD, HD).transpose(1, 2, 0, 3)
        w = w1p[:, None, :, :, None, :] * eye2[None, :, None, None, :, None]
        w2s.append(w.reshape(D_CMP * LANES, m * LANES).astype(BF16))
        pe = cmp_pe[kv].reshape(m, D_CMP, 1, HD)
        pe = jnp.broadcast_to(pe, (m, D_CMP, 2, HD)).reshape(m, D_CMP * LANES)
        pes.append(jnp.concatenate([pe, jnp.zeros((8 - m, D_CMP * LANES), F32)], axis=0))
        bds.append(jnp.kron(eye2, cmp_w2[kv]).astype(BF16))
    tile_rows = LANES
    rpp = tile_rows // D_CMP
    src = np.arange(tile_rows)
    perm = np.zeros((tile_rows, tile_rows), np.float32)
    perm[(src % D_CMP) * rpp + src // D_CMP, src] = 1.0
    return (w2s[0], w2s[1], pes[0], pes[1], bds[0], bds[1], jnp.asarray(perm, BF16))


def _overlap_matrix(nc_pad, ns):
    cs = np.arange(nc_pad) * D_CMP
    ss = np.arange(LANES) * L_SEL
    ov = np.clip(np.minimum(cs[:, None] + L_CMP, ss[None, :] + L_SEL) - np.maximum(cs[:, None], ss[None, :]), 0, None)
    ov = (ov / D_CMP) * (np.arange(LANES)[None, :] < ns)
    return jnp.asarray(ov.astype(np.float32), BF16)


def _expand_matrix(n_keys):
    e = (np.arange(n_keys)[None, :] // L_SEL) == np.arange(LANES)[:, None]
    return jnp.asarray(e.astype(np.float32), BF16)


def _rope_tables(seq, past, s_len, tm):
    half = HD // 2
    inv = THETA ** (-jnp.arange(half, dtype=F32) * 2.0 / HD)
    pos = jnp.concatenate([jnp.arange(seq), past + jnp.arange(tm) % s_len]).astype(F32)
    ang = pos[:, None] * inv[None, :]
    cos = jnp.cos(ang)
    sin = jnp.sin(ang)
    cos_t = jnp.concatenate([cos, cos, cos, cos], axis=1)
    sin_t = jnp.concatenate([-sin, sin, -sin, sin], axis=1)
    return cos_t, sin_t


def kernel(x_prompt, x_sample, cache_cmp_kv, cache_sel_kv, cache_win_kv, state_hgrn, page_table, norm1_g, w_in,
           q_norm_g, k_norm_g, cmp_pe, cmp_w1, cmp_w2, w_proj_a, hgrn_lb_logits, hgrn_norm_g, w_proj_b, w_out,
           norm2_g, w_ff1, w_ff2):
    assert w_in.shape[0] == 1, "single-layer step"
    batch, seq, d = x_prompt.shape
    dec_batch, s_len, _ = x_sample.shape
    n_prompt = batch * seq
    n_sample = dec_batch * s_len
    npg = page_table.shape[1]
    page = cache_cmp_kv.shape[2]
    past = npg * page
    wb = cache_win_kv.shape[2]
    assert seq % Q_BLK == 0 and seq >= WINDOW + Q_BLK and n_prompt % 256 == 0 and n_sample % 256 == 0
    assert s_len % 8 == 0 and page % D_CMP == 0 and 2 * s_len <= HG_DK and dec_batch % 4 == 0 and npg % 2 == 0

    xp = x_prompt.reshape(n_prompt, d)
    xs = x_sample.reshape(n_sample, d)

    w_z2 = _tail_w_in(w_in[0])
    cw = _compress_weights(cmp_pe[0], cmp_w1[0], cmp_w2[0])
    cos_t, sin_t = _rope_tables(seq, past, s_len, 256)
    gq = jnp.tile(q_norm_g[0][None, :], (1, 2))
    gk = jnp.tile(k_norm_g[0], (1, 2))
    logits = hgrn_lb_logits.astype(F32)

    def token_minor(c):
        return jnp.transpose(c, (0, 2, 3, 4, 1)).reshape(c.shape[0], KVW, c.shape[1])

    def token_major(c_t):
        return jnp.transpose(c_t.reshape(c_t.shape[0], 2, KVH, HD, c_t.shape[2]), (0, 4, 1, 2, 3))

    z1 = _in_proj(xp, xs, norm1_g, w_in[0], N_Z1, 896)
    z2 = _in_proj(xp, xs, norm1_g, w_z2, N_Z2, 1024)
    (q_tok, q_heads, q_t, rows_cmp, rows_sel, rows_win, rt_cmp, rt_sel, rt_win,
     ks, vs_t, kw, vw_t) = _qkv_post(z1, cos_t, sin_t, gq, gk, n_prompt, seq)

    kc, vc = _compress_prompt(rt_cmp, cw, batch, seq)
    ov_p = _overlap_matrix(seq // D_CMP, seq // L_SEL)
    ocmp, sel_bias = _cmp_select(q_heads, kc, vc, ov_p, batch, seq)
    oa = _sel_win(q_t, ks, vs_t, kw, vw_t, sel_bias, z1, ocmp, batch, seq)
    ob, p_state = _hgrn_prompt(z2, logits, batch, seq)

    ns_s = -(-(past + s_len) // L_SEL)
    ov_s = _overlap_matrix(past // D_CMP, ns_s)
    oa, s_win_t = _sample_nsa(
        page_table, token_minor(cache_cmp_kv[0]), token_minor(cache_sel_kv[0]), token_minor(cache_win_kv[0]),
        q_tok, rows_sel, rows_win, z1, cw, ov_s, _expand_matrix(past + page), oa, n_prompt, dec_batch, s_len)
    ob, s_state = _hgrn_sample(z2, logits, state_hgrn[0], ob, n_prompt, dec_batch, s_len)

    x2 = _mix(xp, xs, oa, ob, z2, hgrn_norm_g, w_proj_a[0].astype(BF16), w_proj_b[0].astype(BF16),
              w_out[0].astype(BF16))
    y_p, y_s = _ffn(x2, norm2_g, w_ff1[0].astype(BF16), w_ff2[0].astype(BF16), n_prompt)

    kv_shape = (2, KVH, HD)
    wp = min(WINDOW, seq)
    return (
        y_p.reshape(batch, seq, d),
        y_s.reshape(dec_batch, s_len, d),
        token_major(rt_cmp[:batch])[None],
        token_major(rt_sel[:batch])[None],
        token_major(rt_win[:batch, :, seq - wp:])[None],
        p_state[None],
        rows_cmp[n_prompt:].reshape((1, dec_batch, s_len) + kv_shape),
        rows_sel[n_prompt:].reshape((1, dec_batch, s_len) + kv_shape),
        token_major(s_win_t)[None],
        s_state[None],
    )
```

```python
import functools

import numpy as np
import jax
import jax.numpy as jnp
from jax import lax
from jax.experimental import pallas as pl
from jax.experimental.pallas import tpu as pltpu

F32 = jnp.float32
BF16 = jnp.bfloat16

D_MODEL = 2048
HEADS = 16
KVH = 4
GRP = HEADS // KVH
HD = 64
NSA_W = HEADS * HD
L_CMP = 32
D_CMP = 16
L_SEL = 64
N_TOP = 16
WINDOW = 512
Q_BLK = 128
THETA = 10000.0
HG_H = 8
HG_DK = 128
HG_DV = 128
HG_W = HG_H * HG_DK
HG_SUB = 16
SEL_KT = 512
EPS = 1e-6
NEG = -1e30
BIG = 1e9

LANES = 128
KVW = 2 * KVH * HD
KW = KVH * HD

_SIZES = (NSA_W, 6 * KVH * HD, 3 * HEADS, HG_W, HG_W, HG_W, HG_W, D_MODEL, D_MODEL)
C1_Q, C1_KV, C1_BR, N_Z1 = 0, 1024, 2560, 2688
C2_HQ, C2_HF, C2_HI, C2_HG, C2_GA, C2_GB, N_Z2 = 0, 1024, 2048, 3072, 4096, 6144, 8192

VMEM_LIMIT = 56 * 1024 * 1024


def _params(*sem):
    return pltpu.CompilerParams(dimension_semantics=sem, vmem_limit_bytes=VMEM_LIMIT)


def _nt(a, b):
    return lax.dot_general(a, b, (((1,), (1,)), ((), ())), preferred_element_type=F32)


def _dot(a, b):
    return jnp.dot(a, b, preferred_element_type=F32)


def _split3(x):
    hi = x.astype(BF16)
    r1 = x - hi.astype(F32)
    mid = r1.astype(BF16)
    lo = (r1 - mid.astype(F32)).astype(BF16)
    return hi, mid, lo


def _dot_exact_rhs(m, x):
    hi, mid, lo = _split3(x)
    return _dot(m, hi) + _dot(m, mid) + _dot(m, lo)


def _dot_exact_lhs(x, m):
    hi, mid, lo = _split3(x)
    return _dot(hi, m) + _dot(mid, m) + _dot(lo, m)


def _pick_tile(n, cands):
    for c in cands:
        if n % c == 0:
            return c
    raise ValueError(f"no tile in {cands} divides {n}")


def _rmsnorm_bf16(x, g):
    ms = jnp.mean(x * x, axis=-1, keepdims=True)
    return ((x * lax.rsqrt(ms + EPS)) * g).astype(BF16)


def _two_group_specs(n_p, n_s, tm, width, sample_kwargs=None):
    npt = n_p // tm
    assert n_p % tm == 0 and n_s % tm == 0
    spec_p = pl.BlockSpec((tm, width), lambda i, *_: (jnp.minimum(i, npt - 1), 0))
    spec_s = pl.BlockSpec((tm, width), lambda i, *_: (jnp.maximum(i - npt, 0), 0), **(sample_kwargs or {}))
    return npt, spec_p, spec_s


def _rms_matmul_kernel(xp_ref, xs_ref, g_ref, w_ref, o_ref, xn_ref, *, npt):
    i = pl.program_id(0)

    @pl.when((pl.program_id(1) == 0) & (i < npt))
    def _():
        xn_ref[...] = _rmsnorm_bf16(xp_ref[...], g_ref[...])

    @pl.when((pl.program_id(1) == 0) & (i >= npt))
    def _():
        xn_ref[...] = _rmsnorm_bf16(xs_ref[...], g_ref[...])

    o_ref[...] = _dot(xn_ref[...], w_ref[...].astype(BF16))


def _in_proj(xp, xs, g, w, nout, tn):
    (n_p, d), n_s = xp.shape, xs.shape[0]
    n = n_p + n_s
    assert nout % tn == 0 and nout <= -(-w.shape[1] // tn) * tn
    tm = _pick_tile(np.gcd(n_p, n_s), (1024, 512, 256))
    npt, spec_p, spec_s = _two_group_specs(n_p, n_s, tm, d, dict(pipeline_mode=pl.Buffered(1)))
    return pl.pallas_call(
        functools.partial(_rms_matmul_kernel, npt=npt),
        grid=(n // tm, nout // tn),
        in_specs=[
            spec_p, spec_s,
            pl.BlockSpec((1, d), lambda i, j: (0, 0)),
            pl.BlockSpec((d, tn), lambda i, j: (0, j)),
        ],
        out_specs=pl.BlockSpec((tm, tn), lambda i, j: (i, j)),
        out_shape=jax.ShapeDtypeStruct((n, nout), F32),
        scratch_shapes=[pltpu.VMEM((tm, d), BF16)],
        compiler_params=_params("parallel", "arbitrary"),
        name="in_proj",
    )(xp, xs, g, w)


def _qkv_post_kernel(zq_ref, zc_ref, zs_ref, zw_ref, cos_ref, sin_ref, gq_ref, gk_ref,
                     qtok_ref, qh_ref, qt_ref, rc_ref, rs_ref, rw_ref, tc_ref, ts_ref, tw_ref,
                     ks_ref, vs_ref, kw_ref, vw_ref):
    cos = cos_ref[...]
    sin = sin_ref[...]
    lane = lax.broadcasted_iota(jnp.int32, cos.shape, 1)
    first = (lane % HD) < (HD // 2)
    hi_ = lax.broadcasted_iota(jnp.int32, (LANES, LANES), 0) // HD
    hj_ = lax.broadcasted_iota(jnp.int32, (LANES, LANES), 1) // HD
    same_head = jnp.where(hi_ == hj_, 1.0, 0.0).astype(BF16)

    def norm_rope(xc, g_row):
        ms = _dot_exact_lhs(xc * xc, same_head) * (1.0 / HD)
        y = (xc * lax.rsqrt(ms + EPS)) * g_row
        partner = jnp.where(first, pltpu.roll(y, LANES - HD // 2, 1), pltpu.roll(y, HD // 2, 1))
        return y * cos + partner * sin

    gq = gq_ref[...]
    for c in range(NSA_W // LANES):
        sl = slice(c * LANES, (c + 1) * LANES)
        qc = norm_rope(zq_ref[:, sl], gq)
        qtok_ref[:, sl] = qc
        qb = qc.astype(BF16)
        qh_ref[2 * c] = qb[:, :HD]
        qh_ref[2 * c + 1] = qb[:, HD:]
        qtb = qc.T.astype(BF16)
        qt_ref[2 * c] = qtb[:HD]
        qt_ref[2 * c + 1] = qtb[HD:]

    tm = zq_ref.shape[0]
    zin = (zc_ref, zs_ref, zw_ref)
    rows = (rc_ref, rs_ref, rw_ref)
    rows_t = (tc_ref, ts_ref, tw_ref)
    kg = (None, ks_ref, kw_ref)
    vg = (None, vs_ref, vw_ref)
    for br in range(3):
        gk = gk_ref[br:br + 1, :]
        cols = []
        for c in range(KW // LANES):
            cols.append(norm_rope(zin[br][:, c * LANES:(c + 1) * LANES], gk))
        for c in range(KW // LANES):
            cols.append(zin[br][:, KW + c * LANES:KW + (c + 1) * LANES])
        cols_t = [col.T for col in cols]
        for c, col in enumerate(cols):
            rows[br][:, c * LANES:(c + 1) * LANES] = col
            rows_t[br][c * LANES:(c + 1) * LANES, :] = cols_t[c]
        if kg[br] is not None:
            wt = vg[br].shape[-1]
            for c in range(KW // LANES):
                kb = cols[c].astype(BF16)
                kg[br][2 * c] = kb[:, :HD]
                kg[br][2 * c + 1] = kb[:, HD:]
                vtb = cols_t[KW // LANES + c].astype(BF16)
                for t in range(tm // wt):
                    vg[br][2 * c, t] = vtb[:HD, t * wt:(t + 1) * wt]
                    vg[br][2 * c + 1, t] = vtb[HD:, t * wt:(t + 1) * wt]


def _qkv_post(z1, cos_tab, sin_tab, gq, gk, n_prompt, seq):
    n = z1.shape[0]
    tm = 256
    n_ptiles = n_prompt // tm
    tiles_per_seq = seq // tm
    n_slabs = -(-n // seq)

    def tab_map(i):
        return (jnp.where(i < n_ptiles, i % tiles_per_seq, tiles_per_seq), 0)

    row = lambda i: (i, 0)
    head = lambda i: (0, i, 0)
    slab = lambda i: (i // tiles_per_seq, 0, i % tiles_per_seq)
    rows_sd = jax.ShapeDtypeStruct((n, KVW), F32)
    rows_t_sd = jax.ShapeDtypeStruct((n_slabs, KVW, seq), F32)
    grp_sd = jax.ShapeDtypeStruct((KVH, n, HD), BF16)
    assert SEL_KT % tm == 0 and tm % Q_BLK == 0
    out_shape = (
        jax.ShapeDtypeStruct((n, NSA_W), F32),
        jax.ShapeDtypeStruct((HEADS, n, HD), BF16),
        jax.ShapeDtypeStruct((HEADS, HD, n), BF16),
        rows_sd, rows_sd, rows_sd,
        rows_t_sd, rows_t_sd, rows_t_sd,
        grp_sd,
        jax.ShapeDtypeStruct((KVH, -(-n // SEL_KT), HD, SEL_KT), BF16),
        grp_sd,
        jax.ShapeDtypeStruct((KVH, n // Q_BLK, HD, Q_BLK), BF16),
    )
    rows_spec = pl.BlockSpec((tm, KVW), row)
    rows_t_spec = pl.BlockSpec((None, KVW, tm), slab)
    grp_spec = pl.BlockSpec((KVH, tm, HD), head)
    per = SEL_KT // tm
    out_specs = (
        pl.BlockSpec((tm, NSA_W), row),
        pl.BlockSpec((HEADS, tm, HD), head),
        pl.BlockSpec((HEADS, HD, tm), lambda i: (0, 0, i)),
        rows_spec, rows_spec, rows_spec,
        rows_t_spec, rows_t_spec, rows_t_spec,
        grp_spec,
        pl.BlockSpec((KVH, 1, HD, tm), lambda i: (0, i // per, 0, i % per)),
        grp_spec,
        pl.BlockSpec((KVH, tm // Q_BLK, HD, Q_BLK), lambda i: (0, i, 0, 0)),
    )
    kv_in = lambda br: pl.BlockSpec((tm, KVW), lambda i: (i, C1_KV // KVW + br))
    return pl.pallas_call(
        _qkv_post_kernel,
        grid=(n // tm,),
        in_specs=[
            pl.BlockSpec((tm, NSA_W), lambda i: (i, C1_Q // NSA_W)),
            kv_in(0), kv_in(1), kv_in(2),
            pl.BlockSpec((tm, LANES), tab_map),
            pl.BlockSpec((tm, LANES), tab_map),
            pl.BlockSpec((1, LANES), lambda i: (0, 0)),
            pl.BlockSpec((3, LANES), lambda i: (0, 0)),
        ],
        out_specs=out_specs,
        out_shape=out_shape,
        compiler_params=_params("parallel"),
        name="qkv_post",
    )(z1, z1, z1, z1, cos_tab, sin_tab, gq, gk)


def _compress_pair(flat, w2, pe8, w2bd, n_ch):
    ab = _dot(flat, w2)
    hp = _dot(pe8.astype(BF16), w2)
    hpe = hp[0:1, :LANES] + hp[1:2, LANES:]
    nxt = pltpu.roll(ab[:, LANES:], n_ch - 1, 0)
    h = ab[:, :LANES] + nxt + hpe
    act = h * jax.nn.sigmoid(h)
    return _dot(act.astype(BF16), w2bd)


def _scatter_chunk_rows(xs_ref, perm, tile_t, c0):
    rpp = tile_t.shape[1] // D_CMP
    xp = _nt(perm, tile_t)
    for t in range(D_CMP):
        xs_ref[t, c0:c0 + rpp, :] = xp[t * rpp:(t + 1) * rpp, :]


def _chunk_flat(xs_ref, p):
    return jnp.concatenate([xs_ref[t, :, p * LANES:(p + 1) * LANES] for t in range(D_CMP)], axis=1).astype(BF16)


def _compress_prompt_kernel(rows_ref, w2k_ref, w2v_ref, pek_ref, pev_ref, bdk_ref, bdv_ref, perm_ref, kc_ref, vc_ref,
                            xs_ref, *, n_ch):
    w2 = (w2k_ref, w2v_ref)
    pe = (pek_ref, pev_ref)
    bd = (bdk_ref, bdv_ref)
    outs = (kc_ref, vc_ref)
    tile_rows = perm_ref.shape[0]
    rpp = tile_rows // D_CMP
    for i in range(n_ch // rpp):
        _scatter_chunk_rows(xs_ref, perm_ref[...], rows_ref[:, i * tile_rows:(i + 1) * tile_rows].astype(BF16), i * rpp)
    for p in range(KVW // LANES):
        kv = p // 2
        flat = _chunk_flat(xs_ref, p)
        res = _compress_pair(flat, w2[kv][...], pe[kv][...], bd[kv][...], n_ch).astype(BF16)
        g0 = 2 * (p % 2)
        outs[kv][0, g0] = res[:, :HD]
        outs[kv][0, g0 + 1] = res[:, HD:]


def _compress_prompt(rows_cmp, cw, batch, seq):
    n_ch = seq // D_CMP
    const = lambda b: (0, 0)
    out_sd = jax.ShapeDtypeStruct((batch, KVH, n_ch, HD), BF16)
    out_spec = pl.BlockSpec((1, KVH, n_ch, HD), lambda b: (b, 0, 0, 0))
    return pl.pallas_call(
        functools.partial(_compress_prompt_kernel, n_ch=n_ch),
        grid=(batch,),
        in_specs=[pl.BlockSpec((None, KVW, seq), lambda b: (b, 0, 0))] + [pl.BlockSpec(w.shape, const) for w in cw],
        out_specs=(out_spec, out_spec),
        out_shape=(out_sd, out_sd),
        scratch_shapes=[pltpu.VMEM((D_CMP, n_ch, KVW), F32)],
        compiler_params=_params("parallel"),
        name="compress_prompt",
    )(rows_cmp, *cw)


def _rank_select(score_t, n_blocks, n_rows, n_top):
    st = score_t[:n_rows]
    jio = lax.broadcasted_iota(jnp.int32, st.shape, 0)
    cnt = jnp.zeros(st.shape, F32)
    for i in range(n_blocks):
        cnt = cnt + _beats(score_t[i:i + 1, :], st, jio, i)
    return jnp.where(cnt < n_top, 1.0, 0.0)


def _beats(row, st, jio, i):
    return jnp.where(row > st, 1.0, jnp.where(row == st, jnp.where(jio > i, 1.0, 0.0), 0.0))


def _pad_rows(x, n):
    if x.shape[0] == n:
        return x
    return jnp.concatenate([x, jnp.zeros((n - x.shape[0], x.shape[1]), x.dtype)], axis=0)


def _cmp_select_kernel(qh_ref, kc_ref, vc_ref, ov_ref, ocmp_ref, sel_ref, st_ref, *, nc, ns, n_top):
    s0 = pl.program_id(1) * Q_BLK
    ncp = kc_ref.shape[2]
    qpos_c = s0 + lax.broadcasted_iota(jnp.int32, (Q_BLK, ncp), 0)
    cidx = lax.broadcasted_iota(jnp.int32, (Q_BLK, ncp), 1)
    cbias = jnp.where((cidx * D_CMP + (L_CMP - 1) <= qpos_c) & (cidx < nc), 0.0, NEG)
    qpos_1 = s0 + lax.broadcasted_iota(jnp.int32, (GRP * Q_BLK, 1), 0) % Q_BLK
    row_ok = jnp.where((qpos_1 >= L_CMP - 1) & (nc > 0), 1.0, 0.0)
    qpos = s0 + lax.broadcasted_iota(jnp.int32, (Q_BLK, LANES), 0)
    blk = lax.broadcasted_iota(jnp.int32, (Q_BLK, LANES), 1)
    visible = (blk * L_SEL <= qpos) & (blk < ns)
    cur = qpos // L_SEL
    forced = (blk == 0) | (blk == cur) | (blk == cur - 1)
    nsp = -(-ns // 8) * 8
    ov = ov_ref[...]
    grp = range(KVH)
    s = [_nt(jnp.concatenate([qh_ref[GRP * g + r] for r in range(GRP)], axis=0) * (HD ** -0.5), kc_ref[0, g])
         for g in grp]
    s = [jnp.concatenate([s[g][r * Q_BLK:(r + 1) * Q_BLK] + cbias for r in range(GRP)], axis=0) for g in grp]
    e = [jnp.exp(s[g] - jnp.max(s[g], axis=-1, keepdims=True)) for g in grp]
    p = [e[g] * (row_ok / jnp.maximum(jnp.sum(e[g], axis=-1, keepdims=True), 1.0)) for g in grp]
    o = [_dot(p[g].astype(BF16), vc_ref[0, g]) for g in grp]
    for g in grp:
        for r in range(GRP):
            ocmp_ref[GRP * g + r] = o[g][r * Q_BLK:(r + 1) * Q_BLK]
    psum = [p[g][0:Q_BLK] + p[g][Q_BLK:2 * Q_BLK] + p[g][2 * Q_BLK:3 * Q_BLK] + p[g][3 * Q_BLK:] for g in grp]
    imp = [_dot_exact_lhs(psum[g], ov) for g in grp]
    for g in grp:
        score = jnp.where(visible, imp[g] + jnp.where(forced, BIG, 0.0), -BIG)
        st_ref[g] = score.T

    n_vis = jnp.minimum((s0 + Q_BLK - 1) // L_SEL + 1, ns)
    jio = lax.broadcasted_iota(jnp.int32, (nsp, LANES), 0)

    def count(i, cnts):
        return tuple(c + _beats(st_ref[g, pl.ds(i, 1), :], st_ref[g, 0:nsp, :], jio, i) for g, c in enumerate(cnts))

    cnts = lax.fori_loop(0, n_vis, count, tuple(jnp.zeros((nsp, LANES), F32) for _ in range(KVH)))
    vis_t = (jio * L_SEL <= s0 + lax.broadcasted_iota(jnp.int32, (nsp, LANES), 1)) & (jio < ns)
    for g in range(KVH):
        sel_ref[g] = jnp.where((cnts[g] < n_top) & vis_t, 0.0, NEG)


def _cmp_select(q_heads, kc, vc, ov, batch, seq):
    nqt = seq // Q_BLK
    n_ch = kc.shape[2]
    ns = seq // L_SEL
    nsp = -(-ns // 8) * 8
    kern = functools.partial(_cmp_select_kernel, nc=n_ch - 1, ns=ns, n_top=min(N_TOP, ns))
    kv_spec = pl.BlockSpec((1, KVH, n_ch, HD), lambda b, t: (b, 0, 0, 0))
    return pl.pallas_call(
        kern,
        grid=(batch, nqt),
        in_specs=[
            pl.BlockSpec((HEADS, Q_BLK, HD), lambda b, t: (0, b * nqt + t, 0)),
            kv_spec, kv_spec,
            pl.BlockSpec(ov.shape, lambda b, t: (0, 0)),
        ],
        out_specs=(
            pl.BlockSpec((HEADS, Q_BLK, HD), lambda b, t: (0, b * nqt + t, 0)),
            pl.BlockSpec((None, KVH, nsp, LANES), lambda b, t: (b * nqt + t, 0, 0, 0)),
        ),
        out_shape=(
            jax.ShapeDtypeStruct((HEADS, batch * seq, HD), F32),
            jax.ShapeDtypeStruct((batch * nqt, KVH, nsp, LANES), F32),
        ),
        scratch_shapes=[pltpu.VMEM((KVH, LANES, LANES), F32)],
        compiler_params=_params("parallel", "parallel"),
        name="cmp_select",
    )(q_heads, kc, vc, ov)


def _sel_win_kernel(qt_ref, ks_ref, vst_ref, kw_ref, vwt_ref, bias_ref, zbr_ref, ocmp_ref, oa_ref, q_s):
    kt = vst_ref.shape[-1]
    s0 = pl.program_id(1) * Q_BLK
    cols = GRP * Q_BLK
    n_kt = (s0 + Q_BLK + kt - 1) // kt
    bpt = kt // L_SEL
    key_io = lax.broadcasted_iota(jnp.int32, (kt, Q_BLK), 0)
    qpos = s0 + lax.broadcasted_iota(jnp.int32, (kt, Q_BLK), 1)
    ones_row = 2 * 8
    va_rows = HD + ones_row

    for g in range(KVH):
        q_s[g] = jnp.concatenate([qt_ref[GRP * g + r] for r in range(GRP)], axis=1) * (HD ** -0.5)

    def with_bias(st, bias):
        return jnp.concatenate([st[:, r * Q_BLK:(r + 1) * Q_BLK] + bias for r in range(GRP)], axis=1)

    def body(j, carry):
        k0 = pl.multiple_of(j * kt, kt)
        grp = range(KVH)
        causal = jnp.where(k0 + key_io <= qpos, 0.0, NEG)
        st = [_dot(ks_ref[g, pl.ds(k0, kt), :], q_s[g]) for g in grp]
        bias = [jnp.concatenate([jnp.broadcast_to(bias_ref[g, pl.ds(j * bpt + b, 1), :], (L_SEL, Q_BLK))
                                 for b in range(bpt)], axis=0) + causal for g in grp]
        st = [with_bias(st[g], bias[g]) for g in grp]
        m_new = [jnp.maximum(carry[g][0], jnp.max(st[g], axis=0, keepdims=True)) for g in grp]
        corr = [jnp.exp(carry[g][0] - m_new[g]) for g in grp]
        p = [jnp.exp(st[g] - m_new[g]).astype(BF16) for g in grp]
        ones = jnp.ones((ones_row, kt), BF16)
        pv = [_dot(jnp.concatenate([vst_ref[g, j], ones], axis=0), p[g]) for g in grp]
        return tuple((m_new[g], carry[g][1] * corr[g] + pv[g]) for g in grp)

    init = tuple((jnp.full((1, cols), NEG, F32), jnp.zeros((va_rows, cols), F32)) for _ in range(KVH))
    res = lax.fori_loop(0, n_kt, body, init)

    wl = WINDOW + Q_BLK
    wstart = pl.multiple_of(jnp.maximum(s0 - WINDOW, 0), Q_BLK)
    wt0 = wstart // Q_BLK
    kposw = wstart + lax.broadcasted_iota(jnp.int32, (wl, Q_BLK), 0)
    qposw = s0 + lax.broadcasted_iota(jnp.int32, (wl, Q_BLK), 1)
    wbias = jnp.where((kposw <= qposw) & (kposw > qposw - WINDOW), 0.0, NEG)
    ones_w = jnp.ones((ones_row, wl), BF16)
    gates = jax.nn.sigmoid(zbr_ref[...])
    for g in range(KVH):
        acc = res[g][1]
        o_sel_t = acc[:HD] / acc[HD:HD + 1]
        st = with_bias(_dot(kw_ref[g, pl.ds(wstart, wl), :], q_s[g]), wbias)
        m = jnp.max(st, axis=0, keepdims=True)
        e = jnp.exp(st - m).astype(BF16)
        vwt = jnp.concatenate([vwt_ref[g, wt0 + i] for i in range(wl // Q_BLK)], axis=1)
        pv = _dot(jnp.concatenate([vwt, ones_w], axis=0), e)
        o_win_t = pv[:HD] / jnp.maximum(pv[HD:HD + 1], 1.0)
        for r in range(GRP):
            h = GRP * g + r
            cs = slice(r * Q_BLK, (r + 1) * Q_BLK)
            both = jnp.concatenate([o_sel_t[:, cs], o_win_t[:, cs]], axis=0).T
            o = (gates[:, h:h + 1] * ocmp_ref[h]
                 + gates[:, HEADS + h:HEADS + h + 1] * both[:, :HD]
                 + gates[:, 2 * HEADS + h:2 * HEADS + h + 1] * both[:, HD:])
            oa_ref[:, h * HD:(h + 1) * HD] = o


def _sel_win(q_t, ks, vs_t, kw, vw_t, sel_bias, z, ocmp, batch, seq):
    nqt = seq // Q_BLK
    assert seq % SEL_KT == 0
    once = dict(pipeline_mode=pl.Buffered(1))
    k_spec = pl.BlockSpec((KVH, seq, HD), lambda b, t: (0, b, 0), **once)
    tile = lambda b, t: (b * nqt + t, 0, 0, 0)
    return pl.pallas_call(
        _sel_win_kernel,
        grid=(batch, nqt),
        in_specs=[
            pl.BlockSpec((HEADS, HD, Q_BLK), lambda b, t: (0, 0, b * nqt + t)),
            k_spec,
            pl.BlockSpec((KVH, seq // SEL_KT, HD, SEL_KT), lambda b, t: (0, b, 0, 0), **once),
            k_spec,
            pl.BlockSpec((KVH, seq // Q_BLK, HD, Q_BLK), lambda b, t: (0, b, 0, 0), **once),
            pl.BlockSpec((None,) + sel_bias.shape[1:], tile),
            pl.BlockSpec((Q_BLK, LANES), lambda b, t: (b * nqt + t, C1_BR // LANES)),
            pl.BlockSpec((HEADS, Q_BLK, HD), lambda b, t: (0, b * nqt + t, 0)),
        ],
        out_specs=pl.BlockSpec((Q_BLK, NSA_W), lambda b, t: (b * nqt + t, 0)),
        out_shape=jax.ShapeDtypeStruct((z.shape[0], NSA_W), F32),
        scratch_shapes=[pltpu.VMEM((KVH, HD, GRP * Q_BLK), BF16)],
        compiler_params=_params("parallel", "arbitrary"),
        name="sel_win",
    )(q_t, ks, vs_t, kw, vw_t, sel_bias, z, ocmp)


def _hgrn_gates(zq, zf, logits):
    mx = jnp.max(logits, axis=0, keepdims=True)
    ex = jnp.exp(logits - mx)
    lb = ex[0:1] / jnp.sum(ex, axis=0, keepdims=True)
    hq = zq * jax.nn.sigmoid(zq)
    log_sig = -(jnp.maximum(-zf, 0.0) + jnp.log1p(jnp.exp(-jnp.abs(zf))))
    a = jnp.log(lb)
    b = jnp.log1p(-lb) + log_sig
    log_f = jnp.maximum(a, b) + jnp.log1p(jnp.exp(-jnp.abs(a - b)))
    hk = (1.0 - lb) * jax.nn.sigmoid(-zf)
    return hq, hk, log_f


def _intra_chunk(ac, qc, kc, vc):
    n = ac.shape[0]
    sub = 8
    tio = lax.broadcasted_iota(jnp.int32, (sub, ac.shape[1]), 0)
    accs = [jnp.zeros((sub, vc.shape[1]), F32) for _ in range(n // sub)]
    for s in range(n):
        a_s, k_s, v_s = ac[s:s + 1, :], kc[s:s + 1, :], vc[s:s + 1, :]
        for i in range(s // sub, n // sub):
            rs = slice(i * sub, (i + 1) * sub)
            w = qc[rs] * k_s * jnp.exp(ac[rs] - a_s)
            if i == s // sub:
                w = jnp.where(tio >= s - i * sub, w, 0.0)
            accs[i] = accs[i] + jnp.sum(w, axis=-1, keepdims=True) * v_s
    return accs[0] if len(accs) == 1 else jnp.concatenate(accs, axis=0)


def _hgrn_prompt_kernel(zq_ref, zf_ref, zi_ref, lg_ref, tri_ref, full_ref, o_ref, sout_ref,
                        st_ref, a_s, q_s, k_s, qt_s, ktb_s, eal_s, vt_s):
    t = pl.program_id(1)
    tt = zq_ref.shape[0]

    @pl.when(t == 0)
    def _():
        st_ref[...] = jnp.zeros(st_ref.shape, F32)

    hq, hk, log_f = _hgrn_gates(zq_ref[...], zf_ref[...], lg_ref[...])
    a = _dot_exact_rhs(tri_ref[...], log_f)
    al = _dot_exact_rhs(full_ref[...], log_f)
    a_s[...] = a
    q_s[...] = hq
    k_s[...] = hk
    qt_s[...] = hq * jnp.exp(a)
    ktb_s[...] = (hk * jnp.exp(al - a)).astype(BF16)
    eal_s[...] = jnp.exp(al)
    for h in range(HG_H):
        vt_s[h] = zi_ref[:, h * HG_DV:(h + 1) * HG_DV].T.astype(BF16)
    lane_tok = lax.broadcasted_iota(jnp.int32, (HG_DV, tt), 1)

    def chunk(c, carry):
        r0 = pl.multiple_of(c * HG_SUB, HG_SUB)
        in_chunk = (lane_tok >= r0) & (lane_tok < r0 + HG_SUB)
        for h in range(HG_H):
            sl = slice(h * HG_DK, (h + 1) * HG_DK)
            st = st_ref[h]
            o = _nt(qt_s[pl.ds(r0, HG_SUB), sl].astype(BF16), st.astype(BF16))
            o = o + _intra_chunk(a_s[pl.ds(r0, HG_SUB), sl], q_s[pl.ds(r0, HG_SUB), sl],
                                 k_s[pl.ds(r0, HG_SUB), sl], zi_ref[pl.ds(r0, HG_SUB), sl])
            o_ref[pl.ds(r0, HG_SUB), sl] = o
            vtm = jnp.where(in_chunk, vt_s[h], jnp.zeros_like(vt_s[h]))
            st_ref[h] = st * eal_s[pl.ds(r0, 1), sl] + _dot(vtm, ktb_s[:, sl])
        return carry

    lax.fori_loop(0, tt // HG_SUB, chunk, 0)

    @pl.when(t == pl.num_programs(1) - 1)
    def _():
        for h in range(HG_H):
            sout_ref[0, h] = st_ref[h].T


def _hgrn_prompt(z, logits, batch, seq):
    tt = 128
    nt = seq // tt
    sub = np.arange(tt) // HG_SUB
    same = sub[:, None] == sub[None, :]
    tri = jnp.asarray((same & (np.arange(tt)[None, :] <= np.arange(tt)[:, None])).astype(np.float32), BF16)
    full = jnp.asarray(same.astype(np.float32), BF16)
    col = lambda c: (lambda b, t: (b * nt + t, c // HG_W))
    const = lambda b, t: (0, 0)
    return pl.pallas_call(
        _hgrn_prompt_kernel,
        grid=(batch, nt),
        in_specs=[
            pl.BlockSpec((tt, HG_W), col(C2_HQ)),
            pl.BlockSpec((tt, HG_W), col(C2_HF)),
            pl.BlockSpec((tt, HG_W), col(C2_HI)),
            pl.BlockSpec((2, HG_W), const),
            pl.BlockSpec((tt, tt), const),
            pl.BlockSpec((tt, tt), const),
        ],
        out_specs=(
            pl.BlockSpec((tt, HG_W), lambda b, t: (b * nt + t, 0)),
            pl.BlockSpec((1, HG_H, HG_DK, HG_DV), lambda b, t: (b, 0, 0, 0)),
        ),
        out_shape=(
            jax.ShapeDtypeStruct((z.shape[0], HG_W), F32),
            jax.ShapeDtypeStruct((batch, HG_H, HG_DK, HG_DV), F32),
        ),
        scratch_shapes=[
            pltpu.VMEM((HG_H, HG_DV, HG_DK), F32),
            pltpu.VMEM((tt, HG_W), F32),
            pltpu.VMEM((tt, HG_W), F32),
            pltpu.VMEM((tt, HG_W), F32),
            pltpu.VMEM((tt, HG_W), F32),
            pltpu.VMEM((tt, HG_W), BF16),
            pltpu.VMEM((tt, HG_W), F32),
            pltpu.VMEM((HG_H, HG_DV, tt), BF16),
        ],
        compiler_params=_params("parallel", "arbitrary"),
        name="hgrn_prompt",
    )(z, z, z, logits, tri, full)


def _hgrn_sample_kernel(zq_ref, zf_ref, zi_ref, lg_ref, s0_ref, ob_hbm_ref, o_ref, sout_ref, *, s_len):
    del ob_hbm_ref
    n_seq = s0_ref.shape[0]
    hq_all, hk_all, logf_all = _hgrn_gates(zq_ref[...], zf_ref[...], lg_ref[...])
    rio = lax.broadcasted_iota(jnp.int32, (s_len, HG_W), 0)
    row0 = lax.broadcasted_iota(jnp.int32, (s_len, HG_DK), 0) == 0
    pad = HG_DK - 2 * s_len
    for q in range(n_seq):
        rs = slice(q * s_len, (q + 1) * s_len)
        hq, hk, log_f = hq_all[rs], hk_all[rs], logf_all[rs]
        a = log_f
        d = 1
        while d < s_len:
            a = a + jnp.where(rio >= d, pltpu.roll(a, d, 0), 0.0)
            d *= 2
        a_last = jnp.sum(log_f, axis=0, keepdims=True)
        qt = hq * jnp.exp(a)
        kt = hk * jnp.exp(a_last - a)
        ea = jnp.exp(a_last)
        hv = zi_ref[rs, :]
        for h in range(HG_H):
            sl = slice(h * HG_DK, (h + 1) * HG_DK)
            s_old = s0_ref[q, h]
            qt16 = _pad_rows(qt[:, sl], 16).astype(BF16)
            o = _dot(qt16, s_old.astype(BF16))[:s_len]
            o = o + _intra_chunk(a[:, sl], hq[:, sl], hk[:, sl], hv[:, sl])
            o_ref[rs, sl] = o
            m = jnp.concatenate([kt[:, sl], jnp.where(row0, jnp.broadcast_to(ea[:, sl], (s_len, HG_DK)), 0.0),
                                 jnp.zeros((pad, HG_DK), F32)], axis=0)
            mt = m.T
            e_col = mt[:, s_len:s_len + 1]
            vpad = _pad_rows(hv[:, sl], HG_DK).astype(BF16)
            sout_ref[q, h] = e_col * s_old + _dot(mt.astype(BF16), vpad)


def _hgrn_sample(z, logits, state, ob, n_prompt, dec_batch, s_len):
    n_seq = 4
    rows = n_seq * s_len
    base = n_prompt // rows
    col = lambda c: (lambda b: (base + b, c // HG_W))
    st_spec = pl.BlockSpec((n_seq, HG_H, HG_DK, HG_DV), lambda b: (b, 0, 0, 0))
    return pl.pallas_call(
        functools.partial(_hgrn_sample_kernel, s_len=s_len),
        grid=(dec_batch // n_seq,),
        in_specs=[
            pl.BlockSpec((rows, HG_W), col(C2_HQ)),
            pl.BlockSpec((rows, HG_W), col(C2_HF)),
            pl.BlockSpec((rows, HG_W), col(C2_HI)),
            pl.BlockSpec((2, HG_W), lambda b: (0, 0)),
            st_spec,
            pl.BlockSpec(memory_space=pl.ANY),
        ],
        out_specs=(pl.BlockSpec((rows, HG_W), lambda b: (base + b, 0)), st_spec),
        out_shape=(
            jax.ShapeDtypeStruct(ob.shape, F32),
            jax.ShapeDtypeStruct(state.shape, F32),
        ),
        input_output_aliases={5: 0},
        compiler_params=_params("parallel"),
        name="hgrn_sample",
    )(z, z, z, logits, state, ob)


def _sample_nsa_kernel(pt_ref, *refs, npg, s_len, past):
    del pt_ref
    cmp_pages = refs[:npg]
    sel_pages = refs[npg:2 * npg]
    (win_ref, q_ref, rsel_ref, rwin_ref, zbr_ref, w2k_ref, w2v_ref, pek_ref, pev_ref, bdk_ref, bdv_ref, perm_ref,
     ov_ref, e_ref, oa_hbm_ref, oa_ref, swin_ref, xs_ref, kcv_ref) = refs[2 * npg:]
    del oa_hbm_ref
    step = pl.program_id(0)

    @pl.when(step == 0)
    def _():
        kcv_ref[...] = jnp.zeros(kcv_ref.shape, BF16)

    kc_all = kcv_ref[(step + 1) % 2, 0]
    vc_all = kcv_ref[(step + 1) % 2, 1]
    page = cmp_pages[0].shape[1]
    rpp = page // D_CMP
    n_ch = npg * rpp
    nc = n_ch - 1
    wb = win_ref.shape[1]
    ns = -(-(past + s_len) // L_SEL)
    nsp = -(-ns // 8) * 8
    n_top = min(N_TOP, ns)
    rows = HEADS * s_len

    w2 = (w2k_ref, w2v_ref)
    pe = (pek_ref, pev_ref)
    bd = (bdk_ref, bdv_ref)
    comp = []

    def stage1_scatter(lo, hi):
        for i in range(lo, hi):
            _scatter_chunk_rows(xs_ref, perm_ref[...], cmp_pages[i][...].astype(BF16), i * rpp)

    def stage1_compress(p):
        kv = p // 2
        comp.append(_compress_pair(_chunk_flat(xs_ref, p), w2[kv][...], pe[kv][...], bd[kv][...], n_ch))

    q = q_ref[...]
    lane8 = lax.broadcasted_iota(jnp.int32, (s_len, LANES), 1)
    zero8 = jnp.zeros((s_len, LANES), F32)
    blocks = []
    for g in range(KVH):
        for r in range(GRP):
            h = GRP * g + r
            src = q[:, (h // 2) * LANES:(h // 2 + 1) * LANES]
            if (r % 2) != (g % 2):
                src = pltpu.roll(src, HD, 1)
            keep = (lane8 < HD) if g % 2 == 0 else (lane8 >= HD)
            piece = jnp.where(keep, src, 0.0)
            blocks.append(jnp.concatenate([piece, zero8] if g // 2 == 0 else [zero8, piece], axis=1))
    qbd = (jnp.concatenate(blocks, axis=0) * (HD ** -0.5)).astype(BF16)

    def rowpos(shape):
        return past + lax.broadcasted_iota(jnp.int32, shape, 0) % s_len

    stage1_scatter(0, npg // 2)

    s = _nt(qbd, kc_all)
    cidx = lax.broadcasted_iota(jnp.int32, s.shape, 1)
    cvalid = (cidx * D_CMP + (L_CMP - 1) <= rowpos(s.shape)) & (cidx < nc)
    s = jnp.where(cvalid, s, NEG)
    m = jnp.max(s, axis=-1, keepdims=True)
    e = jnp.where(cvalid, jnp.exp(s - m), 0.0)
    p = e / jnp.maximum(jnp.sum(e, axis=-1, keepdims=True), 1.0)
    ocmp = _dot(p.astype(BF16), vc_all)

    stage1_scatter(npg // 2, npg)

    gs = GRP * s_len
    psum = []
    for g in range(KVH):
        acc = p[g * gs:g * gs + s_len]
        for r in range(1, GRP):
            acc = acc + p[g * gs + r * s_len:g * gs + (r + 1) * s_len]
        psum.append(acc)
    imp = _dot_exact_lhs(jnp.concatenate(psum, axis=0), ov_ref[...])
    qpos = rowpos(imp.shape)
    blk = lax.broadcasted_iota(jnp.int32, imp.shape, 1)
    visible = (blk * L_SEL <= qpos) & (blk < ns)
    cur = qpos // L_SEL
    forced = (blk == 0) | (blk == cur) | (blk == cur - 1)
    score = jnp.where(visible, imp + jnp.where(forced, BIG, 0.0), -BIG)
    stage1_compress(0)
    sel_t = _rank_select(_pad_rows(score, LANES).T, ns, nsp, n_top)
    sel = _pad_rows(sel_t, LANES).T[:KVH * s_len]
    sel = jnp.where(visible, sel, 0.0)
    sel_rows = jnp.concatenate([sel[g * s_len:(g + 1) * s_len] for g in range(KVH) for _ in range(GRP)], axis=0)

    zpad = jnp.zeros((page - s_len, KW), F32)
    knew = jnp.concatenate([rsel_ref[:, :KW], zpad], axis=0).astype(BF16)
    vnew = jnp.concatenate([rsel_ref[:, KW:], zpad], axis=0).astype(BF16)
    def page_pair(i, lo):
        return jnp.concatenate([sel_pages[2 * i][lo:lo + KW, :], sel_pages[2 * i + 1][lo:lo + KW, :]],
                               axis=1).astype(BF16)

    s = jnp.concatenate([_dot(qbd, page_pair(i, 0)) for i in range(npg // 2)] + [_nt(qbd, knew)], axis=1)
    mf = _dot(sel_rows.astype(BF16), e_ref[...])
    kpos = lax.broadcasted_iota(jnp.int32, s.shape, 1)
    mask = (mf > 0.5) & (kpos <= rowpos(s.shape))
    stage1_compress(1)
    s = jnp.where(mask, s, NEG)
    m = jnp.max(s, axis=-1, keepdims=True)
    e = jnp.where(mask, jnp.exp(s - m), 0.0)
    l = jnp.sum(e, axis=-1, keepdims=True)
    eb = e.astype(BF16)
    acc = _dot(eb[:, npg * page:], vnew)
    for i in range(npg // 2):
        acc = acc + _nt(eb[:, 2 * i * page:(2 * i + 2) * page], page_pair(i, KW))
    osel = acc / l

    kwn = jnp.concatenate([rwin_ref[:, :KW], zpad], axis=0).astype(BF16)
    vwn = jnp.concatenate([rwin_ref[:, KW:], zpad], axis=0).astype(BF16)
    s = jnp.concatenate([_dot(qbd, win_ref[:KW, :].astype(BF16)), _nt(qbd, kwn)], axis=1)
    kposw = (past - wb) + lax.broadcasted_iota(jnp.int32, s.shape, 1)
    qposw = rowpos(s.shape)
    wmask = (kposw <= qposw) & (kposw > qposw - WINDOW) & (kposw >= 0)
    s = jnp.where(wmask, s, NEG)
    m = jnp.max(s, axis=-1, keepdims=True)
    e = jnp.where(wmask, jnp.exp(s - m), 0.0)
    den = jnp.maximum(jnp.sum(e, axis=-1, keepdims=True), 1.0)
    eb = e.astype(BF16)
    owin = (_nt(eb[:, :wb], win_ref[KW:, :].astype(BF16)) + _dot(eb[:, wb:], vwn)) / den

    stage1_compress(2)

    gates = jax.nn.sigmoid(zbr_ref[...])
    for g in range(KVH):
        gl = slice((g // 2) * LANES, (g // 2 + 1) * LANES)
        for k2 in range(2):
            halves = []
            for r in (2 * k2, 2 * k2 + 1):
                h = GRP * g + r
                rs = slice(h * s_len, (h + 1) * s_len)
                val = (gates[:, h:h + 1] * ocmp[rs, gl]
                       + gates[:, HEADS + h:HEADS + h + 1] * osel[rs, gl]
                       + gates[:, 2 * HEADS + h:2 * HEADS + h + 1] * owin[rs, gl])
                if (r % 2) != (g % 2):
                    val = pltpu.roll(val, HD, 1)
                halves.append(val)
            c = 2 * g + k2
            oa_ref[:, c * LANES:(c + 1) * LANES] = jnp.where(lane8 < HD, halves[0], halves[1])

    new_t = [_pad_rows(rwin_ref[:, c * LANES:(c + 1) * LANES], LANES).T for c in range(KVW // LANES)]
    new_t = jnp.concatenate(new_t, axis=0)
    lane_w = lax.broadcasted_iota(jnp.int32, (KVW, LANES), 1)
    n_col = wb // LANES
    rolled = [pltpu.roll(win_ref[:, c * LANES:(c + 1) * LANES], LANES - s_len, 1) for c in range(n_col)]
    rolled.append(pltpu.roll(new_t, LANES - s_len, 1))
    for c in range(n_col):
        swin_ref[:, c * LANES:(c + 1) * LANES] = jnp.where(lane_w < LANES - s_len, rolled[c], rolled[c + 1])

    stage1_compress(3)
    kcv_ref[step % 2, 0] = jnp.concatenate(comp[0:2], axis=1).astype(BF16)
    kcv_ref[step % 2, 1] = jnp.concatenate(comp[2:4], axis=1).astype(BF16)


def _sample_nsa(page_table, cache_cmp, cache_sel, cache_win, q_tok, rows_sel, rows_win, z, cw, ov, e_mat, oa,
                n_prompt, dec_batch, s_len):
    npg = page_table.shape[1]
    page = cache_cmp.shape[2]
    past = npg * page
    wb = cache_win.shape[2]
    base = n_prompt // s_len
    last = dec_batch - 1
    steps = dec_batch + 1
    seq2 = lambda b: jnp.maximum(b - 1, 0)
    page_table = jnp.concatenate([page_table, page_table[last:], page_table[:1], page_table], axis=0)
    cmp_specs = [pl.BlockSpec((None, KVW, page), (lambda b, pt, i=i: (pt[b, i], 0, 0))) for i in range(npg)]
    sel_specs = [pl.BlockSpec((None, KVW, page), (lambda b, pt, i=i: (pt[steps + b, i], 0, 0))) for i in range(npg)]
    row = lambda b, pt: (base + seq2(b), 0)
    const = lambda b, pt: (0, 0)
    in_specs = (
        cmp_specs + sel_specs
        + [pl.BlockSpec((None, KVW, wb), lambda b, pt: (seq2(b), 0, 0)),
           pl.BlockSpec((s_len, NSA_W), row),
           pl.BlockSpec((s_len, KVW), row),
           pl.BlockSpec((s_len, KVW), row),
           pl.BlockSpec((s_len, LANES), lambda b, pt: (base + seq2(b), C1_BR // LANES))]
        + [pl.BlockSpec(w.shape, const) for w in cw]
        + [pl.BlockSpec(ov.shape, const), pl.BlockSpec(e_mat.shape, const), pl.BlockSpec(memory_space=pl.ANY)]
    )
    n_ch = past // D_CMP
    grid_spec = pltpu.PrefetchScalarGridSpec(
        num_scalar_prefetch=1,
        grid=(dec_batch + 1,),
        in_specs=in_specs,
        out_specs=(
            pl.BlockSpec((s_len, NSA_W), row),
            pl.BlockSpec((None, KVW, wb), lambda b, pt: (seq2(b), 0, 0)),
        ),
        scratch_shapes=[
            pltpu.VMEM((D_CMP, n_ch, KVW), F32),
            pltpu.VMEM((2, 2, n_ch, KW), BF16),
        ],
    )
    args = (page_table, *([cache_cmp] * npg), *([cache_sel] * npg), cache_win, q_tok, rows_sel, rows_win, z, *cw,
            ov, e_mat, oa)
    return pl.pallas_call(
        functools.partial(_sample_nsa_kernel, npg=npg, s_len=s_len, past=past),
        grid_spec=grid_spec,
        out_shape=(
            jax.ShapeDtypeStruct(oa.shape, F32),
            jax.ShapeDtypeStruct((dec_batch, KVW, wb), F32),
        ),
        input_output_aliases={len(args) - 1: 0},
        compiler_params=_params("arbitrary"),
        name="sample_nsa",
    )(*args)


def _mix_kernel(xp_ref, xs_ref, oa_ref, ob_ref, zhg_ref, ga_ref, gb_ref, gn_ref, wa_ref, wb_ref, wo_ref, o_ref, *, npt):
    ya = _dot(oa_ref[...].astype(BF16), wa_ref[...])
    gn = gn_ref[...]
    parts = []
    for h in range(HG_H):
        sl = slice(h * HG_DV, (h + 1) * HG_DV)
        ob = ob_ref[:, sl]
        ms = jnp.mean(ob * ob, axis=-1, keepdims=True)
        zg = zhg_ref[:, sl]
        y = ((ob * lax.rsqrt(ms + EPS)) * gn) * (zg * jax.nn.sigmoid(zg))
        parts.append(y.astype(BF16))
    yb = _dot(jnp.concatenate(parts, axis=1), wb_ref[...])
    mix = jax.nn.sigmoid(ga_ref[...]) * ya + jax.nn.sigmoid(gb_ref[...]) * yb
    y = _dot(mix.astype(BF16), wo_ref[...])

    @pl.when(pl.program_id(0) < npt)
    def _():
        o_ref[...] = xp_ref[...] + y

    @pl.when(pl.program_id(0) >= npt)
    def _():
        o_ref[...] = xs_ref[...] + y


def _mix(xp, xs, oa, ob, z, gn, wa, wb, wo):
    (n_p, d), n_s = xp.shape, xs.shape[0]
    n = n_p + n_s
    tm = 256
    npt, spec_p, spec_s = _two_group_specs(n_p, n_s, tm, d)
    row = lambda i: (i, 0)
    const = lambda i: (0, 0)
    resident = lambda w: pl.BlockSpec(w.shape, const, pipeline_mode=pl.Buffered(1))
    return pl.pallas_call(
        functools.partial(_mix_kernel, npt=npt),
        grid=(n // tm,),
        in_specs=[
            spec_p, spec_s,
            pl.BlockSpec((tm, NSA_W), row),
            pl.BlockSpec((tm, HG_W), row),
            pl.BlockSpec((tm, HG_W), lambda i: (i, C2_HG // HG_W)),
            pl.BlockSpec((tm, d), lambda i: (i, C2_GA // D_MODEL)),
            pl.BlockSpec((tm, d), lambda i: (i, C2_GB // D_MODEL)),
            pl.BlockSpec((1, HG_DV), const),
            resident(wa), resident(wb), resident(wo),
        ],
        out_specs=pl.BlockSpec((tm, d), row),
        out_shape=jax.ShapeDtypeStruct((n, d), F32),
        compiler_params=_params("parallel"),
        name="mix_out",
    )(xp, xs, oa, ob, z, z, z, gn, wa, wb, wo)


def _ffn_kernel(x_ref, g_ref, w1_ref, w2_ref, op_ref, os_ref, xn_ref, acc_ref, *, npt):
    i = pl.program_id(0)
    j = pl.program_id(1)
    last = pl.num_programs(1) - 1

    @pl.when(j == 0)
    def _():
        xn_ref[...] = _rmsnorm_bf16(x_ref[...], g_ref[...])
        acc_ref[...] = jnp.zeros(acc_ref.shape, F32)

    h = _dot(xn_ref[...], w1_ref[...])
    h = jnp.square(jnp.maximum(h, 0.0)).astype(BF16)
    acc_ref[...] += _dot(h, w2_ref[...])

    @pl.when((j == last) & (i < npt))
    def _():
        op_ref[...] = x_ref[...] + acc_ref[...]

    @pl.when((j == last) & (i >= npt))
    def _():
        os_ref[...] = x_ref[...] + acc_ref[...]


def _ffn(x, g, w1, w2, n_p):
    n, d = x.shape
    n_s = n - n_p
    dff = w1.shape[1]
    tm = _pick_tile(np.gcd(n_p, n_s), (512, 256))
    tf = 1024
    npt, spec_p, spec_s = _two_group_specs(n_p, n_s, tm, d)
    return pl.pallas_call(
        functools.partial(_ffn_kernel, npt=npt),
        grid=(n // tm, dff // tf),
        in_specs=[
            pl.BlockSpec((tm, d), lambda i, j: (i, 0)),
            pl.BlockSpec((1, d), lambda i, j: (0, 0)),
            pl.BlockSpec((d, tf), lambda i, j: (0, j)),
            pl.BlockSpec((tf, d), lambda i, j: (j, 0)),
        ],
        out_specs=(spec_p, spec_s),
        out_shape=(jax.ShapeDtypeStruct((n_p, d), F32), jax.ShapeDtypeStruct((n_s, d), F32)),
        scratch_shapes=[pltpu.VMEM((tm, d), BF16), pltpu.VMEM((tm, d), F32)],
        compiler_params=_params("arbitrary", "arbitrary"),
        name="ffn",
    )(x, g, w1, w2)


def _shift_cast_kernel(main_ref, halo_ref, o_ref, *, shift):
    n_col = main_ref.shape[1] // LANES
    lane = lax.broadcasted_iota(jnp.int32, (main_ref.shape[0], LANES), 1)
    cols = [main_ref[:, c * LANES:(c + 1) * LANES] for c in range(n_col)] + [halo_ref[...]]
    rolled = [pltpu.roll(x, LANES - shift, 1) for x in cols]
    for c in range(n_col):
        o_ref[:, c * LANES:(c + 1) * LANES] = jnp.where(lane < LANES - shift, rolled[c], rolled[c + 1]).astype(BF16)


def _tail_w_in(w_in):
    d = w_in.shape[0]
    start = sum(_SIZES[:3])
    width = 512
    base, shift = divmod(start, LANES)
    assert (base * LANES) % width == 0 and N_Z2 % width == 0 and 0 < shift
    first = base * LANES // width
    per = width // LANES
    return pl.pallas_call(
        functools.partial(_shift_cast_kernel, shift=shift),
        grid=(N_Z2 // width,),
        in_specs=[
            pl.BlockSpec((d, width), lambda j: (0, first + j)),
            pl.BlockSpec((d, LANES), lambda j: (0, (first + j + 1) * per)),
        ],
        out_specs=pl.BlockSpec((d, width), lambda j: (0, j)),
        out_shape=jax.ShapeDtypeStruct((d, N_Z2), BF16),
        compiler_params=_params("parallel"),
        name="w_tail",
    )(w_in, w_in)


def _compress_weights(cmp_pe, cmp_w1, cmp_w2):
    m = L_CMP // D_CMP
    eye2 = jnp.eye(2, dtype=F32)
    out = []
    w2s, pes, bds = [], [], []
    for kv in range(2):
        w1p = cmp_w1[kv].reshape(m, D_CMP, HD, HD).transpose(1, 2, 0, 3)
        w = w1p[:, None, :, :, None, :] * eye2[None, :, None, None, :, None]
        w2s.append(w.reshape(D_CMP * LANES, m * LANES).astype(BF16))
        pe = cmp_pe[kv].reshape(m, D_CMP, 1, HD)
        pe = jnp.broadcast_to(pe, (m, D_CMP, 2, HD)).reshape(m, D_CMP * LANES)
        pes.append(jnp.concatenate([pe, jnp.zeros((8 - m, D_CMP * LANES), F32)], axis=0))
        bds.append(jnp.kron(eye2, cmp_w2[kv]).astype(BF16))
    tile_rows = LANES
    rpp = tile_rows // D_CMP
    src = np.arange(tile_rows)
    perm = np.zeros((tile_rows, tile_rows), np.float32)
    perm[(src % D_CMP) * rpp + src // D_CMP, src] = 1.0
    return (w2s[0], w2s[1], pes[0], pes[1], bds[0], bds[1], jnp.asarray(perm, BF16))


def _overlap_matrix(nc_pad, ns):
    cs = np.arange(nc_pad) * D_CMP
    ss = np.arange(LANES) * L_SEL
    ov = np.clip(np.minimum(cs[:, None] + L_CMP, ss[None, :] + L_SEL) - np.maximum(cs[:, None], ss[None, :]), 0, None)
    ov = (ov / D_CMP) * (np.arange(LANES)[None, :] < ns)
    return jnp.asarray(ov.astype(np.float32), BF16)


def _expand_matrix(n_keys):
    e = (np.arange(n_keys)[None, :] // L_SEL) == np.arange(LANES)[:, None]
    return jnp.asarray(e.astype(np.float32), BF16)


def _rope_tables(seq, past, s_len, tm):
    half = HD // 2
    inv = THETA ** (-jnp.arange(half, dtype=F32) * 2.0 / HD)
    pos = jnp.concatenate([jnp.arange(seq), past + jnp.arange(tm) % s_len]).astype(F32)
    ang = pos[:, None] * inv[None, :]
    cos = jnp.cos(ang)
    sin = jnp.sin(ang)
    cos_t = jnp.concatenate([cos, cos, cos, cos], axis=1)
    sin_t = jnp.concatenate([-sin, sin, -sin, sin], axis=1)
    return cos_t, sin_t


def kernel(x_prompt, x_sample, cache_cmp_kv, cache_sel_kv, cache_win_kv, state_hgrn, page_table, norm1_g, w_in,
           q_norm_g, k_norm_g, cmp_pe, cmp_w1, cmp_w2, w_proj_a, hgrn_lb_logits, hgrn_norm_g, w_proj_b, w_out,
           norm2_g, w_ff1, w_ff2):
    assert w_in.shape[0] == 1, "single-layer step"
    batch, seq, d = x_prompt.shape
    dec_batch, s_len, _ = x_sample.shape
    n_prompt = batch * seq
    n_sample = dec_batch * s_len
    npg = page_table.shape[1]
    page = cache_cmp_kv.shape[2]
    past = npg * page
    wb = cache_win_kv.shape[2]
    assert seq % Q_BLK == 0 and seq >= WINDOW + Q_BLK and n_prompt % 256 == 0 and n_sample % 256 == 0
    assert s_len % 8 == 0 and page % D_CMP == 0 and 2 * s_len <= HG_DK and dec_batch % 4 == 0 and npg % 2 == 0

    xp = x_prompt.reshape(n_prompt, d)
    xs = x_sample.reshape(n_sample, d)

    w_z2 = _tail_w_in(w_in[0])
    cw = _compress_weights(cmp_pe[0], cmp_w1[0], cmp_w2[0])
    cos_t, sin_t = _rope_tables(seq, past, s_len, 256)
    gq = jnp.tile(q_norm_g[0][None, :], (1, 2))
    gk = jnp.tile(k_norm_g[0], (1, 2))
    logits = hgrn_lb_logits.astype(F32)

    def token_minor(c):
        return jnp.transpose(c, (0, 2, 3, 4, 1)).reshape(c.shape[0], KVW, c.shape[1])

    def token_major(c_t):
        return jnp.transpose(c_t.reshape(c_t.shape[0], 2, KVH, HD, c_t.shape[2]), (0, 4, 1, 2, 3))

    z1 = _in_proj(xp, xs, norm1_g, w_in[0][:, :N_Z1].astype(BF16), N_Z1, 896)
    z2 = _in_proj(xp, xs, norm1_g, w_z2, N_Z2, 1024)
    (q_tok, q_heads, q_t, rows_cmp, rows_sel, rows_win, rt_cmp, rt_sel, rt_win,
     ks, vs_t, kw, vw_t) = _qkv_post(z1, cos_t, sin_t, gq, gk, n_prompt, seq)

    kc, vc = _compress_prompt(rt_cmp, cw, batch, seq)
    ov_p = _overlap_matrix(seq // D_CMP, seq // L_SEL)
    ocmp, sel_bias = _cmp_select(q_heads, kc, vc, ov_p, batch, seq)
    oa = _sel_win(q_t, ks, vs_t, kw, vw_t, sel_bias, z1, ocmp, batch, seq)
    ob, p_state = _hgrn_prompt(z2, logits, batch, seq)

    ns_s = -(-(past + s_len) // L_SEL)
    ov_s = _overlap_matrix(past // D_CMP, ns_s)
    oa, s_win_t = _sample_nsa(
        page_table, token_minor(cache_cmp_kv[0]), token_minor(cache_sel_kv[0]), token_minor(cache_win_kv[0]),
        q_tok, rows_sel, rows_win, z1, cw, ov_s, _expand_matrix(past + page), oa, n_prompt, dec_batch, s_len)
    ob, s_state = _hgrn_sample(z2, logits, state_hgrn[0], ob, n_prompt, dec_batch, s_len)

    x2 = _mix(xp, xs, oa, ob, z2, hgrn_norm_g, w_proj_a[0].astype(BF16), w_proj_b[0].astype(BF16),
              w_out[0].astype(BF16))
    y_p, y_s = _ffn(x2, norm2_g, w_ff1[0].astype(BF16), w_ff2[0].astype(BF16), n_prompt)

    kv_shape = (2, KVH, HD)
    wp = min(WINDOW, seq)
    return (
        y_p.reshape(batch, seq, d),
        y_s.reshape(dec_batch, s_len, d),
        token_major(rt_cmp[:batch])[None],
        token_major(rt_sel[:batch])[None],
        token_major(rt_win[:batch, :, seq - wp:])[None],
        p_state[None],
        rows_cmp[n_prompt:].reshape((1, dec_batch, s_len) + kv_shape),
        rows_sel[n_prompt:].reshape((1, dec_batch, s_len) + kv_shape),
        token_major(s_win_t)[None],
        s_state[None],
    )
```

```python
import functools

import numpy as np
import jax
import jax.numpy as jnp
from jax import lax
from jax.experimental import pallas as pl
from jax.experimental.pallas import tpu as pltpu

F32 = jnp.float32
BF16 = jnp.bfloat16

D_MODEL = 2048
HEADS = 16
KVH = 4
GRP = HEADS // KVH
HD = 64
NSA_W = HEADS * HD
L_CMP = 32
D_CMP = 16
L_SEL = 64
N_TOP = 16
WINDOW = 512
Q_BLK = 128
THETA = 10000.0
HG_H = 8
HG_DK = 128
HG_DV = 128
HG_W = HG_H * HG_DK
HG_SUB = 16
SEL_KT = 512
EPS = 1e-6
NEG = -1e30
BIG = 1e9

LANES = 128
KVW = 2 * KVH * HD
KW = KVH * HD

_SIZES = (NSA_W, 6 * KVH * HD, 3 * HEADS, HG_W, HG_W, HG_W, HG_W, D_MODEL, D_MODEL)
C1_Q, C1_KV, C1_BR, N_Z1 = 0, 1024, 2560, 2688
C2_HQ, C2_HF, C2_HI, C2_HG, C2_GA, C2_GB, N_Z2 = 0, 1024, 2048, 3072, 4096, 6144, 8192

VMEM_LIMIT = 56 * 1024 * 1024


def _params(*sem):
    return pltpu.CompilerParams(dimension_semantics=sem, vmem_limit_bytes=VMEM_LIMIT)


def _nt(a, b):
    return lax.dot_general(a, b, (((1,), (1,)), ((), ())), preferred_element_type=F32)


def _dot(a, b):
    return jnp.dot(a, b, preferred_element_type=F32)


def _split3(x):
    hi = x.astype(BF16)
    r1 = x - hi.astype(F32)
    mid = r1.astype(BF16)
    lo = (r1 - mid.astype(F32)).astype(BF16)
    return hi, mid, lo


def _dot_exact_rhs(m, x):
    hi, mid, lo = _split3(x)
    return _dot(m, hi) + _dot(m, mid) + _dot(m, lo)


def _dot_exact_lhs(x, m):
    hi, mid, lo = _split3(x)
    return _dot(hi, m) + _dot(mid, m) + _dot(lo, m)


def _pick_tile(n, cands):
    for c in cands:
        if n % c == 0:
            return c
    raise ValueError(f"no tile in {cands} divides {n}")


def _rmsnorm_bf16(x, g):
    ms = jnp.mean(x * x, axis=-1, keepdims=True)
    return ((x * lax.rsqrt(ms + EPS)) * g).astype(BF16)


def _two_group_specs(n_p, n_s, tm, width, sample_kwargs=None):
    npt = n_p // tm
    assert n_p % tm == 0 and n_s % tm == 0
    spec_p = pl.BlockSpec((tm, width), lambda i, *_: (jnp.minimum(i, npt - 1), 0))
    spec_s = pl.BlockSpec((tm, width), lambda i, *_: (jnp.maximum(i - npt, 0), 0), **(sample_kwargs or {}))
    return npt, spec_p, spec_s


def _rms_matmul_kernel(xp_ref, xs_ref, g_ref, w_ref, o_ref, xn_ref, *, npt):
    i = pl.program_id(0)

    @pl.when((pl.program_id(1) == 0) & (i < npt))
    def _():
        xn_ref[...] = _rmsnorm_bf16(xp_ref[...], g_ref[...])

    @pl.when((pl.program_id(1) == 0) & (i >= npt))
    def _():
        xn_ref[...] = _rmsnorm_bf16(xs_ref[...], g_ref[...])

    o_ref[...] = _dot(xn_ref[...], w_ref[...].astype(BF16))


def _in_proj(xp, xs, g, w, nout, tn):
    (n_p, d), n_s = xp.shape, xs.shape[0]
    n = n_p + n_s
    assert nout % tn == 0 and nout <= -(-w.shape[1] // tn) * tn
    tm = _pick_tile(np.gcd(n_p, n_s), (1024, 512, 256))
    npt, spec_p, spec_s = _two_group_specs(n_p, n_s, tm, d, dict(pipeline_mode=pl.Buffered(1)))
    return pl.pallas_call(
        functools.partial(_rms_matmul_kernel, npt=npt),
        grid=(n // tm, nout // tn),
        in_specs=[
            spec_p, spec_s,
            pl.BlockSpec((1, d), lambda i, j: (0, 0)),
            pl.BlockSpec((d, tn), lambda i, j: (0, j)),
        ],
        out_specs=pl.BlockSpec((tm, tn), lambda i, j: (i, j)),
        out_shape=jax.ShapeDtypeStruct((n, nout), F32),
        scratch_shapes=[pltpu.VMEM((tm, d), BF16)],
        compiler_params=_params("parallel", "arbitrary"),
        name="in_proj",
    )(xp, xs, g, w)


def _qkv_post_kernel(zq_ref, zc_ref, zs_ref, zw_ref, cos_ref, sin_ref, gq_ref, gk_ref,
                     qtok_ref, qt_ref, rc_ref, rs_ref, rw_ref, tc_ref, ts_ref, tw_ref,
                     ks_ref, vs_ref, kw_ref, vw_ref):
    cos = cos_ref[...]
    sin = sin_ref[...]
    lane = lax.broadcasted_iota(jnp.int32, cos.shape, 1)
    first = (lane % HD) < (HD // 2)
    hi_ = lax.broadcasted_iota(jnp.int32, (LANES, LANES), 0) // HD
    hj_ = lax.broadcasted_iota(jnp.int32, (LANES, LANES), 1) // HD
    same_head = jnp.where(hi_ == hj_, 1.0, 0.0).astype(BF16)

    def norm_rope(xc, g_row):
        ms = _dot_exact_lhs(xc * xc, same_head) * (1.0 / HD)
        y = (xc * lax.rsqrt(ms + EPS)) * g_row
        partner = jnp.where(first, pltpu.roll(y, LANES - HD // 2, 1), pltpu.roll(y, HD // 2, 1))
        return y * cos + partner * sin

    gq = gq_ref[...]
    for c in range(NSA_W // LANES):
        sl = slice(c * LANES, (c + 1) * LANES)
        qc = norm_rope(zq_ref[:, sl], gq)
        qtok_ref[:, sl] = qc
        qtb = qc.T.astype(BF16)
        qt_ref[2 * c] = qtb[:HD]
        qt_ref[2 * c + 1] = qtb[HD:]

    tm = zq_ref.shape[0]
    zin = (zc_ref, zs_ref, zw_ref)
    rows = (rc_ref, rs_ref, rw_ref)
    rows_t = (tc_ref, ts_ref, tw_ref)
    kg = (None, ks_ref, kw_ref)
    vg = (None, vs_ref, vw_ref)
    for br in range(3):
        gk = gk_ref[br:br + 1, :]
        cols = []
        for c in range(KW // LANES):
            cols.append(norm_rope(zin[br][:, c * LANES:(c + 1) * LANES], gk))
        for c in range(KW // LANES):
            cols.append(zin[br][:, KW + c * LANES:KW + (c + 1) * LANES])
        cols_t = [col.T for col in cols]
        for c, col in enumerate(cols):
            rows[br][:, c * LANES:(c + 1) * LANES] = col
            rows_t[br][c * LANES:(c + 1) * LANES, :] = cols_t[c]
        if kg[br] is not None:
            wt = vg[br].shape[-1]
            for c in range(KW // LANES):
                kb = cols[c].astype(BF16)
                kg[br][2 * c] = kb[:, :HD]
                kg[br][2 * c + 1] = kb[:, HD:]
                vtb = cols_t[KW // LANES + c].astype(BF16)
                for t in range(tm // wt):
                    vg[br][2 * c, t] = vtb[:HD, t * wt:(t + 1) * wt]
                    vg[br][2 * c + 1, t] = vtb[HD:, t * wt:(t + 1) * wt]


def _qkv_post(z1, cos_tab, sin_tab, gq, gk, n_prompt, seq):
    n = z1.shape[0]
    tm = 256
    n_ptiles = n_prompt // tm
    tiles_per_seq = seq // tm
    n_slabs = -(-n // seq)

    def tab_map(i):
        return (jnp.where(i < n_ptiles, i % tiles_per_seq, tiles_per_seq), 0)

    row = lambda i: (i, 0)
    head = lambda i: (0, i, 0)
    slab = lambda i: (i // tiles_per_seq, 0, i % tiles_per_seq)
    rows_sd = jax.ShapeDtypeStruct((n, KVW), F32)
    rows_t_sd = jax.ShapeDtypeStruct((n_slabs, KVW, seq), F32)
    grp_sd = jax.ShapeDtypeStruct((KVH, n, HD), BF16)
    assert SEL_KT % tm == 0 and tm % Q_BLK == 0
    out_shape = (
        jax.ShapeDtypeStruct((n, NSA_W), F32),
        jax.ShapeDtypeStruct((HEADS, HD, n), BF16),
        rows_sd, rows_sd, rows_sd,
        rows_t_sd, rows_t_sd, rows_t_sd,
        grp_sd,
        jax.ShapeDtypeStruct((KVH, -(-n // SEL_KT), HD, SEL_KT), BF16),
        grp_sd,
        jax.ShapeDtypeStruct((KVH, n // Q_BLK, HD, Q_BLK), BF16),
    )
    rows_spec = pl.BlockSpec((tm, KVW), row)
    rows_t_spec = pl.BlockSpec((None, KVW, tm), slab)
    grp_spec = pl.BlockSpec((KVH, tm, HD), head)
    per = SEL_KT // tm
    out_specs = (
        pl.BlockSpec((tm, NSA_W), row),
        pl.BlockSpec((HEADS, HD, tm), lambda i: (0, 0, i)),
        rows_spec, rows_spec, rows_spec,
        rows_t_spec, rows_t_spec, rows_t_spec,
        grp_spec,
        pl.BlockSpec((KVH, 1, HD, tm), lambda i: (0, i // per, 0, i % per)),
        grp_spec,
        pl.BlockSpec((KVH, tm // Q_BLK, HD, Q_BLK), lambda i: (0, i, 0, 0)),
    )
    kv_in = lambda br: pl.BlockSpec((tm, KVW), lambda i: (i, C1_KV // KVW + br))
    return pl.pallas_call(
        _qkv_post_kernel,
        grid=(n // tm,),
        in_specs=[
            pl.BlockSpec((tm, NSA_W), lambda i: (i, C1_Q // NSA_W)),
            kv_in(0), kv_in(1), kv_in(2),
            pl.BlockSpec((tm, LANES), tab_map),
            pl.BlockSpec((tm, LANES), tab_map),
            pl.BlockSpec((1, LANES), lambda i: (0, 0)),
            pl.BlockSpec((3, LANES), lambda i: (0, 0)),
        ],
        out_specs=out_specs,
        out_shape=out_shape,
        compiler_params=_params("parallel"),
        name="qkv_post",
    )(z1, z1, z1, z1, cos_tab, sin_tab, gq, gk)


def _compress_pair(flat, w2, pe8, w2bd, n_ch, transposed=False):
    ab = _dot(flat, w2)
    hp = _dot(pe8.astype(BF16), w2)
    hpe = hp[0:1, :LANES] + hp[1:2, LANES:]
    nxt = pltpu.roll(ab[:, LANES:], n_ch - 1, 0)
    h = ab[:, :LANES] + nxt + hpe
    act = (h * jax.nn.sigmoid(h)).astype(BF16)
    if transposed:
        return _nt(w2bd, act)
    return _dot(act, w2bd)


def _scatter_chunk_rows(xs_ref, perm, tile_t, c0):
    rpp = tile_t.shape[1] // D_CMP
    xp = _nt(perm, tile_t)
    for t in range(D_CMP):
        xs_ref[t, c0:c0 + rpp, :] = xp[t * rpp:(t + 1) * rpp, :]


def _chunk_flat(xs_ref, p):
    return jnp.concatenate([xs_ref[t, :, p * LANES:(p + 1) * LANES] for t in range(D_CMP)], axis=1).astype(BF16)


def _compress_prompt_kernel(rows_ref, w2k_ref, w2v_ref, pek_ref, pev_ref, bdk_ref, bdv_ref, perm_ref, kc_ref, vc_ref,
                            xs_ref, *, n_ch):
    w2 = (w2k_ref, w2v_ref)
    pe = (pek_ref, pev_ref)
    bd = (bdk_ref, bdv_ref)
    tile_rows = perm_ref.shape[0]
    rpp = tile_rows // D_CMP
    for i in range(n_ch // rpp):
        _scatter_chunk_rows(xs_ref, perm_ref[...], rows_ref[:, i * tile_rows:(i + 1) * tile_rows].astype(BF16), i * rpp)
    for p in range(KVW // LANES):
        kv = p // 2
        flat = _chunk_flat(xs_ref, p)
        res = _compress_pair(flat, w2[kv][...], pe[kv][...], bd[kv][...], n_ch, transposed=(kv == 1)).astype(BF16)
        g0 = 2 * (p % 2)
        if kv == 0:
            kc_ref[0, g0] = res[:, :HD]
            kc_ref[0, g0 + 1] = res[:, HD:]
        else:
            vc_ref[0, g0] = res[:HD]
            vc_ref[0, g0 + 1] = res[HD:]


def _compress_prompt(rows_cmp, cw, batch, seq):
    n_ch = seq // D_CMP
    const = lambda b: (0, 0)
    return pl.pallas_call(
        functools.partial(_compress_prompt_kernel, n_ch=n_ch),
        grid=(batch,),
        in_specs=[pl.BlockSpec((None, KVW, seq), lambda b: (b, 0, 0))] + [pl.BlockSpec(w.shape, const) for w in cw],
        out_specs=(pl.BlockSpec((1, KVH, n_ch, HD), lambda b: (b, 0, 0, 0)),
                   pl.BlockSpec((1, KVH, HD, n_ch), lambda b: (b, 0, 0, 0))),
        out_shape=(jax.ShapeDtypeStruct((batch, KVH, n_ch, HD), BF16),
                   jax.ShapeDtypeStruct((batch, KVH, HD, n_ch), BF16)),
        scratch_shapes=[pltpu.VMEM((D_CMP, n_ch, KVW), F32)],
        compiler_params=_params("parallel"),
        name="compress_prompt",
    )(rows_cmp, *cw)


def _rank_select(score_t, n_blocks, n_rows, n_top):
    st = score_t[:n_rows]
    jio = lax.broadcasted_iota(jnp.int32, st.shape, 0)
    cnt = jnp.zeros(st.shape, F32)
    for i in range(n_blocks):
        cnt = cnt + _beats(score_t[i:i + 1, :], st, jio, i)
    return jnp.where(cnt < n_top, 1.0, 0.0)


def _beats(row, st, jio, i):
    return jnp.where(row > st, 1.0, jnp.where(row == st, jnp.where(jio > i, 1.0, 0.0), 0.0))


def _pad_rows(x, n):
    if x.shape[0] == n:
        return x
    return jnp.concatenate([x, jnp.zeros((n - x.shape[0], x.shape[1]), x.dtype)], axis=0)


def _cmp_select_kernel(qt_ref, kc_ref, vct_ref, ovt_ref, ocmp_ref, sel_ref, st_ref, *, nc, ns, n_top):
    s0 = pl.program_id(1) * Q_BLK
    ncp = kc_ref.shape[2]
    nsp = st_ref.shape[1]
    c_io = lax.broadcasted_iota(jnp.int32, (ncp, Q_BLK), 0)
    qpos_c = s0 + lax.broadcasted_iota(jnp.int32, (ncp, Q_BLK), 1)
    cbias = jnp.where((c_io * D_CMP + (L_CMP - 1) <= qpos_c) & (c_io < nc), 0.0, NEG)
    qpos_1 = s0 + lax.broadcasted_iota(jnp.int32, (1, GRP * Q_BLK), 1) % Q_BLK
    col_ok = jnp.where((qpos_1 >= L_CMP - 1) & (nc > 0), 1.0, 0.0)
    jio = lax.broadcasted_iota(jnp.int32, (nsp, LANES), 0)
    qpos_n = s0 + lax.broadcasted_iota(jnp.int32, (nsp, LANES), 1)
    vis_t = (jio * L_SEL <= qpos_n) & (jio < ns)
    cur = qpos_n // L_SEL
    forced = (jio == 0) | (jio == cur) | (jio == cur - 1)
    ovt = ovt_ref[...]
    grp = range(KVH)
    qt = [jnp.concatenate([qt_ref[GRP * g + r] for r in range(GRP)], axis=1) * (HD ** -0.5) for g in grp]
    s = [_dot(kc_ref[0, g], qt[g]) for g in grp]
    s = [jnp.concatenate([s[g][:, r * Q_BLK:(r + 1) * Q_BLK] + cbias for r in range(GRP)], axis=1) for g in grp]
    e = [jnp.exp(s[g] - jnp.max(s[g], axis=0, keepdims=True)) for g in grp]
    p = [e[g] * (col_ok / jnp.maximum(jnp.sum(e[g], axis=0, keepdims=True), 1.0)) for g in grp]
    o = [_dot(vct_ref[0, g], p[g].astype(BF16)) for g in grp]
    for g in grp:
        for r in range(GRP):
            ocmp_ref[GRP * g + r] = o[g][:, r * Q_BLK:(r + 1) * Q_BLK]
    psum = [p[g][:, 0:Q_BLK] + p[g][:, Q_BLK:2 * Q_BLK] + p[g][:, 2 * Q_BLK:3 * Q_BLK] + p[g][:, 3 * Q_BLK:]
            for g in grp]
    imp = [_dot_exact_rhs(ovt, psum[g]) for g in grp]
    for g in grp:
        st_ref[g] = jnp.where(vis_t, imp[g][:nsp] + jnp.where(forced, BIG, 0.0), -BIG)

    n_vis = jnp.minimum((s0 + Q_BLK - 1) // L_SEL + 1, ns)

    def count(i, cnts):
        return tuple(c + _beats(st_ref[g, pl.ds(i, 1), :], st_ref[g, 0:nsp, :], jio, i) for g, c in enumerate(cnts))

    cnts = lax.fori_loop(0, n_vis, count, tuple(jnp.zeros((nsp, LANES), F32) for _ in range(KVH)))
    for g in range(KVH):
        sel_ref[g] = jnp.where((cnts[g] < n_top) & vis_t, 0.0, NEG)


def _cmp_select(q_t, kc, vc_t, ov_t, batch, seq):
    nqt = seq // Q_BLK
    n_ch = kc.shape[2]
    ns = seq // L_SEL
    nsp = -(-ns // 8) * 8
    kern = functools.partial(_cmp_select_kernel, nc=n_ch - 1, ns=ns, n_top=min(N_TOP, ns))
    q_spec = pl.BlockSpec((HEADS, HD, Q_BLK), lambda b, t: (0, 0, b * nqt + t))
    return pl.pallas_call(
        kern,
        grid=(batch, nqt),
        in_specs=[
            q_spec,
            pl.BlockSpec((1, KVH, n_ch, HD), lambda b, t: (b, 0, 0, 0)),
            pl.BlockSpec((1, KVH, HD, n_ch), lambda b, t: (b, 0, 0, 0)),
            pl.BlockSpec(ov_t.shape, lambda b, t: (0, 0)),
        ],
        out_specs=(
            q_spec,
            pl.BlockSpec((None, KVH, nsp, LANES), lambda b, t: (b * nqt + t, 0, 0, 0)),
        ),
        out_shape=(
            jax.ShapeDtypeStruct((HEADS, HD, batch * seq), F32),
            jax.ShapeDtypeStruct((batch * nqt, KVH, nsp, LANES), F32),
        ),
        scratch_shapes=[pltpu.VMEM((KVH, nsp, LANES), F32)],
        compiler_params=_params("parallel", "parallel"),
        name="cmp_select",
    )(q_t, kc, vc_t, ov_t)


def _sel_win_kernel(qt_ref, ks_ref, vst_ref, kw_ref, vwt_ref, bias_ref, zbr_ref, ocmp_ref, oa_ref, q_s):
    kt = vst_ref.shape[-1]
    s0 = pl.program_id(1) * Q_BLK
    cols = GRP * Q_BLK
    n_kt = (s0 + Q_BLK + kt - 1) // kt
    bpt = kt // L_SEL
    key_io = lax.broadcasted_iota(jnp.int32, (kt, Q_BLK), 0)
    qpos = s0 + lax.broadcasted_iota(jnp.int32, (kt, Q_BLK), 1)
    ones_row = 2 * 8
    va_rows = HD + ones_row

    for g in range(KVH):
        q_s[g] = jnp.concatenate([qt_ref[GRP * g + r] for r in range(GRP)], axis=1) * (HD ** -0.5)

    def with_bias(st, bias):
        return jnp.concatenate([st[:, r * Q_BLK:(r + 1) * Q_BLK] + bias for r in range(GRP)], axis=1)

    def body(j, carry):
        k0 = pl.multiple_of(j * kt, kt)
        grp = range(KVH)
        causal = jnp.where(k0 + key_io <= qpos, 0.0, NEG)
        st = [_dot(ks_ref[g, pl.ds(k0, kt), :], q_s[g]) for g in grp]
        bias = [jnp.concatenate([jnp.broadcast_to(bias_ref[g, pl.ds(j * bpt + b, 1), :], (L_SEL, Q_BLK))
                                 for b in range(bpt)], axis=0) + causal for g in grp]
        st = [with_bias(st[g], bias[g]) for g in grp]
        m_new = [jnp.maximum(carry[g][0], jnp.max(st[g], axis=0, keepdims=True)) for g in grp]
        corr = [jnp.exp(carry[g][0] - m_new[g]) for g in grp]
        p = [jnp.exp(st[g] - m_new[g]).astype(BF16) for g in grp]
        ones = jnp.ones((ones_row, kt), BF16)
        pv = [_dot(jnp.concatenate([vst_ref[g, j], ones], axis=0), p[g]) for g in grp]
        return tuple((m_new[g], carry[g][1] * corr[g] + pv[g]) for g in grp)

    init = tuple((jnp.full((1, cols), NEG, F32), jnp.zeros((va_rows, cols), F32)) for _ in range(KVH))
    res = lax.fori_loop(0, n_kt, body, init)

    wl = WINDOW + Q_BLK
    wstart = pl.multiple_of(jnp.maximum(s0 - WINDOW, 0), Q_BLK)
    wt0 = wstart // Q_BLK
    kposw = wstart + lax.broadcasted_iota(jnp.int32, (wl, Q_BLK), 0)
    qposw = s0 + lax.broadcasted_iota(jnp.int32, (wl, Q_BLK), 1)
    wbias = jnp.where((kposw <= qposw) & (kposw > qposw - WINDOW), 0.0, NEG)
    ones_w = jnp.ones((ones_row, wl), BF16)
    gates_t = jax.nn.sigmoid(zbr_ref[...]).T
    o_t = []
    for g in range(KVH):
        acc = res[g][1]
        o_sel_t = acc[:HD] / acc[HD:HD + 1]
        st = with_bias(_dot(kw_ref[g, pl.ds(wstart, wl), :], q_s[g]), wbias)
        m = jnp.max(st, axis=0, keepdims=True)
        e = jnp.exp(st - m).astype(BF16)
        vwt = jnp.concatenate([vwt_ref[g, wt0 + i] for i in range(wl // Q_BLK)], axis=1)
        pv = _dot(jnp.concatenate([vwt, ones_w], axis=0), e)
        o_win_t = pv[:HD] / jnp.maximum(pv[HD:HD + 1], 1.0)
        for r in range(GRP):
            h = GRP * g + r
            cs = slice(r * Q_BLK, (r + 1) * Q_BLK)
            o_t.append(gates_t[h:h + 1] * ocmp_ref[h]
                       + gates_t[HEADS + h:HEADS + h + 1] * o_sel_t[:, cs]
                       + gates_t[2 * HEADS + h:2 * HEADS + h + 1] * o_win_t[:, cs])
    for c in range(HEADS // 2):
        oa_ref[:, c * LANES:(c + 1) * LANES] = jnp.concatenate([o_t[2 * c], o_t[2 * c + 1]], axis=0).T


def _sel_win(q_t, ks, vs_t, kw, vw_t, sel_bias, z, ocmp, batch, seq):
    nqt = seq // Q_BLK
    assert seq % SEL_KT == 0
    once = dict(pipeline_mode=pl.Buffered(1))
    k_spec = pl.BlockSpec((KVH, seq, HD), lambda b, t: (0, b, 0), **once)
    tile = lambda b, t: (b * nqt + t, 0, 0, 0)
    return pl.pallas_call(
        _sel_win_kernel,
        grid=(batch, nqt),
        in_specs=[
            pl.BlockSpec((HEADS, HD, Q_BLK), lambda b, t: (0, 0, b * nqt + t)),
            k_spec,
            pl.BlockSpec((KVH, seq // SEL_KT, HD, SEL_KT), lambda b, t: (0, b, 0, 0), **once),
            k_spec,
            pl.BlockSpec((KVH, seq // Q_BLK, HD, Q_BLK), lambda b, t: (0, b, 0, 0), **once),
            pl.BlockSpec((None,) + sel_bias.shape[1:], tile),
            pl.BlockSpec((Q_BLK, LANES), lambda b, t: (b * nqt + t, C1_BR // LANES)),
            pl.BlockSpec((HEADS, HD, Q_BLK), lambda b, t: (0, 0, b * nqt + t)),
        ],
        out_specs=pl.BlockSpec((Q_BLK, NSA_W), lambda b, t: (b * nqt + t, 0)),
        out_shape=jax.ShapeDtypeStruct((z.shape[0], NSA_W), F32),
        scratch_shapes=[pltpu.VMEM((KVH, HD, GRP * Q_BLK), BF16)],
        compiler_params=_params("parallel", "arbitrary"),
        name="sel_win",
    )(q_t, ks, vs_t, kw, vw_t, sel_bias, z, ocmp)


def _hgrn_gates(zq, zf, logits):
    mx = jnp.max(logits, axis=0, keepdims=True)
    ex = jnp.exp(logits - mx)
    lb = ex[0:1] / jnp.sum(ex, axis=0, keepdims=True)
    hq = zq * jax.nn.sigmoid(zq)
    log_sig = -(jnp.maximum(-zf, 0.0) + jnp.log1p(jnp.exp(-jnp.abs(zf))))
    a = jnp.log(lb)
    b = jnp.log1p(-lb) + log_sig
    log_f = jnp.maximum(a, b) + jnp.log1p(jnp.exp(-jnp.abs(a - b)))
    hk = (1.0 - lb) * jax.nn.sigmoid(-zf)
    return hq, hk, log_f


def _intra_chunk(ac, qc, kc, vc):
    n = ac.shape[0]
    sub = 8
    tio = lax.broadcasted_iota(jnp.int32, (sub, ac.shape[1]), 0)
    accs = [jnp.zeros((sub, vc.shape[1]), F32) for _ in range(n // sub)]
    for s in range(n):
        a_s, k_s, v_s = ac[s:s + 1, :], kc[s:s + 1, :], vc[s:s + 1, :]
        for i in range(s // sub, n // sub):
            rs = slice(i * sub, (i + 1) * sub)
            w = qc[rs] * k_s * jnp.exp(ac[rs] - a_s)
            if i == s // sub:
                w = jnp.where(tio >= s - i * sub, w, 0.0)
            accs[i] = accs[i] + jnp.sum(w, axis=-1, keepdims=True) * v_s
    return accs[0] if len(accs) == 1 else jnp.concatenate(accs, axis=0)


def _hgrn_prompt_kernel(zq_ref, zf_ref, zi_ref, lg_ref, tri_ref, full_ref, o_ref, sout_ref,
                        st_ref, a_s, q_s, k_s, qt_s, ktb_s, eal_s, vt_s):
    t = pl.program_id(1)
    tt = zq_ref.shape[0]

    @pl.when(t == 0)
    def _():
        st_ref[...] = jnp.zeros(st_ref.shape, F32)

    hq, hk, log_f = _hgrn_gates(zq_ref[...], zf_ref[...], lg_ref[...])
    a = _dot_exact_rhs(tri_ref[...], log_f)
    al = _dot_exact_rhs(full_ref[...], log_f)
    a_s[...] = a
    q_s[...] = hq
    k_s[...] = hk
    qt_s[...] = hq * jnp.exp(a)
    ktb_s[...] = (hk * jnp.exp(al - a)).astype(BF16)
    eal_s[...] = jnp.exp(al)
    for h in range(HG_H):
        vt_s[h] = zi_ref[:, h * HG_DV:(h + 1) * HG_DV].T.astype(BF16)
    lane_tok = lax.broadcasted_iota(jnp.int32, (HG_DV, tt), 1)

    def chunk(c, carry):
        r0 = pl.multiple_of(c * HG_SUB, HG_SUB)
        in_chunk = (lane_tok >= r0) & (lane_tok < r0 + HG_SUB)
        for h in range(HG_H):
            sl = slice(h * HG_DK, (h + 1) * HG_DK)
            st = st_ref[h]
            o = _nt(qt_s[pl.ds(r0, HG_SUB), sl].astype(BF16), st.astype(BF16))
            o = o + _intra_chunk(a_s[pl.ds(r0, HG_SUB), sl], q_s[pl.ds(r0, HG_SUB), sl],
                                 k_s[pl.ds(r0, HG_SUB), sl], zi_ref[pl.ds(r0, HG_SUB), sl])
            o_ref[pl.ds(r0, HG_SUB), sl] = o
            vtm = jnp.where(in_chunk, vt_s[h], jnp.zeros_like(vt_s[h]))
            st_ref[h] = st * eal_s[pl.ds(r0, 1), sl] + _dot(vtm, ktb_s[:, sl])
        return carry

    lax.fori_loop(0, tt // HG_SUB, chunk, 0)

    @pl.when(t == pl.num_programs(1) - 1)
    def _():
        for h in range(HG_H):
            sout_ref[0, h] = st_ref[h].T


def _hgrn_prompt(z, logits, batch, seq):
    tt = 128
    nt = seq // tt
    sub = np.arange(tt) // HG_SUB
    same = sub[:, None] == sub[None, :]
    tri = jnp.asarray((same & (np.arange(tt)[None, :] <= np.arange(tt)[:, None])).astype(np.float32), BF16)
    full = jnp.asarray(same.astype(np.float32), BF16)
    col = lambda c: (lambda b, t: (b * nt + t, c // HG_W))
    const = lambda b, t: (0, 0)
    return pl.pallas_call(
        _hgrn_prompt_kernel,
        grid=(batch, nt),
        in_specs=[
            pl.BlockSpec((tt, HG_W), col(C2_HQ)),
            pl.BlockSpec((tt, HG_W), col(C2_HF)),
            pl.BlockSpec((tt, HG_W), col(C2_HI)),
            pl.BlockSpec((2, HG_W), const),
            pl.BlockSpec((tt, tt), const),
            pl.BlockSpec((tt, tt), const),
        ],
        out_specs=(
            pl.BlockSpec((tt, HG_W), lambda b, t: (b * nt + t, 0)),
            pl.BlockSpec((1, HG_H, HG_DK, HG_DV), lambda b, t: (b, 0, 0, 0)),
        ),
        out_shape=(
            jax.ShapeDtypeStruct((z.shape[0], HG_W), F32),
            jax.ShapeDtypeStruct((batch, HG_H, HG_DK, HG_DV), F32),
        ),
        scratch_shapes=[
            pltpu.VMEM((HG_H, HG_DV, HG_DK), F32),
            pltpu.VMEM((tt, HG_W), F32),
            pltpu.VMEM((tt, HG_W), F32),
            pltpu.VMEM((tt, HG_W), F32),
            pltpu.VMEM((tt, HG_W), F32),
            pltpu.VMEM((tt, HG_W), BF16),
            pltpu.VMEM((tt, HG_W), F32),
            pltpu.VMEM((HG_H, HG_DV, tt), BF16),
        ],
        compiler_params=_params("parallel", "arbitrary"),
        name="hgrn_prompt",
    )(z, z, z, logits, tri, full)


def _hgrn_sample_kernel(zq_ref, zf_ref, zi_ref, lg_ref, s0_ref, ob_hbm_ref, o_ref, sout_ref, *, s_len):
    del ob_hbm_ref
    n_seq = s0_ref.shape[0]
    hq_all, hk_all, logf_all = _hgrn_gates(zq_ref[...], zf_ref[...], lg_ref[...])
    rio = lax.broadcasted_iota(jnp.int32, (s_len, HG_W), 0)
    row0 = lax.broadcasted_iota(jnp.int32, (s_len, HG_DK), 0) == 0
    pad = HG_DK - 2 * s_len
    for q in range(n_seq):
        rs = slice(q * s_len, (q + 1) * s_len)
        hq, hk, log_f = hq_all[rs], hk_all[rs], logf_all[rs]
        a = log_f
        d = 1
        while d < s_len:
            a = a + jnp.where(rio >= d, pltpu.roll(a, d, 0), 0.0)
            d *= 2
        a_last = jnp.sum(log_f, axis=0, keepdims=True)
        qt = hq * jnp.exp(a)
        kt = hk * jnp.exp(a_last - a)
        ea = jnp.exp(a_last)
        hv = zi_ref[rs, :]
        for h in range(HG_H):
            sl = slice(h * HG_DK, (h + 1) * HG_DK)
            s_old = s0_ref[q, h]
            qt16 = _pad_rows(qt[:, sl], 16).astype(BF16)
            o = _dot(qt16, s_old.astype(BF16))[:s_len]
            o = o + _intra_chunk(a[:, sl], hq[:, sl], hk[:, sl], hv[:, sl])
            o_ref[rs, sl] = o
            m = jnp.concatenate([kt[:, sl], jnp.where(row0, jnp.broadcast_to(ea[:, sl], (s_len, HG_DK)), 0.0),
                                 jnp.zeros((pad, HG_DK), F32)], axis=0)
            mt = m.T
            e_col = mt[:, s_len:s_len + 1]
            vpad = _pad_rows(hv[:, sl], HG_DK).astype(BF16)
            sout_ref[q, h] = e_col * s_old + _dot(mt.astype(BF16), vpad)


def _hgrn_sample(z, logits, state, ob, n_prompt, dec_batch, s_len):
    n_seq = 4
    rows = n_seq * s_len
    base = n_prompt // rows
    col = lambda c: (lambda b: (base + b, c // HG_W))
    st_spec = pl.BlockSpec((n_seq, HG_H, HG_DK, HG_DV), lambda b: (b, 0, 0, 0))
    return pl.pallas_call(
        functools.partial(_hgrn_sample_kernel, s_len=s_len),
        grid=(dec_batch // n_seq,),
        in_specs=[
            pl.BlockSpec((rows, HG_W), col(C2_HQ)),
            pl.BlockSpec((rows, HG_W), col(C2_HF)),
            pl.BlockSpec((rows, HG_W), col(C2_HI)),
            pl.BlockSpec((2, HG_W), lambda b: (0, 0)),
            st_spec,
            pl.BlockSpec(memory_space=pl.ANY),
        ],
        out_specs=(pl.BlockSpec((rows, HG_W), lambda b: (base + b, 0)), st_spec),
        out_shape=(
            jax.ShapeDtypeStruct(ob.shape, F32),
            jax.ShapeDtypeStruct(state.shape, F32),
        ),
        input_output_aliases={5: 0},
        compiler_params=_params("parallel"),
        name="hgrn_sample",
    )(z, z, z, logits, state, ob)


def _sample_nsa_kernel(pt_ref, *refs, npg, s_len, past):
    del pt_ref
    cmp_pages = refs[:npg]
    sel_pages = refs[npg:2 * npg]
    (win_ref, q_ref, rsel_ref, rwin_ref, zbr_ref, w2k_ref, w2v_ref, pek_ref, pev_ref, bdk_ref, bdv_ref, perm_ref,
     ov_ref, e_ref, oa_hbm_ref, oa_ref, swin_ref, xs_ref, kcv_ref) = refs[2 * npg:]
    del oa_hbm_ref
    step = pl.program_id(0)

    @pl.when(step == 0)
    def _():
        kcv_ref[...] = jnp.zeros(kcv_ref.shape, BF16)

    kc_all = kcv_ref[(step + 1) % 2, 0]
    vc_all = kcv_ref[(step + 1) % 2, 1]
    page = cmp_pages[0].shape[1]
    rpp = page // D_CMP
    n_ch = npg * rpp
    nc = n_ch - 1
    wb = win_ref.shape[1]
    ns = -(-(past + s_len) // L_SEL)
    nsp = -(-ns // 8) * 8
    n_top = min(N_TOP, ns)
    rows = HEADS * s_len

    w2 = (w2k_ref, w2v_ref)
    pe = (pek_ref, pev_ref)
    bd = (bdk_ref, bdv_ref)
    comp = []

    def stage1_scatter(lo, hi):
        for i in range(lo, hi):
            _scatter_chunk_rows(xs_ref, perm_ref[...], cmp_pages[i][...].astype(BF16), i * rpp)

    def stage1_compress(p):
        kv = p // 2
        comp.append(_compress_pair(_chunk_flat(xs_ref, p), w2[kv][...], pe[kv][...], bd[kv][...], n_ch))

    q = q_ref[...]
    lane8 = lax.broadcasted_iota(jnp.int32, (s_len, LANES), 1)
    zero8 = jnp.zeros((s_len, LANES), F32)
    blocks = []
    for g in range(KVH):
        for r in range(GRP):
            h = GRP * g + r
            src = q[:, (h // 2) * LANES:(h // 2 + 1) * LANES]
            if (r % 2) != (g % 2):
                src = pltpu.roll(src, HD, 1)
            keep = (lane8 < HD) if g % 2 == 0 else (lane8 >= HD)
            piece = jnp.where(keep, src, 0.0)
            blocks.append(jnp.concatenate([piece, zero8] if g // 2 == 0 else [zero8, piece], axis=1))
    qbd = (jnp.concatenate(blocks, axis=0) * (HD ** -0.5)).astype(BF16)

    def rowpos(shape):
        return past + lax.broadcasted_iota(jnp.int32, shape, 0) % s_len

    stage1_scatter(0, npg // 2)

    s = _nt(qbd, kc_all)
    cidx = lax.broadcasted_iota(jnp.int32, s.shape, 1)
    cvalid = (cidx * D_CMP + (L_CMP - 1) <= rowpos(s.shape)) & (cidx < nc)
    s = jnp.where(cvalid, s, NEG)
    m = jnp.max(s, axis=-1, keepdims=True)
    e = jnp.where(cvalid, jnp.exp(s - m), 0.0)
    p = e / jnp.maximum(jnp.sum(e, axis=-1, keepdims=True), 1.0)
    ocmp = _dot(p.astype(BF16), vc_all)

    stage1_scatter(npg // 2, npg)

    gs = GRP * s_len
    psum = []
    for g in range(KVH):
        acc = p[g * gs:g * gs + s_len]
        for r in range(1, GRP):
            acc = acc + p[g * gs + r * s_len:g * gs + (r + 1) * s_len]
        psum.append(acc)
    imp = _dot_exact_lhs(jnp.concatenate(psum, axis=0), ov_ref[...])
    qpos = rowpos(imp.shape)
    blk = lax.broadcasted_iota(jnp.int32, imp.shape, 1)
    visible = (blk * L_SEL <= qpos) & (blk < ns)
    cur = qpos // L_SEL
    forced = (blk == 0) | (blk == cur) | (blk == cur - 1)
    score = jnp.where(visible, imp + jnp.where(forced, BIG, 0.0), -BIG)
    stage1_compress(0)
    sel_t = _rank_select(_pad_rows(score, LANES).T, ns, nsp, n_top)
    sel = _pad_rows(sel_t, LANES).T[:KVH * s_len]
    sel = jnp.where(visible, sel, 0.0)
    sel_rows = jnp.concatenate([sel[g * s_len:(g + 1) * s_len] for g in range(KVH) for _ in range(GRP)], axis=0)

    zpad = jnp.zeros((page - s_len, KW), F32)
    knew = jnp.concatenate([rsel_ref[:, :KW], zpad], axis=0).astype(BF16)
    vnew = jnp.concatenate([rsel_ref[:, KW:], zpad], axis=0).astype(BF16)
    def page_pair(i, lo):
        return jnp.concatenate([sel_pages[2 * i][lo:lo + KW, :], sel_pages[2 * i + 1][lo:lo + KW, :]],
                               axis=1).astype(BF16)

    s = jnp.concatenate([_dot(qbd, page_pair(i, 0)) for i in range(npg // 2)] + [_nt(qbd, knew)], axis=1)
    mf = _dot(sel_rows.astype(BF16), e_ref[...])
    kpos = lax.broadcasted_iota(jnp.int32, s.shape, 1)
    mask = (mf > 0.5) & (kpos <= rowpos(s.shape))
    stage1_compress(1)
    s = jnp.where(mask, s, NEG)
    m = jnp.max(s, axis=-1, keepdims=True)
    e = jnp.where(mask, jnp.exp(s - m), 0.0)
    l = jnp.sum(e, axis=-1, keepdims=True)
    eb = e.astype(BF16)
    acc = _dot(eb[:, npg * page:], vnew)
    for i in range(npg // 2):
        acc = acc + _nt(eb[:, 2 * i * page:(2 * i + 2) * page], page_pair(i, KW))
    osel = acc / l

    kwn = jnp.concatenate([rwin_ref[:, :KW], zpad], axis=0).astype(BF16)
    vwn = jnp.concatenate([rwin_ref[:, KW:], zpad], axis=0).astype(BF16)
    s = jnp.concatenate([_dot(qbd, win_ref[:KW, :].astype(BF16)), _nt(qbd, kwn)], axis=1)
    kposw = (past - wb) + lax.broadcasted_iota(jnp.int32, s.shape, 1)
    qposw = rowpos(s.shape)
    wmask = (kposw <= qposw) & (kposw > qposw - WINDOW) & (kposw >= 0)
    s = jnp.where(wmask, s, NEG)
    m = jnp.max(s, axis=-1, keepdims=True)
    e = jnp.where(wmask, jnp.exp(s - m), 0.0)
    den = jnp.maximum(jnp.sum(e, axis=-1, keepdims=True), 1.0)
    eb = e.astype(BF16)
    owin = (_nt(eb[:, :wb], win_ref[KW:, :].astype(BF16)) + _dot(eb[:, wb:], vwn)) / den

    stage1_compress(2)

    gates = jax.nn.sigmoid(zbr_ref[...])
    for g in range(KVH):
        gl = slice((g // 2) * LANES, (g // 2 + 1) * LANES)
        for k2 in range(2):
            halves = []
            for r in (2 * k2, 2 * k2 + 1):
                h = GRP * g + r
                rs = slice(h * s_len, (h + 1) * s_len)
                val = (gates[:, h:h + 1] * ocmp[rs, gl]
                       + gates[:, HEADS + h:HEADS + h + 1] * osel[rs, gl]
                       + gates[:, 2 * HEADS + h:2 * HEADS + h + 1] * owin[rs, gl])
                if (r % 2) != (g % 2):
                    val = pltpu.roll(val, HD, 1)
                halves.append(val)
            c = 2 * g + k2
            oa_ref[:, c * LANES:(c + 1) * LANES] = jnp.where(lane8 < HD, halves[0], halves[1])

    new_t = [_pad_rows(rwin_ref[:, c * LANES:(c + 1) * LANES], LANES).T for c in range(KVW // LANES)]
    new_t = jnp.concatenate(new_t, axis=0)
    lane_w = lax.broadcasted_iota(jnp.int32, (KVW, LANES), 1)
    n_col = wb // LANES
    rolled = [pltpu.roll(win_ref[:, c * LANES:(c + 1) * LANES], LANES - s_len, 1) for c in range(n_col)]
    rolled.append(pltpu.roll(new_t, LANES - s_len, 1))
    for c in range(n_col):
        swin_ref[:, c * LANES:(c + 1) * LANES] = jnp.where(lane_w < LANES - s_len, rolled[c], rolled[c + 1])

    stage1_compress(3)
    kcv_ref[step % 2, 0] = jnp.concatenate(comp[0:2], axis=1).astype(BF16)
    kcv_ref[step % 2, 1] = jnp.concatenate(comp[2:4], axis=1).astype(BF16)


def _sample_nsa(page_table, cache_cmp, cache_sel, cache_win, q_tok, rows_sel, rows_win, z, cw, ov, e_mat, oa,
                n_prompt, dec_batch, s_len):
    npg = page_table.shape[1]
    page = cache_cmp.shape[2]
    past = npg * page
    wb = cache_win.shape[2]
    base = n_prompt // s_len
    last = dec_batch - 1
    steps = dec_batch + 1
    seq2 = lambda b: jnp.maximum(b - 1, 0)
    page_table = jnp.concatenate([page_table, page_table[last:], page_table[:1], page_table], axis=0)
    cmp_specs = [pl.BlockSpec((None, KVW, page), (lambda b, pt, i=i: (pt[b, i], 0, 0))) for i in range(npg)]
    sel_specs = [pl.BlockSpec((None, KVW, page), (lambda b, pt, i=i: (pt[steps + b, i], 0, 0))) for i in range(npg)]
    row = lambda b, pt: (base + seq2(b), 0)
    const = lambda b, pt: (0, 0)
    in_specs = (
        cmp_specs + sel_specs
        + [pl.BlockSpec((None, KVW, wb), lambda b, pt: (seq2(b), 0, 0)),
           pl.BlockSpec((s_len, NSA_W), row),
           pl.BlockSpec((s_len, KVW), row),
           pl.BlockSpec((s_len, KVW), row),
           pl.BlockSpec((s_len, LANES), lambda b, pt: (base + seq2(b), C1_BR // LANES))]
        + [pl.BlockSpec(w.shape, const) for w in cw]
        + [pl.BlockSpec(ov.shape, const), pl.BlockSpec(e_mat.shape, const), pl.BlockSpec(memory_space=pl.ANY)]
    )
    n_ch = past // D_CMP
    grid_spec = pltpu.PrefetchScalarGridSpec(
        num_scalar_prefetch=1,
        grid=(dec_batch + 1,),
        in_specs=in_specs,
        out_specs=(
            pl.BlockSpec((s_len, NSA_W), row),
            pl.BlockSpec((None, KVW, wb), lambda b, pt: (seq2(b), 0, 0)),
        ),
        scratch_shapes=[
            pltpu.VMEM((D_CMP, n_ch, KVW), F32),
            pltpu.VMEM((2, 2, n_ch, KW), BF16),
        ],
    )
    args = (page_table, *([cache_cmp] * npg), *([cache_sel] * npg), cache_win, q_tok, rows_sel, rows_win, z, *cw,
            ov, e_mat, oa)
    return pl.pallas_call(
        functools.partial(_sample_nsa_kernel, npg=npg, s_len=s_len, past=past),
        grid_spec=grid_spec,
        out_shape=(
            jax.ShapeDtypeStruct(oa.shape, F32),
            jax.ShapeDtypeStruct((dec_batch, KVW, wb), F32),
        ),
        input_output_aliases={len(args) - 1: 0},
        compiler_params=_params("arbitrary"),
        name="sample_nsa",
    )(*args)


def _mix_kernel(xp_ref, xs_ref, oa_ref, ob_ref, zhg_ref, ga_ref, gb_ref, gn_ref, wa_ref, wb_ref, wo_ref, o_ref, *, npt):
    ya = _dot(oa_ref[...].astype(BF16), wa_ref[...])
    gn = gn_ref[...]
    parts = []
    for h in range(HG_H):
        sl = slice(h * HG_DV, (h + 1) * HG_DV)
        ob = ob_ref[:, sl]
        ms = jnp.mean(ob * ob, axis=-1, keepdims=True)
        zg = zhg_ref[:, sl]
        y = ((ob * lax.rsqrt(ms + EPS)) * gn) * (zg * jax.nn.sigmoid(zg))
        parts.append(y.astype(BF16))
    yb = _dot(jnp.concatenate(parts, axis=1), wb_ref[...])
    mix = jax.nn.sigmoid(ga_ref[...]) * ya + jax.nn.sigmoid(gb_ref[...]) * yb
    y = _dot(mix.astype(BF16), wo_ref[...])

    @pl.when(pl.program_id(0) < npt)
    def _():
        o_ref[...] = xp_ref[...] + y

    @pl.when(pl.program_id(0) >= npt)
    def _():
        o_ref[...] = xs_ref[...] + y


def _mix(xp, xs, oa, ob, z, gn, wa, wb, wo):
    (n_p, d), n_s = xp.shape, xs.shape[0]
    n = n_p + n_s
    tm = 256
    npt, spec_p, spec_s = _two_group_specs(n_p, n_s, tm, d)
    row = lambda i: (i, 0)
    const = lambda i: (0, 0)
    resident = lambda w: pl.BlockSpec(w.shape, const, pipeline_mode=pl.Buffered(1))
    return pl.pallas_call(
        functools.partial(_mix_kernel, npt=npt),
        grid=(n // tm,),
        in_specs=[
            spec_p, spec_s,
            pl.BlockSpec((tm, NSA_W), row),
            pl.BlockSpec((tm, HG_W), row),
            pl.BlockSpec((tm, HG_W), lambda i: (i, C2_HG // HG_W)),
            pl.BlockSpec((tm, d), lambda i: (i, C2_GA // D_MODEL)),
            pl.BlockSpec((tm, d), lambda i: (i, C2_GB // D_MODEL)),
            pl.BlockSpec((1, HG_DV), const),
            resident(wa), resident(wb), resident(wo),
        ],
        out_specs=pl.BlockSpec((tm, d), row),
        out_shape=jax.ShapeDtypeStruct((n, d), F32),
        compiler_params=_params("parallel"),
        name="mix_out",
    )(xp, xs, oa, ob, z, z, z, gn, wa, wb, wo)


def _ffn_kernel(x_ref, g_ref, w1_ref, w2_ref, op_ref, os_ref, xn_ref, acc_ref, *, npt):
    i = pl.program_id(0)
    j = pl.program_id(1)
    last = pl.num_programs(1) - 1

    @pl.when(j == 0)
    def _():
        xn_ref[...] = _rmsnorm_bf16(x_ref[...], g_ref[...])
        acc_ref[...] = jnp.zeros(acc_ref.shape, F32)

    h = _dot(xn_ref[...], w1_ref[...])
    h = jnp.square(jnp.maximum(h, 0.0)).astype(BF16)
    acc_ref[...] += _dot(h, w2_ref[...])

    @pl.when((j == last) & (i < npt))
    def _():
        op_ref[...] = x_ref[...] + acc_ref[...]

    @pl.when((j == last) & (i >= npt))
    def _():
        os_ref[...] = x_ref[...] + acc_ref[...]


def _ffn(x, g, w1, w2, n_p):
    n, d = x.shape
    n_s = n - n_p
    dff = w1.shape[1]
    tm = _pick_tile(np.gcd(n_p, n_s), (512, 256))
    tf = 1024
    npt, spec_p, spec_s = _two_group_specs(n_p, n_s, tm, d)
    return pl.pallas_call(
        functools.partial(_ffn_kernel, npt=npt),
        grid=(n // tm, dff // tf),
        in_specs=[
            pl.BlockSpec((tm, d), lambda i, j: (i, 0)),
            pl.BlockSpec((1, d), lambda i, j: (0, 0)),
            pl.BlockSpec((d, tf), lambda i, j: (0, j)),
            pl.BlockSpec((tf, d), lambda i, j: (j, 0)),
        ],
        out_specs=(spec_p, spec_s),
        out_shape=(jax.ShapeDtypeStruct((n_p, d), F32), jax.ShapeDtypeStruct((n_s, d), F32)),
        scratch_shapes=[pltpu.VMEM((tm, d), BF16), pltpu.VMEM((tm, d), F32)],
        compiler_params=_params("arbitrary", "arbitrary"),
        name="ffn",
    )(x, g, w1, w2)


def _shift_cast_kernel(main_ref, halo_ref, o_ref, *, shift):
    n_col = main_ref.shape[1] // LANES
    lane = lax.broadcasted_iota(jnp.int32, (main_ref.shape[0], LANES), 1)
    cols = [main_ref[:, c * LANES:(c + 1) * LANES] for c in range(n_col)] + [halo_ref[...]]
    rolled = [pltpu.roll(x, LANES - shift, 1) for x in cols]
    for c in range(n_col):
        o_ref[:, c * LANES:(c + 1) * LANES] = jnp.where(lane < LANES - shift, rolled[c], rolled[c + 1]).astype(BF16)


def _cast_kernel(x_ref, o_ref):
    o_ref[...] = x_ref[...].astype(BF16)


def _head_w_in(w_in, ncols, width):
    d = w_in.shape[0]
    assert ncols % width == 0
    return pl.pallas_call(
        _cast_kernel,
        grid=(ncols // width,),
        in_specs=[pl.BlockSpec((d, width), lambda j: (0, j))],
        out_specs=pl.BlockSpec((d, width), lambda j: (0, j)),
        out_shape=jax.ShapeDtypeStruct((d, ncols), BF16),
        compiler_params=_params("parallel"),
        name="w_head",
    )(w_in)


def _tail_w_in(w_in):
    d = w_in.shape[0]
    start = sum(_SIZES[:3])
    width = 512
    base, shift = divmod(start, LANES)
    assert (base * LANES) % width == 0 and N_Z2 % width == 0 and 0 < shift
    first = base * LANES // width
    per = width // LANES
    return pl.pallas_call(
        functools.partial(_shift_cast_kernel, shift=shift),
        grid=(N_Z2 // width,),
        in_specs=[
            pl.BlockSpec((d, width), lambda j: (0, first + j)),
            pl.BlockSpec((d, LANES), lambda j: (0, (first + j + 1) * per)),
        ],
        out_specs=pl.BlockSpec((d, width), lambda j: (0, j)),
        out_shape=jax.ShapeDtypeStruct((d, N_Z2), BF16),
        compiler_params=_params("parallel"),
        name="w_tail",
    )(w_in, w_in)


def _compress_weights(cmp_pe, cmp_w1, cmp_w2):
    m = L_CMP // D_CMP
    eye2 = jnp.eye(2, dtype=F32)
    out = []
    w2s, pes, bds = [], [], []
    for kv in range(2):
        w1p = cmp_w1[kv].reshape(m, D_CMP, HD, HD).transpose(1, 2, 0, 3)
        w = w1p[:, None, :, :, None, :] * eye2[None, :, None, None, :, None]
        w2s.append(w.reshape(D_CMP * LANES, m * LANES).astype(BF16))
        pe = cmp_pe[kv].reshape(m, D_CMP, 1, HD)
        pe = jnp.broadcast_to(pe, (m, D_CMP, 2, HD)).reshape(m, D_CMP * LANES)
        pes.append(jnp.concatenate([pe, jnp.zeros((8 - m, D_CMP * LANES), F32)], axis=0))
        bds.append(jnp.kron(eye2, cmp_w2[kv]).astype(BF16))
    tile_rows = LANES
    rpp = tile_rows // D_CMP
    src = np.arange(tile_rows)
    perm = np.zeros((tile_rows, tile_rows), np.float32)
    perm[(src % D_CMP) * rpp + src // D_CMP, src] = 1.0
    return (w2s[0], w2s[1], pes[0], pes[1], bds[0], bds[1], jnp.asarray(perm, BF16))


def _overlap_matrix(nc_pad, ns, transposed=False):
    cs = np.arange(nc_pad) * D_CMP
    ss = np.arange(LANES) * L_SEL
    ov = np.clip(np.minimum(cs[:, None] + L_CMP, ss[None, :] + L_SEL) - np.maximum(cs[:, None], ss[None, :]), 0, None)
    ov = ((ov / D_CMP) * (np.arange(LANES)[None, :] < ns)).astype(np.float32)
    return jnp.asarray(np.ascontiguousarray(ov.T) if transposed else ov, BF16)


def _expand_matrix(n_keys):
    e = (np.arange(n_keys)[None, :] // L_SEL) == np.arange(LANES)[:, None]
    return jnp.asarray(e.astype(np.float32), BF16)


def _rope_tables(seq, past, s_len, tm):
    half = HD // 2
    inv = THETA ** (-jnp.arange(half, dtype=F32) * 2.0 / HD)
    pos = jnp.concatenate([jnp.arange(seq), past + jnp.arange(tm) % s_len]).astype(F32)
    ang = pos[:, None] * inv[None, :]
    cos = jnp.cos(ang)
    sin = jnp.sin(ang)
    cos_t = jnp.concatenate([cos, cos, cos, cos], axis=1)
    sin_t = jnp.concatenate([-sin, sin, -sin, sin], axis=1)
    return cos_t, sin_t


def kernel(x_prompt, x_sample, cache_cmp_kv, cache_sel_kv, cache_win_kv, state_hgrn, page_table, norm1_g, w_in,
           q_norm_g, k_norm_g, cmp_pe, cmp_w1, cmp_w2, w_proj_a, hgrn_lb_logits, hgrn_norm_g, w_proj_b, w_out,
           norm2_g, w_ff1, w_ff2):
    assert w_in.shape[0] == 1, "single-layer step"
    batch, seq, d = x_prompt.shape
    dec_batch, s_len, _ = x_sample.shape
    n_prompt = batch * seq
    n_sample = dec_batch * s_len
    npg = page_table.shape[1]
    page = cache_cmp_kv.shape[2]
    past = npg * page
    wb = cache_win_kv.shape[2]
    assert seq % Q_BLK == 0 and seq >= WINDOW + Q_BLK and n_prompt % 256 == 0 and n_sample % 256 == 0
    assert s_len % 8 == 0 and page % D_CMP == 0 and 2 * s_len <= HG_DK and dec_batch % 4 == 0 and npg % 2 == 0

    xp = x_prompt.reshape(n_prompt, d)
    xs = x_sample.reshape(n_sample, d)

    w_z2 = _tail_w_in(w_in[0])
    cw = _compress_weights(cmp_pe[0], cmp_w1[0], cmp_w2[0])
    cos_t, sin_t = _rope_tables(seq, past, s_len, 256)
    gq = jnp.tile(q_norm_g[0][None, :], (1, 2))
    gk = jnp.tile(k_norm_g[0], (1, 2))
    logits = hgrn_lb_logits.astype(F32)

    def token_minor(c):
        return jnp.transpose(c, (0, 2, 3, 4, 1)).reshape(c.shape[0], KVW, c.shape[1])

    def token_major(c_t):
        return jnp.transpose(c_t.reshape(c_t.shape[0], 2, KVH, HD, c_t.shape[2]), (0, 4, 1, 2, 3))

    z1 = _in_proj(xp, xs, norm1_g, _head_w_in(w_in[0], N_Z1, 896), N_Z1, 896)
    z2 = _in_proj(xp, xs, norm1_g, w_z2, N_Z2, 1024)
    (q_tok, q_t, rows_cmp, rows_sel, rows_win, rt_cmp, rt_sel, rt_win,
     ks, vs_t, kw, vw_t) = _qkv_post(z1, cos_t, sin_t, gq, gk, n_prompt, seq)

    bdv_t = lax.optimization_barrier(cw[5].T)
    kc, vc_t = _compress_prompt(rt_cmp, cw[:5] + (bdv_t,) + cw[6:], batch, seq)
    ov_p = _overlap_matrix(seq // D_CMP, seq // L_SEL, transposed=True)
    ocmp, sel_bias = _cmp_select(q_t, kc, vc_t, ov_p, batch, seq)
    oa = _sel_win(q_t, ks, vs_t, kw, vw_t, sel_bias, z1, ocmp, batch, seq)
    ob, p_state = _hgrn_prompt(z2, logits, batch, seq)

    ns_s = -(-(past + s_len) // L_SEL)
    ov_s = _overlap_matrix(past // D_CMP, ns_s)
    oa, s_win_t = _sample_nsa(
        page_table, token_minor(cache_cmp_kv[0]), token_minor(cache_sel_kv[0]), token_minor(cache_win_kv[0]),
        q_tok, rows_sel, rows_win, z1, cw, ov_s, _expand_matrix(past + page), oa, n_prompt, dec_batch, s_len)
    ob, s_state = _hgrn_sample(z2, logits, state_hgrn[0], ob, n_prompt, dec_batch, s_len)

    x2 = _mix(xp, xs, oa, ob, z2, hgrn_norm_g, w_proj_a[0].astype(BF16), w_proj_b[0].astype(BF16),
              w_out[0].astype(BF16))
    y_p, y_s = _ffn(x2, norm2_g, w_ff1[0].astype(BF16), w_ff2[0].astype(BF16), n_prompt)

    kv_shape = (2, KVH, HD)
    wp = min(WINDOW, seq)
    return (
        y_p.reshape(batch, seq, d),
        y_s.reshape(dec_batch, s_len, d),
        token_major(rt_cmp[:batch])[None],
        token_major(rt_sel[:batch])[None],
        token_major(rt_win[:batch, :, seq - wp:])[None],
        p_state[None],
        rows_cmp[n_prompt:].reshape((1, dec_batch, s_len) + kv_shape),
        rows_sel[n_prompt:].reshape((1, dec_batch, s_len) + kv_shape),
        token_major(s_win_t)[None],
        s_state[None],
    )
```

```python
import functools

import numpy as np
import jax
import jax.numpy as jnp
from jax import lax
from jax.experimental import pallas as pl
from jax.experimental.pallas import tpu as pltpu

F32 = jnp.float32
BF16 = jnp.bfloat16

D_MODEL = 2048
HEADS = 16
KVH = 4
GRP = HEADS // KVH
HD = 64
NSA_W = HEADS * HD
L_CMP = 32
D_CMP = 16
L_SEL = 64
N_TOP = 16
WINDOW = 512
Q_BLK = 128
THETA = 10000.0
HG_H = 8
HG_DK = 128
HG_DV = 128
HG_W = HG_H * HG_DK
HG_SUB = 16
SEL_KT = 512
EPS = 1e-6
NEG = -1e30
BIG = 1e9

LANES = 128
KVW = 2 * KVH * HD
KW = KVH * HD

_SIZES = (NSA_W, 6 * KVH * HD, 3 * HEADS, HG_W, HG_W, HG_W, HG_W, D_MODEL, D_MODEL)
C1_Q, C1_KV, C1_BR, N_Z1 = 0, 1024, 2560, 2688
C2_HQ, C2_HF, C2_HI, C2_HG, C2_GA, C2_GB, N_Z2 = 0, 1024, 2048, 3072, 4096, 6144, 8192

VMEM_LIMIT = 56 * 1024 * 1024


def _params(*sem):
    return pltpu.CompilerParams(dimension_semantics=sem, vmem_limit_bytes=VMEM_LIMIT)


def _nt(a, b):
    return lax.dot_general(a, b, (((1,), (1,)), ((), ())), preferred_element_type=F32)


def _dot(a, b):
    return jnp.dot(a, b, preferred_element_type=F32)


def _split3(x):
    hi = x.astype(BF16)
    r1 = x - hi.astype(F32)
    mid = r1.astype(BF16)
    lo = (r1 - mid.astype(F32)).astype(BF16)
    return hi, mid, lo


def _dot_exact_rhs(m, x):
    hi, mid, lo = _split3(x)
    return _dot(m, hi) + _dot(m, mid) + _dot(m, lo)


def _dot_exact_lhs(x, m):
    hi, mid, lo = _split3(x)
    return _dot(hi, m) + _dot(mid, m) + _dot(lo, m)


def _pick_tile(n, cands):
    for c in cands:
        if n % c == 0:
            return c
    raise ValueError(f"no tile in {cands} divides {n}")


def _rmsnorm_bf16(x, g):
    ms = jnp.mean(x * x, axis=-1, keepdims=True)
    return ((x * lax.rsqrt(ms + EPS)) * g).astype(BF16)


def _two_group_specs(n_p, n_s, tm, width, sample_kwargs=None):
    npt = n_p // tm
    assert n_p % tm == 0 and n_s % tm == 0
    spec_p = pl.BlockSpec((tm, width), lambda i, *_: (jnp.minimum(i, npt - 1), 0))
    spec_s = pl.BlockSpec((tm, width), lambda i, *_: (jnp.maximum(i - npt, 0), 0), **(sample_kwargs or {}))
    return npt, spec_p, spec_s


def _rms_matmul_kernel(xp_ref, xs_ref, g_ref, w_ref, o_ref, xn_ref, *, npt):
    i = pl.program_id(0)

    @pl.when((pl.program_id(1) == 0) & (i < npt))
    def _():
        xn_ref[...] = _rmsnorm_bf16(xp_ref[...], g_ref[...])

    @pl.when((pl.program_id(1) == 0) & (i >= npt))
    def _():
        xn_ref[...] = _rmsnorm_bf16(xs_ref[...], g_ref[...])

    o_ref[...] = _nt(xn_ref[...], w_ref[...].astype(BF16))


def _in_proj(xp, xs, g, w_t, row0, nout, tn):
    (n_p, d), n_s = xp.shape, xs.shape[0]
    n = n_p + n_s
    assert nout % tn == 0 and row0 % 8 == 0 and row0 + nout <= -(-w_t.shape[0] // 8) * 8 + tn
    tm = _pick_tile(np.gcd(n_p, n_s), (1024, 512, 256))
    npt, spec_p, spec_s = _two_group_specs(n_p, n_s, tm, d, dict(pipeline_mode=pl.Buffered(1)))
    return pl.pallas_call(
        functools.partial(_rms_matmul_kernel, npt=npt),
        grid=(n // tm, nout // tn),
        in_specs=[
            spec_p, spec_s,
            pl.BlockSpec((1, d), lambda i, j: (0, 0)),
            pl.BlockSpec((pl.Element(tn), pl.Element(d)), lambda i, j: ((row0 // 8 + j * (tn // 8)) * 8, 0)),
        ],
        out_specs=pl.BlockSpec((tm, tn), lambda i, j: (i, j)),
        out_shape=jax.ShapeDtypeStruct((n, nout), F32),
        scratch_shapes=[pltpu.VMEM((tm, d), BF16)],
        compiler_params=_params("parallel", "arbitrary"),
        name="in_proj",
    )(xp, xs, g, w_t)


def _qkv_post_kernel(zq_ref, zc_ref, zs_ref, zw_ref, cos_ref, sin_ref, gq_ref, gk_ref,
                     qtok_ref, qt_ref, rc_ref, rs_ref, rw_ref, tc_ref, ts_ref, tw_ref,
                     ks_ref, vs_ref, kw_ref, vw_ref):
    cos = cos_ref[...]
    sin = sin_ref[...]
    lane = lax.broadcasted_iota(jnp.int32, cos.shape, 1)
    first = (lane % HD) < (HD // 2)
    hi_ = lax.broadcasted_iota(jnp.int32, (LANES, LANES), 0) // HD
    hj_ = lax.broadcasted_iota(jnp.int32, (LANES, LANES), 1) // HD
    same_head = jnp.where(hi_ == hj_, 1.0, 0.0).astype(BF16)

    def norm_rope(xc, g_row):
        ms = _dot_exact_lhs(xc * xc, same_head) * (1.0 / HD)
        y = (xc * lax.rsqrt(ms + EPS)) * g_row
        partner = jnp.where(first, pltpu.roll(y, LANES - HD // 2, 1), pltpu.roll(y, HD // 2, 1))
        return y * cos + partner * sin

    gq = gq_ref[...]
    for c in range(NSA_W // LANES):
        sl = slice(c * LANES, (c + 1) * LANES)
        qc = norm_rope(zq_ref[:, sl], gq)
        qtok_ref[:, sl] = qc
        qtb = qc.T.astype(BF16)
        qt_ref[2 * c] = qtb[:HD]
        qt_ref[2 * c + 1] = qtb[HD:]

    tm = zq_ref.shape[0]
    zin = (zc_ref, zs_ref, zw_ref)
    rows = (rc_ref, rs_ref, rw_ref)
    rows_t = (tc_ref, ts_ref, tw_ref)
    kg = (None, ks_ref, kw_ref)
    vg = (None, vs_ref, vw_ref)
    for br in range(3):
        gk = gk_ref[br:br + 1, :]
        cols = []
        for c in range(KW // LANES):
            cols.append(norm_rope(zin[br][:, c * LANES:(c + 1) * LANES], gk))
        for c in range(KW // LANES):
            cols.append(zin[br][:, KW + c * LANES:KW + (c + 1) * LANES])
        cols_t = [col.T for col in cols]
        for c, col in enumerate(cols):
            rows[br][:, c * LANES:(c + 1) * LANES] = col
            rows_t[br][c * LANES:(c + 1) * LANES, :] = cols_t[c]
        if kg[br] is not None:
            wt = vg[br].shape[-1]
            for c in range(KW // LANES):
                kb = cols[c].astype(BF16)
                kg[br][2 * c] = kb[:, :HD]
                kg[br][2 * c + 1] = kb[:, HD:]
                vtb = cols_t[KW // LANES + c].astype(BF16)
                for t in range(tm // wt):
                    vg[br][2 * c, t] = vtb[:HD, t * wt:(t + 1) * wt]
                    vg[br][2 * c + 1, t] = vtb[HD:, t * wt:(t + 1) * wt]


def _qkv_post(z1, cos_tab, sin_tab, gq, gk, n_prompt, seq):
    n = z1.shape[0]
    tm = 256
    n_ptiles = n_prompt // tm
    tiles_per_seq = seq // tm
    n_slabs = -(-n // seq)

    def tab_map(i):
        return (jnp.where(i < n_ptiles, i % tiles_per_seq, tiles_per_seq), 0)

    row = lambda i: (i, 0)
    head = lambda i: (0, i, 0)
    slab = lambda i: (i // tiles_per_seq, 0, i % tiles_per_seq)
    rows_sd = jax.ShapeDtypeStruct((n, KVW), F32)
    rows_t_sd = jax.ShapeDtypeStruct((n_slabs, KVW, seq), F32)
    grp_sd = jax.ShapeDtypeStruct((KVH, n, HD), BF16)
    assert SEL_KT % tm == 0 and tm % Q_BLK == 0
    out_shape = (
        jax.ShapeDtypeStruct((n, NSA_W), F32),
        jax.ShapeDtypeStruct((HEADS, HD, n), BF16),
        rows_sd, rows_sd, rows_sd,
        rows_t_sd, rows_t_sd, rows_t_sd,
        grp_sd,
        jax.ShapeDtypeStruct((KVH, -(-n // SEL_KT), HD, SEL_KT), BF16),
        grp_sd,
        jax.ShapeDtypeStruct((KVH, n // Q_BLK, HD, Q_BLK), BF16),
    )
    rows_spec = pl.BlockSpec((tm, KVW), row)
    rows_t_spec = pl.BlockSpec((None, KVW, tm), slab)
    grp_spec = pl.BlockSpec((KVH, tm, HD), head)
    per = SEL_KT // tm
    out_specs = (
        pl.BlockSpec((tm, NSA_W), row),
        pl.BlockSpec((HEADS, HD, tm), lambda i: (0, 0, i)),
        rows_spec, rows_spec, rows_spec,
        rows_t_spec, rows_t_spec, rows_t_spec,
        grp_spec,
        pl.BlockSpec((KVH, 1, HD, tm), lambda i: (0, i // per, 0, i % per)),
        grp_spec,
        pl.BlockSpec((KVH, tm // Q_BLK, HD, Q_BLK), lambda i: (0, i, 0, 0)),
    )
    kv_in = lambda br: pl.BlockSpec((tm, KVW), lambda i: (i, C1_KV // KVW + br))
    return pl.pallas_call(
        _qkv_post_kernel,
        grid=(n // tm,),
        in_specs=[
            pl.BlockSpec((tm, NSA_W), lambda i: (i, C1_Q // NSA_W)),
            kv_in(0), kv_in(1), kv_in(2),
            pl.BlockSpec((tm, LANES), tab_map),
            pl.BlockSpec((tm, LANES), tab_map),
            pl.BlockSpec((1, LANES), lambda i: (0, 0)),
            pl.BlockSpec((3, LANES), lambda i: (0, 0)),
        ],
        out_specs=out_specs,
        out_shape=out_shape,
        compiler_params=_params("parallel"),
        name="qkv_post",
    )(z1, z1, z1, z1, cos_tab, sin_tab, gq, gk)


def _compress_pair(flat, w2, pe8, w2bd, n_ch, transposed=False):
    ab = _dot(flat, w2)
    hp = _dot(pe8.astype(BF16), w2)
    hpe = hp[0:1, :LANES] + hp[1:2, LANES:]
    nxt = pltpu.roll(ab[:, LANES:], n_ch - 1, 0)
    h = ab[:, :LANES] + nxt + hpe
    act = (h * jax.nn.sigmoid(h)).astype(BF16)
    if transposed:
        return _nt(w2bd, act)
    return _dot(act, w2bd)


def _scatter_chunk_rows(xs_ref, perm, tile_t, c0):
    rpp = tile_t.shape[1] // D_CMP
    xp = _nt(perm, tile_t)
    for t in range(D_CMP):
        xs_ref[t, c0:c0 + rpp, :] = xp[t * rpp:(t + 1) * rpp, :]


def _chunk_flat(xs_ref, p):
    return jnp.concatenate([xs_ref[t, :, p * LANES:(p + 1) * LANES] for t in range(D_CMP)], axis=1).astype(BF16)


def _compress_prompt_kernel(rows_ref, w2k_ref, w2v_ref, pek_ref, pev_ref, bdk_ref, bdv_ref, perm_ref, kc_ref, vc_ref,
                            xs_ref, *, n_ch):
    w2 = (w2k_ref, w2v_ref)
    pe = (pek_ref, pev_ref)
    bd = (bdk_ref, bdv_ref)
    tile_rows = perm_ref.shape[0]
    rpp = tile_rows // D_CMP
    for i in range(n_ch // rpp):
        _scatter_chunk_rows(xs_ref, perm_ref[...], rows_ref[:, i * tile_rows:(i + 1) * tile_rows].astype(BF16), i * rpp)
    for p in range(KVW // LANES):
        kv = p // 2
        flat = _chunk_flat(xs_ref, p)
        res = _compress_pair(flat, w2[kv][...], pe[kv][...], bd[kv][...], n_ch, transposed=(kv == 1)).astype(BF16)
        g0 = 2 * (p % 2)
        if kv == 0:
            kc_ref[0, g0] = res[:, :HD]
            kc_ref[0, g0 + 1] = res[:, HD:]
        else:
            vc_ref[0, g0] = res[:HD]
            vc_ref[0, g0 + 1] = res[HD:]


def _compress_prompt(rows_cmp, cw, batch, seq):
    n_ch = seq // D_CMP
    const = lambda b: (0, 0)
    return pl.pallas_call(
        functools.partial(_compress_prompt_kernel, n_ch=n_ch),
        grid=(batch,),
        in_specs=[pl.BlockSpec((None, KVW, seq), lambda b: (b, 0, 0))] + [pl.BlockSpec(w.shape, const) for w in cw],
        out_specs=(pl.BlockSpec((1, KVH, n_ch, HD), lambda b: (b, 0, 0, 0)),
                   pl.BlockSpec((1, KVH, HD, n_ch), lambda b: (b, 0, 0, 0))),
        out_shape=(jax.ShapeDtypeStruct((batch, KVH, n_ch, HD), BF16),
                   jax.ShapeDtypeStruct((batch, KVH, HD, n_ch), BF16)),
        scratch_shapes=[pltpu.VMEM((D_CMP, n_ch, KVW), F32)],
        compiler_params=_params("parallel"),
        name="compress_prompt",
    )(rows_cmp, *cw)


def _rank_select(score_t, n_blocks, n_rows, n_top):
    st = score_t[:n_rows]
    jio = lax.broadcasted_iota(jnp.int32, st.shape, 0)
    cnt = jnp.zeros(st.shape, F32)
    for i in range(n_blocks):
        cnt = cnt + _beats(score_t[i:i + 1, :], st, jio, i)
    return jnp.where(cnt < n_top, 1.0, 0.0)


def _beats(row, st, jio, i):
    return jnp.where(row > st, 1.0, jnp.where(row == st, jnp.where(jio > i, 1.0, 0.0), 0.0))


def _pad_rows(x, n):
    if x.shape[0] == n:
        return x
    return jnp.concatenate([x, jnp.zeros((n - x.shape[0], x.shape[1]), x.dtype)], axis=0)


def _cmp_select_kernel(qt_ref, kc_ref, vct_ref, ovt_ref, ocmp_ref, sel_ref, st_ref, *, nc, ns, n_top):
    s0 = pl.program_id(1) * Q_BLK
    ncp = kc_ref.shape[2]
    nsp = st_ref.shape[1]
    c_io = lax.broadcasted_iota(jnp.int32, (ncp, Q_BLK), 0)
    qpos_c = s0 + lax.broadcasted_iota(jnp.int32, (ncp, Q_BLK), 1)
    cbias = jnp.where((c_io * D_CMP + (L_CMP - 1) <= qpos_c) & (c_io < nc), 0.0, NEG)
    qpos_1 = s0 + lax.broadcasted_iota(jnp.int32, (1, GRP * Q_BLK), 1) % Q_BLK
    col_ok = jnp.where((qpos_1 >= L_CMP - 1) & (nc > 0), 1.0, 0.0)
    jio = lax.broadcasted_iota(jnp.int32, (nsp, LANES), 0)
    qpos_n = s0 + lax.broadcasted_iota(jnp.int32, (nsp, LANES), 1)
    vis_t = (jio * L_SEL <= qpos_n) & (jio < ns)
    cur = qpos_n // L_SEL
    forced = (jio == 0) | (jio == cur) | (jio == cur - 1)
    ovt = ovt_ref[...]
    grp = range(KVH)
    qt = [jnp.concatenate([qt_ref[GRP * g + r] for r in range(GRP)], axis=1) * (HD ** -0.5) for g in grp]
    s = [_dot(kc_ref[0, g], qt[g]) for g in grp]
    s = [jnp.concatenate([s[g][:, r * Q_BLK:(r + 1) * Q_BLK] + cbias for r in range(GRP)], axis=1) for g in grp]
    e = [jnp.exp(s[g] - jnp.max(s[g], axis=0, keepdims=True)) for g in grp]
    p = [e[g] * (col_ok / jnp.maximum(jnp.sum(e[g], axis=0, keepdims=True), 1.0)) for g in grp]
    o = [_dot(vct_ref[0, g], p[g].astype(BF16)) for g in grp]
    for g in grp:
        for r in range(GRP):
            ocmp_ref[GRP * g + r] = o[g][:, r * Q_BLK:(r + 1) * Q_BLK]
    psum = [p[g][:, 0:Q_BLK] + p[g][:, Q_BLK:2 * Q_BLK] + p[g][:, 2 * Q_BLK:3 * Q_BLK] + p[g][:, 3 * Q_BLK:]
            for g in grp]
    imp = [_dot_exact_rhs(ovt, psum[g]) for g in grp]
    for g in grp:
        st_ref[g] = jnp.where(vis_t, imp[g][:nsp] + jnp.where(forced, BIG, 0.0), -BIG)

    n_vis = jnp.minimum((s0 + Q_BLK - 1) // L_SEL + 1, ns)

    def count(i, cnts):
        return tuple(c + _beats(st_ref[g, pl.ds(i, 1), :], st_ref[g, 0:nsp, :], jio, i) for g, c in enumerate(cnts))

    cnts = lax.fori_loop(0, n_vis, count, tuple(jnp.zeros((nsp, LANES), F32) for _ in range(KVH)))
    for g in range(KVH):
        sel_ref[g] = jnp.where((cnts[g] < n_top) & vis_t, 0.0, NEG)


def _cmp_select(q_t, kc, vc_t, ov_t, batch, seq):
    nqt = seq // Q_BLK
    n_ch = kc.shape[2]
    ns = seq // L_SEL
    nsp = -(-ns // 8) * 8
    kern = functools.partial(_cmp_select_kernel, nc=n_ch - 1, ns=ns, n_top=min(N_TOP, ns))
    q_spec = pl.BlockSpec((HEADS, HD, Q_BLK), lambda b, t: (0, 0, b * nqt + t))
    return pl.pallas_call(
        kern,
        grid=(batch, nqt),
        in_specs=[
            q_spec,
            pl.BlockSpec((1, KVH, n_ch, HD), lambda b, t: (b, 0, 0, 0)),
            pl.BlockSpec((1, KVH, HD, n_ch), lambda b, t: (b, 0, 0, 0)),
            pl.BlockSpec(ov_t.shape, lambda b, t: (0, 0)),
        ],
        out_specs=(
            q_spec,
            pl.BlockSpec((None, KVH, nsp, LANES), lambda b, t: (b * nqt + t, 0, 0, 0)),
        ),
        out_shape=(
            jax.ShapeDtypeStruct((HEADS, HD, batch * seq), F32),
            jax.ShapeDtypeStruct((batch * nqt, KVH, nsp, LANES), F32),
        ),
        scratch_shapes=[pltpu.VMEM((KVH, nsp, LANES), F32)],
        compiler_params=_params("parallel", "parallel"),
        name="cmp_select",
    )(q_t, kc, vc_t, ov_t)


def _sel_win_kernel(qt_ref, ks_ref, vst_ref, kw_ref, vwt_ref, bias_ref, zbr_ref, ocmp_ref, oa_ref, q_s):
    kt = vst_ref.shape[-1]
    s0 = pl.program_id(1) * Q_BLK
    cols = GRP * Q_BLK
    n_kt = (s0 + Q_BLK + kt - 1) // kt
    bpt = kt // L_SEL
    key_io = lax.broadcasted_iota(jnp.int32, (kt, Q_BLK), 0)
    qpos = s0 + lax.broadcasted_iota(jnp.int32, (kt, Q_BLK), 1)
    ones_row = 2 * 8
    va_rows = HD + ones_row

    for g in range(KVH):
        q_s[g] = jnp.concatenate([qt_ref[GRP * g + r] for r in range(GRP)], axis=1) * (HD ** -0.5)

    def with_bias(st, bias):
        return jnp.concatenate([st[:, r * Q_BLK:(r + 1) * Q_BLK] + bias for r in range(GRP)], axis=1)

    def body(j, carry):
        k0 = pl.multiple_of(j * kt, kt)
        grp = range(KVH)
        causal = jnp.where(k0 + key_io <= qpos, 0.0, NEG)
        st = [_dot(ks_ref[g, pl.ds(k0, kt), :], q_s[g]) for g in grp]
        bias = [jnp.concatenate([jnp.broadcast_to(bias_ref[g, pl.ds(j * bpt + b, 1), :], (L_SEL, Q_BLK))
                                 for b in range(bpt)], axis=0) + causal for g in grp]
        st = [with_bias(st[g], bias[g]) for g in grp]
        m_new = [jnp.maximum(carry[g][0], jnp.max(st[g], axis=0, keepdims=True)) for g in grp]
        corr = [jnp.exp(carry[g][0] - m_new[g]) for g in grp]
        p = [jnp.exp(st[g] - m_new[g]).astype(BF16) for g in grp]
        ones = jnp.ones((ones_row, kt), BF16)
        pv = [_dot(jnp.concatenate([vst_ref[g, j], ones], axis=0), p[g]) for g in grp]
        return tuple((m_new[g], carry[g][1] * corr[g] + pv[g]) for g in grp)

    init = tuple((jnp.full((1, cols), NEG, F32), jnp.zeros((va_rows, cols), F32)) for _ in range(KVH))
    res = lax.fori_loop(0, n_kt, body, init)

    wl = WINDOW + Q_BLK
    wstart = pl.multiple_of(jnp.maximum(s0 - WINDOW, 0), Q_BLK)
    wt0 = wstart // Q_BLK
    kposw = wstart + lax.broadcasted_iota(jnp.int32, (wl, Q_BLK), 0)
    qposw = s0 + lax.broadcasted_iota(jnp.int32, (wl, Q_BLK), 1)
    wbias = jnp.where((kposw <= qposw) & (kposw > qposw - WINDOW), 0.0, NEG)
    ones_w = jnp.ones((ones_row, wl), BF16)
    gates_t = jax.nn.sigmoid(zbr_ref[...]).T
    o_t = []
    for g in range(KVH):
        acc = res[g][1]
        o_sel_t = acc[:HD] / acc[HD:HD + 1]
        st = with_bias(_dot(kw_ref[g, pl.ds(wstart, wl), :], q_s[g]), wbias)
        m = jnp.max(st, axis=0, keepdims=True)
        e = jnp.exp(st - m).astype(BF16)
        vwt = jnp.concatenate([vwt_ref[g, wt0 + i] for i in range(wl // Q_BLK)], axis=1)
        pv = _dot(jnp.concatenate([vwt, ones_w], axis=0), e)
        o_win_t = pv[:HD] / jnp.maximum(pv[HD:HD + 1], 1.0)
        for r in range(GRP):
            h = GRP * g + r
            cs = slice(r * Q_BLK, (r + 1) * Q_BLK)
            o_t.append(gates_t[h:h + 1] * ocmp_ref[h]
                       + gates_t[HEADS + h:HEADS + h + 1] * o_sel_t[:, cs]
                       + gates_t[2 * HEADS + h:2 * HEADS + h + 1] * o_win_t[:, cs])
    for c in range(HEADS // 2):
        oa_ref[:, c * LANES:(c + 1) * LANES] = jnp.concatenate([o_t[2 * c], o_t[2 * c + 1]], axis=0).T


def _sel_win(q_t, ks, vs_t, kw, vw_t, sel_bias, z, ocmp, batch, seq):
    nqt = seq // Q_BLK
    assert seq % SEL_KT == 0
    once = dict(pipeline_mode=pl.Buffered(1))
    k_spec = pl.BlockSpec((KVH, seq, HD), lambda b, t: (0, b, 0), **once)
    tile = lambda b, t: (b * nqt + t, 0, 0, 0)
    return pl.pallas_call(
        _sel_win_kernel,
        grid=(batch, nqt),
        in_specs=[
            pl.BlockSpec((HEADS, HD, Q_BLK), lambda b, t: (0, 0, b * nqt + t)),
            k_spec,
            pl.BlockSpec((KVH, seq // SEL_KT, HD, SEL_KT), lambda b, t: (0, b, 0, 0), **once),
            k_spec,
            pl.BlockSpec((KVH, seq // Q_BLK, HD, Q_BLK), lambda b, t: (0, b, 0, 0), **once),
            pl.BlockSpec((None,) + sel_bias.shape[1:], tile),
            pl.BlockSpec((Q_BLK, LANES), lambda b, t: (b * nqt + t, C1_BR // LANES)),
            pl.BlockSpec((HEADS, HD, Q_BLK), lambda b, t: (0, 0, b * nqt + t)),
        ],
        out_specs=pl.BlockSpec((Q_BLK, NSA_W), lambda b, t: (b * nqt + t, 0)),
        out_shape=jax.ShapeDtypeStruct((z.shape[0], NSA_W), F32),
        scratch_shapes=[pltpu.VMEM((KVH, HD, GRP * Q_BLK), BF16)],
        compiler_params=_params("parallel", "arbitrary"),
        name="sel_win",
    )(q_t, ks, vs_t, kw, vw_t, sel_bias, z, ocmp)


def _hgrn_gates(zq, zf, logits):
    mx = jnp.max(logits, axis=0, keepdims=True)
    ex = jnp.exp(logits - mx)
    lb = ex[0:1] / jnp.sum(ex, axis=0, keepdims=True)
    hq = zq * jax.nn.sigmoid(zq)
    log_sig = -(jnp.maximum(-zf, 0.0) + jnp.log1p(jnp.exp(-jnp.abs(zf))))
    a = jnp.log(lb)
    b = jnp.log1p(-lb) + log_sig
    log_f = jnp.maximum(a, b) + jnp.log1p(jnp.exp(-jnp.abs(a - b)))
    hk = (1.0 - lb) * jax.nn.sigmoid(-zf)
    return hq, hk, log_f


def _intra_chunk(ac, qc, kc, vc):
    n = ac.shape[0]
    sub = 8
    tio = lax.broadcasted_iota(jnp.int32, (sub, ac.shape[1]), 0)
    accs = [jnp.zeros((sub, vc.shape[1]), F32) for _ in range(n // sub)]
    for s in range(n):
        a_s, k_s, v_s = ac[s:s + 1, :], kc[s:s + 1, :], vc[s:s + 1, :]
        for i in range(s // sub, n // sub):
            rs = slice(i * sub, (i + 1) * sub)
            w = qc[rs] * k_s * jnp.exp(ac[rs] - a_s)
            if i == s // sub:
                w = jnp.where(tio >= s - i * sub, w, 0.0)
            accs[i] = accs[i] + jnp.sum(w, axis=-1, keepdims=True) * v_s
    return accs[0] if len(accs) == 1 else jnp.concatenate(accs, axis=0)


def _hgrn_prompt_kernel(zq_ref, zf_ref, zi_ref, lg_ref, tri_ref, full_ref, o_ref, sout_ref,
                        st_ref, a_s, q_s, k_s, qt_s, ktb_s, eal_s, vt_s):
    t = pl.program_id(1)
    tt = zq_ref.shape[0]

    @pl.when(t == 0)
    def _():
        st_ref[...] = jnp.zeros(st_ref.shape, F32)

    hq, hk, log_f = _hgrn_gates(zq_ref[...], zf_ref[...], lg_ref[...])
    a = _dot_exact_rhs(tri_ref[...], log_f)
    al = _dot_exact_rhs(full_ref[...], log_f)
    a_s[...] = a
    q_s[...] = hq
    k_s[...] = hk
    qt_s[...] = hq * jnp.exp(a)
    ktb_s[...] = (hk * jnp.exp(al - a)).astype(BF16)
    eal_s[...] = jnp.exp(al)
    for h in range(HG_H):
        vt_s[h] = zi_ref[:, h * HG_DV:(h + 1) * HG_DV].T.astype(BF16)
    lane_tok = lax.broadcasted_iota(jnp.int32, (HG_DV, tt), 1)

    def chunk(c, carry):
        r0 = pl.multiple_of(c * HG_SUB, HG_SUB)
        in_chunk = (lane_tok >= r0) & (lane_tok < r0 + HG_SUB)
        for h in range(HG_H):
            sl = slice(h * HG_DK, (h + 1) * HG_DK)
            st = st_ref[h]
            o = _nt(qt_s[pl.ds(r0, HG_SUB), sl].astype(BF16), st.astype(BF16))
            o = o + _intra_chunk(a_s[pl.ds(r0, HG_SUB), sl], q_s[pl.ds(r0, HG_SUB), sl],
                                 k_s[pl.ds(r0, HG_SUB), sl], zi_ref[pl.ds(r0, HG_SUB), sl])
            o_ref[pl.ds(r0, HG_SUB), sl] = o
            vtm = jnp.where(in_chunk, vt_s[h], jnp.zeros_like(vt_s[h]))
            st_ref[h] = st * eal_s[pl.ds(r0, 1), sl] + _dot(vtm, ktb_s[:, sl])
        return carry

    lax.fori_loop(0, tt // HG_SUB, chunk, 0)

    @pl.when(t == pl.num_programs(1) - 1)
    def _():
        for h in range(HG_H):
            sout_ref[0, h] = st_ref[h].T


def _hgrn_prompt(z, logits, batch, seq):
    tt = 128
    nt = seq // tt
    sub = np.arange(tt) // HG_SUB
    same = sub[:, None] == sub[None, :]
    tri = jnp.asarray((same & (np.arange(tt)[None, :] <= np.arange(tt)[:, None])).astype(np.float32), BF16)
    full = jnp.asarray(same.astype(np.float32), BF16)
    col = lambda c: (lambda b, t: (b * nt + t, c // HG_W))
    const = lambda b, t: (0, 0)
    return pl.pallas_call(
        _hgrn_prompt_kernel,
        grid=(batch, nt),
        in_specs=[
            pl.BlockSpec((tt, HG_W), col(C2_HQ)),
            pl.BlockSpec((tt, HG_W), col(C2_HF)),
            pl.BlockSpec((tt, HG_W), col(C2_HI)),
            pl.BlockSpec((2, HG_W), const),
            pl.BlockSpec((tt, tt), const),
            pl.BlockSpec((tt, tt), const),
        ],
        out_specs=(
            pl.BlockSpec((tt, HG_W), lambda b, t: (b * nt + t, 0)),
            pl.BlockSpec((1, HG_H, HG_DK, HG_DV), lambda b, t: (b, 0, 0, 0)),
        ),
        out_shape=(
            jax.ShapeDtypeStruct((z.shape[0], HG_W), F32),
            jax.ShapeDtypeStruct((batch, HG_H, HG_DK, HG_DV), F32),
        ),
        scratch_shapes=[
            pltpu.VMEM((HG_H, HG_DV, HG_DK), F32),
            pltpu.VMEM((tt, HG_W), F32),
            pltpu.VMEM((tt, HG_W), F32),
            pltpu.VMEM((tt, HG_W), F32),
            pltpu.VMEM((tt, HG_W), F32),
            pltpu.VMEM((tt, HG_W), BF16),
            pltpu.VMEM((tt, HG_W), F32),
            pltpu.VMEM((HG_H, HG_DV, tt), BF16),
        ],
        compiler_params=_params("parallel", "arbitrary"),
        name="hgrn_prompt",
    )(z, z, z, logits, tri, full)


def _hgrn_sample_kernel(zq_ref, zf_ref, zi_ref, lg_ref, s0_ref, ob_hbm_ref, o_ref, sout_ref, *, s_len):
    del ob_hbm_ref
    n_seq = s0_ref.shape[0]
    hq_all, hk_all, logf_all = _hgrn_gates(zq_ref[...], zf_ref[...], lg_ref[...])
    rio = lax.broadcasted_iota(jnp.int32, (s_len, HG_W), 0)
    row0 = lax.broadcasted_iota(jnp.int32, (s_len, HG_DK), 0) == 0
    pad = HG_DK - 2 * s_len
    for q in range(n_seq):
        rs = slice(q * s_len, (q + 1) * s_len)
        hq, hk, log_f = hq_all[rs], hk_all[rs], logf_all[rs]
        a = log_f
        d = 1
        while d < s_len:
            a = a + jnp.where(rio >= d, pltpu.roll(a, d, 0), 0.0)
            d *= 2
        a_last = jnp.sum(log_f, axis=0, keepdims=True)
        qt = hq * jnp.exp(a)
        kt = hk * jnp.exp(a_last - a)
        ea = jnp.exp(a_last)
        hv = zi_ref[rs, :]
        for h in range(HG_H):
            sl = slice(h * HG_DK, (h + 1) * HG_DK)
            s_old = s0_ref[q, h]
            qt16 = _pad_rows(qt[:, sl], 16).astype(BF16)
            o = _dot(qt16, s_old.astype(BF16))[:s_len]
            o = o + _intra_chunk(a[:, sl], hq[:, sl], hk[:, sl], hv[:, sl])
            o_ref[rs, sl] = o
            m = jnp.concatenate([kt[:, sl], jnp.where(row0, jnp.broadcast_to(ea[:, sl], (s_len, HG_DK)), 0.0),
                                 jnp.zeros((pad, HG_DK), F32)], axis=0)
            mt = m.T
            e_col = mt[:, s_len:s_len + 1]
            vpad = _pad_rows(hv[:, sl], HG_DK).astype(BF16)
            sout_ref[q, h] = e_col * s_old + _dot(mt.astype(BF16), vpad)


def _hgrn_sample(z, logits, state, ob, n_prompt, dec_batch, s_len):
    n_seq = 4
    rows = n_seq * s_len
    base = n_prompt // rows
    col = lambda c: (lambda b: (base + b, c // HG_W))
    st_spec = pl.BlockSpec((n_seq, HG_H, HG_DK, HG_DV), lambda b: (b, 0, 0, 0))
    return pl.pallas_call(
        functools.partial(_hgrn_sample_kernel, s_len=s_len),
        grid=(dec_batch // n_seq,),
        in_specs=[
            pl.BlockSpec((rows, HG_W), col(C2_HQ)),
            pl.BlockSpec((rows, HG_W), col(C2_HF)),
            pl.BlockSpec((rows, HG_W), col(C2_HI)),
            pl.BlockSpec((2, HG_W), lambda b: (0, 0)),
            st_spec,
            pl.BlockSpec(memory_space=pl.ANY),
        ],
        out_specs=(pl.BlockSpec((rows, HG_W), lambda b: (base + b, 0)), st_spec),
        out_shape=(
            jax.ShapeDtypeStruct(ob.shape, F32),
            jax.ShapeDtypeStruct(state.shape, F32),
        ),
        input_output_aliases={5: 0},
        compiler_params=_params("parallel"),
        name="hgrn_sample",
    )(z, z, z, logits, state, ob)


def _sample_nsa_kernel(pt_ref, *refs, npg, s_len, past):
    del pt_ref
    cmp_pages = refs[:npg]
    sel_pages = refs[npg:2 * npg]
    (win_ref, q_ref, rsel_ref, rwin_ref, zbr_ref, w2k_ref, w2v_ref, pek_ref, pev_ref, bdk_ref, bdv_ref, perm_ref,
     ov_ref, e_ref, oa_hbm_ref, oa_ref, swin_ref, xs_ref, kcv_ref) = refs[2 * npg:]
    del oa_hbm_ref
    step = pl.program_id(0)

    @pl.when(step == 0)
    def _():
        kcv_ref[...] = jnp.zeros(kcv_ref.shape, BF16)

    kc_all = kcv_ref[(step + 1) % 2, 0]
    vc_all = kcv_ref[(step + 1) % 2, 1]
    page = cmp_pages[0].shape[1]
    rpp = page // D_CMP
    n_ch = npg * rpp
    nc = n_ch - 1
    wb = win_ref.shape[1]
    ns = -(-(past + s_len) // L_SEL)
    nsp = -(-ns // 8) * 8
    n_top = min(N_TOP, ns)
    rows = HEADS * s_len

    w2 = (w2k_ref, w2v_ref)
    pe = (pek_ref, pev_ref)
    bd = (bdk_ref, bdv_ref)
    comp = []

    def stage1_scatter(lo, hi):
        for i in range(lo, hi):
            _scatter_chunk_rows(xs_ref, perm_ref[...], cmp_pages[i][...].astype(BF16), i * rpp)

    def stage1_compress(p):
        kv = p // 2
        comp.append(_compress_pair(_chunk_flat(xs_ref, p), w2[kv][...], pe[kv][...], bd[kv][...], n_ch))

    q = q_ref[...]
    lane8 = lax.broadcasted_iota(jnp.int32, (s_len, LANES), 1)
    zero8 = jnp.zeros((s_len, LANES), F32)
    blocks = []
    for g in range(KVH):
        for r in range(GRP):
            h = GRP * g + r
            src = q[:, (h // 2) * LANES:(h // 2 + 1) * LANES]
            if (r % 2) != (g % 2):
                src = pltpu.roll(src, HD, 1)
            keep = (lane8 < HD) if g % 2 == 0 else (lane8 >= HD)
            piece = jnp.where(keep, src, 0.0)
            blocks.append(jnp.concatenate([piece, zero8] if g // 2 == 0 else [zero8, piece], axis=1))
    qbd = (jnp.concatenate(blocks, axis=0) * (HD ** -0.5)).astype(BF16)

    def rowpos(shape):
        return past + lax.broadcasted_iota(jnp.int32, shape, 0) % s_len

    stage1_scatter(0, npg // 2)

    s = _nt(qbd, kc_all)
    cidx = lax.broadcasted_iota(jnp.int32, s.shape, 1)
    cvalid = (cidx * D_CMP + (L_CMP - 1) <= rowpos(s.shape)) & (cidx < nc)
    s = jnp.where(cvalid, s, NEG)
    m = jnp.max(s, axis=-1, keepdims=True)
    e = jnp.where(cvalid, jnp.exp(s - m), 0.0)
    p = e / jnp.maximum(jnp.sum(e, axis=-1, keepdims=True), 1.0)
    ocmp = _dot(p.astype(BF16), vc_all)

    stage1_scatter(npg // 2, npg)

    gs = GRP * s_len
    psum = []
    for g in range(KVH):
        acc = p[g * gs:g * gs + s_len]
        for r in range(1, GRP):
            acc = acc + p[g * gs + r * s_len:g * gs + (r + 1) * s_len]
        psum.append(acc)
    imp = _dot_exact_lhs(jnp.concatenate(psum, axis=0), ov_ref[...])
    qpos = rowpos(imp.shape)
    blk = lax.broadcasted_iota(jnp.int32, imp.shape, 1)
    visible = (blk * L_SEL <= qpos) & (blk < ns)
    cur = qpos // L_SEL
    forced = (blk == 0) | (blk == cur) | (blk == cur - 1)
    score = jnp.where(visible, imp + jnp.where(forced, BIG, 0.0), -BIG)
    stage1_compress(0)
    sel_t = _rank_select(_pad_rows(score, LANES).T, ns, nsp, n_top)
    sel = _pad_rows(sel_t, LANES).T[:KVH * s_len]
    sel = jnp.where(visible, sel, 0.0)
    sel_rows = jnp.concatenate([sel[g * s_len:(g + 1) * s_len] for g in range(KVH) for _ in range(GRP)], axis=0)

    zpad = jnp.zeros((page - s_len, KW), F32)
    knew = jnp.concatenate([rsel_ref[:, :KW], zpad], axis=0).astype(BF16)
    vnew = jnp.concatenate([rsel_ref[:, KW:], zpad], axis=0).astype(BF16)
    def page_pair(i, lo):
        return jnp.concatenate([sel_pages[2 * i][lo:lo + KW, :], sel_pages[2 * i + 1][lo:lo + KW, :]],
                               axis=1).astype(BF16)

    s = jnp.concatenate([_dot(qbd, page_pair(i, 0)) for i in range(npg // 2)] + [_nt(qbd, knew)], axis=1)
    mf = _dot(sel_rows.astype(BF16), e_ref[...])
    kpos = lax.broadcasted_iota(jnp.int32, s.shape, 1)
    mask = (mf > 0.5) & (kpos <= rowpos(s.shape))
    stage1_compress(1)
    s = jnp.where(mask, s, NEG)
    m = jnp.max(s, axis=-1, keepdims=True)
    e = jnp.where(mask, jnp.exp(s - m), 0.0)
    l = jnp.sum(e, axis=-1, keepdims=True)
    eb = e.astype(BF16)
    acc = _dot(eb[:, npg * page:], vnew)
    for i in range(npg // 2):
        acc = acc + _nt(eb[:, 2 * i * page:(2 * i + 2) * page], page_pair(i, KW))
    osel = acc / l

    kwn = jnp.concatenate([rwin_ref[:, :KW], zpad], axis=0).astype(BF16)
    vwn = jnp.concatenate([rwin_ref[:, KW:], zpad], axis=0).astype(BF16)
    s = jnp.concatenate([_dot(qbd, win_ref[:KW, :].astype(BF16)), _nt(qbd, kwn)], axis=1)
    kposw = (past - wb) + lax.broadcasted_iota(jnp.int32, s.shape, 1)
    qposw = rowpos(s.shape)
    wmask = (kposw <= qposw) & (kposw > qposw - WINDOW) & (kposw >= 0)
    s = jnp.where(wmask, s, NEG)
    m = jnp.max(s, axis=-1, keepdims=True)
    e = jnp.where(wmask, jnp.exp(s - m), 0.0)
    den = jnp.maximum(jnp.sum(e, axis=-1, keepdims=True), 1.0)
    eb = e.astype(BF16)
    owin = (_nt(eb[:, :wb], win_ref[KW:, :].astype(BF16)) + _dot(eb[:, wb:], vwn)) / den

    stage1_compress(2)

    gates = jax.nn.sigmoid(zbr_ref[...])
    for g in range(KVH):
        gl = slice((g // 2) * LANES, (g // 2 + 1) * LANES)
        for k2 in range(2):
            halves = []
            for r in (2 * k2, 2 * k2 + 1):
                h = GRP * g + r
                rs = slice(h * s_len, (h + 1) * s_len)
                val = (gates[:, h:h + 1] * ocmp[rs, gl]
                       + gates[:, HEADS + h:HEADS + h + 1] * osel[rs, gl]
                       + gates[:, 2 * HEADS + h:2 * HEADS + h + 1] * owin[rs, gl])
                if (r % 2) != (g % 2):
                    val = pltpu.roll(val, HD, 1)
                halves.append(val)
            c = 2 * g + k2
            oa_ref[:, c * LANES:(c + 1) * LANES] = jnp.where(lane8 < HD, halves[0], halves[1])

    new_t = [_pad_rows(rwin_ref[:, c * LANES:(c + 1) * LANES], LANES).T for c in range(KVW // LANES)]
    new_t = jnp.concatenate(new_t, axis=0)
    lane_w = lax.broadcasted_iota(jnp.int32, (KVW, LANES), 1)
    n_col = wb // LANES
    rolled = [pltpu.roll(win_ref[:, c * LANES:(c + 1) * LANES], LANES - s_len, 1) for c in range(n_col)]
    rolled.append(pltpu.roll(new_t, LANES - s_len, 1))
    for c in range(n_col):
        swin_ref[:, c * LANES:(c + 1) * LANES] = jnp.where(lane_w < LANES - s_len, rolled[c], rolled[c + 1])

    stage1_compress(3)
    kcv_ref[step % 2, 0] = jnp.concatenate(comp[0:2], axis=1).astype(BF16)
    kcv_ref[step % 2, 1] = jnp.concatenate(comp[2:4], axis=1).astype(BF16)


def _sample_nsa(page_table, cache_cmp, cache_sel, cache_win, q_tok, rows_sel, rows_win, z, cw, ov, e_mat, oa,
                n_prompt, dec_batch, s_len):
    npg = page_table.shape[1]
    page = cache_cmp.shape[2]
    past = npg * page
    wb = cache_win.shape[2]
    base = n_prompt // s_len
    last = dec_batch - 1
    steps = dec_batch + 1
    seq2 = lambda b: jnp.maximum(b - 1, 0)
    page_table = jnp.concatenate([page_table, page_table[last:], page_table[:1], page_table], axis=0)
    cmp_specs = [pl.BlockSpec((None, KVW, page), (lambda b, pt, i=i: (pt[b, i], 0, 0))) for i in range(npg)]
    sel_specs = [pl.BlockSpec((None, KVW, page), (lambda b, pt, i=i: (pt[steps + b, i], 0, 0))) for i in range(npg)]
    row = lambda b, pt: (base + seq2(b), 0)
    const = lambda b, pt: (0, 0)
    in_specs = (
        cmp_specs + sel_specs
        + [pl.BlockSpec((None, KVW, wb), lambda b, pt: (seq2(b), 0, 0)),
           pl.BlockSpec((s_len, NSA_W), row),
           pl.BlockSpec((s_len, KVW), row),
           pl.BlockSpec((s_len, KVW), row),
           pl.BlockSpec((s_len, LANES), lambda b, pt: (base + seq2(b), C1_BR // LANES))]
        + [pl.BlockSpec(w.shape, const) for w in cw]
        + [pl.BlockSpec(ov.shape, const), pl.BlockSpec(e_mat.shape, const), pl.BlockSpec(memory_space=pl.ANY)]
    )
    n_ch = past // D_CMP
    grid_spec = pltpu.PrefetchScalarGridSpec(
        num_scalar_prefetch=1,
        grid=(dec_batch + 1,),
        in_specs=in_specs,
        out_specs=(
            pl.BlockSpec((s_len, NSA_W), row),
            pl.BlockSpec((None, KVW, wb), lambda b, pt: (seq2(b), 0, 0)),
        ),
        scratch_shapes=[
            pltpu.VMEM((D_CMP, n_ch, KVW), F32),
            pltpu.VMEM((2, 2, n_ch, KW), BF16),
        ],
    )
    args = (page_table, *([cache_cmp] * npg), *([cache_sel] * npg), cache_win, q_tok, rows_sel, rows_win, z, *cw,
            ov, e_mat, oa)
    return pl.pallas_call(
        functools.partial(_sample_nsa_kernel, npg=npg, s_len=s_len, past=past),
        grid_spec=grid_spec,
        out_shape=(
            jax.ShapeDtypeStruct(oa.shape, F32),
            jax.ShapeDtypeStruct((dec_batch, KVW, wb), F32),
        ),
        input_output_aliases={len(args) - 1: 0},
        compiler_params=_params("arbitrary"),
        name="sample_nsa",
    )(*args)


def _mix_kernel(xp_ref, xs_ref, oa_ref, ob_ref, zhg_ref, ga_ref, gb_ref, gn_ref, wa_ref, wb_ref, wo_ref, o_ref, *, npt):
    ya = _dot(oa_ref[...].astype(BF16), wa_ref[...])
    gn = gn_ref[...]
    parts = []
    for h in range(HG_H):
        sl = slice(h * HG_DV, (h + 1) * HG_DV)
        ob = ob_ref[:, sl]
        ms = jnp.mean(ob * ob, axis=-1, keepdims=True)
        zg = zhg_ref[:, sl]
        y = ((ob * lax.rsqrt(ms + EPS)) * gn) * (zg * jax.nn.sigmoid(zg))
        parts.append(y.astype(BF16))
    yb = _dot(jnp.concatenate(parts, axis=1), wb_ref[...])
    mix = jax.nn.sigmoid(ga_ref[...]) * ya + jax.nn.sigmoid(gb_ref[...]) * yb
    y = _dot(mix.astype(BF16), wo_ref[...])

    @pl.when(pl.program_id(0) < npt)
    def _():
        o_ref[...] = xp_ref[...] + y

    @pl.when(pl.program_id(0) >= npt)
    def _():
        o_ref[...] = xs_ref[...] + y


def _mix(xp, xs, oa, ob, z, gn, wa, wb, wo):
    (n_p, d), n_s = xp.shape, xs.shape[0]
    n = n_p + n_s
    tm = 256
    npt, spec_p, spec_s = _two_group_specs(n_p, n_s, tm, d)
    row = lambda i: (i, 0)
    const = lambda i: (0, 0)
    resident = lambda w: pl.BlockSpec(w.shape, const, pipeline_mode=pl.Buffered(1))
    return pl.pallas_call(
        functools.partial(_mix_kernel, npt=npt),
        grid=(n // tm,),
        in_specs=[
            spec_p, spec_s,
            pl.BlockSpec((tm, NSA_W), row),
            pl.BlockSpec((tm, HG_W), row),
            pl.BlockSpec((tm, HG_W), lambda i: (i, C2_HG // HG_W)),
            pl.BlockSpec((tm, d), lambda i: (i, C2_GA // D_MODEL)),
            pl.BlockSpec((tm, d), lambda i: (i, C2_GB // D_MODEL)),
            pl.BlockSpec((1, HG_DV), const),
            resident(wa), resident(wb), resident(wo),
        ],
        out_specs=pl.BlockSpec((tm, d), row),
        out_shape=jax.ShapeDtypeStruct((n, d), F32),
        compiler_params=_params("parallel"),
        name="mix_out",
    )(xp, xs, oa, ob, z, z, z, gn, wa, wb, wo)


def _ffn_kernel(x_ref, g_ref, w1_ref, w2_ref, op_ref, os_ref, xn_ref, acc_ref, *, npt):
    i = pl.program_id(0)
    j = pl.program_id(1)
    last = pl.num_programs(1) - 1

    @pl.when(j == 0)
    def _():
        xn_ref[...] = _rmsnorm_bf16(x_ref[...], g_ref[...])
        acc_ref[...] = jnp.zeros(acc_ref.shape, F32)

    h = _dot(xn_ref[...], w1_ref[...])
    h = jnp.square(jnp.maximum(h, 0.0)).astype(BF16)
    acc_ref[...] += _dot(h, w2_ref[...])

    @pl.when((j == last) & (i < npt))
    def _():
        op_ref[...] = x_ref[...] + acc_ref[...]

    @pl.when((j == last) & (i >= npt))
    def _():
        os_ref[...] = x_ref[...] + acc_ref[...]


def _ffn(x, g, w1, w2, n_p):
    n, d = x.shape
    n_s = n - n_p
    dff = w1.shape[1]
    tm = _pick_tile(np.gcd(n_p, n_s), (512, 256))
    tf = 1024
    npt, spec_p, spec_s = _two_group_specs(n_p, n_s, tm, d)
    return pl.pallas_call(
        functools.partial(_ffn_kernel, npt=npt),
        grid=(n // tm, dff // tf),
        in_specs=[
            pl.BlockSpec((tm, d), lambda i, j: (i, 0)),
            pl.BlockSpec((1, d), lambda i, j: (0, 0)),
            pl.BlockSpec((d, tf), lambda i, j: (0, j)),
            pl.BlockSpec((tf, d), lambda i, j: (j, 0)),
        ],
        out_specs=(spec_p, spec_s),
        out_shape=(jax.ShapeDtypeStruct((n_p, d), F32), jax.ShapeDtypeStruct((n_s, d), F32)),
        scratch_shapes=[pltpu.VMEM((tm, d), BF16), pltpu.VMEM((tm, d), F32)],
        compiler_params=_params("arbitrary", "arbitrary"),
        name="ffn",
    )(x, g, w1, w2)


def _compress_weights(cmp_pe, cmp_w1, cmp_w2):
    m = L_CMP // D_CMP
    eye2 = jnp.eye(2, dtype=F32)
    out = []
    w2s, pes, bds = [], [], []
    for kv in range(2):
        w1p = cmp_w1[kv].reshape(m, D_CMP, HD, HD).transpose(1, 2, 0, 3)
        w = w1p[:, None, :, :, None, :] * eye2[None, :, None, None, :, None]
        w2s.append(w.reshape(D_CMP * LANES, m * LANES).astype(BF16))
        pe = cmp_pe[kv].reshape(m, D_CMP, 1, HD)
        pe = jnp.broadcast_to(pe, (m, D_CMP, 2, HD)).reshape(m, D_CMP * LANES)
        pes.append(jnp.concatenate([pe, jnp.zeros((8 - m, D_CMP * LANES), F32)], axis=0))
        bds.append(jnp.kron(eye2, cmp_w2[kv]).astype(BF16))
    tile_rows = LANES
    rpp = tile_rows // D_CMP
    src = np.arange(tile_rows)
    perm = np.zeros((tile_rows, tile_rows), np.float32)
    perm[(src % D_CMP) * rpp + src // D_CMP, src] = 1.0
    return (w2s[0], w2s[1], pes[0], pes[1], bds[0], bds[1], jnp.asarray(perm, BF16))


def _overlap_matrix(nc_pad, ns, transposed=False):
    cs = np.arange(nc_pad) * D_CMP
    ss = np.arange(LANES) * L_SEL
    ov = np.clip(np.minimum(cs[:, None] + L_CMP, ss[None, :] + L_SEL) - np.maximum(cs[:, None], ss[None, :]), 0, None)
    ov = ((ov / D_CMP) * (np.arange(LANES)[None, :] < ns)).astype(np.float32)
    return jnp.asarray(np.ascontiguousarray(ov.T) if transposed else ov, BF16)


def _expand_matrix(n_keys):
    e = (np.arange(n_keys)[None, :] // L_SEL) == np.arange(LANES)[:, None]
    return jnp.asarray(e.astype(np.float32), BF16)


def _rope_tables(seq, past, s_len, tm):
    half = HD // 2
    inv = THETA ** (-jnp.arange(half, dtype=F32) * 2.0 / HD)
    pos = jnp.concatenate([jnp.arange(seq), past + jnp.arange(tm) % s_len]).astype(F32)
    ang = pos[:, None] * inv[None, :]
    cos = jnp.cos(ang)
    sin = jnp.sin(ang)
    cos_t = jnp.concatenate([cos, cos, cos, cos], axis=1)
    sin_t = jnp.concatenate([-sin, sin, -sin, sin], axis=1)
    return cos_t, sin_t


def kernel(x_prompt, x_sample, cache_cmp_kv, cache_sel_kv, cache_win_kv, state_hgrn, page_table, norm1_g, w_in,
           q_norm_g, k_norm_g, cmp_pe, cmp_w1, cmp_w2, w_proj_a, hgrn_lb_logits, hgrn_norm_g, w_proj_b, w_out,
           norm2_g, w_ff1, w_ff2):
    assert w_in.shape[0] == 1, "single-layer step"
    batch, seq, d = x_prompt.shape
    dec_batch, s_len, _ = x_sample.shape
    n_prompt = batch * seq
    n_sample = dec_batch * s_len
    npg = page_table.shape[1]
    page = cache_cmp_kv.shape[2]
    past = npg * page
    wb = cache_win_kv.shape[2]
    assert seq % Q_BLK == 0 and seq >= WINDOW + Q_BLK and n_prompt % 256 == 0 and n_sample % 256 == 0
    assert s_len % 8 == 0 and page % D_CMP == 0 and 2 * s_len <= HG_DK and dec_batch % 4 == 0 and npg % 2 == 0

    xp = x_prompt.reshape(n_prompt, d)
    xs = x_sample.reshape(n_sample, d)

    cw = _compress_weights(cmp_pe[0], cmp_w1[0], cmp_w2[0])
    cos_t, sin_t = _rope_tables(seq, past, s_len, 256)
    gq = jnp.tile(q_norm_g[0][None, :], (1, 2))
    gk = jnp.tile(k_norm_g[0], (1, 2))
    logits = hgrn_lb_logits.astype(F32)

    def token_minor(c):
        return jnp.transpose(c, (0, 2, 3, 4, 1)).reshape(c.shape[0], KVW, c.shape[1])

    def token_major(c_t):
        return jnp.transpose(c_t.reshape(c_t.shape[0], 2, KVH, HD, c_t.shape[2]), (0, 4, 1, 2, 3))

    w_t = jnp.transpose(w_in[0])
    z1 = _in_proj(xp, xs, norm1_g, w_t, 0, N_Z1, 896)
    z2 = _in_proj(xp, xs, norm1_g, w_t, sum(_SIZES[:3]), N_Z2, 1024)
    (q_tok, q_t, rows_cmp, rows_sel, rows_win, rt_cmp, rt_sel, rt_win,
     ks, vs_t, kw, vw_t) = _qkv_post(z1, cos_t, sin_t, gq, gk, n_prompt, seq)

    bdv_t = lax.optimization_barrier(cw[5].T)
    kc, vc_t = _compress_prompt(rt_cmp, cw[:5] + (bdv_t,) + cw[6:], batch, seq)
    ov_p = _overlap_matrix(seq // D_CMP, seq // L_SEL, transposed=True)
    ocmp, sel_bias = _cmp_select(q_t, kc, vc_t, ov_p, batch, seq)
    oa = _sel_win(q_t, ks, vs_t, kw, vw_t, sel_bias, z1, ocmp, batch, seq)
    ob, p_state = _hgrn_prompt(z2, logits, batch, seq)

    ns_s = -(-(past + s_len) // L_SEL)
    ov_s = _overlap_matrix(past // D_CMP, ns_s)
    oa, s_win_t = _sample_nsa(
        page_table, token_minor(cache_cmp_kv[0]), token_minor(cache_sel_kv[0]), token_minor(cache_win_kv[0]),
        q_tok, rows_sel, rows_win, z1, cw, ov_s, _expand_matrix(past + page), oa, n_prompt, dec_batch, s_len)
    ob, s_state = _hgrn_sample(z2, logits, state_hgrn[0], ob, n_prompt, dec_batch, s_len)

    x2 = _mix(xp, xs, oa, ob, z2, hgrn_norm_g, w_proj_a[0].astype(BF16), w_proj_b[0].astype(BF16),
              w_out[0].astype(BF16))
    y_p, y_s = _ffn(x2, norm2_g, w_ff1[0].astype(BF16), w_ff2[0].astype(BF16), n_prompt)

    kv_shape = (2, KVH, HD)
    wp = min(WINDOW, seq)
    return (
        y_p.reshape(batch, seq, d),
        y_s.reshape(dec_batch, s_len, d),
        token_major(rt_cmp[:batch])[None],
        token_major(rt_sel[:batch])[None],
        token_major(rt_win[:batch, :, seq - wp:])[None],
        p_state[None],
        rows_cmp[n_prompt:].reshape((1, dec_batch, s_len) + kv_shape),
        rows_sel[n_prompt:].reshape((1, dec_batch, s_len) + kv_shape),
        token_major(s_win_t)[None],
        s_state[None],
    )
```

```python
import functools

import numpy as np
import jax
import jax.numpy as jnp
from jax import lax
from jax.experimental import pallas as pl
from jax.experimental.pallas import tpu as pltpu

F32 = jnp.float32
BF16 = jnp.bfloat16

D_MODEL = 2048
HEADS = 16
KVH = 4
GRP = HEADS // KVH
HD = 64
NSA_W = HEADS * HD
L_CMP = 32
D_CMP = 16
L_SEL = 64
N_TOP = 16
WINDOW = 512
Q_BLK = 128
THETA = 10000.0
HG_H = 8
HG_DK = 128
HG_DV = 128
HG_W = HG_H * HG_DK
HG_SUB = 16
SEL_KT = 512
EPS = 1e-6
NEG = -1e30
BIG = 1e9

LANES = 128
KVW = 2 * KVH * HD
KW = KVH * HD

_SIZES = (NSA_W, 6 * KVH * HD, 3 * HEADS, HG_W, HG_W, HG_W, HG_W, D_MODEL, D_MODEL)
C1_Q, C1_KV, C1_BR, N_Z1 = 0, 1024, 2560, 2688
C2_HQ, C2_HF, C2_HI, C2_HG, C2_GA, C2_GB, N_Z2 = 0, 1024, 2048, 3072, 4096, 6144, 8192

VMEM_LIMIT = 56 * 1024 * 1024


def _params(*sem):
    return pltpu.CompilerParams(dimension_semantics=sem, vmem_limit_bytes=VMEM_LIMIT)


def _nt(a, b):
    return lax.dot_general(a, b, (((1,), (1,)), ((), ())), preferred_element_type=F32)


def _dot(a, b):
    return jnp.dot(a, b, preferred_element_type=F32)


def _split3(x):
    hi = x.astype(BF16)
    r1 = x - hi.astype(F32)
    mid = r1.astype(BF16)
    lo = (r1 - mid.astype(F32)).astype(BF16)
    return hi, mid, lo


def _dot_exact_rhs(m, x):
    hi, mid, lo = _split3(x)
    return _dot(m, hi) + _dot(m, mid) + _dot(m, lo)


def _dot_exact_lhs(x, m):
    hi, mid, lo = _split3(x)
    return _dot(hi, m) + _dot(mid, m) + _dot(lo, m)


def _pick_tile(n, cands):
    for c in cands:
        if n % c == 0:
            return c
    raise ValueError(f"no tile in {cands} divides {n}")


def _rmsnorm_bf16(x, g):
    ms = jnp.mean(x * x, axis=-1, keepdims=True)
    return ((x * lax.rsqrt(ms + EPS)) * g).astype(BF16)


def _two_group_specs(n_p, n_s, tm, width, sample_kwargs=None):
    npt = n_p // tm
    assert n_p % tm == 0 and n_s % tm == 0
    spec_p = pl.BlockSpec((tm, width), lambda i, *_: (jnp.minimum(i, npt - 1), 0))
    spec_s = pl.BlockSpec((tm, width), lambda i, *_: (jnp.maximum(i - npt, 0), 0), **(sample_kwargs or {}))
    return npt, spec_p, spec_s


def _rms_matmul_kernel(xp_ref, xs_ref, g_ref, w_ref, o_ref, xn_ref, *, npt):
    i = pl.program_id(0)

    @pl.when((pl.program_id(1) == 0) & (i < npt))
    def _():
        xn_ref[...] = _rmsnorm_bf16(xp_ref[...], g_ref[...])

    @pl.when((pl.program_id(1) == 0) & (i >= npt))
    def _():
        xn_ref[...] = _rmsnorm_bf16(xs_ref[...], g_ref[...])

    o_ref[...] = _nt(xn_ref[...], w_ref[...].astype(BF16))


def _in_proj(xp, xs, g, w_t, row0, nout, tn):
    (n_p, d), n_s = xp.shape, xs.shape[0]
    n = n_p + n_s
    assert nout % tn == 0 and row0 % 8 == 0 and row0 + nout <= -(-w_t.shape[0] // 8) * 8 + tn
    tm = _pick_tile(np.gcd(n_p, n_s), (1024, 512, 256))
    npt, spec_p, spec_s = _two_group_specs(n_p, n_s, tm, d, dict(pipeline_mode=pl.Buffered(1)))
    return pl.pallas_call(
        functools.partial(_rms_matmul_kernel, npt=npt),
        grid=(n // tm, nout // tn),
        in_specs=[
            spec_p, spec_s,
            pl.BlockSpec((1, d), lambda i, j: (0, 0)),
            pl.BlockSpec((pl.Element(tn), pl.Element(d)), lambda i, j: ((row0 // 8 + j * (tn // 8)) * 8, 0)),
        ],
        out_specs=pl.BlockSpec((tm, tn), lambda i, j: (i, j)),
        out_shape=jax.ShapeDtypeStruct((n, nout), F32),
        scratch_shapes=[pltpu.VMEM((tm, d), BF16)],
        compiler_params=_params("parallel", "arbitrary"),
        name="in_proj",
    )(xp, xs, g, w_t)


def _qkv_post_kernel(zq_ref, zc_ref, zs_ref, zw_ref, cos_ref, sin_ref, gq_ref, gk_ref,
                     qtok_ref, qt_ref, rc_ref, rs_ref, rw_ref, tc_ref, ts_ref, tw_ref,
                     ks_ref, vs_ref, kw_ref, vw_ref):
    cos = cos_ref[...]
    sin = sin_ref[...]
    lane = lax.broadcasted_iota(jnp.int32, cos.shape, 1)
    first = (lane % HD) < (HD // 2)
    hi_ = lax.broadcasted_iota(jnp.int32, (LANES, LANES), 0) // HD
    hj_ = lax.broadcasted_iota(jnp.int32, (LANES, LANES), 1) // HD
    same_head = jnp.where(hi_ == hj_, 1.0, 0.0).astype(BF16)

    def norm_rope(xc, g_row):
        ms = _dot_exact_lhs(xc * xc, same_head) * (1.0 / HD)
        y = (xc * lax.rsqrt(ms + EPS)) * g_row
        partner = jnp.where(first, pltpu.roll(y, LANES - HD // 2, 1), pltpu.roll(y, HD // 2, 1))
        return y * cos + partner * sin

    gq = gq_ref[...]
    for c in range(NSA_W // LANES):
        sl = slice(c * LANES, (c + 1) * LANES)
        qc = norm_rope(zq_ref[:, sl], gq)
        qtok_ref[:, sl] = qc
        qtb = qc.T.astype(BF16)
        qt_ref[2 * c] = qtb[:HD]
        qt_ref[2 * c + 1] = qtb[HD:]

    tm = zq_ref.shape[0]
    zin = (zc_ref, zs_ref, zw_ref)
    rows = (rc_ref, rs_ref, rw_ref)
    rows_t = (tc_ref, ts_ref, tw_ref)
    kg = (None, ks_ref, kw_ref)
    vg = (None, vs_ref, vw_ref)
    for br in range(3):
        gk = gk_ref[br:br + 1, :]
        cols = []
        for c in range(KW // LANES):
            cols.append(norm_rope(zin[br][:, c * LANES:(c + 1) * LANES], gk))
        for c in range(KW // LANES):
            cols.append(zin[br][:, KW + c * LANES:KW + (c + 1) * LANES])
        cols_t = [col.T for col in cols]
        for c, col in enumerate(cols):
            rows[br][:, c * LANES:(c + 1) * LANES] = col
            rows_t[br][c * LANES:(c + 1) * LANES, :] = cols_t[c]
        if kg[br] is not None:
            wt = vg[br].shape[-1]
            for c in range(KW // LANES):
                kb = cols[c].astype(BF16)
                kg[br][2 * c] = kb[:, :HD]
                kg[br][2 * c + 1] = kb[:, HD:]
                vtb = cols_t[KW // LANES + c].astype(BF16)
                for t in range(tm // wt):
                    vg[br][2 * c, t] = vtb[:HD, t * wt:(t + 1) * wt]
                    vg[br][2 * c + 1, t] = vtb[HD:, t * wt:(t + 1) * wt]


def _qkv_post(z1, cos_tab, sin_tab, gq, gk, n_prompt, seq):
    n = z1.shape[0]
    tm = 256
    n_ptiles = n_prompt // tm
    tiles_per_seq = seq // tm
    n_slabs = -(-n // seq)

    def tab_map(i):
        return (jnp.where(i < n_ptiles, i % tiles_per_seq, tiles_per_seq), 0)

    row = lambda i: (i, 0)
    head = lambda i: (0, i, 0)
    slab = lambda i: (i // tiles_per_seq, 0, i % tiles_per_seq)
    rows_sd = jax.ShapeDtypeStruct((n, KVW), F32)
    rows_t_sd = jax.ShapeDtypeStruct((n_slabs, KVW, seq), F32)
    grp_sd = jax.ShapeDtypeStruct((KVH, n, HD), BF16)
    assert SEL_KT % tm == 0 and tm % Q_BLK == 0
    out_shape = (
        jax.ShapeDtypeStruct((n, NSA_W), F32),
        jax.ShapeDtypeStruct((HEADS, HD, n), BF16),
        rows_sd, rows_sd, rows_sd,
        rows_t_sd, rows_t_sd, rows_t_sd,
        grp_sd,
        jax.ShapeDtypeStruct((KVH, -(-n // SEL_KT), HD, SEL_KT), BF16),
        grp_sd,
        jax.ShapeDtypeStruct((KVH, n // Q_BLK, HD, Q_BLK), BF16),
    )
    rows_spec = pl.BlockSpec((tm, KVW), row)
    rows_t_spec = pl.BlockSpec((None, KVW, tm), slab)
    grp_spec = pl.BlockSpec((KVH, tm, HD), head)
    per = SEL_KT // tm
    out_specs = (
        pl.BlockSpec((tm, NSA_W), row),
        pl.BlockSpec((HEADS, HD, tm), lambda i: (0, 0, i)),
        rows_spec, rows_spec, rows_spec,
        rows_t_spec, rows_t_spec, rows_t_spec,
        grp_spec,
        pl.BlockSpec((KVH, 1, HD, tm), lambda i: (0, i // per, 0, i % per)),
        grp_spec,
        pl.BlockSpec((KVH, tm // Q_BLK, HD, Q_BLK), lambda i: (0, i, 0, 0)),
    )
    kv_in = lambda br: pl.BlockSpec((tm, KVW), lambda i: (i, C1_KV // KVW + br))
    return pl.pallas_call(
        _qkv_post_kernel,
        grid=(n // tm,),
        in_specs=[
            pl.BlockSpec((tm, NSA_W), lambda i: (i, C1_Q // NSA_W)),
            kv_in(0), kv_in(1), kv_in(2),
            pl.BlockSpec((tm, LANES), tab_map),
            pl.BlockSpec((tm, LANES), tab_map),
            pl.BlockSpec((1, LANES), lambda i: (0, 0)),
            pl.BlockSpec((3, LANES), lambda i: (0, 0)),
        ],
        out_specs=out_specs,
        out_shape=out_shape,
        compiler_params=_params("parallel"),
        name="qkv_post",
    )(z1, z1, z1, z1, cos_tab, sin_tab, gq, gk)


def _compress_pair(flat, w2, pe8, w2bd, n_ch, transposed=False):
    ab = _dot(flat, w2)
    hp = _dot(pe8.astype(BF16), w2)
    hpe = hp[0:1, :LANES] + hp[1:2, LANES:]
    nxt = pltpu.roll(ab[:, LANES:], n_ch - 1, 0)
    h = ab[:, :LANES] + nxt + hpe
    act = (h * jax.nn.sigmoid(h)).astype(BF16)
    if transposed:
        return _nt(w2bd, act)
    return _dot(act, w2bd)


def _scatter_chunk_rows(xs_ref, perm, tile_t, c0):
    rpp = tile_t.shape[1] // D_CMP
    xp = _nt(perm, tile_t)
    for t in range(D_CMP):
        xs_ref[t, c0:c0 + rpp, :] = xp[t * rpp:(t + 1) * rpp, :]


def _chunk_flat(xs_ref, p):
    return jnp.concatenate([xs_ref[t, :, p * LANES:(p + 1) * LANES] for t in range(D_CMP)], axis=1).astype(BF16)


def _compress_prompt_kernel(rows_ref, w2k_ref, w2v_ref, pek_ref, pev_ref, bdk_ref, bdv_ref, perm_ref, kc_ref, vc_ref,
                            xs_ref, *, n_ch):
    w2 = (w2k_ref, w2v_ref)
    pe = (pek_ref, pev_ref)
    bd = (bdk_ref, bdv_ref)
    tile_rows = perm_ref.shape[0]
    rpp = tile_rows // D_CMP
    for i in range(n_ch // rpp):
        _scatter_chunk_rows(xs_ref, perm_ref[...], rows_ref[:, i * tile_rows:(i + 1) * tile_rows].astype(BF16), i * rpp)
    for p in range(KVW // LANES):
        kv = p // 2
        flat = _chunk_flat(xs_ref, p)
        res = _compress_pair(flat, w2[kv][...], pe[kv][...], bd[kv][...], n_ch, transposed=(kv == 1)).astype(BF16)
        g0 = 2 * (p % 2)
        if kv == 0:
            kc_ref[0, g0] = res[:, :HD]
            kc_ref[0, g0 + 1] = res[:, HD:]
        else:
            vc_ref[0, g0] = res[:HD]
            vc_ref[0, g0 + 1] = res[HD:]


def _compress_prompt(rows_cmp, cw, batch, seq):
    n_ch = seq // D_CMP
    const = lambda b: (0, 0)
    return pl.pallas_call(
        functools.partial(_compress_prompt_kernel, n_ch=n_ch),
        grid=(batch,),
        in_specs=[pl.BlockSpec((None, KVW, seq), lambda b: (b, 0, 0))] + [pl.BlockSpec(w.shape, const) for w in cw],
        out_specs=(pl.BlockSpec((1, KVH, n_ch, HD), lambda b: (b, 0, 0, 0)),
                   pl.BlockSpec((1, KVH, HD, n_ch), lambda b: (b, 0, 0, 0))),
        out_shape=(jax.ShapeDtypeStruct((batch, KVH, n_ch, HD), BF16),
                   jax.ShapeDtypeStruct((batch, KVH, HD, n_ch), BF16)),
        scratch_shapes=[pltpu.VMEM((D_CMP, n_ch, KVW), F32)],
        compiler_params=_params("parallel"),
        name="compress_prompt",
    )(rows_cmp, *cw)


def _rank_select(score_t, n_blocks, n_rows, n_top):
    st = score_t[:n_rows]
    jio = lax.broadcasted_iota(jnp.int32, st.shape, 0)
    cnt = jnp.zeros(st.shape, F32)
    for i in range(n_blocks):
        cnt = cnt + _beats(score_t[i:i + 1, :], st, jio, i)
    return jnp.where(cnt < n_top, 1.0, 0.0)


def _beats(row, st, jio, i):
    return jnp.where(row > st, 1.0, jnp.where(row == st, jnp.where(jio > i, 1.0, 0.0), 0.0))


def _pad_rows(x, n):
    if x.shape[0] == n:
        return x
    return jnp.concatenate([x, jnp.zeros((n - x.shape[0], x.shape[1]), x.dtype)], axis=0)


def _cmp_select_kernel(qt_ref, kc_ref, vct_ref, ovt_ref, ocmp_ref, sel_ref, st_ref, *, nc, ns, n_top):
    s0 = pl.program_id(1) * Q_BLK
    ncp = kc_ref.shape[2]
    nsp = st_ref.shape[1]
    c_io = lax.broadcasted_iota(jnp.int32, (ncp, Q_BLK), 0)
    qpos_c = s0 + lax.broadcasted_iota(jnp.int32, (ncp, Q_BLK), 1)
    cbias = jnp.where((c_io * D_CMP + (L_CMP - 1) <= qpos_c) & (c_io < nc), 0.0, NEG)
    qpos_1 = s0 + lax.broadcasted_iota(jnp.int32, (1, GRP * Q_BLK), 1) % Q_BLK
    col_ok = jnp.where((qpos_1 >= L_CMP - 1) & (nc > 0), 1.0, 0.0)
    jio = lax.broadcasted_iota(jnp.int32, (nsp, LANES), 0)
    qpos_n = s0 + lax.broadcasted_iota(jnp.int32, (nsp, LANES), 1)
    vis_t = (jio * L_SEL <= qpos_n) & (jio < ns)
    cur = qpos_n // L_SEL
    forced = (jio == 0) | (jio == cur) | (jio == cur - 1)
    ovt = ovt_ref[...]
    grp = range(KVH)
    qt = [jnp.concatenate([qt_ref[GRP * g + r] for r in range(GRP)], axis=1) * (HD ** -0.5) for g in grp]
    s = [_dot(kc_ref[0, g], qt[g]) for g in grp]
    s = [jnp.concatenate([s[g][:, r * Q_BLK:(r + 1) * Q_BLK] + cbias for r in range(GRP)], axis=1) for g in grp]
    e = [jnp.exp(s[g] - jnp.max(s[g], axis=0, keepdims=True)) for g in grp]
    p = [e[g] * (col_ok / jnp.maximum(jnp.sum(e[g], axis=0, keepdims=True), 1.0)) for g in grp]
    o = [_dot(vct_ref[0, g], p[g].astype(BF16)) for g in grp]
    for g in grp:
        for r in range(GRP):
            ocmp_ref[GRP * g + r] = o[g][:, r * Q_BLK:(r + 1) * Q_BLK]
    psum = [p[g][:, 0:Q_BLK] + p[g][:, Q_BLK:2 * Q_BLK] + p[g][:, 2 * Q_BLK:3 * Q_BLK] + p[g][:, 3 * Q_BLK:]
            for g in grp]
    imp = [_dot_exact_rhs(ovt, psum[g]) for g in grp]
    for g in grp:
        st_ref[g] = jnp.where(vis_t, imp[g][:nsp] + jnp.where(forced, BIG, 0.0), -BIG)

    n_vis = jnp.minimum((s0 + Q_BLK - 1) // L_SEL + 1, ns)

    def count(i, cnts):
        return tuple(c + _beats(st_ref[g, pl.ds(i, 1), :], st_ref[g, 0:nsp, :], jio, i) for g, c in enumerate(cnts))

    cnts = lax.fori_loop(0, n_vis, count, tuple(jnp.zeros((nsp, LANES), F32) for _ in range(KVH)))
    for g in range(KVH):
        sel_ref[g] = jnp.where((cnts[g] < n_top) & vis_t, 0.0, NEG)


def _cmp_select(q_t, kc, vc_t, ov_t, batch, seq):
    nqt = seq // Q_BLK
    n_ch = kc.shape[2]
    ns = seq // L_SEL
    nsp = -(-ns // 8) * 8
    kern = functools.partial(_cmp_select_kernel, nc=n_ch - 1, ns=ns, n_top=min(N_TOP, ns))
    q_spec = pl.BlockSpec((HEADS, HD, Q_BLK), lambda b, t: (0, 0, b * nqt + t))
    return pl.pallas_call(
        kern,
        grid=(batch, nqt),
        in_specs=[
            q_spec,
            pl.BlockSpec((1, KVH, n_ch, HD), lambda b, t: (b, 0, 0, 0)),
            pl.BlockSpec((1, KVH, HD, n_ch), lambda b, t: (b, 0, 0, 0)),
            pl.BlockSpec(ov_t.shape, lambda b, t: (0, 0)),
        ],
        out_specs=(
            q_spec,
            pl.BlockSpec((None, KVH, nsp, LANES), lambda b, t: (b * nqt + t, 0, 0, 0)),
        ),
        out_shape=(
            jax.ShapeDtypeStruct((HEADS, HD, batch * seq), F32),
            jax.ShapeDtypeStruct((batch * nqt, KVH, nsp, LANES), F32),
        ),
        scratch_shapes=[pltpu.VMEM((KVH, nsp, LANES), F32)],
        compiler_params=_params("parallel", "parallel"),
        name="cmp_select",
    )(q_t, kc, vc_t, ov_t)


def _sel_win_kernel(qt_ref, ks_ref, vst_ref, kw_ref, vwt_ref, bias_ref, zbr_ref, ocmp_ref, oa_ref, q_s):
    kt = vst_ref.shape[-1]
    s0 = pl.program_id(1) * Q_BLK
    cols = GRP * Q_BLK
    n_kt = (s0 + Q_BLK + kt - 1) // kt
    bpt = kt // L_SEL
    key_io = lax.broadcasted_iota(jnp.int32, (kt, Q_BLK), 0)
    qpos = s0 + lax.broadcasted_iota(jnp.int32, (kt, Q_BLK), 1)
    ones_row = 2 * 8
    va_rows = HD + ones_row

    for g in range(KVH):
        q_s[g] = jnp.concatenate([qt_ref[GRP * g + r] for r in range(GRP)], axis=1) * (HD ** -0.5)

    def with_bias(st, bias):
        return jnp.concatenate([st[:, r * Q_BLK:(r + 1) * Q_BLK] + bias for r in range(GRP)], axis=1)

    def body(j, carry):
        k0 = pl.multiple_of(j * kt, kt)
        grp = range(KVH)
        causal = jnp.where(k0 + key_io <= qpos, 0.0, NEG)
        st = [_dot(ks_ref[g, pl.ds(k0, kt), :], q_s[g]) for g in grp]
        bias = [jnp.concatenate([jnp.broadcast_to(bias_ref[g, pl.ds(j * bpt + b, 1), :], (L_SEL, Q_BLK))
                                 for b in range(bpt)], axis=0) + causal for g in grp]
        st = [with_bias(st[g], bias[g]) for g in grp]
        m_new = [jnp.maximum(carry[g][0], jnp.max(st[g], axis=0, keepdims=True)) for g in grp]
        corr = [jnp.exp(carry[g][0] - m_new[g]) for g in grp]
        p = [jnp.exp(st[g] - m_new[g]).astype(BF16) for g in grp]
        ones = jnp.ones((ones_row, kt), BF16)
        pv = [_dot(jnp.concatenate([vst_ref[g, j], ones], axis=0), p[g]) for g in grp]
        return tuple((m_new[g], carry[g][1] * corr[g] + pv[g]) for g in grp)

    init = tuple((jnp.full((1, cols), NEG, F32), jnp.zeros((va_rows, cols), F32)) for _ in range(KVH))
    res = lax.fori_loop(0, n_kt, body, init)

    wl = WINDOW + Q_BLK
    wstart = pl.multiple_of(jnp.maximum(s0 - WINDOW, 0), Q_BLK)
    wt0 = wstart // Q_BLK
    kposw = wstart + lax.broadcasted_iota(jnp.int32, (wl, Q_BLK), 0)
    qposw = s0 + lax.broadcasted_iota(jnp.int32, (wl, Q_BLK), 1)
    wbias = jnp.where((kposw <= qposw) & (kposw > qposw - WINDOW), 0.0, NEG)
    ones_w = jnp.ones((ones_row, wl), BF16)
    gates_t = jax.nn.sigmoid(zbr_ref[...]).T
    o_t = []
    for g in range(KVH):
        acc = res[g][1]
        o_sel_t = acc[:HD] / acc[HD:HD + 1]
        st = with_bias(_dot(kw_ref[g, pl.ds(wstart, wl), :], q_s[g]), wbias)
        m = jnp.max(st, axis=0, keepdims=True)
        e = jnp.exp(st - m).astype(BF16)
        vwt = jnp.concatenate([vwt_ref[g, wt0 + i] for i in range(wl // Q_BLK)], axis=1)
        pv = _dot(jnp.concatenate([vwt, ones_w], axis=0), e)
        o_win_t = pv[:HD] / jnp.maximum(pv[HD:HD + 1], 1.0)
        for r in range(GRP):
            h = GRP * g + r
            cs = slice(r * Q_BLK, (r + 1) * Q_BLK)
            o_t.append(gates_t[h:h + 1] * ocmp_ref[h]
                       + gates_t[HEADS + h:HEADS + h + 1] * o_sel_t[:, cs]
                       + gates_t[2 * HEADS + h:2 * HEADS + h + 1] * o_win_t[:, cs])
    for c in range(HEADS // 2):
        oa_ref[:, c * LANES:(c + 1) * LANES] = jnp.concatenate([o_t[2 * c], o_t[2 * c + 1]], axis=0).T


def _sel_win(q_t, ks, vs_t, kw, vw_t, sel_bias, z, ocmp, batch, seq):
    nqt = seq // Q_BLK
    assert seq % SEL_KT == 0
    once = dict(pipeline_mode=pl.Buffered(1))
    k_spec = pl.BlockSpec((KVH, seq, HD), lambda b, t: (0, b, 0), **once)
    tile = lambda b, t: (b * nqt + t, 0, 0, 0)
    return pl.pallas_call(
        _sel_win_kernel,
        grid=(batch, nqt),
        in_specs=[
            pl.BlockSpec((HEADS, HD, Q_BLK), lambda b, t: (0, 0, b * nqt + t)),
            k_spec,
            pl.BlockSpec((KVH, seq // SEL_KT, HD, SEL_KT), lambda b, t: (0, b, 0, 0), **once),
            k_spec,
            pl.BlockSpec((KVH, seq // Q_BLK, HD, Q_BLK), lambda b, t: (0, b, 0, 0), **once),
            pl.BlockSpec((None,) + sel_bias.shape[1:], tile),
            pl.BlockSpec((Q_BLK, LANES), lambda b, t: (b * nqt + t, C1_BR // LANES)),
            pl.BlockSpec((HEADS, HD, Q_BLK), lambda b, t: (0, 0, b * nqt + t)),
        ],
        out_specs=pl.BlockSpec((Q_BLK, NSA_W), lambda b, t: (b * nqt + t, 0)),
        out_shape=jax.ShapeDtypeStruct((z.shape[0], NSA_W), F32),
        scratch_shapes=[pltpu.VMEM((KVH, HD, GRP * Q_BLK), BF16)],
        compiler_params=_params("parallel", "arbitrary"),
        name="sel_win",
    )(q_t, ks, vs_t, kw, vw_t, sel_bias, z, ocmp)


def _hgrn_gates(zq, zf, logits):
    mx = jnp.max(logits, axis=0, keepdims=True)
    ex = jnp.exp(logits - mx)
    lb = ex[0:1] / jnp.sum(ex, axis=0, keepdims=True)
    hq = zq * jax.nn.sigmoid(zq)
    e = jnp.exp(-jnp.abs(zf))
    r = 1.0 / (1.0 + e)
    pos = zf >= 0.0
    sig = jnp.where(pos, r, e * r)
    sig_neg = jnp.where(pos, e * r, r)
    f = lb + (1.0 - lb) * sig
    log_f = jnp.where(f > 0.0, jnp.log(f), jnp.log1p(-lb) + zf)
    hk = (1.0 - lb) * sig_neg
    return hq, hk, log_f


def _intra_chunk(ac, qc, kc, vc):
    n = ac.shape[0]
    sub = 8
    tio = lax.broadcasted_iota(jnp.int32, (sub, ac.shape[1]), 0)
    accs = [jnp.zeros((sub, vc.shape[1]), F32) for _ in range(n // sub)]
    for s in range(n):
        a_s, k_s, v_s = ac[s:s + 1, :], kc[s:s + 1, :], vc[s:s + 1, :]
        for i in range(s // sub, n // sub):
            rs = slice(i * sub, (i + 1) * sub)
            w = qc[rs] * k_s * jnp.exp(ac[rs] - a_s)
            if i == s // sub:
                w = jnp.where(tio >= s - i * sub, w, 0.0)
            accs[i] = accs[i] + jnp.sum(w, axis=-1, keepdims=True) * v_s
    return accs[0] if len(accs) == 1 else jnp.concatenate(accs, axis=0)


def _hgrn_prompt_kernel(zq_ref, zf_ref, zi_ref, lg_ref, tri_ref, full_ref, o_ref, sout_ref,
                        st_ref, a_s, q_s, k_s, qt_s, ktb_s, eal_s, vt_s):
    t = pl.program_id(1)
    tt = zq_ref.shape[0]

    @pl.when(t == 0)
    def _():
        st_ref[...] = jnp.zeros(st_ref.shape, F32)

    hq, hk, log_f = _hgrn_gates(zq_ref[...], zf_ref[...], lg_ref[...])
    a = _dot_exact_rhs(tri_ref[...], log_f)
    al = _dot_exact_rhs(full_ref[...], log_f)
    a_s[...] = a
    q_s[...] = hq
    k_s[...] = hk
    qt_s[...] = hq * jnp.exp(a)
    ktb_s[...] = (hk * jnp.exp(al - a)).astype(BF16)
    eal_s[...] = jnp.exp(al)
    for h in range(HG_H):
        vt_s[h] = zi_ref[:, h * HG_DV:(h + 1) * HG_DV].T.astype(BF16)
    lane_tok = lax.broadcasted_iota(jnp.int32, (HG_DV, tt), 1)

    def chunk(c, carry):
        r0 = pl.multiple_of(c * HG_SUB, HG_SUB)
        in_chunk = (lane_tok >= r0) & (lane_tok < r0 + HG_SUB)
        for h in range(HG_H):
            sl = slice(h * HG_DK, (h + 1) * HG_DK)
            st = st_ref[h]
            o = _nt(qt_s[pl.ds(r0, HG_SUB), sl].astype(BF16), st.astype(BF16))
            o = o + _intra_chunk(a_s[pl.ds(r0, HG_SUB), sl], q_s[pl.ds(r0, HG_SUB), sl],
                                 k_s[pl.ds(r0, HG_SUB), sl], zi_ref[pl.ds(r0, HG_SUB), sl])
            o_ref[pl.ds(r0, HG_SUB), sl] = o
            vtm = jnp.where(in_chunk, vt_s[h], jnp.zeros_like(vt_s[h]))
            st_ref[h] = st * eal_s[pl.ds(r0, 1), sl] + _dot(vtm, ktb_s[:, sl])
        return carry

    lax.fori_loop(0, tt // HG_SUB, chunk, 0)

    @pl.when(t == pl.num_programs(1) - 1)
    def _():
        for h in range(HG_H):
            sout_ref[0, h] = st_ref[h].T


def _hgrn_prompt(z, logits, batch, seq):
    tt = 128
    nt = seq // tt
    sub = np.arange(tt) // HG_SUB
    same = sub[:, None] == sub[None, :]
    tri = jnp.asarray((same & (np.arange(tt)[None, :] <= np.arange(tt)[:, None])).astype(np.float32), BF16)
    full = jnp.asarray(same.astype(np.float32), BF16)
    col = lambda c: (lambda b, t: (b * nt + t, c // HG_W))
    const = lambda b, t: (0, 0)
    return pl.pallas_call(
        _hgrn_prompt_kernel,
        grid=(batch, nt),
        in_specs=[
            pl.BlockSpec((tt, HG_W), col(C2_HQ)),
            pl.BlockSpec((tt, HG_W), col(C2_HF)),
            pl.BlockSpec((tt, HG_W), col(C2_HI)),
            pl.BlockSpec((2, HG_W), const),
            pl.BlockSpec((tt, tt), const),
            pl.BlockSpec((tt, tt), const),
        ],
        out_specs=(
            pl.BlockSpec((tt, HG_W), lambda b, t: (b * nt + t, 0)),
            pl.BlockSpec((1, HG_H, HG_DK, HG_DV), lambda b, t: (b, 0, 0, 0)),
        ),
        out_shape=(
            jax.ShapeDtypeStruct((z.shape[0], HG_W), F32),
            jax.ShapeDtypeStruct((batch, HG_H, HG_DK, HG_DV), F32),
        ),
        scratch_shapes=[
            pltpu.VMEM((HG_H, HG_DV, HG_DK), F32),
            pltpu.VMEM((tt, HG_W), F32),
            pltpu.VMEM((tt, HG_W), F32),
            pltpu.VMEM((tt, HG_W), F32),
            pltpu.VMEM((tt, HG_W), F32),
            pltpu.VMEM((tt, HG_W), BF16),
            pltpu.VMEM((tt, HG_W), F32),
            pltpu.VMEM((HG_H, HG_DV, tt), BF16),
        ],
        compiler_params=_params("parallel", "arbitrary"),
        name="hgrn_prompt",
    )(z, z, z, logits, tri, full)


def _hgrn_sample_kernel(zq_ref, zf_ref, zi_ref, lg_ref, s0_ref, ob_hbm_ref, o_ref, sout_ref, *, s_len):
    del ob_hbm_ref
    n_seq = s0_ref.shape[0]
    hq_all, hk_all, logf_all = _hgrn_gates(zq_ref[...], zf_ref[...], lg_ref[...])
    rio = lax.broadcasted_iota(jnp.int32, (s_len, HG_W), 0)
    row0 = lax.broadcasted_iota(jnp.int32, (s_len, HG_DK), 0) == 0
    pad = HG_DK - 2 * s_len
    for q in range(n_seq):
        rs = slice(q * s_len, (q + 1) * s_len)
        hq, hk, log_f = hq_all[rs], hk_all[rs], logf_all[rs]
        a = log_f
        d = 1
        while d < s_len:
            a = a + jnp.where(rio >= d, pltpu.roll(a, d, 0), 0.0)
            d *= 2
        a_last = jnp.sum(log_f, axis=0, keepdims=True)
        qt = hq * jnp.exp(a)
        kt = hk * jnp.exp(a_last - a)
        ea = jnp.exp(a_last)
        hv = zi_ref[rs, :]
        for h in range(HG_H):
            sl = slice(h * HG_DK, (h + 1) * HG_DK)
            s_old = s0_ref[q, h]
            qt16 = _pad_rows(qt[:, sl], 16).astype(BF16)
            o = _dot(qt16, s_old.astype(BF16))[:s_len]
            o = o + _intra_chunk(a[:, sl], hq[:, sl], hk[:, sl], hv[:, sl])
            o_ref[rs, sl] = o
            m = jnp.concatenate([kt[:, sl], jnp.where(row0, jnp.broadcast_to(ea[:, sl], (s_len, HG_DK)), 0.0),
                                 jnp.zeros((pad, HG_DK), F32)], axis=0)
            mt = m.T
            e_col = mt[:, s_len:s_len + 1]
            vpad = _pad_rows(hv[:, sl], HG_DK).astype(BF16)
            sout_ref[q, h] = e_col * s_old + _dot(mt.astype(BF16), vpad)


def _hgrn_sample(z, logits, state, ob, n_prompt, dec_batch, s_len):
    n_seq = 4
    rows = n_seq * s_len
    base = n_prompt // rows
    col = lambda c: (lambda b: (base + b, c // HG_W))
    st_spec = pl.BlockSpec((n_seq, HG_H, HG_DK, HG_DV), lambda b: (b, 0, 0, 0))
    return pl.pallas_call(
        functools.partial(_hgrn_sample_kernel, s_len=s_len),
        grid=(dec_batch // n_seq,),
        in_specs=[
            pl.BlockSpec((rows, HG_W), col(C2_HQ)),
            pl.BlockSpec((rows, HG_W), col(C2_HF)),
            pl.BlockSpec((rows, HG_W), col(C2_HI)),
            pl.BlockSpec((2, HG_W), lambda b: (0, 0)),
            st_spec,
            pl.BlockSpec(memory_space=pl.ANY),
        ],
        out_specs=(pl.BlockSpec((rows, HG_W), lambda b: (base + b, 0)), st_spec),
        out_shape=(
            jax.ShapeDtypeStruct(ob.shape, F32),
            jax.ShapeDtypeStruct(state.shape, F32),
        ),
        input_output_aliases={5: 0},
        compiler_params=_params("parallel"),
        name="hgrn_sample",
    )(z, z, z, logits, state, ob)


def _sample_nsa_kernel(pt_ref, *refs, npg, s_len, past):
    del pt_ref
    cmp_pages = refs[:npg]
    sel_pages = refs[npg:2 * npg]
    (win_ref, q_ref, rsel_ref, rwin_ref, zbr_ref, w2k_ref, w2v_ref, pek_ref, pev_ref, bdk_ref, bdv_ref, perm_ref,
     ov_ref, e_ref, oa_hbm_ref, oa_ref, swin_ref, xs_ref, kcv_ref) = refs[2 * npg:]
    del oa_hbm_ref
    step = pl.program_id(0)

    @pl.when(step == 0)
    def _():
        kcv_ref[...] = jnp.zeros(kcv_ref.shape, BF16)

    kc_all = kcv_ref[(step + 1) % 2, 0]
    vc_all = kcv_ref[(step + 1) % 2, 1]
    page = cmp_pages[0].shape[1]
    rpp = page // D_CMP
    n_ch = npg * rpp
    nc = n_ch - 1
    wb = win_ref.shape[1]
    ns = -(-(past + s_len) // L_SEL)
    nsp = -(-ns // 8) * 8
    n_top = min(N_TOP, ns)
    rows = HEADS * s_len

    w2 = (w2k_ref, w2v_ref)
    pe = (pek_ref, pev_ref)
    bd = (bdk_ref, bdv_ref)
    comp = []

    def stage1_scatter(lo, hi):
        for i in range(lo, hi):
            _scatter_chunk_rows(xs_ref, perm_ref[...], cmp_pages[i][...].astype(BF16), i * rpp)

    def stage1_compress(p):
        kv = p // 2
        comp.append(_compress_pair(_chunk_flat(xs_ref, p), w2[kv][...], pe[kv][...], bd[kv][...], n_ch))

    q = q_ref[...]
    lane8 = lax.broadcasted_iota(jnp.int32, (s_len, LANES), 1)
    zero8 = jnp.zeros((s_len, LANES), F32)
    blocks = []
    for g in range(KVH):
        for r in range(GRP):
            h = GRP * g + r
            src = q[:, (h // 2) * LANES:(h // 2 + 1) * LANES]
            if (r % 2) != (g % 2):
                src = pltpu.roll(src, HD, 1)
            keep = (lane8 < HD) if g % 2 == 0 else (lane8 >= HD)
            piece = jnp.where(keep, src, 0.0)
            blocks.append(jnp.concatenate([piece, zero8] if g // 2 == 0 else [zero8, piece], axis=1))
    qbd = (jnp.concatenate(blocks, axis=0) * (HD ** -0.5)).astype(BF16)

    def rowpos(shape):
        return past + lax.broadcasted_iota(jnp.int32, shape, 0) % s_len

    stage1_scatter(0, npg // 2)

    s = _nt(qbd, kc_all)
    cidx = lax.broadcasted_iota(jnp.int32, s.shape, 1)
    cvalid = (cidx * D_CMP + (L_CMP - 1) <= rowpos(s.shape)) & (cidx < nc)
    s = jnp.where(cvalid, s, NEG)
    m = jnp.max(s, axis=-1, keepdims=True)
    e = jnp.where(cvalid, jnp.exp(s - m), 0.0)
    p = e / jnp.maximum(jnp.sum(e, axis=-1, keepdims=True), 1.0)
    ocmp = _dot(p.astype(BF16), vc_all)

    stage1_scatter(npg // 2, npg)

    gs = GRP * s_len
    psum = []
    for g in range(KVH):
        acc = p[g * gs:g * gs + s_len]
        for r in range(1, GRP):
            acc = acc + p[g * gs + r * s_len:g * gs + (r + 1) * s_len]
        psum.append(acc)
    imp = _dot_exact_lhs(jnp.concatenate(psum, axis=0), ov_ref[...])
    qpos = rowpos(imp.shape)
    blk = lax.broadcasted_iota(jnp.int32, imp.shape, 1)
    visible = (blk * L_SEL <= qpos) & (blk < ns)
    cur = qpos // L_SEL
    forced = (blk == 0) | (blk == cur) | (blk == cur - 1)
    score = jnp.where(visible, imp + jnp.where(forced, BIG, 0.0), -BIG)
    stage1_compress(0)
    sel_t = _rank_select(_pad_rows(score, LANES).T, ns, nsp, n_top)
    sel = _pad_rows(sel_t, LANES).T[:KVH * s_len]
    sel = jnp.where(visible, sel, 0.0)
    sel_rows = jnp.concatenate([sel[g * s_len:(g + 1) * s_len] for g in range(KVH) for _ in range(GRP)], axis=0)

    zpad = jnp.zeros((page - s_len, KW), F32)
    knew = jnp.concatenate([rsel_ref[:, :KW], zpad], axis=0).astype(BF16)
    vnew = jnp.concatenate([rsel_ref[:, KW:], zpad], axis=0).astype(BF16)
    def page_pair(i, lo):
        return jnp.concatenate([sel_pages[2 * i][lo:lo + KW, :], sel_pages[2 * i + 1][lo:lo + KW, :]],
                               axis=1).astype(BF16)

    s = jnp.concatenate([_dot(qbd, page_pair(i, 0)) for i in range(npg // 2)] + [_nt(qbd, knew)], axis=1)
    mf = _dot(sel_rows.astype(BF16), e_ref[...])
    kpos = lax.broadcasted_iota(jnp.int32, s.shape, 1)
    mask = (mf > 0.5) & (kpos <= rowpos(s.shape))
    stage1_compress(1)
    s = jnp.where(mask, s, NEG)
    m = jnp.max(s, axis=-1, keepdims=True)
    e = jnp.where(mask, jnp.exp(s - m), 0.0)
    l = jnp.sum(e, axis=-1, keepdims=True)
    eb = e.astype(BF16)
    acc = _dot(eb[:, npg * page:], vnew)
    for i in range(npg // 2):
        acc = acc + _nt(eb[:, 2 * i * page:(2 * i + 2) * page], page_pair(i, KW))
    osel = acc / l

    kwn = jnp.concatenate([rwin_ref[:, :KW], zpad], axis=0).astype(BF16)
    vwn = jnp.concatenate([rwin_ref[:, KW:], zpad], axis=0).astype(BF16)
    s = jnp.concatenate([_dot(qbd, win_ref[:KW, :].astype(BF16)), _nt(qbd, kwn)], axis=1)
    kposw = (past - wb) + lax.broadcasted_iota(jnp.int32, s.shape, 1)
    qposw = rowpos(s.shape)
    wmask = (kposw <= qposw) & (kposw > qposw - WINDOW) & (kposw >= 0)
    s = jnp.where(wmask, s, NEG)
    m = jnp.max(s, axis=-1, keepdims=True)
    e = jnp.where(wmask, jnp.exp(s - m), 0.0)
    den = jnp.maximum(jnp.sum(e, axis=-1, keepdims=True), 1.0)
    eb = e.astype(BF16)
    owin = (_nt(eb[:, :wb], win_ref[KW:, :].astype(BF16)) + _dot(eb[:, wb:], vwn)) / den

    stage1_compress(2)

    gates = jax.nn.sigmoid(zbr_ref[...])
    for g in range(KVH):
        gl = slice((g // 2) * LANES, (g // 2 + 1) * LANES)
        for k2 in range(2):
            halves = []
            for r in (2 * k2, 2 * k2 + 1):
                h = GRP * g + r
                rs = slice(h * s_len, (h + 1) * s_len)
                val = (gates[:, h:h + 1] * ocmp[rs, gl]
                       + gates[:, HEADS + h:HEADS + h + 1] * osel[rs, gl]
                       + gates[:, 2 * HEADS + h:2 * HEADS + h + 1] * owin[rs, gl])
                if (r % 2) != (g % 2):
                    val = pltpu.roll(val, HD, 1)
                halves.append(val)
            c = 2 * g + k2
            oa_ref[:, c * LANES:(c + 1) * LANES] = jnp.where(lane8 < HD, halves[0], halves[1])

    new_t = [_pad_rows(rwin_ref[:, c * LANES:(c + 1) * LANES], LANES).T for c in range(KVW // LANES)]
    new_t = jnp.concatenate(new_t, axis=0)
    lane_w = lax.broadcasted_iota(jnp.int32, (KVW, LANES), 1)
    n_col = wb // LANES
    rolled = [pltpu.roll(win_ref[:, c * LANES:(c + 1) * LANES], LANES - s_len, 1) for c in range(n_col)]
    rolled.append(pltpu.roll(new_t, LANES - s_len, 1))
    for c in range(n_col):
        swin_ref[:, c * LANES:(c + 1) * LANES] = jnp.where(lane_w < LANES - s_len, rolled[c], rolled[c + 1])

    stage1_compress(3)
    kcv_ref[step % 2, 0] = jnp.concatenate(comp[0:2], axis=1).astype(BF16)
    kcv_ref[step % 2, 1] = jnp.concatenate(comp[2:4], axis=1).astype(BF16)


def _sample_nsa(page_table, cache_cmp, cache_sel, cache_win, q_tok, rows_sel, rows_win, z, cw, ov, e_mat, oa,
                n_prompt, dec_batch, s_len):
    npg = page_table.shape[1]
    page = cache_cmp.shape[2]
    past = npg * page
    wb = cache_win.shape[2]
    base = n_prompt // s_len
    last = dec_batch - 1
    steps = dec_batch + 1
    seq2 = lambda b: jnp.maximum(b - 1, 0)
    page_table = jnp.concatenate([page_table, page_table[last:], page_table[:1], page_table], axis=0)
    cmp_specs = [pl.BlockSpec((None, KVW, page), (lambda b, pt, i=i: (pt[b, i], 0, 0))) for i in range(npg)]
    sel_specs = [pl.BlockSpec((None, KVW, page), (lambda b, pt, i=i: (pt[steps + b, i], 0, 0))) for i in range(npg)]
    row = lambda b, pt: (base + seq2(b), 0)
    const = lambda b, pt: (0, 0)
    in_specs = (
        cmp_specs + sel_specs
        + [pl.BlockSpec((None, KVW, wb), lambda b, pt: (seq2(b), 0, 0)),
           pl.BlockSpec((s_len, NSA_W), row),
           pl.BlockSpec((s_len, KVW), row),
           pl.BlockSpec((s_len, KVW), row),
           pl.BlockSpec((s_len, LANES), lambda b, pt: (base + seq2(b), C1_BR // LANES))]
        + [pl.BlockSpec(w.shape, const) for w in cw]
        + [pl.BlockSpec(ov.shape, const), pl.BlockSpec(e_mat.shape, const), pl.BlockSpec(memory_space=pl.ANY)]
    )
    n_ch = past // D_CMP
    grid_spec = pltpu.PrefetchScalarGridSpec(
        num_scalar_prefetch=1,
        grid=(dec_batch + 1,),
        in_specs=in_specs,
        out_specs=(
            pl.BlockSpec((s_len, NSA_W), row),
            pl.BlockSpec((None, KVW, wb), lambda b, pt: (seq2(b), 0, 0)),
        ),
        scratch_shapes=[
            pltpu.VMEM((D_CMP, n_ch, KVW), F32),
            pltpu.VMEM((2, 2, n_ch, KW), BF16),
        ],
    )
    args = (page_table, *([cache_cmp] * npg), *([cache_sel] * npg), cache_win, q_tok, rows_sel, rows_win, z, *cw,
            ov, e_mat, oa)
    return pl.pallas_call(
        functools.partial(_sample_nsa_kernel, npg=npg, s_len=s_len, past=past),
        grid_spec=grid_spec,
        out_shape=(
            jax.ShapeDtypeStruct(oa.shape, F32),
            jax.ShapeDtypeStruct((dec_batch, KVW, wb), F32),
        ),
        input_output_aliases={len(args) - 1: 0},
        compiler_params=_params("arbitrary"),
        name="sample_nsa",
    )(*args)


def _mix_kernel(xp_ref, xs_ref, oa_ref, ob_ref, zhg_ref, ga_ref, gb_ref, gn_ref, wa_ref, wb_ref, wo_ref, o_ref, *, npt):
    ya = _dot(oa_ref[...].astype(BF16), wa_ref[...])
    gn = gn_ref[...]
    parts = []
    for h in range(HG_H):
        sl = slice(h * HG_DV, (h + 1) * HG_DV)
        ob = ob_ref[:, sl]
        ms = jnp.mean(ob * ob, axis=-1, keepdims=True)
        zg = zhg_ref[:, sl]
        y = ((ob * lax.rsqrt(ms + EPS)) * gn) * (zg * jax.nn.sigmoid(zg))
        parts.append(y.astype(BF16))
    yb = _dot(jnp.concatenate(parts, axis=1), wb_ref[...])
    mix = jax.nn.sigmoid(ga_ref[...]) * ya + jax.nn.sigmoid(gb_ref[...]) * yb
    y = _dot(mix.astype(BF16), wo_ref[...])

    @pl.when(pl.program_id(0) < npt)
    def _():
        o_ref[...] = xp_ref[...] + y

    @pl.when(pl.program_id(0) >= npt)
    def _():
        o_ref[...] = xs_ref[...] + y


def _mix(xp, xs, oa, ob, z, gn, wa, wb, wo):
    (n_p, d), n_s = xp.shape, xs.shape[0]
    n = n_p + n_s
    tm = 256
    npt, spec_p, spec_s = _two_group_specs(n_p, n_s, tm, d)
    row = lambda i: (i, 0)
    const = lambda i: (0, 0)
    resident = lambda w: pl.BlockSpec(w.shape, const, pipeline_mode=pl.Buffered(1))
    return pl.pallas_call(
        functools.partial(_mix_kernel, npt=npt),
        grid=(n // tm,),
        in_specs=[
            spec_p, spec_s,
            pl.BlockSpec((tm, NSA_W), row),
            pl.BlockSpec((tm, HG_W), row),
            pl.BlockSpec((tm, HG_W), lambda i: (i, C2_HG // HG_W)),
            pl.BlockSpec((tm, d), lambda i: (i, C2_GA // D_MODEL)),
            pl.BlockSpec((tm, d), lambda i: (i, C2_GB // D_MODEL)),
            pl.BlockSpec((1, HG_DV), const),
            resident(wa), resident(wb), resident(wo),
        ],
        out_specs=pl.BlockSpec((tm, d), row),
        out_shape=jax.ShapeDtypeStruct((n, d), F32),
        compiler_params=_params("parallel"),
        name="mix_out",
    )(xp, xs, oa, ob, z, z, z, gn, wa, wb, wo)


def _ffn_kernel(x_ref, g_ref, w1_ref, w2_ref, op_ref, os_ref, xn_ref, acc_ref, *, npt):
    i = pl.program_id(0)
    j = pl.program_id(1)
    last = pl.num_programs(1) - 1

    @pl.when(j == 0)
    def _():
        xn_ref[...] = _rmsnorm_bf16(x_ref[...], g_ref[...])
        acc_ref[...] = jnp.zeros(acc_ref.shape, F32)

    h = _dot(xn_ref[...], w1_ref[...])
    h = jnp.square(jnp.maximum(h, 0.0)).astype(BF16)
    acc_ref[...] += _dot(h, w2_ref[...])

    @pl.when((j == last) & (i < npt))
    def _():
        op_ref[...] = x_ref[...] + acc_ref[...]

    @pl.when((j == last) & (i >= npt))
    def _():
        os_ref[...] = x_ref[...] + acc_ref[...]


def _ffn(x, g, w1, w2, n_p):
    n, d = x.shape
    n_s = n - n_p
    dff = w1.shape[1]
    tm = _pick_tile(np.gcd(n_p, n_s), (512, 256))
    tf = 1024
    npt, spec_p, spec_s = _two_group_specs(n_p, n_s, tm, d)
    return pl.pallas_call(
        functools.partial(_ffn_kernel, npt=npt),
        grid=(n // tm, dff // tf),
        in_specs=[
            pl.BlockSpec((tm, d), lambda i, j: (i, 0)),
            pl.BlockSpec((1, d), lambda i, j: (0, 0)),
            pl.BlockSpec((d, tf), lambda i, j: (0, j)),
            pl.BlockSpec((tf, d), lambda i, j: (j, 0)),
        ],
        out_specs=(spec_p, spec_s),
        out_shape=(jax.ShapeDtypeStruct((n_p, d), F32), jax.ShapeDtypeStruct((n_s, d), F32)),
        scratch_shapes=[pltpu.VMEM((tm, d), BF16), pltpu.VMEM((tm, d), F32)],
        compiler_params=_params("arbitrary", "arbitrary"),
        name="ffn",
    )(x, g, w1, w2)


def _compress_weights(cmp_pe, cmp_w1, cmp_w2):
    m = L_CMP // D_CMP
    eye2 = jnp.eye(2, dtype=F32)
    out = []
    w2s, pes, bds = [], [], []
    for kv in range(2):
        w1p = cmp_w1[kv].reshape(m, D_CMP, HD, HD).transpose(1, 2, 0, 3)
        w = w1p[:, None, :, :, None, :] * eye2[None, :, None, None, :, None]
        w2s.append(w.reshape(D_CMP * LANES, m * LANES).astype(BF16))
        pe = cmp_pe[kv].reshape(m, D_CMP, 1, HD)
        pe = jnp.broadcast_to(pe, (m, D_CMP, 2, HD)).reshape(m, D_CMP * LANES)
        pes.append(jnp.concatenate([pe, jnp.zeros((8 - m, D_CMP * LANES), F32)], axis=0))
        bds.append(jnp.kron(eye2, cmp_w2[kv]).astype(BF16))
    tile_rows = LANES
    rpp = tile_rows // D_CMP
    src = np.arange(tile_rows)
    perm = np.zeros((tile_rows, tile_rows), np.float32)
    perm[(src % D_CMP) * rpp + src // D_CMP, src] = 1.0
    return (w2s[0], w2s[1], pes[0], pes[1], bds[0], bds[1], jnp.asarray(perm, BF16))


def _overlap_matrix(nc_pad, ns, transposed=False):
    cs = np.arange(nc_pad) * D_CMP
    ss = np.arange(LANES) * L_SEL
    ov = np.clip(np.minimum(cs[:, None] + L_CMP, ss[None, :] + L_SEL) - np.maximum(cs[:, None], ss[None, :]), 0, None)
    ov = ((ov / D_CMP) * (np.arange(LANES)[None, :] < ns)).astype(np.float32)
    return jnp.asarray(np.ascontiguousarray(ov.T) if transposed else ov, BF16)


def _expand_matrix(n_keys):
    e = (np.arange(n_keys)[None, :] // L_SEL) == np.arange(LANES)[:, None]
    return jnp.asarray(e.astype(np.float32), BF16)


def _rope_tables(seq, past, s_len, tm):
    half = HD // 2
    inv = THETA ** (-jnp.arange(half, dtype=F32) * 2.0 / HD)
    pos = jnp.concatenate([jnp.arange(seq), past + jnp.arange(tm) % s_len]).astype(F32)
    ang = pos[:, None] * inv[None, :]
    cos = jnp.cos(ang)
    sin = jnp.sin(ang)
    cos_t = jnp.concatenate([cos, cos, cos, cos], axis=1)
    sin_t = jnp.concatenate([-sin, sin, -sin, sin], axis=1)
    return cos_t, sin_t


def kernel(x_prompt, x_sample, cache_cmp_kv, cache_sel_kv, cache_win_kv, state_hgrn, page_table, norm1_g, w_in,
           q_norm_g, k_norm_g, cmp_pe, cmp_w1, cmp_w2, w_proj_a, hgrn_lb_logits, hgrn_norm_g, w_proj_b, w_out,
           norm2_g, w_ff1, w_ff2):
    assert w_in.shape[0] == 1, "single-layer step"
    batch, seq, d = x_prompt.shape
    dec_batch, s_len, _ = x_sample.shape
    n_prompt = batch * seq
    n_sample = dec_batch * s_len
    npg = page_table.shape[1]
    page = cache_cmp_kv.shape[2]
    past = npg * page
    wb = cache_win_kv.shape[2]
    assert seq % Q_BLK == 0 and seq >= WINDOW + Q_BLK and n_prompt % 256 == 0 and n_sample % 256 == 0
    assert s_len % 8 == 0 and page % D_CMP == 0 and 2 * s_len <= HG_DK and dec_batch % 4 == 0 and npg % 2 == 0

    xp = x_prompt.reshape(n_prompt, d)
    xs = x_sample.reshape(n_sample, d)

    cw = _compress_weights(cmp_pe[0], cmp_w1[0], cmp_w2[0])
    cos_t, sin_t = _rope_tables(seq, past, s_len, 256)
    gq = jnp.tile(q_norm_g[0][None, :], (1, 2))
    gk = jnp.tile(k_norm_g[0], (1, 2))
    logits = hgrn_lb_logits.astype(F32)

    def token_minor(c):
        return jnp.transpose(c, (0, 2, 3, 4, 1)).reshape(c.shape[0], KVW, c.shape[1])

    def token_major(c_t):
        return jnp.transpose(c_t.reshape(c_t.shape[0], 2, KVH, HD, c_t.shape[2]), (0, 4, 1, 2, 3))

    w_t = jnp.transpose(w_in[0])
    z1 = _in_proj(xp, xs, norm1_g, w_t, 0, N_Z1, 896)
    z2 = _in_proj(xp, xs, norm1_g, w_t, sum(_SIZES[:3]), N_Z2, 1024)
    (q_tok, q_t, rows_cmp, rows_sel, rows_win, rt_cmp, rt_sel, rt_win,
     ks, vs_t, kw, vw_t) = _qkv_post(z1, cos_t, sin_t, gq, gk, n_prompt, seq)

    bdv_t = lax.optimization_barrier(cw[5].T)
    kc, vc_t = _compress_prompt(rt_cmp, cw[:5] + (bdv_t,) + cw[6:], batch, seq)
    ov_p = _overlap_matrix(seq // D_CMP, seq // L_SEL, transposed=True)
    ocmp, sel_bias = _cmp_select(q_t, kc, vc_t, ov_p, batch, seq)
    oa = _sel_win(q_t, ks, vs_t, kw, vw_t, sel_bias, z1, ocmp, batch, seq)
    ob, p_state = _hgrn_prompt(z2, logits, batch, seq)

    ns_s = -(-(past + s_len) // L_SEL)
    ov_s = _overlap_matrix(past // D_CMP, ns_s)
    oa, s_win_t = _sample_nsa(
        page_table, token_minor(cache_cmp_kv[0]), token_minor(cache_sel_kv[0]), token_minor(cache_win_kv[0]),
        q_tok, rows_sel, rows_win, z1, cw, ov_s, _expand_matrix(past + page), oa, n_prompt, dec_batch, s_len)
    ob, s_state = _hgrn_sample(z2, logits, state_hgrn[0], ob, n_prompt, dec_batch, s_len)

    x2 = _mix(xp, xs, oa, ob, z2, hgrn_norm_g, w_proj_a[0].astype(BF16), w_proj_b[0].astype(BF16),
              w_out[0].astype(BF16))
    y_p, y_s = _ffn(x2, norm2_g, w_ff1[0].astype(BF16), w_ff2[0].astype(BF16), n_prompt)

    kv_shape = (2, KVH, HD)
    wp = min(WINDOW, seq)
    return (
        y_p.reshape(batch, seq, d),
        y_s.reshape(dec_batch, s_len, d),
        token_major(rt_cmp[:batch])[None],
        token_major(rt_sel[:batch])[None],
        token_major(rt_win[:batch, :, seq - wp:])[None],
        p_state[None],
        rows_cmp[n_prompt:].reshape((1, dec_batch, s_len) + kv_shape),
        rows_sel[n_prompt:].reshape((1, dec_batch, s_len) + kv_shape),
        token_major(s_win_t)[None],
        s_state[None],
    )
```

```python
import functools

import numpy as np
import jax
import jax.numpy as jnp
from jax import lax
from jax.experimental import pallas as pl
from jax.experimental.pallas import tpu as pltpu

F32 = jnp.float32
BF16 = jnp.bfloat16

D_MODEL = 2048
HEADS = 16
KVH = 4
GRP = HEADS // KVH
HD = 64
NSA_W = HEADS * HD
L_CMP = 32
D_CMP = 16
L_SEL = 64
N_TOP = 16
WINDOW = 512
Q_BLK = 128
THETA = 10000.0
HG_H = 8
HG_DK = 128
HG_DV = 128
HG_W = HG_H * HG_DK
HG_SUB = 16
SEL_KT = 512
EPS = 1e-6
NEG = -1e30
BIG = 1e9

LANES = 128
KVW = 2 * KVH * HD
KW = KVH * HD

_SIZES = (NSA_W, 6 * KVH * HD, 3 * HEADS, HG_W, HG_W, HG_W, HG_W, D_MODEL, D_MODEL)
C1_Q, C1_KV, C1_BR, N_Z1 = 0, 1024, 2560, 2688
C2_HQ, C2_HF, C2_HI, C2_HG, C2_GA, C2_GB, N_Z2 = 0, 1024, 2048, 3072, 4096, 6144, 8192

VMEM_LIMIT = 56 * 1024 * 1024


def _params(*sem):
    return pltpu.CompilerParams(dimension_semantics=sem, vmem_limit_bytes=VMEM_LIMIT)


def _nt(a, b):
    return lax.dot_general(a, b, (((1,), (1,)), ((), ())), preferred_element_type=F32)


def _dot(a, b):
    return jnp.dot(a, b, preferred_element_type=F32)


def _split3(x):
    hi = x.astype(BF16)
    r1 = x - hi.astype(F32)
    mid = r1.astype(BF16)
    lo = (r1 - mid.astype(F32)).astype(BF16)
    return hi, mid, lo


def _dot_exact_rhs(m, x):
    hi, mid, lo = _split3(x)
    return _dot(m, hi) + _dot(m, mid) + _dot(m, lo)


def _dot_exact_lhs(x, m):
    hi, mid, lo = _split3(x)
    return _dot(hi, m) + _dot(mid, m) + _dot(lo, m)


def _pick_tile(n, cands):
    for c in cands:
        if n % c == 0:
            return c
    raise ValueError(f"no tile in {cands} divides {n}")


def _rmsnorm_bf16(x, g):
    ms = jnp.mean(x * x, axis=-1, keepdims=True)
    return ((x * lax.rsqrt(ms + EPS)) * g).astype(BF16)


def _two_group_specs(n_p, n_s, tm, width, sample_kwargs=None):
    npt = n_p // tm
    assert n_p % tm == 0 and n_s % tm == 0
    spec_p = pl.BlockSpec((tm, width), lambda i, *_: (jnp.minimum(i, npt - 1), 0))
    spec_s = pl.BlockSpec((tm, width), lambda i, *_: (jnp.maximum(i - npt, 0), 0), **(sample_kwargs or {}))
    return npt, spec_p, spec_s


def _xnorm_kernel(xp_ref, xs_ref, g_ref, o_ref, *, npt):
    @pl.when(pl.program_id(0) < npt)
    def _():
        o_ref[...] = _rmsnorm_bf16(xp_ref[...], g_ref[...])

    @pl.when(pl.program_id(0) >= npt)
    def _():
        o_ref[...] = _rmsnorm_bf16(xs_ref[...], g_ref[...])


def _xnorm(xp, xs, g):
    (n_p, d), n_s = xp.shape, xs.shape[0]
    tm = _pick_tile(np.gcd(n_p, n_s), (512, 256))
    npt, spec_p, spec_s = _two_group_specs(n_p, n_s, tm, d)
    return pl.pallas_call(
        functools.partial(_xnorm_kernel, npt=npt),
        grid=((n_p + n_s) // tm,),
        in_specs=[spec_p, spec_s, pl.BlockSpec((1, d), lambda i: (0, 0))],
        out_specs=pl.BlockSpec((tm, d), lambda i: (i, 0)),
        out_shape=jax.ShapeDtypeStruct((n_p + n_s, d), BF16),
        compiler_params=_params("parallel"),
        name="xnorm",
    )(xp, xs, g)


def _matmul_wt_kernel(x_ref, w_ref, o_ref, wb_ref):
    @pl.when(pl.program_id(1) == 0)
    def _():
        wb_ref[...] = w_ref[...].astype(BF16)

    o_ref[...] = _nt(x_ref[...], wb_ref[...])


def _in_proj(xn, w_t, row0, nout, tn):
    n, d = xn.shape
    assert nout % tn == 0 and row0 % 8 == 0 and row0 + nout <= w_t.shape[0]
    tm = _pick_tile(n, (1024, 512, 256))
    return pl.pallas_call(
        _matmul_wt_kernel,
        grid=(nout // tn, n // tm),
        in_specs=[
            pl.BlockSpec((tm, d), lambda j, i: (i, 0)),
            pl.BlockSpec((pl.Element(tn), pl.Element(d)), lambda j, i: ((row0 // 8 + j * (tn // 8)) * 8, 0)),
        ],
        out_specs=pl.BlockSpec((tm, tn), lambda j, i: (i, j)),
        out_shape=jax.ShapeDtypeStruct((n, nout), F32),
        scratch_shapes=[pltpu.VMEM((tn, d), BF16)],
        compiler_params=_params("arbitrary", "arbitrary"),
        name="in_proj",
    )(xn, w_t)


def _qkv_post_kernel(zq_ref, zc_ref, zs_ref, zw_ref, cos_ref, sin_ref, gq_ref, gk_ref,
                     qtok_ref, qt_ref, rc_ref, rs_ref, rw_ref, tc_ref, ts_ref, tw_ref,
                     ks_ref, vs_ref, kw_ref, vw_ref):
    cos = cos_ref[...]
    sin = sin_ref[...]
    lane = lax.broadcasted_iota(jnp.int32, cos.shape, 1)
    first = (lane % HD) < (HD // 2)
    hi_ = lax.broadcasted_iota(jnp.int32, (LANES, LANES), 0) // HD
    hj_ = lax.broadcasted_iota(jnp.int32, (LANES, LANES), 1) // HD
    same_head = jnp.where(hi_ == hj_, 1.0, 0.0).astype(BF16)

    def norm_rope(xc, g_row):
        ms = _dot_exact_lhs(xc * xc, same_head) * (1.0 / HD)
        y = (xc * lax.rsqrt(ms + EPS)) * g_row
        partner = jnp.where(first, pltpu.roll(y, LANES - HD // 2, 1), pltpu.roll(y, HD // 2, 1))
        return y * cos + partner * sin

    gq = gq_ref[...]
    for c in range(NSA_W // LANES):
        sl = slice(c * LANES, (c + 1) * LANES)
        qc = norm_rope(zq_ref[:, sl], gq)
        qtok_ref[:, sl] = qc
        qtb = qc.T.astype(BF16)
        qt_ref[2 * c] = qtb[:HD]
        qt_ref[2 * c + 1] = qtb[HD:]

    tm = zq_ref.shape[0]
    zin = (zc_ref, zs_ref, zw_ref)
    rows = (rc_ref, rs_ref, rw_ref)
    rows_t = (tc_ref, ts_ref, tw_ref)
    kg = (None, ks_ref, kw_ref)
    vg = (None, vs_ref, vw_ref)
    for br in range(3):
        gk = gk_ref[br:br + 1, :]
        cols = []
        for c in range(KW // LANES):
            cols.append(norm_rope(zin[br][:, c * LANES:(c + 1) * LANES], gk))
        for c in range(KW // LANES):
            cols.append(zin[br][:, KW + c * LANES:KW + (c + 1) * LANES])
        cols_t = [col.T for col in cols]
        for c, col in enumerate(cols):
            rows[br][:, c * LANES:(c + 1) * LANES] = col
            rows_t[br][c * LANES:(c + 1) * LANES, :] = cols_t[c]
        if kg[br] is not None:
            wt = vg[br].shape[-1]
            for c in range(KW // LANES):
                kb = cols[c].astype(BF16)
                kg[br][2 * c] = kb[:, :HD]
                kg[br][2 * c + 1] = kb[:, HD:]
                vtb = cols_t[KW // LANES + c].astype(BF16)
                for t in range(tm // wt):
                    vg[br][2 * c, t] = vtb[:HD, t * wt:(t + 1) * wt]
                    vg[br][2 * c + 1, t] = vtb[HD:, t * wt:(t + 1) * wt]


def _qkv_post(z1, cos_tab, sin_tab, gq, gk, n_prompt, seq):
    n = z1.shape[0]
    tm = 256
    n_ptiles = n_prompt // tm
    tiles_per_seq = seq // tm
    n_slabs = -(-n // seq)

    def tab_map(i):
        return (jnp.where(i < n_ptiles, i % tiles_per_seq, tiles_per_seq), 0)

    row = lambda i: (i, 0)
    head = lambda i: (0, i, 0)
    slab = lambda i: (i // tiles_per_seq, 0, i % tiles_per_seq)
    rows_sd = jax.ShapeDtypeStruct((n, KVW), F32)
    rows_t_sd = jax.ShapeDtypeStruct((n_slabs, KVW, seq), F32)
    grp_sd = jax.ShapeDtypeStruct((KVH, n, HD), BF16)
    assert SEL_KT % tm == 0 and tm % Q_BLK == 0
    out_shape = (
        jax.ShapeDtypeStruct((n, NSA_W), F32),
        jax.ShapeDtypeStruct((HEADS, HD, n), BF16),
        rows_sd, rows_sd, rows_sd,
        rows_t_sd, rows_t_sd, rows_t_sd,
        grp_sd,
        jax.ShapeDtypeStruct((KVH, -(-n // SEL_KT), HD, SEL_KT), BF16),
        grp_sd,
        jax.ShapeDtypeStruct((KVH, n // Q_BLK, HD, Q_BLK), BF16),
    )
    rows_spec = pl.BlockSpec((tm, KVW), row)
    rows_t_spec = pl.BlockSpec((None, KVW, tm), slab)
    grp_spec = pl.BlockSpec((KVH, tm, HD), head)
    per = SEL_KT // tm
    out_specs = (
        pl.BlockSpec((tm, NSA_W), row),
        pl.BlockSpec((HEADS, HD, tm), lambda i: (0, 0, i)),
        rows_spec, rows_spec, rows_spec,
        rows_t_spec, rows_t_spec, rows_t_spec,
        grp_spec,
        pl.BlockSpec((KVH, 1, HD, tm), lambda i: (0, i // per, 0, i % per)),
        grp_spec,
        pl.BlockSpec((KVH, tm // Q_BLK, HD, Q_BLK), lambda i: (0, i, 0, 0)),
    )
    kv_in = lambda br: pl.BlockSpec((tm, KVW), lambda i: (i, C1_KV // KVW + br))
    return pl.pallas_call(
        _qkv_post_kernel,
        grid=(n // tm,),
        in_specs=[
            pl.BlockSpec((tm, NSA_W), lambda i: (i, C1_Q // NSA_W)),
            kv_in(0), kv_in(1), kv_in(2),
            pl.BlockSpec((tm, LANES), tab_map),
            pl.BlockSpec((tm, LANES), tab_map),
            pl.BlockSpec((1, LANES), lambda i: (0, 0)),
            pl.BlockSpec((3, LANES), lambda i: (0, 0)),
        ],
        out_specs=out_specs,
        out_shape=out_shape,
        compiler_params=_params("parallel"),
        name="qkv_post",
    )(z1, z1, z1, z1, cos_tab, sin_tab, gq, gk)


def _compress_pair(flat, w2, pe8, w2bd, n_ch, transposed=False):
    ab = _dot(flat, w2)
    hp = _dot(pe8.astype(BF16), w2)
    hpe = hp[0:1, :LANES] + hp[1:2, LANES:]
    nxt = pltpu.roll(ab[:, LANES:], n_ch - 1, 0)
    h = ab[:, :LANES] + nxt + hpe
    act = (h * jax.nn.sigmoid(h)).astype(BF16)
    if transposed:
        return _nt(w2bd, act)
    return _dot(act, w2bd)


def _scatter_chunk_rows(xs_ref, perm, tile_t, c0):
    rpp = tile_t.shape[1] // D_CMP
    xp = _nt(perm, tile_t)
    for t in range(D_CMP):
        xs_ref[t, c0:c0 + rpp, :] = xp[t * rpp:(t + 1) * rpp, :]


def _chunk_flat(xs_ref, p):
    return jnp.concatenate([xs_ref[t, :, p * LANES:(p + 1) * LANES] for t in range(D_CMP)], axis=1).astype(BF16)


def _compress_prompt_kernel(rows_ref, w2k_ref, w2v_ref, pek_ref, pev_ref, bdk_ref, bdv_ref, perm_ref, kc_ref, vc_ref,
                            xs_ref, *, n_ch):
    w2 = (w2k_ref, w2v_ref)
    pe = (pek_ref, pev_ref)
    bd = (bdk_ref, bdv_ref)
    tile_rows = perm_ref.shape[0]
    rpp = tile_rows // D_CMP
    for i in range(n_ch // rpp):
        _scatter_chunk_rows(xs_ref, perm_ref[...], rows_ref[:, i * tile_rows:(i + 1) * tile_rows].astype(BF16), i * rpp)
    for p in range(KVW // LANES):
        kv = p // 2
        flat = _chunk_flat(xs_ref, p)
        res = _compress_pair(flat, w2[kv][...], pe[kv][...], bd[kv][...], n_ch, transposed=(kv == 1)).astype(BF16)
        g0 = 2 * (p % 2)
        if kv == 0:
            kc_ref[0, g0] = res[:, :HD]
            kc_ref[0, g0 + 1] = res[:, HD:]
        else:
            vc_ref[0, g0] = res[:HD]
            vc_ref[0, g0 + 1] = res[HD:]


def _compress_prompt(rows_cmp, cw, batch, seq):
    n_ch = seq // D_CMP
    const = lambda b: (0, 0)
    return pl.pallas_call(
        functools.partial(_compress_prompt_kernel, n_ch=n_ch),
        grid=(batch,),
        in_specs=[pl.BlockSpec((None, KVW, seq), lambda b: (b, 0, 0))] + [pl.BlockSpec(w.shape, const) for w in cw],
        out_specs=(pl.BlockSpec((1, KVH, n_ch, HD), lambda b: (b, 0, 0, 0)),
                   pl.BlockSpec((1, KVH, HD, n_ch), lambda b: (b, 0, 0, 0))),
        out_shape=(jax.ShapeDtypeStruct((batch, KVH, n_ch, HD), BF16),
                   jax.ShapeDtypeStruct((batch, KVH, HD, n_ch), BF16)),
        scratch_shapes=[pltpu.VMEM((D_CMP, n_ch, KVW), F32)],
        compiler_params=_params("parallel"),
        name="compress_prompt",
    )(rows_cmp, *cw)


def _rank_select(score_t, n_blocks, n_rows, n_top):
    st = score_t[:n_rows]
    jio = lax.broadcasted_iota(jnp.int32, st.shape, 0)
    cnt = jnp.zeros(st.shape, F32)
    for i in range(n_blocks):
        cnt = cnt + _beats(score_t[i:i + 1, :], st, jio, i)
    return jnp.where(cnt < n_top, 1.0, 0.0)


def _beats(row, st, jio, i):
    return jnp.where(row > st, 1.0, jnp.where(row == st, jnp.where(jio > i, 1.0, 0.0), 0.0))


def _pad_rows(x, n):
    if x.shape[0] == n:
        return x
    return jnp.concatenate([x, jnp.zeros((n - x.shape[0], x.shape[1]), x.dtype)], axis=0)


def _cmp_select_kernel(qt_ref, kc_ref, vct_ref, ovt_ref, ocmp_ref, sel_ref, st_ref, *, nc, ns, n_top):
    s0 = pl.program_id(1) * Q_BLK
    ncp = kc_ref.shape[2]
    nsp = st_ref.shape[1]
    c_io = lax.broadcasted_iota(jnp.int32, (ncp, Q_BLK), 0)
    qpos_c = s0 + lax.broadcasted_iota(jnp.int32, (ncp, Q_BLK), 1)
    cbias = jnp.where((c_io * D_CMP + (L_CMP - 1) <= qpos_c) & (c_io < nc), 0.0, NEG)
    qpos_1 = s0 + lax.broadcasted_iota(jnp.int32, (1, GRP * Q_BLK), 1) % Q_BLK
    col_ok = jnp.where((qpos_1 >= L_CMP - 1) & (nc > 0), 1.0, 0.0)
    jio = lax.broadcasted_iota(jnp.int32, (nsp, LANES), 0)
    qpos_n = s0 + lax.broadcasted_iota(jnp.int32, (nsp, LANES), 1)
    vis_t = (jio * L_SEL <= qpos_n) & (jio < ns)
    cur = qpos_n // L_SEL
    forced = (jio == 0) | (jio == cur) | (jio == cur - 1)
    ovt = ovt_ref[...]
    grp = range(KVH)
    qt = [jnp.concatenate([qt_ref[GRP * g + r] for r in range(GRP)], axis=1) * (HD ** -0.5) for g in grp]
    s = [_dot(kc_ref[0, g], qt[g]) for g in grp]
    s = [jnp.concatenate([s[g][:, r * Q_BLK:(r + 1) * Q_BLK] + cbias for r in range(GRP)], axis=1) for g in grp]
    e = [jnp.exp(s[g] - jnp.max(s[g], axis=0, keepdims=True)) for g in grp]
    p = [e[g] * (col_ok / jnp.maximum(jnp.sum(e[g], axis=0, keepdims=True), 1.0)) for g in grp]
    o = [_dot(vct_ref[0, g], p[g].astype(BF16)) for g in grp]
    for g in grp:
        for r in range(GRP):
            ocmp_ref[GRP * g + r] = o[g][:, r * Q_BLK:(r + 1) * Q_BLK]
    psum = [p[g][:, 0:Q_BLK] + p[g][:, Q_BLK:2 * Q_BLK] + p[g][:, 2 * Q_BLK:3 * Q_BLK] + p[g][:, 3 * Q_BLK:]
            for g in grp]
    imp = [_dot_exact_rhs(ovt, psum[g]) for g in grp]
    for g in grp:
        st_ref[g] = jnp.where(vis_t, imp[g][:nsp] + jnp.where(forced, BIG, 0.0), -BIG)

    n_vis = jnp.minimum((s0 + Q_BLK - 1) // L_SEL + 1, ns)

    def count(i, cnts):
        return tuple(c + _beats(st_ref[g, pl.ds(i, 1), :], st_ref[g, 0:nsp, :], jio, i) for g, c in enumerate(cnts))

    cnts = lax.fori_loop(0, n_vis, count, tuple(jnp.zeros((nsp, LANES), F32) for _ in range(KVH)))
    for g in range(KVH):
        sel_ref[g] = jnp.where((cnts[g] < n_top) & vis_t, 0.0, NEG)


def _cmp_select(q_t, kc, vc_t, ov_t, batch, seq):
    nqt = seq // Q_BLK
    n_ch = kc.shape[2]
    ns = seq // L_SEL
    nsp = -(-ns // 8) * 8
    kern = functools.partial(_cmp_select_kernel, nc=n_ch - 1, ns=ns, n_top=min(N_TOP, ns))
    q_spec = pl.BlockSpec((HEADS, HD, Q_BLK), lambda b, t: (0, 0, b * nqt + t))
    return pl.pallas_call(
        kern,
        grid=(batch, nqt),
        in_specs=[
            q_spec,
            pl.BlockSpec((1, KVH, n_ch, HD), lambda b, t: (b, 0, 0, 0)),
            pl.BlockSpec((1, KVH, HD, n_ch), lambda b, t: (b, 0, 0, 0)),
            pl.BlockSpec(ov_t.shape, lambda b, t: (0, 0)),
        ],
        out_specs=(
            q_spec,
            pl.BlockSpec((None, KVH, nsp, LANES), lambda b, t: (b * nqt + t, 0, 0, 0)),
        ),
        out_shape=(
            jax.ShapeDtypeStruct((HEADS, HD, batch * seq), F32),
            jax.ShapeDtypeStruct((batch * nqt, KVH, nsp, LANES), F32),
        ),
        scratch_shapes=[pltpu.VMEM((KVH, nsp, LANES), F32)],
        compiler_params=_params("parallel", "parallel"),
        name="cmp_select",
    )(q_t, kc, vc_t, ov_t)


def _sel_win_kernel(qt_ref, ks_ref, vst_ref, kw_ref, vwt_ref, bias_ref, zbr_ref, ocmp_ref, oa_ref, q_s):
    kt = vst_ref.shape[-1]
    s0 = pl.program_id(1) * Q_BLK
    cols = GRP * Q_BLK
    n_kt = (s0 + Q_BLK + kt - 1) // kt
    bpt = kt // L_SEL
    key_io = lax.broadcasted_iota(jnp.int32, (kt, Q_BLK), 0)
    qpos = s0 + lax.broadcasted_iota(jnp.int32, (kt, Q_BLK), 1)
    ones_row = 2 * 8
    va_rows = HD + ones_row

    for g in range(KVH):
        q_s[g] = jnp.concatenate([qt_ref[GRP * g + r] for r in range(GRP)], axis=1) * (HD ** -0.5)

    def with_bias(st, bias):
        return jnp.concatenate([st[:, r * Q_BLK:(r + 1) * Q_BLK] + bias for r in range(GRP)], axis=1)

    def body(j, carry):
        k0 = pl.multiple_of(j * kt, kt)
        grp = range(KVH)
        causal = jnp.where(k0 + key_io <= qpos, 0.0, NEG)
        st = [_dot(ks_ref[g, pl.ds(k0, kt), :], q_s[g]) for g in grp]
        bias = [jnp.concatenate([jnp.broadcast_to(bias_ref[g, pl.ds(j * bpt + b, 1), :], (L_SEL, Q_BLK))
                                 for b in range(bpt)], axis=0) + causal for g in grp]
        st = [with_bias(st[g], bias[g]) for g in grp]
        m_new = [jnp.maximum(carry[g][0], jnp.max(st[g], axis=0, keepdims=True)) for g in grp]
        corr = [jnp.exp(carry[g][0] - m_new[g]) for g in grp]
        p = [jnp.exp(st[g] - m_new[g]).astype(BF16) for g in grp]
        ones = jnp.ones((ones_row, kt), BF16)
        pv = [_dot(jnp.concatenate([vst_ref[g, j], ones], axis=0), p[g]) for g in grp]
        return tuple((m_new[g], carry[g][1] * corr[g] + pv[g]) for g in grp)

    init = tuple((jnp.full((1, cols), NEG, F32), jnp.zeros((va_rows, cols), F32)) for _ in range(KVH))
    res = lax.fori_loop(0, n_kt, body, init)

    wl = WINDOW + Q_BLK
    wstart = pl.multiple_of(jnp.maximum(s0 - WINDOW, 0), Q_BLK)
    wt0 = wstart // Q_BLK
    kposw = wstart + lax.broadcasted_iota(jnp.int32, (wl, Q_BLK), 0)
    qposw = s0 + lax.broadcasted_iota(jnp.int32, (wl, Q_BLK), 1)
    wbias = jnp.where((kposw <= qposw) & (kposw > qposw - WINDOW), 0.0, NEG)
    ones_w = jnp.ones((ones_row, wl), BF16)
    gates_t = jax.nn.sigmoid(zbr_ref[...]).T
    o_t = []
    st_w = [_dot(kw_ref[g, pl.ds(wstart, wl), :], q_s[g]) for g in range(KVH)]
    st_w = [with_bias(st_w[g], wbias) for g in range(KVH)]
    e_w = [jnp.exp(st_w[g] - jnp.max(st_w[g], axis=0, keepdims=True)).astype(BF16) for g in range(KVH)]
    for g in range(KVH):
        acc = res[g][1]
        o_sel_t = acc[:HD] / acc[HD:HD + 1]
        e = e_w[g]
        vwt = jnp.concatenate([vwt_ref[g, wt0 + i] for i in range(wl // Q_BLK)], axis=1)
        pv = _dot(jnp.concatenate([vwt, ones_w], axis=0), e)
        o_win_t = pv[:HD] / jnp.maximum(pv[HD:HD + 1], 1.0)
        for r in range(GRP):
            h = GRP * g + r
            cs = slice(r * Q_BLK, (r + 1) * Q_BLK)
            o_t.append(gates_t[h:h + 1] * ocmp_ref[h]
                       + gates_t[HEADS + h:HEADS + h + 1] * o_sel_t[:, cs]
                       + gates_t[2 * HEADS + h:2 * HEADS + h + 1] * o_win_t[:, cs])
    for c in range(HEADS // 2):
        oa_ref[:, c * LANES:(c + 1) * LANES] = jnp.concatenate([o_t[2 * c], o_t[2 * c + 1]], axis=0).T


def _sel_win(q_t, ks, vs_t, kw, vw_t, sel_bias, z, ocmp, batch, seq):
    nqt = seq // Q_BLK
    assert seq % SEL_KT == 0
    once = dict(pipeline_mode=pl.Buffered(1))
    k_spec = pl.BlockSpec((KVH, seq, HD), lambda b, t: (0, b, 0), **once)
    tile = lambda b, t: (b * nqt + t, 0, 0, 0)
    return pl.pallas_call(
        _sel_win_kernel,
        grid=(batch, nqt),
        in_specs=[
            pl.BlockSpec((HEADS, HD, Q_BLK), lambda b, t: (0, 0, b * nqt + t)),
            k_spec,
            pl.BlockSpec((KVH, seq // SEL_KT, HD, SEL_KT), lambda b, t: (0, b, 0, 0), **once),
            k_spec,
            pl.BlockSpec((KVH, seq // Q_BLK, HD, Q_BLK), lambda b, t: (0, b, 0, 0), **once),
            pl.BlockSpec((None,) + sel_bias.shape[1:], tile),
            pl.BlockSpec((Q_BLK, LANES), lambda b, t: (b * nqt + t, C1_BR // LANES)),
            pl.BlockSpec((HEADS, HD, Q_BLK), lambda b, t: (0, 0, b * nqt + t)),
        ],
        out_specs=pl.BlockSpec((Q_BLK, NSA_W), lambda b, t: (b * nqt + t, 0)),
        out_shape=jax.ShapeDtypeStruct((z.shape[0], NSA_W), F32),
        scratch_shapes=[pltpu.VMEM((KVH, HD, GRP * Q_BLK), BF16)],
        compiler_params=_params("parallel", "arbitrary"),
        name="sel_win",
    )(q_t, ks, vs_t, kw, vw_t, sel_bias, z, ocmp)


def _hgrn_gates(zq, zf, logits):
    mx = jnp.max(logits, axis=0, keepdims=True)
    ex = jnp.exp(logits - mx)
    lb = ex[0:1] / jnp.sum(ex, axis=0, keepdims=True)
    hq = zq * jax.nn.sigmoid(zq)
    e = jnp.exp(-jnp.abs(zf))
    r = 1.0 / (1.0 + e)
    pos = zf >= 0.0
    sig = jnp.where(pos, r, e * r)
    sig_neg = jnp.where(pos, e * r, r)
    f = lb + (1.0 - lb) * sig
    log_f = jnp.where(f > 0.0, jnp.log(f), jnp.log1p(-lb) + zf)
    hk = (1.0 - lb) * sig_neg
    return hq, hk, log_f


def _intra_chunk(ac, qc, kc, vc):
    n = ac.shape[0]
    sub = 8
    tio = lax.broadcasted_iota(jnp.int32, (sub, ac.shape[1]), 0)
    accs = [jnp.zeros((sub, vc.shape[1]), F32) for _ in range(n // sub)]
    for s in range(n):
        a_s, k_s, v_s = ac[s:s + 1, :], kc[s:s + 1, :], vc[s:s + 1, :]
        for i in range(s // sub, n // sub):
            rs = slice(i * sub, (i + 1) * sub)
            w = qc[rs] * k_s * jnp.exp(ac[rs] - a_s)
            if i == s // sub:
                w = jnp.where(tio >= s - i * sub, w, 0.0)
            accs[i] = accs[i] + jnp.sum(w, axis=-1, keepdims=True) * v_s
    return accs[0] if len(accs) == 1 else jnp.concatenate(accs, axis=0)


def _hgrn_prompt_kernel(zq_ref, zf_ref, zi_ref, lg_ref, tri_ref, full_ref, o_ref, sout_ref,
                        st_ref, a_s, q_s, k_s, qt_s, ktb_s, eal_s, vt_s):
    t = pl.program_id(1)
    tt = zq_ref.shape[0]

    @pl.when(t == 0)
    def _():
        st_ref[...] = jnp.zeros(st_ref.shape, F32)

    hq, hk, log_f = _hgrn_gates(zq_ref[...], zf_ref[...], lg_ref[...])
    a = _dot_exact_rhs(tri_ref[...], log_f)
    al = _dot_exact_rhs(full_ref[...], log_f)
    a_s[...] = a
    q_s[...] = hq
    k_s[...] = hk
    qt_s[...] = hq * jnp.exp(a)
    ktb_s[...] = (hk * jnp.exp(al - a)).astype(BF16)
    eal_s[...] = jnp.exp(al)
    for h in range(HG_H):
        vt_s[h] = zi_ref[:, h * HG_DV:(h + 1) * HG_DV].T.astype(BF16)
    lane_tok = lax.broadcasted_iota(jnp.int32, (HG_DV, tt), 1)

    def chunk(c, carry):
        r0 = pl.multiple_of(c * HG_SUB, HG_SUB)
        in_chunk = (lane_tok >= r0) & (lane_tok < r0 + HG_SUB)
        for h in range(HG_H):
            sl = slice(h * HG_DK, (h + 1) * HG_DK)
            st = st_ref[h]
            o = _nt(qt_s[pl.ds(r0, HG_SUB), sl].astype(BF16), st.astype(BF16))
            o = o + _intra_chunk(a_s[pl.ds(r0, HG_SUB), sl], q_s[pl.ds(r0, HG_SUB), sl],
                                 k_s[pl.ds(r0, HG_SUB), sl], zi_ref[pl.ds(r0, HG_SUB), sl])
            o_ref[pl.ds(r0, HG_SUB), sl] = o
            vtm = jnp.where(in_chunk, vt_s[h], jnp.zeros_like(vt_s[h]))
            st_ref[h] = st * eal_s[pl.ds(r0, 1), sl] + _dot(vtm, ktb_s[:, sl])
        return carry

    lax.fori_loop(0, tt // HG_SUB, chunk, 0)

    @pl.when(t == pl.num_programs(1) - 1)
    def _():
        for h in range(HG_H):
            sout_ref[0, h] = st_ref[h].T


def _hgrn_prompt(z, logits, batch, seq):
    tt = 128
    nt = seq // tt
    sub = np.arange(tt) // HG_SUB
    same = sub[:, None] == sub[None, :]
    tri = jnp.asarray((same & (np.arange(tt)[None, :] <= np.arange(tt)[:, None])).astype(np.float32), BF16)
    full = jnp.asarray(same.astype(np.float32), BF16)
    col = lambda c: (lambda b, t: (b * nt + t, c // HG_W))
    const = lambda b, t: (0, 0)
    return pl.pallas_call(
        _hgrn_prompt_kernel,
        grid=(batch, nt),
        in_specs=[
            pl.BlockSpec((tt, HG_W), col(C2_HQ)),
            pl.BlockSpec((tt, HG_W), col(C2_HF)),
            pl.BlockSpec((tt, HG_W), col(C2_HI)),
            pl.BlockSpec((2, HG_W), const),
            pl.BlockSpec((tt, tt), const),
            pl.BlockSpec((tt, tt), const),
        ],
        out_specs=(
            pl.BlockSpec((tt, HG_W), lambda b, t: (b * nt + t, 0)),
            pl.BlockSpec((1, HG_H, HG_DK, HG_DV), lambda b, t: (b, 0, 0, 0)),
        ),
        out_shape=(
            jax.ShapeDtypeStruct((z.shape[0], HG_W), F32),
            jax.ShapeDtypeStruct((batch, HG_H, HG_DK, HG_DV), F32),
        ),
        scratch_shapes=[
            pltpu.VMEM((HG_H, HG_DV, HG_DK), F32),
            pltpu.VMEM((tt, HG_W), F32),
            pltpu.VMEM((tt, HG_W), F32),
            pltpu.VMEM((tt, HG_W), F32),
            pltpu.VMEM((tt, HG_W), F32),
            pltpu.VMEM((tt, HG_W), BF16),
            pltpu.VMEM((tt, HG_W), F32),
            pltpu.VMEM((HG_H, HG_DV, tt), BF16),
        ],
        compiler_params=_params("parallel", "arbitrary"),
        name="hgrn_prompt",
    )(z, z, z, logits, tri, full)


def _hgrn_sample_kernel(zq_ref, zf_ref, zi_ref, lg_ref, s0_ref, ob_hbm_ref, o_ref, sout_ref, *, s_len):
    del ob_hbm_ref
    n_seq = s0_ref.shape[0]
    hq_all, hk_all, logf_all = _hgrn_gates(zq_ref[...], zf_ref[...], lg_ref[...])
    rio = lax.broadcasted_iota(jnp.int32, (s_len, HG_W), 0)
    row0 = lax.broadcasted_iota(jnp.int32, (s_len, HG_DK), 0) == 0
    pad = HG_DK - 2 * s_len
    for q in range(n_seq):
        rs = slice(q * s_len, (q + 1) * s_len)
        hq, hk, log_f = hq_all[rs], hk_all[rs], logf_all[rs]
        a = log_f
        d = 1
        while d < s_len:
            a = a + jnp.where(rio >= d, pltpu.roll(a, d, 0), 0.0)
            d *= 2
        a_last = jnp.sum(log_f, axis=0, keepdims=True)
        qt = hq * jnp.exp(a)
        kt = hk * jnp.exp(a_last - a)
        ea = jnp.exp(a_last)
        hv = zi_ref[rs, :]
        for h in range(HG_H):
            sl = slice(h * HG_DK, (h + 1) * HG_DK)
            s_old = s0_ref[q, h]
            qt16 = _pad_rows(qt[:, sl], 16).astype(BF16)
            o = _dot(qt16, s_old.astype(BF16))[:s_len]
            o = o + _intra_chunk(a[:, sl], hq[:, sl], hk[:, sl], hv[:, sl])
            o_ref[rs, sl] = o
            m = jnp.concatenate([kt[:, sl], jnp.where(row0, jnp.broadcast_to(ea[:, sl], (s_len, HG_DK)), 0.0),
                                 jnp.zeros((pad, HG_DK), F32)], axis=0)
            mt = m.T
            e_col = mt[:, s_len:s_len + 1]
            vpad = _pad_rows(hv[:, sl], HG_DK).astype(BF16)
            sout_ref[q, h] = e_col * s_old + _dot(mt.astype(BF16), vpad)


def _hgrn_sample(z, logits, state, ob, n_prompt, dec_batch, s_len):
    n_seq = 4
    rows = n_seq * s_len
    base = n_prompt // rows
    col = lambda c: (lambda b: (base + b, c // HG_W))
    st_spec = pl.BlockSpec((n_seq, HG_H, HG_DK, HG_DV), lambda b: (b, 0, 0, 0))
    return pl.pallas_call(
        functools.partial(_hgrn_sample_kernel, s_len=s_len),
        grid=(dec_batch // n_seq,),
        in_specs=[
            pl.BlockSpec((rows, HG_W), col(C2_HQ)),
            pl.BlockSpec((rows, HG_W), col(C2_HF)),
            pl.BlockSpec((rows, HG_W), col(C2_HI)),
            pl.BlockSpec((2, HG_W), lambda b: (0, 0)),
            st_spec,
            pl.BlockSpec(memory_space=pl.ANY),
        ],
        out_specs=(pl.BlockSpec((rows, HG_W), lambda b: (base + b, 0)), st_spec),
        out_shape=(
            jax.ShapeDtypeStruct(ob.shape, F32),
            jax.ShapeDtypeStruct(state.shape, F32),
        ),
        input_output_aliases={5: 0},
        compiler_params=_params("parallel"),
        name="hgrn_sample",
    )(z, z, z, logits, state, ob)


def _sample_nsa_kernel(pt_ref, *refs, npg, s_len, past):
    del pt_ref
    cmp_pages = refs[:npg]
    sel_pages = refs[npg:2 * npg]
    (win_ref, q_ref, rsel_ref, rwin_ref, zbr_ref, w2k_ref, w2v_ref, pek_ref, pev_ref, bdk_ref, bdv_ref, perm_ref,
     ov_ref, e_ref, oa_hbm_ref, oa_ref, swin_ref, xs_ref, kcv_ref) = refs[2 * npg:]
    del oa_hbm_ref
    step = pl.program_id(0)

    @pl.when(step == 0)
    def _():
        kcv_ref[...] = jnp.zeros(kcv_ref.shape, BF16)

    kc_all = kcv_ref[(step + 1) % 2, 0]
    vc_all = kcv_ref[(step + 1) % 2, 1]
    page = cmp_pages[0].shape[1]
    rpp = page // D_CMP
    n_ch = npg * rpp
    nc = n_ch - 1
    wb = win_ref.shape[1]
    ns = -(-(past + s_len) // L_SEL)
    nsp = -(-ns // 8) * 8
    n_top = min(N_TOP, ns)
    rows = HEADS * s_len

    w2 = (w2k_ref, w2v_ref)
    pe = (pek_ref, pev_ref)
    bd = (bdk_ref, bdv_ref)
    comp = []

    def stage1_scatter(lo, hi):
        for i in range(lo, hi):
            _scatter_chunk_rows(xs_ref, perm_ref[...], cmp_pages[i][...].astype(BF16), i * rpp)

    def stage1_compress(p):
        kv = p // 2
        comp.append(_compress_pair(_chunk_flat(xs_ref, p), w2[kv][...], pe[kv][...], bd[kv][...], n_ch))

    q = q_ref[...]
    lane8 = lax.broadcasted_iota(jnp.int32, (s_len, LANES), 1)
    zero8 = jnp.zeros((s_len, LANES), F32)
    blocks = []
    for g in range(KVH):
        for r in range(GRP):
            h = GRP * g + r
            src = q[:, (h // 2) * LANES:(h // 2 + 1) * LANES]
            if (r % 2) != (g % 2):
                src = pltpu.roll(src, HD, 1)
            keep = (lane8 < HD) if g % 2 == 0 else (lane8 >= HD)
            piece = jnp.where(keep, src, 0.0)
            blocks.append(jnp.concatenate([piece, zero8] if g // 2 == 0 else [zero8, piece], axis=1))
    qbd = (jnp.concatenate(blocks, axis=0) * (HD ** -0.5)).astype(BF16)

    def rowpos(shape):
        return past + lax.broadcasted_iota(jnp.int32, shape, 0) % s_len

    stage1_scatter(0, npg // 2)

    s = _nt(qbd, kc_all)
    cidx = lax.broadcasted_iota(jnp.int32, s.shape, 1)
    cvalid = (cidx * D_CMP + (L_CMP - 1) <= rowpos(s.shape)) & (cidx < nc)
    s = jnp.where(cvalid, s, NEG)
    m = jnp.max(s, axis=-1, keepdims=True)
    e = jnp.where(cvalid, jnp.exp(s - m), 0.0)
    p = e / jnp.maximum(jnp.sum(e, axis=-1, keepdims=True), 1.0)
    ocmp = _dot(p.astype(BF16), vc_all)

    stage1_scatter(npg // 2, npg)

    gs = GRP * s_len
    psum = []
    for g in range(KVH):
        acc = p[g * gs:g * gs + s_len]
        for r in range(1, GRP):
            acc = acc + p[g * gs + r * s_len:g * gs + (r + 1) * s_len]
        psum.append(acc)
    imp = _dot_exact_lhs(jnp.concatenate(psum, axis=0), ov_ref[...])
    qpos = rowpos(imp.shape)
    blk = lax.broadcasted_iota(jnp.int32, imp.shape, 1)
    visible = (blk * L_SEL <= qpos) & (blk < ns)
    cur = qpos // L_SEL
    forced = (blk == 0) | (blk == cur) | (blk == cur - 1)
    score = jnp.where(visible, imp + jnp.where(forced, BIG, 0.0), -BIG)
    stage1_compress(0)
    sel_t = _rank_select(_pad_rows(score, LANES).T, ns, nsp, n_top)
    sel = _pad_rows(sel_t, LANES).T[:KVH * s_len]
    sel = jnp.where(visible, sel, 0.0)
    sel_rows = jnp.concatenate([sel[g * s_len:(g + 1) * s_len] for g in range(KVH) for _ in range(GRP)], axis=0)

    zpad = jnp.zeros((page - s_len, KW), F32)
    knew = jnp.concatenate([rsel_ref[:, :KW], zpad], axis=0).astype(BF16)
    vnew = jnp.concatenate([rsel_ref[:, KW:], zpad], axis=0).astype(BF16)
    def page_pair(i, lo):
        return jnp.concatenate([sel_pages[2 * i][lo:lo + KW, :], sel_pages[2 * i + 1][lo:lo + KW, :]],
                               axis=1).astype(BF16)

    s = jnp.concatenate([_dot(qbd, page_pair(i, 0)) for i in range(npg // 2)] + [_nt(qbd, knew)], axis=1)
    mf = _dot(sel_rows.astype(BF16), e_ref[...])
    kpos = lax.broadcasted_iota(jnp.int32, s.shape, 1)
    mask = (mf > 0.5) & (kpos <= rowpos(s.shape))
    stage1_compress(1)
    s = jnp.where(mask, s, NEG)
    m = jnp.max(s, axis=-1, keepdims=True)
    e = jnp.where(mask, jnp.exp(s - m), 0.0)
    l = jnp.sum(e, axis=-1, keepdims=True)
    eb = e.astype(BF16)
    acc = _dot(eb[:, npg * page:], vnew)
    for i in range(npg // 2):
        acc = acc + _nt(eb[:, 2 * i * page:(2 * i + 2) * page], page_pair(i, KW))
    osel = acc / l

    kwn = jnp.concatenate([rwin_ref[:, :KW], zpad], axis=0).astype(BF16)
    vwn = jnp.concatenate([rwin_ref[:, KW:], zpad], axis=0).astype(BF16)
    s = jnp.concatenate([_dot(qbd, win_ref[:KW, :].astype(BF16)), _nt(qbd, kwn)], axis=1)
    kposw = (past - wb) + lax.broadcasted_iota(jnp.int32, s.shape, 1)
    qposw = rowpos(s.shape)
    wmask = (kposw <= qposw) & (kposw > qposw - WINDOW) & (kposw >= 0)
    s = jnp.where(wmask, s, NEG)
    m = jnp.max(s, axis=-1, keepdims=True)
    e = jnp.where(wmask, jnp.exp(s - m), 0.0)
    den = jnp.maximum(jnp.sum(e, axis=-1, keepdims=True), 1.0)
    eb = e.astype(BF16)
    owin = (_nt(eb[:, :wb], win_ref[KW:, :].astype(BF16)) + _dot(eb[:, wb:], vwn)) / den

    stage1_compress(2)

    gates = jax.nn.sigmoid(zbr_ref[...])
    for g in range(KVH):
        gl = slice((g // 2) * LANES, (g // 2 + 1) * LANES)
        for k2 in range(2):
            halves = []
            for r in (2 * k2, 2 * k2 + 1):
                h = GRP * g + r
                rs = slice(h * s_len, (h + 1) * s_len)
                val = (gates[:, h:h + 1] * ocmp[rs, gl]
                       + gates[:, HEADS + h:HEADS + h + 1] * osel[rs, gl]
                       + gates[:, 2 * HEADS + h:2 * HEADS + h + 1] * owin[rs, gl])
                if (r % 2) != (g % 2):
                    val = pltpu.roll(val, HD, 1)
                halves.append(val)
            c = 2 * g + k2
            oa_ref[:, c * LANES:(c + 1) * LANES] = jnp.where(lane8 < HD, halves[0], halves[1])

    new_t = [_pad_rows(rwin_ref[:, c * LANES:(c + 1) * LANES], LANES).T for c in range(KVW // LANES)]
    new_t = jnp.concatenate(new_t, axis=0)
    lane_w = lax.broadcasted_iota(jnp.int32, (KVW, LANES), 1)
    n_col = wb // LANES
    rolled = [pltpu.roll(win_ref[:, c * LANES:(c + 1) * LANES], LANES - s_len, 1) for c in range(n_col)]
    rolled.append(pltpu.roll(new_t, LANES - s_len, 1))
    for c in range(n_col):
        swin_ref[:, c * LANES:(c + 1) * LANES] = jnp.where(lane_w < LANES - s_len, rolled[c], rolled[c + 1])

    stage1_compress(3)
    kcv_ref[step % 2, 0] = jnp.concatenate(comp[0:2], axis=1).astype(BF16)
    kcv_ref[step % 2, 1] = jnp.concatenate(comp[2:4], axis=1).astype(BF16)


def _sample_nsa(page_table, cache_cmp, cache_sel, cache_win, q_tok, rows_sel, rows_win, z, cw, ov, e_mat, oa,
                n_prompt, dec_batch, s_len):
    npg = page_table.shape[1]
    page = cache_cmp.shape[2]
    past = npg * page
    wb = cache_win.shape[2]
    base = n_prompt // s_len
    last = dec_batch - 1
    steps = dec_batch + 1
    seq2 = lambda b: jnp.maximum(b - 1, 0)
    page_table = jnp.concatenate([page_table, page_table[last:], page_table[:1], page_table], axis=0)
    cmp_specs = [pl.BlockSpec((None, KVW, page), (lambda b, pt, i=i: (pt[b, i], 0, 0))) for i in range(npg)]
    sel_specs = [pl.BlockSpec((None, KVW, page), (lambda b, pt, i=i: (pt[steps + b, i], 0, 0))) for i in range(npg)]
    row = lambda b, pt: (base + seq2(b), 0)
    const = lambda b, pt: (0, 0)
    in_specs = (
        cmp_specs + sel_specs
        + [pl.BlockSpec((None, KVW, wb), lambda b, pt: (seq2(b), 0, 0)),
           pl.BlockSpec((s_len, NSA_W), row),
           pl.BlockSpec((s_len, KVW), row),
           pl.BlockSpec((s_len, KVW), row),
           pl.BlockSpec((s_len, LANES), lambda b, pt: (base + seq2(b), C1_BR // LANES))]
        + [pl.BlockSpec(w.shape, const) for w in cw]
        + [pl.BlockSpec(ov.shape, const), pl.BlockSpec(e_mat.shape, const), pl.BlockSpec(memory_space=pl.ANY)]
    )
    n_ch = past // D_CMP
    grid_spec = pltpu.PrefetchScalarGridSpec(
        num_scalar_prefetch=1,
        grid=(dec_batch + 1,),
        in_specs=in_specs,
        out_specs=(
            pl.BlockSpec((s_len, NSA_W), row),
            pl.BlockSpec((None, KVW, wb), lambda b, pt: (seq2(b), 0, 0)),
        ),
        scratch_shapes=[
            pltpu.VMEM((D_CMP, n_ch, KVW), F32),
            pltpu.VMEM((2, 2, n_ch, KW), BF16),
        ],
    )
    args = (page_table, *([cache_cmp] * npg), *([cache_sel] * npg), cache_win, q_tok, rows_sel, rows_win, z, *cw,
            ov, e_mat, oa)
    return pl.pallas_call(
        functools.partial(_sample_nsa_kernel, npg=npg, s_len=s_len, past=past),
        grid_spec=grid_spec,
        out_shape=(
            jax.ShapeDtypeStruct(oa.shape, F32),
            jax.ShapeDtypeStruct((dec_batch, KVW, wb), F32),
        ),
        input_output_aliases={len(args) - 1: 0},
        compiler_params=_params("arbitrary"),
        name="sample_nsa",
    )(*args)


def _mix_kernel(xp_ref, xs_ref, oa_ref, ob_ref, zhg_ref, ga_ref, gb_ref, gn_ref, wa_ref, wb_ref, wo_ref, o_ref, *, npt):
    ya = _dot(oa_ref[...].astype(BF16), wa_ref[...])
    gn = gn_ref[...]
    parts = []
    for h in range(HG_H):
        sl = slice(h * HG_DV, (h + 1) * HG_DV)
        ob = ob_ref[:, sl]
        ms = jnp.mean(ob * ob, axis=-1, keepdims=True)
        zg = zhg_ref[:, sl]
        y = ((ob * lax.rsqrt(ms + EPS)) * gn) * (zg * jax.nn.sigmoid(zg))
        parts.append(y.astype(BF16))
    yb = _dot(jnp.concatenate(parts, axis=1), wb_ref[...])
    mix = jax.nn.sigmoid(ga_ref[...]) * ya + jax.nn.sigmoid(gb_ref[...]) * yb
    y = _dot(mix.astype(BF16), wo_ref[...])

    @pl.when(pl.program_id(0) < npt)
    def _():
        o_ref[...] = xp_ref[...] + y

    @pl.when(pl.program_id(0) >= npt)
    def _():
        o_ref[...] = xs_ref[...] + y


def _mix(xp, xs, oa, ob, z, gn, wa, wb, wo):
    (n_p, d), n_s = xp.shape, xs.shape[0]
    n = n_p + n_s
    tm = 256
    npt, spec_p, spec_s = _two_group_specs(n_p, n_s, tm, d)
    row = lambda i: (i, 0)
    const = lambda i: (0, 0)
    resident = lambda w: pl.BlockSpec(w.shape, const, pipeline_mode=pl.Buffered(1))
    return pl.pallas_call(
        functools.partial(_mix_kernel, npt=npt),
        grid=(n // tm,),
        in_specs=[
            spec_p, spec_s,
            pl.BlockSpec((tm, NSA_W), row),
            pl.BlockSpec((tm, HG_W), row),
            pl.BlockSpec((tm, HG_W), lambda i: (i, C2_HG // HG_W)),
            pl.BlockSpec((tm, d), lambda i: (i, C2_GA // D_MODEL)),
            pl.BlockSpec((tm, d), lambda i: (i, C2_GB // D_MODEL)),
            pl.BlockSpec((1, HG_DV), const),
            resident(wa), resident(wb), resident(wo),
        ],
        out_specs=pl.BlockSpec((tm, d), row),
        out_shape=jax.ShapeDtypeStruct((n, d), F32),
        compiler_params=_params("parallel"),
        name="mix_out",
    )(xp, xs, oa, ob, z, z, z, gn, wa, wb, wo)


def _ffn_kernel(x_ref, g_ref, w1_ref, w2_ref, op_ref, os_ref, xn_ref, acc_ref, *, npt):
    i = pl.program_id(0)
    j = pl.program_id(1)
    last = pl.num_programs(1) - 1

    @pl.when(j == 0)
    def _():
        xn_ref[...] = _rmsnorm_bf16(x_ref[...], g_ref[...])
        acc_ref[...] = jnp.zeros(acc_ref.shape, F32)

    h = _dot(xn_ref[...], w1_ref[...])
    h = jnp.square(jnp.maximum(h, 0.0)).astype(BF16)
    acc_ref[...] += _dot(h, w2_ref[...])

    @pl.when((j == last) & (i < npt))
    def _():
        op_ref[...] = x_ref[...] + acc_ref[...]

    @pl.when((j == last) & (i >= npt))
    def _():
        os_ref[...] = x_ref[...] + acc_ref[...]


def _ffn(x, g, w1, w2, n_p):
    n, d = x.shape
    n_s = n - n_p
    dff = w1.shape[1]
    tm = _pick_tile(np.gcd(n_p, n_s), (512, 256))
    tf = 1024
    npt, spec_p, spec_s = _two_group_specs(n_p, n_s, tm, d)
    return pl.pallas_call(
        functools.partial(_ffn_kernel, npt=npt),
        grid=(n // tm, dff // tf),
        in_specs=[
            pl.BlockSpec((tm, d), lambda i, j: (i, 0)),
            pl.BlockSpec((1, d), lambda i, j: (0, 0)),
            pl.BlockSpec((d, tf), lambda i, j: (0, j)),
            pl.BlockSpec((tf, d), lambda i, j: (j, 0)),
        ],
        out_specs=(spec_p, spec_s),
        out_shape=(jax.ShapeDtypeStruct((n_p, d), F32), jax.ShapeDtypeStruct((n_s, d), F32)),
        scratch_shapes=[pltpu.VMEM((tm, d), BF16), pltpu.VMEM((tm, d), F32)],
        compiler_params=_params("arbitrary", "arbitrary"),
        name="ffn",
    )(x, g, w1, w2)


def _compress_weights(cmp_pe, cmp_w1, cmp_w2):
    m = L_CMP // D_CMP
    eye2 = jnp.eye(2, dtype=F32)
    out = []
    w2s, pes, bds = [], [], []
    for kv in range(2):
        w1p = cmp_w1[kv].reshape(m, D_CMP, HD, HD).transpose(1, 2, 0, 3)
        w = w1p[:, None, :, :, None, :] * eye2[None, :, None, None, :, None]
        w2s.append(w.reshape(D_CMP * LANES, m * LANES).astype(BF16))
        pe = cmp_pe[kv].reshape(m, D_CMP, 1, HD)
        pe = jnp.broadcast_to(pe, (m, D_CMP, 2, HD)).reshape(m, D_CMP * LANES)
        pes.append(jnp.concatenate([pe, jnp.zeros((8 - m, D_CMP * LANES), F32)], axis=0))
        bds.append(jnp.kron(eye2, cmp_w2[kv]).astype(BF16))
    tile_rows = LANES
    rpp = tile_rows // D_CMP
    src = np.arange(tile_rows)
    perm = np.zeros((tile_rows, tile_rows), np.float32)
    perm[(src % D_CMP) * rpp + src // D_CMP, src] = 1.0
    return (w2s[0], w2s[1], pes[0], pes[1], bds[0], bds[1], jnp.asarray(perm, BF16))


def _overlap_matrix(nc_pad, ns, transposed=False):
    cs = np.arange(nc_pad) * D_CMP
    ss = np.arange(LANES) * L_SEL
    ov = np.clip(np.minimum(cs[:, None] + L_CMP, ss[None, :] + L_SEL) - np.maximum(cs[:, None], ss[None, :]), 0, None)
    ov = ((ov / D_CMP) * (np.arange(LANES)[None, :] < ns)).astype(np.float32)
    return jnp.asarray(np.ascontiguousarray(ov.T) if transposed else ov, BF16)


def _expand_matrix(n_keys):
    e = (np.arange(n_keys)[None, :] // L_SEL) == np.arange(LANES)[:, None]
    return jnp.asarray(e.astype(np.float32), BF16)


def _rope_tables(seq, past, s_len, tm):
    half = HD // 2
    inv = THETA ** (-jnp.arange(half, dtype=F32) * 2.0 / HD)
    pos = jnp.concatenate([jnp.arange(seq), past + jnp.arange(tm) % s_len]).astype(F32)
    ang = pos[:, None] * inv[None, :]
    cos = jnp.cos(ang)
    sin = jnp.sin(ang)
    cos_t = jnp.concatenate([cos, cos, cos, cos], axis=1)
    sin_t = jnp.concatenate([-sin, sin, -sin, sin], axis=1)
    return cos_t, sin_t


def kernel(x_prompt, x_sample, cache_cmp_kv, cache_sel_kv, cache_win_kv, state_hgrn, page_table, norm1_g, w_in,
           q_norm_g, k_norm_g, cmp_pe, cmp_w1, cmp_w2, w_proj_a, hgrn_lb_logits, hgrn_norm_g, w_proj_b, w_out,
           norm2_g, w_ff1, w_ff2):
    assert w_in.shape[0] == 1, "single-layer step"
    batch, seq, d = x_prompt.shape
    dec_batch, s_len, _ = x_sample.shape
    n_prompt = batch * seq
    n_sample = dec_batch * s_len
    npg = page_table.shape[1]
    page = cache_cmp_kv.shape[2]
    past = npg * page
    wb = cache_win_kv.shape[2]
    assert seq % Q_BLK == 0 and seq >= WINDOW + Q_BLK and n_prompt % 256 == 0 and n_sample % 256 == 0
    assert s_len % 8 == 0 and page % D_CMP == 0 and 2 * s_len <= HG_DK and dec_batch % 4 == 0 and npg % 2 == 0

    xp = x_prompt.reshape(n_prompt, d)
    xs = x_sample.reshape(n_sample, d)

    cw = _compress_weights(cmp_pe[0], cmp_w1[0], cmp_w2[0])
    cos_t, sin_t = _rope_tables(seq, past, s_len, 256)
    gq = jnp.tile(q_norm_g[0][None, :], (1, 2))
    gk = jnp.tile(k_norm_g[0], (1, 2))
    logits = hgrn_lb_logits.astype(F32)

    def token_minor(c):
        return jnp.transpose(c, (0, 2, 3, 4, 1)).reshape(c.shape[0], KVW, c.shape[1])

    def token_major(c_t):
        return jnp.transpose(c_t.reshape(c_t.shape[0], 2, KVH, HD, c_t.shape[2]), (0, 4, 1, 2, 3))

    w_t = jnp.transpose(w_in[0])
    xn = _xnorm(xp, xs, norm1_g)
    z1 = _in_proj(xn, w_t, 0, N_Z1, 896)
    z2 = _in_proj(xn, w_t, sum(_SIZES[:3]), N_Z2, 1024)
    (q_tok, q_t, rows_cmp, rows_sel, rows_win, rt_cmp, rt_sel, rt_win,
     ks, vs_t, kw, vw_t) = _qkv_post(z1, cos_t, sin_t, gq, gk, n_prompt, seq)

    bdv_t = lax.optimization_barrier(cw[5].T)
    kc, vc_t = _compress_prompt(rt_cmp, cw[:5] + (bdv_t,) + cw[6:], batch, seq)
    ov_p = _overlap_matrix(seq // D_CMP, seq // L_SEL, transposed=True)
    ocmp, sel_bias = _cmp_select(q_t, kc, vc_t, ov_p, batch, seq)
    oa = _sel_win(q_t, ks, vs_t, kw, vw_t, sel_bias, z1, ocmp, batch, seq)
    ob, p_state = _hgrn_prompt(z2, logits, batch, seq)

    ns_s = -(-(past + s_len) // L_SEL)
    ov_s = _overlap_matrix(past // D_CMP, ns_s)
    oa, s_win_t = _sample_nsa(
        page_table, token_minor(cache_cmp_kv[0]), token_minor(cache_sel_kv[0]), token_minor(cache_win_kv[0]),
        q_tok, rows_sel, rows_win, z1, cw, ov_s, _expand_matrix(past + page), oa, n_prompt, dec_batch, s_len)
    ob, s_state = _hgrn_sample(z2, logits, state_hgrn[0], ob, n_prompt, dec_batch, s_len)

    x2 = _mix(xp, xs, oa, ob, z2, hgrn_norm_g, w_proj_a[0].astype(BF16), w_proj_b[0].astype(BF16),
              w_out[0].astype(BF16))
    y_p, y_s = _ffn(x2, norm2_g, w_ff1[0].astype(BF16), w_ff2[0].astype(BF16), n_prompt)

    kv_shape = (2, KVH, HD)
    wp = min(WINDOW, seq)
    return (
        y_p.reshape(batch, seq, d),
        y_s.reshape(dec_batch, s_len, d),
        token_major(rt_cmp[:batch])[None],
        token_major(rt_sel[:batch])[None],
        token_major(rt_win[:batch, :, seq - wp:])[None],
        p_state[None],
        rows_cmp[n_prompt:].reshape((1, dec_batch, s_len) + kv_shape),
        rows_sel[n_prompt:].reshape((1, dec_batch, s_len) + kv_shape),
        token_major(s_win_t)[None],
        s_state[None],
    )
```

```python
import functools

import numpy as np
import jax
import jax.numpy as jnp
from jax import lax
from jax.experimental import pallas as pl
from jax.experimental.pallas import tpu as pltpu

F32 = jnp.float32
BF16 = jnp.bfloat16

D_MODEL = 2048
HEADS = 16
KVH = 4
GRP = HEADS // KVH
HD = 64
NSA_W = HEADS * HD
L_CMP = 32
D_CMP = 16
L_SEL = 64
N_TOP = 16
WINDOW = 512
Q_BLK = 128
THETA = 10000.0
HG_H = 8
HG_DK = 128
HG_DV = 128
HG_W = HG_H * HG_DK
HG_SUB = 16
SEL_KT = 512
EPS = 1e-6
NEG = -1e30
BIG = 1e9

LANES = 128
KVW = 2 * KVH * HD
KW = KVH * HD

_SIZES = (NSA_W, 6 * KVH * HD, 3 * HEADS, HG_W, HG_W, HG_W, HG_W, D_MODEL, D_MODEL)
C1_Q, C1_KV, C1_BR, N_Z1 = 0, 1024, 2560, 2688
C2_HQ, C2_HF, C2_HI, C2_HG, C2_GA, C2_GB, N_Z2 = 0, 1024, 2048, 3072, 4096, 6144, 8192

VMEM_LIMIT = 56 * 1024 * 1024


def _params(*sem):
    return pltpu.CompilerParams(dimension_semantics=sem, vmem_limit_bytes=VMEM_LIMIT)


def _nt(a, b):
    return lax.dot_general(a, b, (((1,), (1,)), ((), ())), preferred_element_type=F32)


def _dot(a, b):
    return jnp.dot(a, b, preferred_element_type=F32)


def _split3(x):
    hi = x.astype(BF16)
    r1 = x - hi.astype(F32)
    mid = r1.astype(BF16)
    lo = (r1 - mid.astype(F32)).astype(BF16)
    return hi, mid, lo


def _dot_exact_rhs(m, x):
    hi, mid, lo = _split3(x)
    return _dot(m, hi) + _dot(m, mid) + _dot(m, lo)


def _dot_exact_lhs(x, m):
    hi, mid, lo = _split3(x)
    return _dot(hi, m) + _dot(mid, m) + _dot(lo, m)


def _pick_tile(n, cands):
    for c in cands:
        if n % c == 0:
            return c
    raise ValueError(f"no tile in {cands} divides {n}")


def _rmsnorm_bf16(x, g):
    ms = jnp.mean(x * x, axis=-1, keepdims=True)
    return ((x * lax.rsqrt(ms + EPS)) * g).astype(BF16)


def _two_group_specs(n_p, n_s, tm, width, sample_kwargs=None):
    npt = n_p // tm
    assert n_p % tm == 0 and n_s % tm == 0
    spec_p = pl.BlockSpec((tm, width), lambda i, *_: (jnp.minimum(i, npt - 1), 0))
    spec_s = pl.BlockSpec((tm, width), lambda i, *_: (jnp.maximum(i - npt, 0), 0), **(sample_kwargs or {}))
    return npt, spec_p, spec_s


def _xnorm_kernel(xp_ref, xs_ref, g_ref, o_ref, *, npt):
    @pl.when(pl.program_id(0) < npt)
    def _():
        o_ref[...] = _rmsnorm_bf16(xp_ref[...], g_ref[...])

    @pl.when(pl.program_id(0) >= npt)
    def _():
        o_ref[...] = _rmsnorm_bf16(xs_ref[...], g_ref[...])


def _xnorm(xp, xs, g):
    (n_p, d), n_s = xp.shape, xs.shape[0]
    tm = _pick_tile(np.gcd(n_p, n_s), (512, 256))
    npt, spec_p, spec_s = _two_group_specs(n_p, n_s, tm, d)
    return pl.pallas_call(
        functools.partial(_xnorm_kernel, npt=npt),
        grid=((n_p + n_s) // tm,),
        in_specs=[spec_p, spec_s, pl.BlockSpec((1, d), lambda i: (0, 0))],
        out_specs=pl.BlockSpec((tm, d), lambda i: (i, 0)),
        out_shape=jax.ShapeDtypeStruct((n_p + n_s, d), BF16),
        compiler_params=_params("parallel"),
        name="xnorm",
    )(xp, xs, g)


def _matmul_wt_kernel(x_ref, w_ref, o_ref, wb_ref):
    @pl.when(pl.program_id(1) == 0)
    def _():
        wb_ref[...] = w_ref[...].astype(BF16)

    o_ref[...] = _nt(x_ref[...], wb_ref[...])


def _in_proj(xn, w_t, row0, nout, tn):
    n, d = xn.shape
    assert nout % tn == 0 and row0 % 8 == 0 and row0 + nout <= w_t.shape[0]
    tm = _pick_tile(n, (1024, 512, 256))
    return pl.pallas_call(
        _matmul_wt_kernel,
        grid=(nout // tn, n // tm),
        in_specs=[
            pl.BlockSpec((tm, d), lambda j, i: (i, 0)),
            pl.BlockSpec((pl.Element(tn), pl.Element(d)), lambda j, i: ((row0 // 8 + j * (tn // 8)) * 8, 0)),
        ],
        out_specs=pl.BlockSpec((tm, tn), lambda j, i: (i, j)),
        out_shape=jax.ShapeDtypeStruct((n, nout), F32),
        scratch_shapes=[pltpu.VMEM((tn, d), BF16)],
        compiler_params=_params("arbitrary", "arbitrary"),
        name="in_proj",
    )(xn, w_t)


def _qkv_post_kernel(zq_ref, zc_ref, zs_ref, zw_ref, cos_ref, sin_ref, gq_ref, gk_ref,
                     qtok_ref, qt_ref, rc_ref, rs_ref, rw_ref, tc_ref, ts_ref, tw_ref, tcs_ref, tss_ref, tws_ref,
                     ks_ref, vs_ref, kw_ref, vw_ref, *, npt):
    cos = cos_ref[...]
    sin = sin_ref[...]
    lane = lax.broadcasted_iota(jnp.int32, cos.shape, 1)
    first = (lane % HD) < (HD // 2)
    hi_ = lax.broadcasted_iota(jnp.int32, (LANES, LANES), 0) // HD
    hj_ = lax.broadcasted_iota(jnp.int32, (LANES, LANES), 1) // HD
    same_head = jnp.where(hi_ == hj_, 1.0, 0.0).astype(BF16)

    def norm_rope(xc, g_row):
        ms = _dot_exact_lhs(xc * xc, same_head) * (1.0 / HD)
        y = (xc * lax.rsqrt(ms + EPS)) * g_row
        partner = jnp.where(first, pltpu.roll(y, LANES - HD // 2, 1), pltpu.roll(y, HD // 2, 1))
        return y * cos + partner * sin

    gq = gq_ref[...]
    for c in range(NSA_W // LANES):
        sl = slice(c * LANES, (c + 1) * LANES)
        qc = norm_rope(zq_ref[:, sl], gq)
        qtok_ref[:, sl] = qc
        qtb = qc.T.astype(BF16)
        qt_ref[2 * c] = qtb[:HD]
        qt_ref[2 * c + 1] = qtb[HD:]

    tm = zq_ref.shape[0]
    zin = (zc_ref, zs_ref, zw_ref)
    rows = (rc_ref, rs_ref, rw_ref)
    rows_t = (tc_ref, ts_ref, tw_ref)
    rows_t_s = (tcs_ref, tss_ref, tws_ref)
    is_prompt = pl.program_id(0) < npt
    kg = (None, ks_ref, kw_ref)
    vg = (None, vs_ref, vw_ref)
    for br in range(3):
        gk = gk_ref[br:br + 1, :]
        cols = []
        for c in range(KW // LANES):
            cols.append(norm_rope(zin[br][:, c * LANES:(c + 1) * LANES], gk))
        for c in range(KW // LANES):
            cols.append(zin[br][:, KW + c * LANES:KW + (c + 1) * LANES])
        cols_t = [col.T for col in cols]
        for c, col in enumerate(cols):
            rows[br][:, c * LANES:(c + 1) * LANES] = col
        rows_tile_t = jnp.concatenate(cols_t, axis=0)

        @pl.when(is_prompt)
        def _(br=br, rows_tile_t=rows_tile_t):
            rows_t[br][...] = rows_tile_t

        @pl.when(jnp.logical_not(is_prompt))
        def _(br=br, rows_tile_t=rows_tile_t):
            rows_t_s[br][...] = rows_tile_t
        if kg[br] is not None:
            wt = vg[br].shape[-1]
            for c in range(KW // LANES):
                kb = cols[c].astype(BF16)
                kg[br][2 * c] = kb[:, :HD]
                kg[br][2 * c + 1] = kb[:, HD:]
                vtb = cols_t[KW // LANES + c].astype(BF16)
                for t in range(tm // wt):
                    vg[br][2 * c, t] = vtb[:HD, t * wt:(t + 1) * wt]
                    vg[br][2 * c + 1, t] = vtb[HD:, t * wt:(t + 1) * wt]


def _qkv_post(z1, cos_tab, sin_tab, gq, gk, n_prompt, seq):
    n = z1.shape[0]
    tm = 256
    n_ptiles = n_prompt // tm
    tiles_per_seq = seq // tm
    n_sample = n - n_prompt
    last_p = n_ptiles - 1

    def tab_map(i):
        return (jnp.where(i < n_ptiles, i % tiles_per_seq, tiles_per_seq), 0)

    row = lambda i: (i, 0)
    head = lambda i: (0, i, 0)
    slab = lambda i: (jnp.minimum(i, last_p) // tiles_per_seq, 0, jnp.minimum(i, last_p) % tiles_per_seq)
    slab_s = lambda i: (0, 0, jnp.maximum(i - n_ptiles, 0))
    rows_sd = jax.ShapeDtypeStruct((n, KVW), F32)
    rows_t_sd = jax.ShapeDtypeStruct((n_prompt // seq, KVW, seq), F32)
    rows_ts_sd = jax.ShapeDtypeStruct((1, KVW, n_sample), F32)
    grp_sd = jax.ShapeDtypeStruct((KVH, n, HD), BF16)
    assert SEL_KT % tm == 0 and tm % Q_BLK == 0
    out_shape = (
        jax.ShapeDtypeStruct((n, NSA_W), F32),
        jax.ShapeDtypeStruct((HEADS, HD, n), BF16),
        rows_sd, rows_sd, rows_sd,
        rows_t_sd, rows_t_sd, rows_t_sd,
        rows_ts_sd, rows_ts_sd, rows_ts_sd,
        grp_sd,
        jax.ShapeDtypeStruct((KVH, -(-n // SEL_KT), HD, SEL_KT), BF16),
        grp_sd,
        jax.ShapeDtypeStruct((KVH, n // Q_BLK, HD, Q_BLK), BF16),
    )
    rows_spec = pl.BlockSpec((tm, KVW), row)
    rows_t_spec = pl.BlockSpec((None, KVW, tm), slab)
    rows_ts_spec = pl.BlockSpec((None, KVW, tm), slab_s)
    grp_spec = pl.BlockSpec((KVH, tm, HD), head)
    per = SEL_KT // tm
    out_specs = (
        pl.BlockSpec((tm, NSA_W), row),
        pl.BlockSpec((HEADS, HD, tm), lambda i: (0, 0, i)),
        rows_spec, rows_spec, rows_spec,
        rows_t_spec, rows_t_spec, rows_t_spec,
        rows_ts_spec, rows_ts_spec, rows_ts_spec,
        grp_spec,
        pl.BlockSpec((KVH, 1, HD, tm), lambda i: (0, i // per, 0, i % per)),
        grp_spec,
        pl.BlockSpec((KVH, tm // Q_BLK, HD, Q_BLK), lambda i: (0, i, 0, 0)),
    )
    kv_in = lambda br: pl.BlockSpec((tm, KVW), lambda i: (i, C1_KV // KVW + br))
    return pl.pallas_call(
        functools.partial(_qkv_post_kernel, npt=n_ptiles),
        grid=(n // tm,),
        in_specs=[
            pl.BlockSpec((tm, NSA_W), lambda i: (i, C1_Q // NSA_W)),
            kv_in(0), kv_in(1), kv_in(2),
            pl.BlockSpec((tm, LANES), tab_map),
            pl.BlockSpec((tm, LANES), tab_map),
            pl.BlockSpec((1, LANES), lambda i: (0, 0)),
            pl.BlockSpec((3, LANES), lambda i: (0, 0)),
        ],
        out_specs=out_specs,
        out_shape=out_shape,
        compiler_params=_params("parallel"),
        name="qkv_post",
    )(z1, z1, z1, z1, cos_tab, sin_tab, gq, gk)


def _compress_pair(flat, w2, pe8, w2bd, n_ch, transposed=False):
    ab = _dot(flat, w2)
    hp = _dot(pe8.astype(BF16), w2)
    hpe = hp[0:1, :LANES] + hp[1:2, LANES:]
    nxt = pltpu.roll(ab[:, LANES:], n_ch - 1, 0)
    h = ab[:, :LANES] + nxt + hpe
    act = (h * jax.nn.sigmoid(h)).astype(BF16)
    if transposed:
        return _nt(w2bd, act)
    return _dot(act, w2bd)


def _scatter_chunk_rows(xs_ref, perm, tile_t, c0):
    rpp = tile_t.shape[1] // D_CMP
    xp = _nt(perm, tile_t)
    for t in range(D_CMP):
        xs_ref[t, c0:c0 + rpp, :] = xp[t * rpp:(t + 1) * rpp, :]


def _chunk_flat(xs_ref, p):
    return jnp.concatenate([xs_ref[t, :, p * LANES:(p + 1) * LANES] for t in range(D_CMP)], axis=1).astype(BF16)


def _compress_prompt_kernel(rows_ref, w2k_ref, w2v_ref, pek_ref, pev_ref, bdk_ref, bdv_ref, perm_ref, kc_ref, vc_ref,
                            xs_ref, *, n_ch):
    w2 = (w2k_ref, w2v_ref)
    pe = (pek_ref, pev_ref)
    bd = (bdk_ref, bdv_ref)
    tile_rows = perm_ref.shape[0]
    rpp = tile_rows // D_CMP
    for i in range(n_ch // rpp):
        _scatter_chunk_rows(xs_ref, perm_ref[...], rows_ref[:, i * tile_rows:(i + 1) * tile_rows].astype(BF16), i * rpp)
    for p in range(KVW // LANES):
        kv = p // 2
        flat = _chunk_flat(xs_ref, p)
        res = _compress_pair(flat, w2[kv][...], pe[kv][...], bd[kv][...], n_ch, transposed=(kv == 1)).astype(BF16)
        g0 = 2 * (p % 2)
        if kv == 0:
            kc_ref[0, g0] = res[:, :HD]
            kc_ref[0, g0 + 1] = res[:, HD:]
        else:
            vc_ref[0, g0] = res[:HD]
            vc_ref[0, g0 + 1] = res[HD:]


def _compress_prompt(rows_cmp, cw, batch, seq):
    n_ch = seq // D_CMP
    const = lambda b: (0, 0)
    return pl.pallas_call(
        functools.partial(_compress_prompt_kernel, n_ch=n_ch),
        grid=(batch,),
        in_specs=[pl.BlockSpec((None, KVW, seq), lambda b: (b, 0, 0))] + [pl.BlockSpec(w.shape, const) for w in cw],
        out_specs=(pl.BlockSpec((1, KVH, n_ch, HD), lambda b: (b, 0, 0, 0)),
                   pl.BlockSpec((1, KVH, HD, n_ch), lambda b: (b, 0, 0, 0))),
        out_shape=(jax.ShapeDtypeStruct((batch, KVH, n_ch, HD), BF16),
                   jax.ShapeDtypeStruct((batch, KVH, HD, n_ch), BF16)),
        scratch_shapes=[pltpu.VMEM((D_CMP, n_ch, KVW), F32)],
        compiler_params=_params("parallel"),
        name="compress_prompt",
    )(rows_cmp, *cw)


def _rank_select(score_t, n_blocks, n_rows, n_top):
    st = score_t[:n_rows]
    jio = lax.broadcasted_iota(jnp.int32, st.shape, 0)
    cnt = jnp.zeros(st.shape, F32)
    for i in range(n_blocks):
        cnt = cnt + _beats(score_t[i:i + 1, :], st, jio, i)
    return jnp.where(cnt < n_top, 1.0, 0.0)


def _beats(row, st, jio, i):
    return jnp.where(row > st, 1.0, jnp.where(row == st, jnp.where(jio > i, 1.0, 0.0), 0.0))


def _pad_rows(x, n):
    if x.shape[0] == n:
        return x
    return jnp.concatenate([x, jnp.zeros((n - x.shape[0], x.shape[1]), x.dtype)], axis=0)


def _cmp_select_kernel(qt_ref, kc_ref, vct_ref, ovt_ref, ocmp_ref, sel_ref, st_ref, *, nc, ns, n_top):
    s0 = pl.program_id(1) * Q_BLK
    ncp = kc_ref.shape[2]
    nsp = st_ref.shape[1]
    c_io = lax.broadcasted_iota(jnp.int32, (ncp, Q_BLK), 0)
    qpos_c = s0 + lax.broadcasted_iota(jnp.int32, (ncp, Q_BLK), 1)
    cbias = jnp.where((c_io * D_CMP + (L_CMP - 1) <= qpos_c) & (c_io < nc), 0.0, NEG)
    qpos_1 = s0 + lax.broadcasted_iota(jnp.int32, (1, GRP * Q_BLK), 1) % Q_BLK
    col_ok = jnp.where((qpos_1 >= L_CMP - 1) & (nc > 0), 1.0, 0.0)
    jio = lax.broadcasted_iota(jnp.int32, (nsp, LANES), 0)
    qpos_n = s0 + lax.broadcasted_iota(jnp.int32, (nsp, LANES), 1)
    vis_t = (jio * L_SEL <= qpos_n) & (jio < ns)
    cur = qpos_n // L_SEL
    forced = (jio == 0) | (jio == cur) | (jio == cur - 1)
    ovt = ovt_ref[...]
    grp = range(KVH)
    qt = [jnp.concatenate([qt_ref[GRP * g + r] for r in range(GRP)], axis=1) * (HD ** -0.5) for g in grp]
    s = [_dot(kc_ref[0, g], qt[g]) for g in grp]
    s = [jnp.concatenate([s[g][:, r * Q_BLK:(r + 1) * Q_BLK] + cbias for r in range(GRP)], axis=1) for g in grp]
    e = [jnp.exp(s[g] - jnp.max(s[g], axis=0, keepdims=True)) for g in grp]
    p = [e[g] * (col_ok / jnp.maximum(jnp.sum(e[g], axis=0, keepdims=True), 1.0)) for g in grp]
    o = [_dot(vct_ref[0, g], p[g].astype(BF16)) for g in grp]
    for g in grp:
        for r in range(GRP):
            ocmp_ref[GRP * g + r] = o[g][:, r * Q_BLK:(r + 1) * Q_BLK]
    psum = [p[g][:, 0:Q_BLK] + p[g][:, Q_BLK:2 * Q_BLK] + p[g][:, 2 * Q_BLK:3 * Q_BLK] + p[g][:, 3 * Q_BLK:]
            for g in grp]
    imp = [_dot_exact_rhs(ovt, psum[g]) for g in grp]
    for g in grp:
        st_ref[g] = jnp.where(vis_t, imp[g][:nsp] + jnp.where(forced, BIG, 0.0), -BIG)

    n_vis = jnp.minimum((s0 + Q_BLK - 1) // L_SEL + 1, ns)

    def count(i, cnts):
        return tuple(c + _beats(st_ref[g, pl.ds(i, 1), :], st_ref[g, 0:nsp, :], jio, i) for g, c in enumerate(cnts))

    cnts = lax.fori_loop(0, n_vis, count, tuple(jnp.zeros((nsp, LANES), F32) for _ in range(KVH)))
    for g in range(KVH):
        sel_ref[g] = jnp.where((cnts[g] < n_top) & vis_t, 0.0, NEG)


def _cmp_select(q_t, kc, vc_t, ov_t, batch, seq):
    nqt = seq // Q_BLK
    n_ch = kc.shape[2]
    ns = seq // L_SEL
    nsp = -(-ns // 8) * 8
    kern = functools.partial(_cmp_select_kernel, nc=n_ch - 1, ns=ns, n_top=min(N_TOP, ns))
    q_spec = pl.BlockSpec((HEADS, HD, Q_BLK), lambda b, t: (0, 0, b * nqt + t))
    return pl.pallas_call(
        kern,
        grid=(batch, nqt),
        in_specs=[
            q_spec,
            pl.BlockSpec((1, KVH, n_ch, HD), lambda b, t: (b, 0, 0, 0)),
            pl.BlockSpec((1, KVH, HD, n_ch), lambda b, t: (b, 0, 0, 0)),
            pl.BlockSpec(ov_t.shape, lambda b, t: (0, 0)),
        ],
        out_specs=(
            q_spec,
            pl.BlockSpec((None, KVH, nsp, LANES), lambda b, t: (b * nqt + t, 0, 0, 0)),
        ),
        out_shape=(
            jax.ShapeDtypeStruct((HEADS, HD, batch * seq), F32),
            jax.ShapeDtypeStruct((batch * nqt, KVH, nsp, LANES), F32),
        ),
        scratch_shapes=[pltpu.VMEM((KVH, nsp, LANES), F32)],
        compiler_params=_params("parallel", "parallel"),
        name="cmp_select",
    )(q_t, kc, vc_t, ov_t)


def _sel_win_kernel(qt_ref, ks_ref, vst_ref, kw_ref, vwt_ref, bias_ref, zbr_ref, ocmp_ref, oa_ref, q_s):
    kt = vst_ref.shape[-1]
    s0 = pl.program_id(1) * Q_BLK
    cols = GRP * Q_BLK
    n_kt = (s0 + Q_BLK + kt - 1) // kt
    bpt = kt // L_SEL
    key_io = lax.broadcasted_iota(jnp.int32, (kt, Q_BLK), 0)
    qpos = s0 + lax.broadcasted_iota(jnp.int32, (kt, Q_BLK), 1)
    ones_row = 2 * 8
    va_rows = HD + ones_row

    for g in range(KVH):
        q_s[g] = jnp.concatenate([qt_ref[GRP * g + r] for r in range(GRP)], axis=1) * (HD ** -0.5)

    def with_bias(st, bias):
        return jnp.concatenate([st[:, r * Q_BLK:(r + 1) * Q_BLK] + bias for r in range(GRP)], axis=1)

    def body(j, carry):
        k0 = pl.multiple_of(j * kt, kt)
        grp = range(KVH)
        causal = jnp.where(k0 + key_io <= qpos, 0.0, NEG)
        st = [_dot(ks_ref[g, pl.ds(k0, kt), :], q_s[g]) for g in grp]
        bias = [jnp.concatenate([jnp.broadcast_to(bias_ref[g, pl.ds(j * bpt + b, 1), :], (L_SEL, Q_BLK))
                                 for b in range(bpt)], axis=0) + causal for g in grp]
        st = [with_bias(st[g], bias[g]) for g in grp]
        m_new = [jnp.maximum(carry[g][0], jnp.max(st[g], axis=0, keepdims=True)) for g in grp]
        corr = [jnp.exp(carry[g][0] - m_new[g]) for g in grp]
        p = [jnp.exp(st[g] - m_new[g]).astype(BF16) for g in grp]
        ones = jnp.ones((ones_row, kt), BF16)
        pv = [_dot(jnp.concatenate([vst_ref[g, j], ones], axis=0), p[g]) for g in grp]
        return tuple((m_new[g], carry[g][1] * corr[g] + pv[g]) for g in grp)

    init = tuple((jnp.full((1, cols), NEG, F32), jnp.zeros((va_rows, cols), F32)) for _ in range(KVH))
    res = lax.fori_loop(0, n_kt, body, init)

    wl = WINDOW + Q_BLK
    wstart = pl.multiple_of(jnp.maximum(s0 - WINDOW, 0), Q_BLK)
    wt0 = wstart // Q_BLK
    kposw = wstart + lax.broadcasted_iota(jnp.int32, (wl, Q_BLK), 0)
    qposw = s0 + lax.broadcasted_iota(jnp.int32, (wl, Q_BLK), 1)
    wbias = jnp.where((kposw <= qposw) & (kposw > qposw - WINDOW), 0.0, NEG)
    ones_w = jnp.ones((ones_row, wl), BF16)
    gates_t = jax.nn.sigmoid(zbr_ref[...]).T
    o_t = []
    st_w = [_dot(kw_ref[g, pl.ds(wstart, wl), :], q_s[g]) for g in range(KVH)]
    st_w = [with_bias(st_w[g], wbias) for g in range(KVH)]
    e_w = [jnp.exp(st_w[g] - jnp.max(st_w[g], axis=0, keepdims=True)).astype(BF16) for g in range(KVH)]
    for g in range(KVH):
        acc = res[g][1]
        o_sel_t = acc[:HD] / acc[HD:HD + 1]
        e = e_w[g]
        vwt = jnp.concatenate([vwt_ref[g, wt0 + i] for i in range(wl // Q_BLK)], axis=1)
        pv = _dot(jnp.concatenate([vwt, ones_w], axis=0), e)
        o_win_t = pv[:HD] / jnp.maximum(pv[HD:HD + 1], 1.0)
        for r in range(GRP):
            h = GRP * g + r
            cs = slice(r * Q_BLK, (r + 1) * Q_BLK)
            o_t.append(gates_t[h:h + 1] * ocmp_ref[h]
                       + gates_t[HEADS + h:HEADS + h + 1] * o_sel_t[:, cs]
                       + gates_t[2 * HEADS + h:2 * HEADS + h + 1] * o_win_t[:, cs])
    for c in range(HEADS // 2):
        oa_ref[:, c * LANES:(c + 1) * LANES] = jnp.concatenate([o_t[2 * c], o_t[2 * c + 1]], axis=0).T


def _sel_win(q_t, ks, vs_t, kw, vw_t, sel_bias, z, ocmp, batch, seq):
    nqt = seq // Q_BLK
    assert seq % SEL_KT == 0
    once = dict(pipeline_mode=pl.Buffered(1))
    k_spec = pl.BlockSpec((KVH, seq, HD), lambda b, t: (0, b, 0), **once)
    tile = lambda b, t: (b * nqt + t, 0, 0, 0)
    return pl.pallas_call(
        _sel_win_kernel,
        grid=(batch, nqt),
        in_specs=[
            pl.BlockSpec((HEADS, HD, Q_BLK), lambda b, t: (0, 0, b * nqt + t)),
            k_spec,
            pl.BlockSpec((KVH, seq // SEL_KT, HD, SEL_KT), lambda b, t: (0, b, 0, 0), **once),
            k_spec,
            pl.BlockSpec((KVH, seq // Q_BLK, HD, Q_BLK), lambda b, t: (0, b, 0, 0), **once),
            pl.BlockSpec((None,) + sel_bias.shape[1:], tile),
            pl.BlockSpec((Q_BLK, LANES), lambda b, t: (b * nqt + t, C1_BR // LANES)),
            pl.BlockSpec((HEADS, HD, Q_BLK), lambda b, t: (0, 0, b * nqt + t)),
        ],
        out_specs=pl.BlockSpec((Q_BLK, NSA_W), lambda b, t: (b * nqt + t, 0)),
        out_shape=jax.ShapeDtypeStruct((batch * seq, NSA_W), F32),
        scratch_shapes=[pltpu.VMEM((KVH, HD, GRP * Q_BLK), BF16)],
        compiler_params=_params("parallel", "arbitrary"),
        name="sel_win",
    )(q_t, ks, vs_t, kw, vw_t, sel_bias, z, ocmp)


def _hgrn_gates(zq, zf, logits):
    mx = jnp.max(logits, axis=0, keepdims=True)
    ex = jnp.exp(logits - mx)
    lb = ex[0:1] / jnp.sum(ex, axis=0, keepdims=True)
    hq = zq * jax.nn.sigmoid(zq)
    e = jnp.exp(-jnp.abs(zf))
    r = 1.0 / (1.0 + e)
    pos = zf >= 0.0
    sig = jnp.where(pos, r, e * r)
    sig_neg = jnp.where(pos, e * r, r)
    f = lb + (1.0 - lb) * sig
    log_f = jnp.where(f > 0.0, jnp.log(f), jnp.log1p(-lb) + zf)
    hk = (1.0 - lb) * sig_neg
    return hq, hk, log_f


def _intra_chunk(ac, qc, kc, vc):
    n = ac.shape[0]
    sub = 8
    tio = lax.broadcasted_iota(jnp.int32, (sub, ac.shape[1]), 0)
    accs = [jnp.zeros((sub, vc.shape[1]), F32) for _ in range(n // sub)]
    for s in range(n):
        a_s, k_s, v_s = ac[s:s + 1, :], kc[s:s + 1, :], vc[s:s + 1, :]
        for i in range(s // sub, n // sub):
            rs = slice(i * sub, (i + 1) * sub)
            w = qc[rs] * k_s * jnp.exp(ac[rs] - a_s)
            if i == s // sub:
                w = jnp.where(tio >= s - i * sub, w, 0.0)
            accs[i] = accs[i] + jnp.sum(w, axis=-1, keepdims=True) * v_s
    return accs[0] if len(accs) == 1 else jnp.concatenate(accs, axis=0)


def _hgrn_prompt_kernel(zq_ref, zf_ref, zi_ref, lg_ref, tri_ref, full_ref, o_ref, sout_ref,
                        st_ref, a_s, q_s, k_s, qt_s, ktb_s, eal_s, vt_s):
    t = pl.program_id(1)
    tt = zq_ref.shape[0]

    @pl.when(t == 0)
    def _():
        st_ref[...] = jnp.zeros(st_ref.shape, F32)

    hq, hk, log_f = _hgrn_gates(zq_ref[...], zf_ref[...], lg_ref[...])
    a = _dot_exact_rhs(tri_ref[...], log_f)
    al = _dot_exact_rhs(full_ref[...], log_f)
    a_s[...] = a
    q_s[...] = hq
    k_s[...] = hk
    qt_s[...] = hq * jnp.exp(a)
    ktb_s[...] = (hk * jnp.exp(al - a)).astype(BF16)
    eal_s[...] = jnp.exp(al)
    for h in range(HG_H):
        vt_s[h] = zi_ref[:, h * HG_DV:(h + 1) * HG_DV].T.astype(BF16)
    lane_tok = lax.broadcasted_iota(jnp.int32, (HG_DV, tt), 1)

    def chunk(c, carry):
        r0 = pl.multiple_of(c * HG_SUB, HG_SUB)
        in_chunk = (lane_tok >= r0) & (lane_tok < r0 + HG_SUB)
        for h in range(HG_H):
            sl = slice(h * HG_DK, (h + 1) * HG_DK)
            st = st_ref[h]
            o = _nt(qt_s[pl.ds(r0, HG_SUB), sl].astype(BF16), st.astype(BF16))
            o = o + _intra_chunk(a_s[pl.ds(r0, HG_SUB), sl], q_s[pl.ds(r0, HG_SUB), sl],
                                 k_s[pl.ds(r0, HG_SUB), sl], zi_ref[pl.ds(r0, HG_SUB), sl])
            o_ref[pl.ds(r0, HG_SUB), sl] = o
            vtm = jnp.where(in_chunk, vt_s[h], jnp.zeros_like(vt_s[h]))
            st_ref[h] = st * eal_s[pl.ds(r0, 1), sl] + _dot(vtm, ktb_s[:, sl])
        return carry

    lax.fori_loop(0, tt // HG_SUB, chunk, 0)

    @pl.when(t == pl.num_programs(1) - 1)
    def _():
        for h in range(HG_H):
            sout_ref[0, h] = st_ref[h].T


def _hgrn_prompt(z, logits, batch, seq):
    tt = 128
    nt = seq // tt
    sub = np.arange(tt) // HG_SUB
    same = sub[:, None] == sub[None, :]
    tri = jnp.asarray((same & (np.arange(tt)[None, :] <= np.arange(tt)[:, None])).astype(np.float32), BF16)
    full = jnp.asarray(same.astype(np.float32), BF16)
    col = lambda c: (lambda b, t: (b * nt + t, c // HG_W))
    const = lambda b, t: (0, 0)
    return pl.pallas_call(
        _hgrn_prompt_kernel,
        grid=(batch, nt),
        in_specs=[
            pl.BlockSpec((tt, HG_W), col(C2_HQ)),
            pl.BlockSpec((tt, HG_W), col(C2_HF)),
            pl.BlockSpec((tt, HG_W), col(C2_HI)),
            pl.BlockSpec((2, HG_W), const),
            pl.BlockSpec((tt, tt), const),
            pl.BlockSpec((tt, tt), const),
        ],
        out_specs=(
            pl.BlockSpec((tt, HG_W), lambda b, t: (b * nt + t, 0)),
            pl.BlockSpec((1, HG_H, HG_DK, HG_DV), lambda b, t: (b, 0, 0, 0)),
        ),
        out_shape=(
            jax.ShapeDtypeStruct((batch * seq, HG_W), F32),
            jax.ShapeDtypeStruct((batch, HG_H, HG_DK, HG_DV), F32),
        ),
        scratch_shapes=[
            pltpu.VMEM((HG_H, HG_DV, HG_DK), F32),
            pltpu.VMEM((tt, HG_W), F32),
            pltpu.VMEM((tt, HG_W), F32),
            pltpu.VMEM((tt, HG_W), F32),
            pltpu.VMEM((tt, HG_W), F32),
            pltpu.VMEM((tt, HG_W), BF16),
            pltpu.VMEM((tt, HG_W), F32),
            pltpu.VMEM((HG_H, HG_DV, tt), BF16),
        ],
        compiler_params=_params("parallel", "arbitrary"),
        name="hgrn_prompt",
    )(z, z, z, logits, tri, full)


def _hgrn_sample_kernel(zq_ref, zf_ref, zi_ref, lg_ref, s0_ref, o_ref, sout_ref, *, s_len):
    n_seq = s0_ref.shape[0]
    hq_all, hk_all, logf_all = _hgrn_gates(zq_ref[...], zf_ref[...], lg_ref[...])
    rio = lax.broadcasted_iota(jnp.int32, (s_len, HG_W), 0)
    row0 = lax.broadcasted_iota(jnp.int32, (s_len, HG_DK), 0) == 0
    pad = HG_DK - 2 * s_len
    for q in range(n_seq):
        rs = slice(q * s_len, (q + 1) * s_len)
        hq, hk, log_f = hq_all[rs], hk_all[rs], logf_all[rs]
        a = log_f
        d = 1
        while d < s_len:
            a = a + jnp.where(rio >= d, pltpu.roll(a, d, 0), 0.0)
            d *= 2
        a_last = jnp.sum(log_f, axis=0, keepdims=True)
        qt = hq * jnp.exp(a)
        kt = hk * jnp.exp(a_last - a)
        ea = jnp.exp(a_last)
        hv = zi_ref[rs, :]
        for h in range(HG_H):
            sl = slice(h * HG_DK, (h + 1) * HG_DK)
            s_old = s0_ref[q, h]
            qt16 = _pad_rows(qt[:, sl], 16).astype(BF16)
            o = _dot(qt16, s_old.astype(BF16))[:s_len]
            o = o + _intra_chunk(a[:, sl], hq[:, sl], hk[:, sl], hv[:, sl])
            o_ref[rs, sl] = o
            m = jnp.concatenate([kt[:, sl], jnp.where(row0, jnp.broadcast_to(ea[:, sl], (s_len, HG_DK)), 0.0),
                                 jnp.zeros((pad, HG_DK), F32)], axis=0)
            mt = m.T
            e_col = mt[:, s_len:s_len + 1]
            vpad = _pad_rows(hv[:, sl], HG_DK).astype(BF16)
            sout_ref[q, h] = e_col * s_old + _dot(mt.astype(BF16), vpad)


def _hgrn_sample(z, logits, state, n_prompt, dec_batch, s_len):
    n_seq = 4
    rows = n_seq * s_len
    base = n_prompt // rows
    col = lambda c: (lambda b: (base + b, c // HG_W))
    st_spec = pl.BlockSpec((n_seq, HG_H, HG_DK, HG_DV), lambda b: (b, 0, 0, 0))
    return pl.pallas_call(
        functools.partial(_hgrn_sample_kernel, s_len=s_len),
        grid=(dec_batch // n_seq,),
        in_specs=[
            pl.BlockSpec((rows, HG_W), col(C2_HQ)),
            pl.BlockSpec((rows, HG_W), col(C2_HF)),
            pl.BlockSpec((rows, HG_W), col(C2_HI)),
            pl.BlockSpec((2, HG_W), lambda b: (0, 0)),
            st_spec,
        ],
        out_specs=(pl.BlockSpec((rows, HG_W), lambda b: (b, 0)), st_spec),
        out_shape=(
            jax.ShapeDtypeStruct((dec_batch * s_len, HG_W), F32),
            jax.ShapeDtypeStruct(state.shape, F32),
        ),
        compiler_params=_params("parallel"),
        name="hgrn_sample",
    )(z, z, z, logits, state)


def _sample_nsa_kernel(pt_ref, *refs, npg, s_len, past):
    del pt_ref
    cmp_pages = refs[:npg]
    sel_pages = refs[npg:2 * npg]
    (win_ref, q_ref, rsel_ref, rwin_ref, zbr_ref, w2k_ref, w2v_ref, pek_ref, pev_ref, bdk_ref, bdv_ref, perm_ref,
     ov_ref, e_ref, oa_ref, swin_ref, xs_ref, kcv_ref) = refs[2 * npg:]
    step = pl.program_id(0)

    @pl.when(step == 0)
    def _():
        kcv_ref[...] = jnp.zeros(kcv_ref.shape, BF16)

    kc_all = kcv_ref[(step + 1) % 2, 0]
    vc_all = kcv_ref[(step + 1) % 2, 1]
    page = cmp_pages[0].shape[1]
    rpp = page // D_CMP
    n_ch = npg * rpp
    nc = n_ch - 1
    wb = win_ref.shape[1]
    ns = -(-(past + s_len) // L_SEL)
    nsp = -(-ns // 8) * 8
    n_top = min(N_TOP, ns)
    rows = HEADS * s_len

    w2 = (w2k_ref, w2v_ref)
    pe = (pek_ref, pev_ref)
    bd = (bdk_ref, bdv_ref)
    comp = []

    def stage1_scatter(lo, hi):
        for i in range(lo, hi):
            _scatter_chunk_rows(xs_ref, perm_ref[...], cmp_pages[i][...].astype(BF16), i * rpp)

    def stage1_compress(p):
        kv = p // 2
        comp.append(_compress_pair(_chunk_flat(xs_ref, p), w2[kv][...], pe[kv][...], bd[kv][...], n_ch))

    q = q_ref[...]
    lane8 = lax.broadcasted_iota(jnp.int32, (s_len, LANES), 1)
    zero8 = jnp.zeros((s_len, LANES), F32)
    blocks = []
    for g in range(KVH):
        for r in range(GRP):
            h = GRP * g + r
            src = q[:, (h // 2) * LANES:(h // 2 + 1) * LANES]
            if (r % 2) != (g % 2):
                src = pltpu.roll(src, HD, 1)
            keep = (lane8 < HD) if g % 2 == 0 else (lane8 >= HD)
            piece = jnp.where(keep, src, 0.0)
            blocks.append(jnp.concatenate([piece, zero8] if g // 2 == 0 else [zero8, piece], axis=1))
    qbd = (jnp.concatenate(blocks, axis=0) * (HD ** -0.5)).astype(BF16)

    def rowpos(shape):
        return past + lax.broadcasted_iota(jnp.int32, shape, 0) % s_len

    stage1_scatter(0, npg // 2)

    s = _nt(qbd, kc_all)
    cidx = lax.broadcasted_iota(jnp.int32, s.shape, 1)
    cvalid = (cidx * D_CMP + (L_CMP - 1) <= rowpos(s.shape)) & (cidx < nc)
    s = jnp.where(cvalid, s, NEG)
    m = jnp.max(s, axis=-1, keepdims=True)
    e = jnp.where(cvalid, jnp.exp(s - m), 0.0)
    p = e / jnp.maximum(jnp.sum(e, axis=-1, keepdims=True), 1.0)
    ocmp = _dot(p.astype(BF16), vc_all)

    stage1_scatter(npg // 2, npg)

    gs = GRP * s_len
    psum = []
    for g in range(KVH):
        acc = p[g * gs:g * gs + s_len]
        for r in range(1, GRP):
            acc = acc + p[g * gs + r * s_len:g * gs + (r + 1) * s_len]
        psum.append(acc)
    imp = _dot_exact_lhs(jnp.concatenate(psum, axis=0), ov_ref[...])
    qpos = rowpos(imp.shape)
    blk = lax.broadcasted_iota(jnp.int32, imp.shape, 1)
    visible = (blk * L_SEL <= qpos) & (blk < ns)
    cur = qpos // L_SEL
    forced = (blk == 0) | (blk == cur) | (blk == cur - 1)
    score = jnp.where(visible, imp + jnp.where(forced, BIG, 0.0), -BIG)
    stage1_compress(0)
    sel_t = _rank_select(_pad_rows(score, LANES).T, ns, nsp, n_top)
    sel = _pad_rows(sel_t, LANES).T[:KVH * s_len]
    sel = jnp.where(visible, sel, 0.0)
    sel_rows = jnp.concatenate([sel[g * s_len:(g + 1) * s_len] for g in range(KVH) for _ in range(GRP)], axis=0)

    zpad = jnp.zeros((page - s_len, KW), F32)
    knew = jnp.concatenate([rsel_ref[:, :KW], zpad], axis=0).astype(BF16)
    vnew = jnp.concatenate([rsel_ref[:, KW:], zpad], axis=0).astype(BF16)
    def page_pair(i, lo):
        return jnp.concatenate([sel_pages[2 * i][lo:lo + KW, :], sel_pages[2 * i + 1][lo:lo + KW, :]],
                               axis=1).astype(BF16)

    s = jnp.concatenate([_dot(qbd, page_pair(i, 0)) for i in range(npg // 2)] + [_nt(qbd, knew)], axis=1)
    mf = _dot(sel_rows.astype(BF16), e_ref[...])
    kpos = lax.broadcasted_iota(jnp.int32, s.shape, 1)
    mask = (mf > 0.5) & (kpos <= rowpos(s.shape))
    stage1_compress(1)
    s = jnp.where(mask, s, NEG)
    m = jnp.max(s, axis=-1, keepdims=True)
    e = jnp.where(mask, jnp.exp(s - m), 0.0)
    l = jnp.sum(e, axis=-1, keepdims=True)
    eb = e.astype(BF16)
    acc = _dot(eb[:, npg * page:], vnew)
    for i in range(npg // 2):
        acc = acc + _nt(eb[:, 2 * i * page:(2 * i + 2) * page], page_pair(i, KW))
    osel = acc / l

    kwn = jnp.concatenate([rwin_ref[:, :KW], zpad], axis=0).astype(BF16)
    vwn = jnp.concatenate([rwin_ref[:, KW:], zpad], axis=0).astype(BF16)
    s = jnp.concatenate([_dot(qbd, win_ref[:KW, :].astype(BF16)), _nt(qbd, kwn)], axis=1)
    kposw = (past - wb) + lax.broadcasted_iota(jnp.int32, s.shape, 1)
    qposw = rowpos(s.shape)
    wmask = (kposw <= qposw) & (kposw > qposw - WINDOW) & (kposw >= 0)
    s = jnp.where(wmask, s, NEG)
    m = jnp.max(s, axis=-1, keepdims=True)
    e = jnp.where(wmask, jnp.exp(s - m), 0.0)
    den = jnp.maximum(jnp.sum(e, axis=-1, keepdims=True), 1.0)
    eb = e.astype(BF16)
    owin = (_nt(eb[:, :wb], win_ref[KW:, :].astype(BF16)) + _dot(eb[:, wb:], vwn)) / den

    stage1_compress(2)

    gates = jax.nn.sigmoid(zbr_ref[...])
    for g in range(KVH):
        gl = slice((g // 2) * LANES, (g // 2 + 1) * LANES)
        for k2 in range(2):
            halves = []
            for r in (2 * k2, 2 * k2 + 1):
                h = GRP * g + r
                rs = slice(h * s_len, (h + 1) * s_len)
                val = (gates[:, h:h + 1] * ocmp[rs, gl]
                       + gates[:, HEADS + h:HEADS + h + 1] * osel[rs, gl]
                       + gates[:, 2 * HEADS + h:2 * HEADS + h + 1] * owin[rs, gl])
                if (r % 2) != (g % 2):
                    val = pltpu.roll(val, HD, 1)
                halves.append(val)
            c = 2 * g + k2
            oa_ref[:, c * LANES:(c + 1) * LANES] = jnp.where(lane8 < HD, halves[0], halves[1])

    new_t = [_pad_rows(rwin_ref[:, c * LANES:(c + 1) * LANES], LANES).T for c in range(KVW // LANES)]
    new_t = jnp.concatenate(new_t, axis=0)
    lane_w = lax.broadcasted_iota(jnp.int32, (KVW, LANES), 1)
    n_col = wb // LANES
    rolled = [pltpu.roll(win_ref[:, c * LANES:(c + 1) * LANES], LANES - s_len, 1) for c in range(n_col)]
    rolled.append(pltpu.roll(new_t, LANES - s_len, 1))
    for c in range(n_col):
        swin_ref[:, c * LANES:(c + 1) * LANES] = jnp.where(lane_w < LANES - s_len, rolled[c], rolled[c + 1])

    stage1_compress(3)
    kcv_ref[step % 2, 0] = jnp.concatenate(comp[0:2], axis=1).astype(BF16)
    kcv_ref[step % 2, 1] = jnp.concatenate(comp[2:4], axis=1).astype(BF16)


def _sample_nsa(page_table, cache_cmp, cache_sel, cache_win, q_tok, rows_sel, rows_win, z, cw, ov, e_mat,
                n_prompt, dec_batch, s_len):
    npg = page_table.shape[1]
    page = cache_cmp.shape[2]
    past = npg * page
    wb = cache_win.shape[2]
    base = n_prompt // s_len
    last = dec_batch - 1
    steps = dec_batch + 1
    seq2 = lambda b: jnp.maximum(b - 1, 0)
    page_table = jnp.concatenate([page_table, page_table[last:], page_table[:1], page_table], axis=0)
    cmp_specs = [pl.BlockSpec((None, KVW, page), (lambda b, pt, i=i: (pt[b, i], 0, 0))) for i in range(npg)]
    sel_specs = [pl.BlockSpec((None, KVW, page), (lambda b, pt, i=i: (pt[steps + b, i], 0, 0))) for i in range(npg)]
    row = lambda b, pt: (base + seq2(b), 0)
    const = lambda b, pt: (0, 0)
    in_specs = (
        cmp_specs + sel_specs
        + [pl.BlockSpec((None, KVW, wb), lambda b, pt: (seq2(b), 0, 0)),
           pl.BlockSpec((s_len, NSA_W), row),
           pl.BlockSpec((s_len, KVW), row),
           pl.BlockSpec((s_len, KVW), row),
           pl.BlockSpec((s_len, LANES), lambda b, pt: (base + seq2(b), C1_BR // LANES))]
        + [pl.BlockSpec(w.shape, const) for w in cw]
        + [pl.BlockSpec(ov.shape, const), pl.BlockSpec(e_mat.shape, const)]
    )
    n_ch = past // D_CMP
    grid_spec = pltpu.PrefetchScalarGridSpec(
        num_scalar_prefetch=1,
        grid=(dec_batch + 1,),
        in_specs=in_specs,
        out_specs=(
            pl.BlockSpec((s_len, NSA_W), lambda b, pt: (seq2(b), 0)),
            pl.BlockSpec((None, KVW, wb), lambda b, pt: (seq2(b), 0, 0)),
        ),
        scratch_shapes=[
            pltpu.VMEM((D_CMP, n_ch, KVW), F32),
            pltpu.VMEM((2, 2, n_ch, KW), BF16),
        ],
    )
    args = (page_table, *([cache_cmp] * npg), *([cache_sel] * npg), cache_win, q_tok, rows_sel, rows_win, z, *cw,
            ov, e_mat)
    return pl.pallas_call(
        functools.partial(_sample_nsa_kernel, npg=npg, s_len=s_len, past=past),
        grid_spec=grid_spec,
        out_shape=(
            jax.ShapeDtypeStruct((dec_batch * s_len, NSA_W), F32),
            jax.ShapeDtypeStruct((dec_batch, KVW, wb), F32),
        ),
        compiler_params=_params("arbitrary"),
        name="sample_nsa",
    )(*args)


def _mix_kernel(xp_ref, xs_ref, oap_ref, oas_ref, obp_ref, obs_ref, zhg_ref, ga_ref, gb_ref, gn_ref,
                wa_ref, wb_ref, wo_ref, o_ref, *, npt):
    is_prompt = pl.program_id(0) < npt
    ya = _dot(jnp.where(is_prompt, oap_ref[...], oas_ref[...]).astype(BF16), wa_ref[...])
    gn = gn_ref[...]
    parts = []
    for h in range(HG_H):
        sl = slice(h * HG_DV, (h + 1) * HG_DV)
        ob = jnp.where(is_prompt, obp_ref[:, sl], obs_ref[:, sl])
        ms = jnp.mean(ob * ob, axis=-1, keepdims=True)
        zg = zhg_ref[:, sl]
        y = ((ob * lax.rsqrt(ms + EPS)) * gn) * (zg * jax.nn.sigmoid(zg))
        parts.append(y.astype(BF16))
    yb = _dot(jnp.concatenate(parts, axis=1), wb_ref[...])
    mix = jax.nn.sigmoid(ga_ref[...]) * ya + jax.nn.sigmoid(gb_ref[...]) * yb
    y = _dot(mix.astype(BF16), wo_ref[...])

    @pl.when(pl.program_id(0) < npt)
    def _():
        o_ref[...] = xp_ref[...] + y

    @pl.when(pl.program_id(0) >= npt)
    def _():
        o_ref[...] = xs_ref[...] + y


def _mix(xp, xs, oa_p, oa_s, ob_p, ob_s, z, gn, wa, wb, wo):
    (n_p, d), n_s = xp.shape, xs.shape[0]
    n = n_p + n_s
    tm = 256
    npt, spec_p, spec_s = _two_group_specs(n_p, n_s, tm, d)
    _, mixer_p, mixer_s = _two_group_specs(n_p, n_s, tm, NSA_W)
    assert NSA_W == HG_W
    row = lambda i: (i, 0)
    const = lambda i: (0, 0)
    resident = lambda w: pl.BlockSpec(w.shape, const, pipeline_mode=pl.Buffered(1))
    return pl.pallas_call(
        functools.partial(_mix_kernel, npt=npt),
        grid=(n // tm,),
        in_specs=[
            spec_p, spec_s,
            mixer_p, mixer_s, mixer_p, mixer_s,
            pl.BlockSpec((tm, HG_W), lambda i: (i, C2_HG // HG_W)),
            pl.BlockSpec((tm, d), lambda i: (i, C2_GA // D_MODEL)),
            pl.BlockSpec((tm, d), lambda i: (i, C2_GB // D_MODEL)),
            pl.BlockSpec((1, HG_DV), const),
            resident(wa), resident(wb), resident(wo),
        ],
        out_specs=pl.BlockSpec((tm, d), row),
        out_shape=jax.ShapeDtypeStruct((n, d), F32),
        compiler_params=_params("parallel"),
        name="mix_out",
    )(xp, xs, oa_p, oa_s, ob_p, ob_s, z, z, z, gn, wa, wb, wo)


def _ffn_kernel(x_ref, g_ref, w1_ref, w2_ref, op_ref, os_ref, xn_ref, acc_ref, *, npt):
    i = pl.program_id(0)
    j = pl.program_id(1)
    last = pl.num_programs(1) - 1

    @pl.when(j == 0)
    def _():
        xn_ref[...] = _rmsnorm_bf16(x_ref[...], g_ref[...])
        acc_ref[...] = jnp.zeros(acc_ref.shape, F32)

    h = _dot(xn_ref[...], w1_ref[...])
    h = jnp.square(jnp.maximum(h, 0.0)).astype(BF16)
    acc_ref[...] += _dot(h, w2_ref[...])

    @pl.when((j == last) & (i < npt))
    def _():
        op_ref[...] = x_ref[...] + acc_ref[...]

    @pl.when((j == last) & (i >= npt))
    def _():
        os_ref[...] = x_ref[...] + acc_ref[...]


def _ffn(x, g, w1, w2, n_p):
    n, d = x.shape
    n_s = n - n_p
    dff = w1.shape[1]
    tm = _pick_tile(np.gcd(n_p, n_s), (512, 256))
    tf = 1024
    npt, spec_p, spec_s = _two_group_specs(n_p, n_s, tm, d)
    return pl.pallas_call(
        functools.partial(_ffn_kernel, npt=npt),
        grid=(n // tm, dff // tf),
        in_specs=[
            pl.BlockSpec((tm, d), lambda i, j: (i, 0)),
            pl.BlockSpec((1, d), lambda i, j: (0, 0)),
            pl.BlockSpec((d, tf), lambda i, j: (0, j)),
            pl.BlockSpec((tf, d), lambda i, j: (j, 0)),
        ],
        out_specs=(spec_p, spec_s),
        out_shape=(jax.ShapeDtypeStruct((n_p, d), F32), jax.ShapeDtypeStruct((n_s, d), F32)),
        scratch_shapes=[pltpu.VMEM((tm, d), BF16), pltpu.VMEM((tm, d), F32)],
        compiler_params=_params("arbitrary", "arbitrary"),
        name="ffn",
    )(x, g, w1, w2)


def _compress_weights(cmp_pe, cmp_w1, cmp_w2):
    m = L_CMP // D_CMP
    eye2 = jnp.eye(2, dtype=F32)
    out = []
    w2s, pes, bds = [], [], []
    for kv in range(2):
        w1p = cmp_w1[kv].reshape(m, D_CMP, HD, HD).transpose(1, 2, 0, 3)
        w = w1p[:, None, :, :, None, :] * eye2[None, :, None, None, :, None]
        w2s.append(w.reshape(D_CMP * LANES, m * LANES).astype(BF16))
        pe = cmp_pe[kv].reshape(m, D_CMP, 1, HD)
        pe = jnp.broadcast_to(pe, (m, D_CMP, 2, HD)).reshape(m, D_CMP * LANES)
        pes.append(jnp.concatenate([pe, jnp.zeros((8 - m, D_CMP * LANES), F32)], axis=0))
        bds.append(jnp.kron(eye2, cmp_w2[kv]).astype(BF16))
    tile_rows = LANES
    rpp = tile_rows // D_CMP
    src = np.arange(tile_rows)
    perm = np.zeros((tile_rows, tile_rows), np.float32)
    perm[(src % D_CMP) * rpp + src // D_CMP, src] = 1.0
    return (w2s[0], w2s[1], pes[0], pes[1], bds[0], bds[1], jnp.asarray(perm, BF16))


def _overlap_matrix(nc_pad, ns, transposed=False):
    cs = np.arange(nc_pad) * D_CMP
    ss = np.arange(LANES) * L_SEL
    ov = np.clip(np.minimum(cs[:, None] + L_CMP, ss[None, :] + L_SEL) - np.maximum(cs[:, None], ss[None, :]), 0, None)
    ov = ((ov / D_CMP) * (np.arange(LANES)[None, :] < ns)).astype(np.float32)
    return jnp.asarray(np.ascontiguousarray(ov.T) if transposed else ov, BF16)


def _expand_matrix(n_keys):
    e = (np.arange(n_keys)[None, :] // L_SEL) == np.arange(LANES)[:, None]
    return jnp.asarray(e.astype(np.float32), BF16)


def _rope_tables(seq, past, s_len, tm):
    half = HD // 2
    inv = THETA ** (-jnp.arange(half, dtype=F32) * 2.0 / HD)
    pos = jnp.concatenate([jnp.arange(seq), past + jnp.arange(tm) % s_len]).astype(F32)
    ang = pos[:, None] * inv[None, :]
    cos = jnp.cos(ang)
    sin = jnp.sin(ang)
    cos_t = jnp.concatenate([cos, cos, cos, cos], axis=1)
    sin_t = jnp.concatenate([-sin, sin, -sin, sin], axis=1)
    return cos_t, sin_t


def kernel(x_prompt, x_sample, cache_cmp_kv, cache_sel_kv, cache_win_kv, state_hgrn, page_table, norm1_g, w_in,
           q_norm_g, k_norm_g, cmp_pe, cmp_w1, cmp_w2, w_proj_a, hgrn_lb_logits, hgrn_norm_g, w_proj_b, w_out,
           norm2_g, w_ff1, w_ff2):
    assert w_in.shape[0] == 1, "single-layer step"
    batch, seq, d = x_prompt.shape
    dec_batch, s_len, _ = x_sample.shape
    n_prompt = batch * seq
    n_sample = dec_batch * s_len
    npg = page_table.shape[1]
    page = cache_cmp_kv.shape[2]
    past = npg * page
    wb = cache_win_kv.shape[2]
    assert seq % Q_BLK == 0 and seq >= WINDOW + Q_BLK and n_prompt % 256 == 0 and n_sample % 256 == 0
    assert s_len % 8 == 0 and page % D_CMP == 0 and 2 * s_len <= HG_DK and dec_batch % 4 == 0 and npg % 2 == 0

    xp = x_prompt.reshape(n_prompt, d)
    xs = x_sample.reshape(n_sample, d)

    cw = _compress_weights(cmp_pe[0], cmp_w1[0], cmp_w2[0])
    cos_t, sin_t = _rope_tables(seq, past, s_len, 256)
    gq = jnp.tile(q_norm_g[0][None, :], (1, 2))
    gk = jnp.tile(k_norm_g[0], (1, 2))
    logits = hgrn_lb_logits.astype(F32)

    def token_minor(c):
        return jnp.transpose(c, (0, 2, 3, 4, 1)).reshape(c.shape[0], KVW, c.shape[1])

    def token_major(c_t):
        return jnp.transpose(c_t.reshape(c_t.shape[0], 2, KVH, HD, c_t.shape[2]), (0, 4, 1, 2, 3))

    w_t = jnp.transpose(w_in[0])
    xn = _xnorm(xp, xs, norm1_g)
    z1 = _in_proj(xn, w_t, 0, N_Z1, 896)
    z2 = _in_proj(xn, w_t, sum(_SIZES[:3]), N_Z2, 1024)
    (q_tok, q_t, rows_cmp, rows_sel, rows_win, rt_cmp, rt_sel, rt_win, _, _, _,
     ks, vs_t, kw, vw_t) = _qkv_post(z1, cos_t, sin_t, gq, gk, n_prompt, seq)

    bdv_t = lax.optimization_barrier(cw[5].T)
    kc, vc_t = _compress_prompt(rt_cmp, cw[:5] + (bdv_t,) + cw[6:], batch, seq)
    ov_p = _overlap_matrix(seq // D_CMP, seq // L_SEL, transposed=True)
    ocmp, sel_bias = _cmp_select(q_t, kc, vc_t, ov_p, batch, seq)
    oa_p = _sel_win(q_t, ks, vs_t, kw, vw_t, sel_bias, z1, ocmp, batch, seq)
    ob_p, p_state = _hgrn_prompt(z2, logits, batch, seq)

    ns_s = -(-(past + s_len) // L_SEL)
    ov_s = _overlap_matrix(past // D_CMP, ns_s)
    oa_s, s_win_t = _sample_nsa(
        page_table, token_minor(cache_cmp_kv[0]), token_minor(cache_sel_kv[0]), token_minor(cache_win_kv[0]),
        q_tok, rows_sel, rows_win, z1, cw, ov_s, _expand_matrix(past + page), n_prompt, dec_batch, s_len)
    ob_s, s_state = _hgrn_sample(z2, logits, state_hgrn[0], n_prompt, dec_batch, s_len)

    x2 = _mix(xp, xs, oa_p, oa_s, ob_p, ob_s, z2, hgrn_norm_g, w_proj_a[0].astype(BF16), w_proj_b[0].astype(BF16),
              w_out[0].astype(BF16))
    y_p, y_s = _ffn(x2, norm2_g, w_ff1[0].astype(BF16), w_ff2[0].astype(BF16), n_prompt)

    kv_shape = (2, KVH, HD)
    wp = min(WINDOW, seq)
    return (
        y_p.reshape(batch, seq, d),
        y_s.reshape(dec_batch, s_len, d),
        token_major(rt_cmp)[None],
        token_major(rt_sel)[None],
        token_major(rt_win[:, :, seq - wp:])[None],
        p_state[None],
        rows_cmp[n_prompt:].reshape((1, dec_batch, s_len) + kv_shape),
        rows_sel[n_prompt:].reshape((1, dec_batch, s_len) + kv_shape),
        token_major(s_win_t)[None],
        s_state[None],
    )
```
